```python
import math
import jax, jax.numpy as jnp
from jax import lax
import numpy as np

D_MODEL = 2048
BATCH = 1
SEQ = 8192
DEPTH = 2

GRID_W = 64
CTX_LEN = 256
N_BRANCH = 4
BRANCH_W = D_MODEL // 4
CHUNK = 128
A_GROUPS = 4
CONV_W = 31
F_GROUPS = 4
QK_NOPE = 64
QK_ROPE = 32
V_HEAD = 64
MLA_HEADS = BRANCH_W // V_HEAD
Q_LORA = D_MODEL // 4
KV_LORA = D_MODEL // 8
ROPE_THETA = 10000.0
Q_BLOCK = 128
MLA_SCALE = (QK_NOPE + QK_ROPE) ** -0.5
N_GROUPS = 4
EXP_PER_GROUP = 8
N_EXPERTS = N_GROUPS * EXP_PER_GROUP
TOP_K = 2
D_EXPERT = D_MODEL // 2
MOE_BLOCK = 128
COL_A = 0
COL_B = COL_A + 2 * BRANCH_W
COL_C = COL_B + 2 * BRANCH_W
COL_D = COL_C + BRANCH_W
COL_KV = COL_D + Q_LORA
COL_KR = COL_KV + KV_LORA
COL_G = COL_KR + QK_ROPE
D_IN = COL_G + N_BRANCH * D_MODEL
ALPHA = (2 * DEPTH) ** 0.25
BETA = (8 * DEPTH) ** -0.25
LN_EPS = 1e-6

kernel_name = "hybrid_gated_mixers_mla_hmoe_dit"


def _ln0(x):
    xf = x.astype(jnp.float32)
    mu = xf.mean(-1, keepdims=True)
    var = jnp.square(xf - mu).mean(-1, keepdims=True)
    return ((xf - mu) * lax.rsqrt(var + LN_EPS)).astype(x.dtype)


def _ln(x, g, b):
    xf = x.astype(jnp.float32)
    mu = xf.mean(-1, keepdims=True)
    var = jnp.square(xf - mu).mean(-1, keepdims=True)
    y = (xf - mu) * lax.rsqrt(var + LN_EPS)
    return (y * g.astype(jnp.float32) + b.astype(jnp.float32)).astype(x.dtype)


def _rms(x, g):
    xf = x.astype(jnp.float32)
    y = xf * lax.rsqrt(jnp.mean(xf * xf, -1, keepdims=True) + LN_EPS)
    return (y * g.astype(jnp.float32)).astype(x.dtype)


def _modulate(x, shift, scale):
    return _ln0(x) * (1 + scale) + shift


def _axial_rope_tables(s_len, dtype):
    rows = s_len // GRID_W
    row = jnp.repeat(jnp.arange(rows, dtype=jnp.float32), GRID_W)
    col = jnp.tile(jnp.arange(GRID_W, dtype=jnp.float32), rows)
    half = QK_ROPE // 2
    inv = ROPE_THETA ** (-jnp.arange(0, half, 2, dtype=jnp.float32) / half)
    ar = row[:, None] * inv
    ac = col[:, None] * inv
    ang = jnp.concatenate([ar, ar, ac, ac], -1)
    return jnp.cos(ang).astype(dtype), jnp.sin(ang).astype(dtype)


def _rope(x, cos, sin):
    x1, x2, x3, x4 = jnp.split(x, 4, -1)
    rot = jnp.concatenate([-x2, x1, -x4, x3], -1)
    return x * cos + rot * sin


def _chunk_spatial_gate(uv, p):
    b_, n_tok, _ = uv.shape
    uv = jax.nn.gelu(uv)
    u, v = jnp.split(uv, 2, -1)
    v = _ln(v, p["a_ln_g"], p["a_ln_b"])
    v = v.reshape(b_, n_tok // CHUNK, CHUNK, A_GROUPS, BRANCH_W // A_GROUPS)
    mixed = jnp.einsum("gpq,bnqgc->bnpgc", p["a_w_s"], v) + p["a_b_s"].T[None, None, :, :, None]
    return u * mixed.reshape(b_, n_tok, BRANCH_W)


def _conformer_conv(ab, p):
    a, g = jnp.split(ab, 2, -1)
    y = a * jax.nn.sigmoid(g)
    y = lax.conv_general_dilated(y, p["b_w_dw"][:, None, :], (1,), [(CONV_W // 2, CONV_W // 2)],
                                 dimension_numbers=("NWC", "WIO", "NWC"), feature_group_count=BRANCH_W)
    y = y + p["b_b_dw"]
    return jax.nn.silu(_ln(y, p["b_ln_g"], p["b_ln_b"]))


def _fourier_mix(z):
    b_, n_tok, ch = z.shape
    zf = z.astype(jnp.float32).reshape(b_, n_tok, F_GROUPS, ch // F_GROUPS)
    y = jnp.fft.fft2(zf, axes=(1, 3), norm="ortho").real
    return y.reshape(b_, n_tok, ch).astype(z.dtype)


def _local_mixers(z, p):
    return (_chunk_spatial_gate(z[..., COL_A:COL_B], p),
            _conformer_conv(z[..., COL_B:COL_C], p),
            _fourier_mix(z[..., COL_C:COL_D]))


def _mla_queries(cq, p):
    b_, n_tok, _ = cq.shape
    q = (_rms(cq, p["d_q_g"]) @ p["d_w_uq"]).reshape(b_, n_tok, MLA_HEADS, QK_NOPE + QK_ROPE)
    return q[..., :QK_NOPE], q[..., QK_NOPE:]


def _mla_kv(ckv, p):
    b_, n_tok, _ = ckv.shape
    ckv = _rms(ckv, p["d_kv_g"])
    kn = (ckv @ p["d_w_uk"]).reshape(b_, n_tok, MLA_HEADS, QK_NOPE)
    v = (ckv @ p["d_w_uv"]).reshape(b_, n_tok, MLA_HEADS, V_HEAD)
    return kn, v


def _attend(qn, qr, kn, kr, v):
    s = jnp.einsum("bqhd,bkhd->bhqk", qn, kn) + jnp.einsum("bqhr,bkr->bhqk", qr, kr)
    pr = jax.nn.softmax(s.astype(jnp.float32) * MLA_SCALE, axis=-1).astype(v.dtype)
    o = jnp.einsum("bhqk,bkhd->bqhd", pr, v)
    return o.reshape(o.shape[0], o.shape[1], MLA_HEADS * V_HEAD)


def _attend_blocked(qn, qr, kn, kr, v):
    b_, n_tok = qn.shape[:2]
    nb = n_tok // Q_BLOCK

    def blk(t):
        return t.reshape(b_, nb, Q_BLOCK, *t.shape[2:]).swapaxes(0, 1)

    o = lax.map(lambda a: _attend(a[0], a[1], kn, kr, v), (blk(qn), blk(qr)))
    return o.swapaxes(0, 1).reshape(b_, n_tok, MLA_HEADS * V_HEAD)


def _merge(z, branches, p):
    b_, n_tok, _ = z.shape
    gates = jax.nn.sigmoid(z[..., COL_G:].reshape(b_, n_tok, N_BRANCH, D_MODEL))
    y = 0
    for i, br in enumerate(branches):
        y = y + gates[:, :, i, :] * (br @ p["w_branch"][i])
    return y @ p["w_o"]


def _hier_moe(h, w_grp, b_grp, w_exp, b_exp, w1, w3, w2):
    n_t, d = h.shape
    g_logit = (h @ w_grp).astype(jnp.float32) + b_grp.astype(jnp.float32)
    g_idx = jnp.argmax(g_logit, -1).astype(jnp.int32)
    g_p = jnp.take_along_axis(jax.nn.softmax(g_logit, -1), g_idx[:, None], -1)
    e_logit = ((h @ w_exp).astype(jnp.float32) + b_exp.astype(jnp.float32)).reshape(n_t, N_GROUPS, EXP_PER_GROUP)
    e_logit = jnp.take_along_axis(e_logit, g_idx[:, None, None], 1)[:, 0]
    top_v, top_i = lax.top_k(e_logit, TOP_K)
    gate = (jax.nn.softmax(top_v, -1) * g_p).astype(h.dtype).reshape(-1)
    expert = (g_idx[:, None] * EXP_PER_GROUP + top_i.astype(jnp.int32)).reshape(-1)
    tok = jnp.repeat(jnp.arange(n_t, dtype=jnp.int32), TOP_K)
    n = n_t * TOP_K
    order = jnp.argsort(expert)
    se, stok, sgate = expert[order], tok[order], gate[order]
    counts = jax.ops.segment_sum(jnp.ones((n,), jnp.int32), expert, num_segments=N_EXPERTS)
    starts = jnp.cumsum(counts) - counts
    padded = (counts + MOE_BLOCK - 1) // MOE_BLOCK * MOE_BLOCK
    pends = jnp.cumsum(padded)
    pstarts = pends - padded
    dest = pstarts[se] + jnp.arange(n, dtype=jnp.int32) - starts[se]
    n_rows = -(-(n + N_EXPERTS * (MOE_BLOCK - 1)) // MOE_BLOCK) * MOE_BLOCK
    n_blk = n_rows // MOE_BLOCK
    row_tok = jnp.zeros((n_rows,), jnp.int32).at[dest].set(stok)
    row_gate = jnp.zeros((n_rows,), h.dtype).at[dest].set(sgate)
    blk_start = jnp.arange(n_blk, dtype=jnp.int32) * MOE_BLOCK
    blk_exp = jnp.minimum(jnp.sum(pends[None, :] <= blk_start[:, None], axis=1), N_EXPERTS - 1).astype(jnp.int32)

    def run(args):
        t, e = args
        xb = h[t]
        return (jax.nn.silu(xb @ w1[e]) * (xb @ w3[e])) @ w2[e]

    y = lax.map(run, (row_tok.reshape(n_blk, MOE_BLOCK), blk_exp)).reshape(n_rows, d)
    return jnp.zeros_like(h).at[row_tok].add(y * row_gate[:, None])


def setup_inputs(seed: int = 0) -> dict:
    key = jax.random.key(seed)
    ks = iter(jax.random.split(key, 48))
    L, D = DEPTH, D_MODEL

    def nrm(shape, s):
        return jax.random.normal(next(ks), shape, jnp.float32) * s

    return {
        "x": nrm((BATCH, SEQ, D), 1.0),
        "c": nrm((BATCH, D), 1.0),
        "ctx": nrm((BATCH, CTX_LEN, D), 1.0),
        "c_ctx": nrm((D,), 1.0),
        "w_ada": nrm((L, D, 6 * D), 0.5 * D ** -0.5),
        "b_ada": nrm((L, 6 * D), 0.01),
        "w_in": nrm((L, D, D_IN), D ** -0.5),
        "b_in": nrm((L, D_IN), 0.01),
        "a_ln_g": 1.0 + nrm((L, BRANCH_W), 0.05),
        "a_ln_b": nrm((L, BRANCH_W), 0.01),
        "a_w_s": nrm((L, A_GROUPS, CHUNK, CHUNK), CHUNK ** -0.5),
        "a_b_s": 1.0 + nrm((L, A_GROUPS, CHUNK), 0.05),
        "b_w_dw": nrm((L, CONV_W, BRANCH_W), CONV_W ** -0.5),
        "b_b_dw": nrm((L, BRANCH_W), 0.01),
        "b_ln_g": 1.0 + nrm((L, BRANCH_W), 0.05),
        "b_ln_b": nrm((L, BRANCH_W), 0.01),
        "d_q_g": 1.0 + nrm((L, Q_LORA), 0.05),
        "d_w_uq": nrm((L, Q_LORA, MLA_HEADS * (QK_NOPE + QK_ROPE)), Q_LORA ** -0.5),
        "d_kv_g": 1.0 + nrm((L, KV_LORA), 0.05),
        "d_w_uk": nrm((L, KV_LORA, MLA_HEADS * QK_NOPE), KV_LORA ** -0.5),
        "d_w_uv": nrm((L, KV_LORA, MLA_HEADS * V_HEAD), KV_LORA ** -0.5),
        "w_branch": nrm((L, N_BRANCH, BRANCH_W, D), BETA * BRANCH_W ** -0.5),
        "w_o": nrm((L, D, D), BETA * D ** -0.5),
        "ln1_g": 1.0 + nrm((L, D), 0.05),
        "ln1_b": nrm((L, D), 0.01),
        "r_w_grp": nrm((L, D, N_GROUPS), D ** -0.5),
        "r_b_grp": nrm((L, N_GROUPS), 0.01),
        "r_w_exp": nrm((L, D, N_EXPERTS), D ** -0.5),
        "r_b_exp": nrm((L, N_EXPERTS), 0.01),
        "e_w1": nrm((L, N_EXPERTS, D, D_EXPERT), D ** -0.5),
        "e_w3": nrm((L, N_EXPERTS, D, D_EXPERT), D ** -0.5),
        "e_w2": nrm((L, N_EXPERTS, D_EXPERT, D), BETA * D_EXPERT ** -0.5),
        "ln2_g": 1.0 + nrm((L, D), 0.05),
        "ln2_b": nrm((L, D), 0.01),
    }


def reference(x, c, ctx, c_ctx, w_ada, b_ada, w_in, b_in, a_ln_g, a_ln_b, a_w_s, a_b_s,
              b_w_dw, b_b_dw, b_ln_g, b_ln_b, d_q_g, d_w_uq, d_kv_g, d_w_uk, d_w_uv,
              w_branch, w_o, ln1_g, ln1_b, r_w_grp, r_b_grp, r_w_exp, r_b_exp,
              e_w1, e_w3, e_w2, ln2_g, ln2_b):
    b_, s_len, d = x.shape
    n_ctx = ctx.shape[1]
    cos, sin = _axial_rope_tables(s_len, x.dtype)
    s_lat = jax.nn.silu(c)
    s_ctx = jax.nn.silu(c_ctx)[None]
    xl, xc = x, ctx
    for l in range(DEPTH):
        last = l == DEPTH - 1
        p = {"a_ln_g": a_ln_g[l], "a_ln_b": a_ln_b[l], "a_w_s": a_w_s[l], "a_b_s": a_b_s[l],
             "b_w_dw": b_w_dw[l], "b_b_dw": b_b_dw[l], "b_ln_g": b_ln_g[l], "b_ln_b": b_ln_b[l],
             "d_q_g": d_q_g[l], "d_w_uq": d_w_uq[l], "d_kv_g": d_kv_g[l], "d_w_uk": d_w_uk[l],
             "d_w_uv": d_w_uv[l], "w_branch": w_branch[l], "w_o": w_o[l]}
        ml = jnp.split((s_lat @ w_ada[l] + b_ada[l])[:, None, :], 6, -1)
        mc = jnp.split((s_ctx @ w_ada[l] + b_ada[l])[:, None, :], 6, -1)

        hl = _modulate(xl, ml[0], ml[1])
        hc = _modulate(xc, mc[0], mc[1])
        zl = hl @ w_in[l] + b_in[l]
        if last:
            zc = hc @ w_in[l][:, COL_KV:COL_G] + b_in[l][COL_KV:COL_G]
            zc_kv = zc
        else:
            zc = hc @ w_in[l] + b_in[l]
            zc_kv = zc[..., COL_KV:COL_G]
        kn_c, v_c = _mla_kv(zc_kv[..., :KV_LORA], p)
        kr_c = zc_kv[..., KV_LORA:]
        qn_l, qr_l = _mla_queries(zl[..., COL_D:COL_KV], p)
        qr_l = _rope(qr_l, cos[:, None, :], sin[:, None, :])
        kn_l, v_l = _mla_kv(zl[..., COL_KV:COL_KR], p)
        kr_l = _rope(zl[..., COL_KR:COL_G], cos, sin)
        yd_l = _attend_blocked(qn_l, qr_l,
                               jnp.concatenate([kn_c, kn_l], 1),
                               jnp.concatenate([kr_c, kr_l], 1),
                               jnp.concatenate([v_c, v_l], 1))
        mix_l = _merge(zl, _local_mixers(zl, p) + (yd_l,), p)
        if not last:
            qn_c, qr_c = _mla_queries(zc[..., COL_D:COL_KV], p)
            yd_c = _attend(qn_c, qr_c, kn_c, kr_c, v_c)
            mix_c = _merge(zc, _local_mixers(zc, p) + (yd_c,), p)
            xc = _ln(ALPHA * xc + mc[2] * mix_c, ln1_g[l], ln1_b[l])
        xl = _ln(ALPHA * xl + ml[2] * mix_l, ln1_g[l], ln1_b[l])

        hl = _modulate(xl, ml[3], ml[4]).reshape(-1, d)
        if last:
            fl = _hier_moe(hl, r_w_grp[l], r_b_grp[l], r_w_exp[l], r_b_exp[l],
                           e_w1[l], e_w3[l], e_w2[l]).reshape(b_, s_len, d)
        else:
            hc = _modulate(xc, mc[3], mc[4]).reshape(-1, d)
            f = _hier_moe(jnp.concatenate([hc, hl], 0), r_w_grp[l], r_b_grp[l], r_w_exp[l], r_b_exp[l],
                          e_w1[l], e_w3[l], e_w2[l])
            fc = f[:b_ * n_ctx].reshape(b_, n_ctx, d)
            fl = f[b_ * n_ctx:].reshape(b_, s_len, d)
            xc = _ln(ALPHA * xc + mc[5] * fc, ln2_g[l], ln2_b[l])
        xl = _ln(ALPHA * xl + ml[5] * fl, ln2_g[l], ln2_b[l])
    return xl
```

```python
import functools
import math

import numpy as np
import jax
import jax.numpy as jnp
from jax import lax
from jax.experimental import pallas as pl
from jax.experimental.pallas import tpu as pltpu

F32 = jnp.float32
BF16 = jnp.bfloat16

D = 2048
SEQ = 8192
N_CTX = 256
T = N_CTX + SEQ
DEPTH = 2
GRID_W = 64
BW = 512
CHUNK = 128
A_GROUPS = 4
CONV_W = 31
F_GROUPS = 4
QK_NOPE = 64
QK_ROPE = 32
V_HEAD = 64
HEADS = 8
Q_LORA = 512
KV_LORA = 256
ROPE_THETA = 10000.0
MLA_SCALE = (QK_NOPE + QK_ROPE) ** -0.5
N_GROUPS = 4
EXP_PER_GROUP = 8
N_EXPERTS = 32
TOP_K = 2
D_EXPERT = 1024
MOE_BLOCK = 128
COL_A = 0
COL_B = 1024
COL_C = 2048
COL_D = 2560
COL_KV = 3072
COL_KR = 3328
COL_G = 3360
ALPHA = (2 * DEPTH) ** 0.25
LN_EPS = 1e-6

LANE = 128
HEAD_PAD = 128
RB = 256
NRB = T // RB
FFT_N1 = 64
FFT_N2 = 128
ROW_SUB = D // LANE
VMEM_LIMIT = 56 * 1024 * 1024


def _cparams(sem):
    return pltpu.CompilerParams(dimension_semantics=sem, vmem_limit_bytes=VMEM_LIMIT)


def _dot(a, b):
    return jnp.dot(a, b, preferred_element_type=F32)


def _dot_exact(a, b):
    return jnp.dot(a, b, preferred_element_type=F32, precision=lax.Precision.HIGHEST)


def _ln0(x):
    mu = jnp.mean(x, axis=-1, keepdims=True)
    xc = x - mu
    var = jnp.mean(xc * xc, axis=-1, keepdims=True)
    return xc * lax.rsqrt(var + LN_EPS)


def _mod_row(mod_ref, q):
    lat = mod_ref[0:1, q * D:(q + 1) * D]
    ctx = mod_ref[1:2, q * D:(q + 1) * D]
    return jnp.where(pl.program_id(0) == 0, ctx, lat)


def _ada_kernel(c_ref, w_ref, b_ref, o_ref):
    s = c_ref[...]
    s = s * jax.nn.sigmoid(s)
    o_ref[0] = _dot_exact(s, w_ref[0]) + b_ref[0]


def ada_mod(cvec, w_ada, b_ada):
    tn = 1536
    return pl.pallas_call(
        _ada_kernel,
        grid=(DEPTH, 6 * D // tn),
        in_specs=[pl.BlockSpec((8, D), lambda l, j: (0, 0)),
                  pl.BlockSpec((1, D, tn), lambda l, j: (l, 0, j)),
                  pl.BlockSpec((1, 1, tn), lambda l, j: (l, 0, j))],
        out_specs=pl.BlockSpec((1, 8, tn), lambda l, j: (l, 0, j)),
        out_shape=jax.ShapeDtypeStruct((DEPTH, 8, 6 * D), F32),
        compiler_params=_cparams(("arbitrary", "arbitrary")),
        name="ada_mod",
    )(cvec, w_ada, b_ada.reshape(DEPTH, 1, 6 * D))


def _modln_kernel(x_ref, mod_ref, h_ref):
    h = _ln0(x_ref[...]) * (1.0 + _mod_row(mod_ref, 1)) + _mod_row(mod_ref, 0)
    h_ref[...] = h.astype(BF16)


def modln(x, mod):
    return pl.pallas_call(
        _modln_kernel,
        grid=(NRB,),
        in_specs=[pl.BlockSpec((RB, D), lambda i: (i, 0)),
                  pl.BlockSpec((8, 6 * D), lambda i: (0, 0))],
        out_specs=pl.BlockSpec((RB, D), lambda i: (i, 0)),
        out_shape=jax.ShapeDtypeStruct((T, D), BF16),
        compiler_params=_cparams(("arbitrary",)),
        name="modln",
    )(x, mod)


def _mix_a_kernel(h_ref, w_ref, b_ref, g_ref, bb_ref, ws_ref, bs_ref, o_ref):
    uv = jax.nn.gelu(_dot(h_ref[...], w_ref[...]) + b_ref[...])
    u = uv[:, :BW]
    v = _ln0(uv[:, BW:]) * g_ref[...] + bb_ref[...]
    v = v.astype(BF16)
    gw = BW // A_GROUPS
    for ch in range(RB // CHUNK):
        rows = slice(ch * CHUNK, (ch + 1) * CHUNK)
        parts = [_dot(ws_ref[g], v[rows, g * gw:(g + 1) * gw]) + bs_ref[g] for g in range(A_GROUPS)]
        mixed = jnp.concatenate(parts, axis=-1)
        o_ref[rows, :] = (u[rows, :] * mixed).astype(BF16)


def mix_a(h, w, b, ln_g, ln_b, ws, bs):
    return pl.pallas_call(
        _mix_a_kernel,
        grid=(NRB,),
        in_specs=[pl.BlockSpec((RB, D), lambda i: (i, 0)),
                  pl.BlockSpec((D, 2 * BW), lambda i: (0, 0)),
                  pl.BlockSpec((1, 2 * BW), lambda i: (0, 0)),
                  pl.BlockSpec((1, BW), lambda i: (0, 0)),
                  pl.BlockSpec((1, BW), lambda i: (0, 0)),
                  pl.BlockSpec((A_GROUPS, CHUNK, CHUNK), lambda i: (0, 0, 0)),
                  pl.BlockSpec((A_GROUPS, CHUNK, CHUNK), lambda i: (0, 0, 0))],
        out_specs=pl.BlockSpec((RB, BW), lambda i: (i, 0)),
        out_shape=jax.ShapeDtypeStruct((T, BW), BF16),
        compiler_params=_cparams(("arbitrary",)),
        name="mix_a",
    )(h, w, b, ln_g, ln_b, ws, bs)


def _glu_kernel(h_ref, w_ref, b_ref, o_ref):
    ab = _dot(h_ref[...], w_ref[...]) + b_ref[...]
    o_ref[...] = ab[:, :BW] * jax.nn.sigmoid(ab[:, BW:])


def glu(h, w, b):
    return pl.pallas_call(
        _glu_kernel,
        grid=(NRB,),
        in_specs=[pl.BlockSpec((RB, D), lambda i: (i, 0)),
                  pl.BlockSpec((D, 2 * BW), lambda i: (0, 0)),
                  pl.BlockSpec((1, 2 * BW), lambda i: (0, 0))],
        out_specs=pl.BlockSpec((RB, BW), lambda i: (i, 0)),
        out_shape=jax.ShapeDtypeStruct((T, BW), F32),
        compiler_params=_cparams(("arbitrary",)),
        name="glu",
    )(h, w, b)


CONV_HALO = 16


def _conv_kernel(prev_ref, cur_ref, next_ref, w_ref, b_ref, g_ref, bb_ref, o_ref, ext_ref):
    i = pl.program_id(0)
    has_prev = i >= 2
    has_next = jnp.logical_and(i >= 1, i < NRB - 1)
    ext_ref[0:CONV_HALO, :] = jnp.where(has_prev, prev_ref[RB - CONV_HALO:RB, :], 0.0)
    ext_ref[CONV_HALO:CONV_HALO + RB, :] = cur_ref[...]
    ext_ref[CONV_HALO + RB:2 * CONV_HALO + RB, :] = jnp.where(has_next, next_ref[0:CONV_HALO, :], 0.0)
    off = CONV_HALO - CONV_W // 2
    acc = jnp.zeros((RB, BW), F32)
    for k in range(CONV_W):
        acc = acc + ext_ref[off + k:off + k + RB, :] * w_ref[k:k + 1, :]
    y = _ln0(acc + b_ref[...]) * g_ref[...] + bb_ref[...]
    o_ref[...] = (y * jax.nn.sigmoid(y)).astype(BF16)


def conv_ln(y, w_dw, b_dw, ln_g, ln_b):
    return pl.pallas_call(
        _conv_kernel,
        grid=(NRB,),
        in_specs=[pl.BlockSpec((RB, BW), lambda i: (jnp.maximum(i - 1, 0), 0)),
                  pl.BlockSpec((RB, BW), lambda i: (i, 0)),
                  pl.BlockSpec((RB, BW), lambda i: (jnp.minimum(i + 1, NRB - 1), 0)),
                  pl.BlockSpec((CONV_W + 1, BW), lambda i: (0, 0)),
                  pl.BlockSpec((1, BW), lambda i: (0, 0)),
                  pl.BlockSpec((1, BW), lambda i: (0, 0)),
                  pl.BlockSpec((1, BW), lambda i: (0, 0))],
        out_specs=pl.BlockSpec((RB, BW), lambda i: (i, 0)),
        out_shape=jax.ShapeDtypeStruct((T, BW), BF16),
        scratch_shapes=[pltpu.VMEM((RB + 2 * CONV_HALO, BW), F32)],
        compiler_params=_cparams(("arbitrary",)),
        name="conv_ln",
    )(y, y, y, w_dw, b_dw, ln_g, ln_b)


def _dft_tables(n):
    k = np.arange(n, dtype=np.int64)
    ang = 2.0 * np.pi * ((k[:, None] * k[None, :]) % n).astype(np.float64) / n
    return np.cos(ang), np.sin(ang)


def _fproj_kernel(h_ref, w_ref, b_ref, cs_ref, ar_ref, ai_ref):
    z = _dot(h_ref[...], w_ref[...]) + b_ref[...]
    a = _dot_exact(z, cs_ref[...])
    ar_ref[...] = a[:, :BW]
    ai_ref[...] = a[:, BW:]


def fourier_proj(h, w, b, cs):
    return pl.pallas_call(
        _fproj_kernel,
        grid=(NRB,),
        in_specs=[pl.BlockSpec((RB, D), lambda i: (i, 0)),
                  pl.BlockSpec((D, BW), lambda i: (0, 0)),
                  pl.BlockSpec((1, BW), lambda i: (0, 0)),
                  pl.BlockSpec((BW, 2 * BW), lambda i: (0, 0))],
        out_specs=[pl.BlockSpec((RB, BW), lambda i: (i, 0)),
                   pl.BlockSpec((RB, BW), lambda i: (i, 0))],
        out_shape=[jax.ShapeDtypeStruct((T, BW), F32), jax.ShapeDtypeStruct((T, BW), F32)],
        compiler_params=_cparams(("arbitrary",)),
        name="fourier_proj",
    )(h, w, b, cs)


FFT_TILE_N2 = 4


def _fft1_kernel(ar_ref, ai_ref, m_ref, ct_ref, st_ref, br_ref, bi_ref):
    x = jnp.concatenate([ar_ref[...], ai_ref[...]], axis=0)
    b = _dot_exact(m_ref[...], x)
    b_re, b_im = b[:FFT_N1], b[FFT_N1:]
    reps = BW // LANE
    for j in range(FFT_TILE_N2):
        ct = jnp.concatenate([ct_ref[j]] * reps, axis=-1)
        st = jnp.concatenate([st_ref[j]] * reps, axis=-1)
        lanes = slice(j * BW, (j + 1) * BW)
        br_ref[:, lanes] = b_re[:, lanes] * ct + b_im[:, lanes] * st
        bi_ref[:, lanes] = b_im[:, lanes] * ct - b_re[:, lanes] * st


def fft_stage1(ar2d, ai2d, m1, ct, st):
    tn = FFT_TILE_N2 * BW
    return pl.pallas_call(
        _fft1_kernel,
        grid=(FFT_N2 // FFT_TILE_N2,),
        in_specs=[pl.BlockSpec((FFT_N1, tn), lambda j: (0, j)),
                  pl.BlockSpec((FFT_N1, tn), lambda j: (0, j)),
                  pl.BlockSpec((2 * FFT_N1, 2 * FFT_N1), lambda j: (0, 0)),
                  pl.BlockSpec((FFT_TILE_N2, FFT_N1, LANE), lambda j: (j, 0, 0)),
                  pl.BlockSpec((FFT_TILE_N2, FFT_N1, LANE), lambda j: (j, 0, 0))],
        out_specs=[pl.BlockSpec((FFT_N1, tn), lambda j: (0, j)),
                   pl.BlockSpec((FFT_N1, tn), lambda j: (0, j))],
        out_shape=[jax.ShapeDtypeStruct((FFT_N1, FFT_N2 * BW), F32)] * 2,
        compiler_params=_cparams(("arbitrary",)),
        name="fft_stage1",
    )(ar2d, ai2d, m1, ct, st)


FFT_TILE_K1 = 8


def _fft2_kernel(br_ref, bi_ref, c_ref, s_ref, p_ref, o_ref, *, scale):
    ys = [(_dot_exact(c_ref[...], br_ref[a]) + _dot_exact(s_ref[...], bi_ref[a])) * scale
          for a in range(FFT_TILE_K1)]
    y = jnp.concatenate(ys, axis=0).astype(BF16)
    y = _dot(p_ref[...], y)
    o_ref[...] = y.reshape(FFT_N2, FFT_TILE_K1, BW)


def fft_stage2(br3d, bi3d, c128, s128, perm, scale):
    return pl.pallas_call(
        functools.partial(_fft2_kernel, scale=scale),
        grid=(FFT_N1 // FFT_TILE_K1,),
        in_specs=[pl.BlockSpec((FFT_TILE_K1, FFT_N2, BW), lambda a: (a, 0, 0)),
                  pl.BlockSpec((FFT_TILE_K1, FFT_N2, BW), lambda a: (a, 0, 0)),
                  pl.BlockSpec((FFT_N2, FFT_N2), lambda a: (0, 0)),
                  pl.BlockSpec((FFT_N2, FFT_N2), lambda a: (0, 0)),
                  pl.BlockSpec((FFT_TILE_K1 * FFT_N2, FFT_TILE_K1 * FFT_N2), lambda a: (0, 0))],
        out_specs=pl.BlockSpec((FFT_N2, FFT_TILE_K1, BW), lambda a: (0, a, 0)),
        out_shape=jax.ShapeDtypeStruct((FFT_N2, FFT_N1, BW), F32),
        compiler_params=_cparams(("arbitrary",)),
        name="fft_stage2",
    )(br3d, bi3d, c128, s128, perm)


def _dft_ctx_kernel(ar_ref, ai_ref, c_ref, s_ref, o_ref, *, scale):
    o_ref[...] = (_dot_exact(c_ref[...], ar_ref[...]) + _dot_exact(s_ref[...], ai_ref[...])) * scale


def dft_ctx(ar, ai, c, s, scale):
    return pl.pallas_call(
        functools.partial(_dft_ctx_kernel, scale=scale),
        grid=(1,),
        in_specs=[pl.BlockSpec((N_CTX, BW), lambda i: (0, 0)),
                  pl.BlockSpec((N_CTX, BW), lambda i: (0, 0)),
                  pl.BlockSpec((N_CTX, N_CTX), lambda i: (0, 0)),
                  pl.BlockSpec((N_CTX, N_CTX), lambda i: (0, 0))],
        out_specs=pl.BlockSpec((N_CTX, BW), lambda i: (0, 0)),
        out_shape=jax.ShapeDtypeStruct((N_CTX, BW), F32),
        compiler_params=_cparams(("arbitrary",)),
        name="dft_ctx",
    )(ar, ai, c, s)


def _fourier_consts():
    c64, s64 = _dft_tables(FFT_N1)
    m1 = np.block([[c64, s64], [-s64, c64]]).astype(np.float32)
    k1 = np.arange(FFT_N1, dtype=np.float64)
    n2 = np.arange(FFT_N2, dtype=np.float64)
    phi = 2.0 * np.pi * n2[:, None] * k1[None, :] / SEQ
    ct = np.repeat(np.cos(phi)[:, :, None], LANE, axis=2).astype(np.float32)
    st = np.repeat(np.sin(phi)[:, :, None], LANE, axis=2).astype(np.float32)
    c128, s128 = _dft_tables(FFT_N2)
    gw = BW // F_GROUPS
    cg, sg = _dft_tables(gw)
    cs = np.zeros((BW, 2 * BW), np.float32)
    for g in range(F_GROUPS):
        sl = slice(g * gw, (g + 1) * gw)
        cs[sl, sl] = cg
        cs[sl, BW + g * gw:BW + (g + 1) * gw] = -sg
    rows = np.arange(FFT_TILE_K1 * FFT_N2)
    perm = np.zeros((rows.size, rows.size), np.float32)
    perm[rows, (rows % FFT_TILE_K1) * FFT_N2 + rows // FFT_TILE_K1] = 1.0
    cc, sc = _dft_tables(N_CTX)
    return dict(m1=jnp.asarray(m1), ct=jnp.asarray(ct), st=jnp.asarray(st),
                c128=jnp.asarray(c128.astype(np.float32)), s128=jnp.asarray(s128.astype(np.float32)),
                cs=jnp.asarray(cs), perm=jnp.asarray(perm, dtype=BF16),
                cc=jnp.asarray(cc.astype(np.float32)), sc=jnp.asarray(sc.astype(np.float32)))


def fourier_mix(h, w, b, fc):
    gw = BW // F_GROUPS
    ar, ai = fourier_proj(h, w, b, fc["cs"])
    y_ctx = dft_ctx(ar, ai, fc["cc"], fc["sc"], 1.0 / math.sqrt(N_CTX * gw))
    ar2 = ar[N_CTX:].reshape(FFT_N1, FFT_N2 * BW)
    ai2 = ai[N_CTX:].reshape(FFT_N1, FFT_N2 * BW)
    br, bi = fft_stage1(ar2, ai2, fc["m1"], fc["ct"], fc["st"])
    y = fft_stage2(br.reshape(FFT_N1, FFT_N2, BW), bi.reshape(FFT_N1, FFT_N2, BW),
                   fc["c128"], fc["s128"], fc["perm"], 1.0 / math.sqrt(SEQ * gw))
    return jnp.concatenate([y_ctx, y.reshape(SEQ, BW)], axis=0)


D_PROJ_W = 896


def _mla_proj_kernel(h_ref, w_ref, b_ref, qg_ref, kvg_ref, wq_ref, wqr_ref, wk_ref, wvt_ref, e_ref,
                     tq_c_ref, tq_s_ref, tk_ref, q_ref, k_ref, vt_ref):
    z = _dot(h_ref[...], w_ref[...]) + b_ref[...]
    cq = z[:, :Q_LORA]
    cq = cq * lax.rsqrt(jnp.mean(cq * cq, axis=-1, keepdims=True) + LN_EPS) * qg_ref[...]
    cq = cq.astype(BF16)
    cos_t = jnp.concatenate([tq_c_ref[...]] * HEADS, axis=-1)
    sin_t = jnp.concatenate([tq_s_ref[...]] * HEADS, axis=-1)
    q = (_dot(cq, wq_ref[...]) * cos_t + _dot(cq, wqr_ref[...]) * sin_t) * MLA_SCALE
    q_ref[...] = q.astype(BF16)
    ckv = z[:, Q_LORA:Q_LORA + KV_LORA]
    ckv = ckv * lax.rsqrt(jnp.mean(ckv * ckv, axis=-1, keepdims=True) + LN_EPS) * kvg_ref[...]
    ckv = ckv.astype(BF16)
    kr = (z[:, Q_LORA + KV_LORA:] * tk_ref[...]).astype(BF16)
    k_ref[...] = (_dot(ckv, wk_ref[...]) + _dot(kr, e_ref[...])).astype(BF16)
    vt = lax.dot_general(wvt_ref[...], ckv, (((1,), (1,)), ((), ())), preferred_element_type=F32)
    vt_ref[0] = vt.astype(BF16)


def mla_proj(h, w, b, qg, kvg, wq, wqr, wk, wvt, e, tq_c, tq_s, tk):
    const = lambda i: (0, 0)
    row = lambda i: (i, 0)
    return pl.pallas_call(
        _mla_proj_kernel,
        grid=(NRB,),
        in_specs=[pl.BlockSpec((RB, D), row),
                  pl.BlockSpec((D, D_PROJ_W), const),
                  pl.BlockSpec((1, D_PROJ_W), const),
                  pl.BlockSpec((1, Q_LORA), const),
                  pl.BlockSpec((1, KV_LORA), const),
                  pl.BlockSpec((Q_LORA, HEADS * HEAD_PAD), const),
                  pl.BlockSpec((Q_LORA, HEADS * HEAD_PAD), const),
                  pl.BlockSpec((KV_LORA, HEADS * HEAD_PAD), const),
                  pl.BlockSpec((HEADS * V_HEAD, KV_LORA), const),
                  pl.BlockSpec((LANE, HEADS * HEAD_PAD), const),
                  pl.BlockSpec((RB, HEAD_PAD), row),
                  pl.BlockSpec((RB, HEAD_PAD), row),
                  pl.BlockSpec((RB, LANE), row)],
        out_specs=[pl.BlockSpec((RB, HEADS * HEAD_PAD), row),
                   pl.BlockSpec((RB, HEADS * HEAD_PAD), row),
                   pl.BlockSpec((1, HEADS * V_HEAD, RB), lambda i: (i, 0, 0))],
        out_shape=[jax.ShapeDtypeStruct((T, HEADS * HEAD_PAD), BF16),
                   jax.ShapeDtypeStruct((T, HEADS * HEAD_PAD), BF16),
                   jax.ShapeDtypeStruct((T // RB, HEADS * V_HEAD, RB), BF16)],
        compiler_params=_cparams(("arbitrary",)),
        name="mla_proj",
    )(h, w, b, qg, kvg, wq, wqr, wk, wvt, e, tq_c, tq_s, tk)


KV_TILE = 256


def _attn_kernel(q_ref, k_ref, vt_ref, o_ref):
    i = pl.program_id(1)
    n_tiles = jnp.where(i == 0, N_CTX // KV_TILE, T // KV_TILE)
    q = q_ref[...]

    def body(t, carry):
        m, l, acc = carry
        start = pl.multiple_of(t * KV_TILE, KV_TILE)
        kt = k_ref[pl.ds(start, KV_TILE), :]
        s = lax.dot_general(kt, q, (((1,), (1,)), ((), ())), preferred_element_type=F32)
        m_new = jnp.maximum(m, jnp.max(s, axis=0, keepdims=True))
        p = jnp.exp(s - m_new)
        a = jnp.exp(m - m_new)
        l = a * l + jnp.sum(p, axis=0, keepdims=True)
        acc = a * acc + _dot(vt_ref[t], p.astype(BF16))
        return m_new, l, acc

    m0 = jnp.full((1, RB), -jnp.inf, F32)
    l0 = jnp.zeros((1, RB), F32)
    acc0 = jnp.zeros((V_HEAD, RB), F32)
    _, l, acc = lax.fori_loop(0, n_tiles, body, (m0, l0, acc0))
    o_ref[...] = acc / l


def attention(q, k, vt):
    return pl.pallas_call(
        _attn_kernel,
        grid=(HEADS, NRB),
        in_specs=[pl.BlockSpec((RB, HEAD_PAD), lambda h, i: (i, h)),
                  pl.BlockSpec((T, HEAD_PAD), lambda h, i: (0, h)),
                  pl.BlockSpec((T // KV_TILE, V_HEAD, KV_TILE), lambda h, i: (0, h, 0))],
        out_specs=pl.BlockSpec((V_HEAD, RB), lambda h, i: (h, i)),
        out_shape=jax.ShapeDtypeStruct((HEADS * V_HEAD, T), F32),
        compiler_params=_cparams(("arbitrary", "arbitrary")),
        name="attention",
    )(q, k, vt)


MERGE_TN = 512


def _merge_kernel(h_ref, a_ref, b_ref, c_ref, dt_ref, wg_ref, bg_ref, wb_ref, y_ref):
    h = h_ref[...]
    branches = (a_ref[...], b_ref[...], c_ref[...].astype(BF16), dt_ref[...].T.astype(BF16))
    y = jnp.zeros((RB, MERGE_TN), F32)
    for n, br in enumerate(branches):
        gate = jax.nn.sigmoid(_dot(h, wg_ref[n]) + bg_ref[n])
        y = y + gate * _dot(br, wb_ref[n])
    y_ref[...] = y.astype(BF16)


def merge(h, br_a, br_b, br_c, br_dt, wg, bg, wb):
    row = lambda j, i: (i, 0)
    return pl.pallas_call(
        _merge_kernel,
        grid=(D // MERGE_TN, NRB),
        in_specs=[pl.BlockSpec((RB, D), row),
                  pl.BlockSpec((RB, BW), row),
                  pl.BlockSpec((RB, BW), row),
                  pl.BlockSpec((RB, BW), row),
                  pl.BlockSpec((BW, RB), lambda j, i: (0, i)),
                  pl.BlockSpec((4, D, MERGE_TN), lambda j, i: (0, 0, j)),
                  pl.BlockSpec((4, 1, MERGE_TN), lambda j, i: (0, 0, j)),
                  pl.BlockSpec((4, BW, MERGE_TN), lambda j, i: (0, 0, j))],
        out_specs=pl.BlockSpec((RB, MERGE_TN), lambda j, i: (i, j)),
        out_shape=jax.ShapeDtypeStruct((T, D), BF16),
        compiler_params=_cparams(("arbitrary", "arbitrary")),
        name="merge",
    )(h, br_a, br_b, br_c, br_dt, wg, bg, wb)


ROUTER_W = 128


def _post_mix_kernel(y_ref, x_ref, mod_ref, wo_ref, g_ref, b_ref, wr_ref, br_ref,
                     x1_ref, h2_ref, ids_ref, gates_ref):
    mix = _dot(y_ref[...], wo_ref[...])
    x1 = _ln0(ALPHA * x_ref[...] + _mod_row(mod_ref, 2) * mix) * g_ref[...] + b_ref[...]
    x1_ref[...] = x1
    h2 = _ln0(x1) * (1.0 + _mod_row(mod_ref, 4)) + _mod_row(mod_ref, 3)
    _rows_to_slabs(h2, h2_ref)
    logits = _dot_exact(h2, wr_ref[...]) + br_ref[...]
    lane = lax.broadcasted_iota(jnp.int32, (RB, ROUTER_W), 1).astype(F32)
    neg = jnp.float32(-jnp.inf)
    big = jnp.float32(ROUTER_W)
    gl = jnp.where(lane < N_GROUPS, logits, neg)
    gmax = jnp.max(gl, axis=-1, keepdims=True)
    g_idx = jnp.min(jnp.where(gl == gmax, lane, big), axis=-1, keepdims=True)
    g_p = 1.0 / jnp.sum(jnp.exp(gl - gmax), axis=-1, keepdims=True)
    e_lane = lane - N_GROUPS
    in_grp = jnp.logical_and(e_lane >= g_idx * EXP_PER_GROUP, e_lane < (g_idx + 1) * EXP_PER_GROUP)
    el = jnp.where(in_grp, logits, neg)
    v1 = jnp.max(el, axis=-1, keepdims=True)
    i1 = jnp.min(jnp.where(el == v1, lane, big), axis=-1, keepdims=True)
    el2 = jnp.where(lane == i1, neg, el)
    v2 = jnp.max(el2, axis=-1, keepdims=True)
    i2 = jnp.min(jnp.where(el2 == v2, lane, big), axis=-1, keepdims=True)
    e21 = jnp.exp(v2 - v1)
    p1 = 1.0 / (1.0 + e21)
    p2 = e21 * p1
    ids_ref[...] = jnp.where(lane == 0, i1 - N_GROUPS, jnp.where(lane == 1, i2 - N_GROUPS, 0.0)).astype(jnp.int32)
    gates_ref[...] = jnp.where(lane == 0, p1 * g_p, jnp.where(lane == 1, p2 * g_p, 0.0))


def post_mix(y, x, mod, wo, ln_g, ln_b, wr, br):
    row = lambda i: (i, 0)
    const = lambda i: (0, 0)
    return pl.pallas_call(
        _post_mix_kernel,
        grid=(NRB,),
        in_specs=[pl.BlockSpec((RB, D), row),
                  pl.BlockSpec((RB, D), row),
                  pl.BlockSpec((8, 6 * D), const),
                  pl.BlockSpec((D, D), const),
                  pl.BlockSpec((1, D), const),
                  pl.BlockSpec((1, D), const),
                  pl.BlockSpec((D, ROUTER_W), const),
                  pl.BlockSpec((1, ROUTER_W), const)],
        out_specs=[pl.BlockSpec((RB, D), row),
                   pl.BlockSpec((RB * ROW_SUB, LANE), row),
                   pl.BlockSpec((RB, ROUTER_W), row),
                   pl.BlockSpec((RB, ROUTER_W), row)],
        out_shape=[jax.ShapeDtypeStruct((T, D), F32),
                   jax.ShapeDtypeStruct((T * ROW_SUB, LANE), F32),
                   jax.ShapeDtypeStruct((T, ROUTER_W), jnp.int32),
                   jax.ShapeDtypeStruct((T, ROUTER_W), F32)],
        compiler_params=_cparams(("arbitrary",)),
        name="post_mix",
    )(y, x, mod, wo, ln_g, ln_b, wr, br)


N_ROWS = -(-(T * TOP_K + N_EXPERTS * (MOE_BLOCK - 1)) // MOE_BLOCK) * MOE_BLOCK
N_BLK = N_ROWS // MOE_BLOCK


def _row_gather(src_hbm, idx_ref, base, n, dst_ref, sem):
    def issue(r, _):
        src_row = pl.multiple_of(idx_ref[base + r] * ROW_SUB, ROW_SUB)
        dst_row = pl.multiple_of(r * ROW_SUB, ROW_SUB)
        pltpu.make_async_copy(src_hbm.at[pl.ds(src_row, ROW_SUB), :],
                              dst_ref.at[pl.ds(dst_row, ROW_SUB), :], sem).start()
        return 0
    lax.fori_loop(0, n, issue, 0)


def _row_gather_wait(src_hbm, n, dst_ref, sem):
    pltpu.make_async_copy(src_hbm.at[pl.ds(0, n * ROW_SUB), :], dst_ref, sem).wait()


def _rows_from_slabs(buf_ref, start, n, stride):
    return jnp.concatenate([buf_ref[pl.ds(start + c, n, stride=stride), :] for c in range(ROW_SUB)], axis=-1)


def _rows_to_slabs(val, out_ref):
    n = val.shape[0]
    for c in range(ROW_SUB):
        out_ref[pl.ds(c, n, stride=ROW_SUB), :] = val[:, c * LANE:(c + 1) * LANE]


def _expert_kernel(blk_exp_ref, row_tok_ref, h_hbm, w1_ref, w3_ref, w2_ref, y_ref, xbuf, sems):
    b = pl.program_id(0)
    slot = b % 2

    @pl.when(b == 0)
    def _():
        _row_gather(h_hbm, row_tok_ref, 0, MOE_BLOCK, xbuf.at[0], sems.at[0])

    @pl.when(b + 1 < N_BLK)
    def _():
        _row_gather(h_hbm, row_tok_ref, (b + 1) * MOE_BLOCK, MOE_BLOCK, xbuf.at[1 - slot], sems.at[1 - slot])

    _row_gather_wait(h_hbm, MOE_BLOCK, xbuf.at[slot], sems.at[slot])
    xb = _rows_from_slabs(xbuf.at[slot], 0, MOE_BLOCK, ROW_SUB).astype(BF16)
    a = _dot(xb, w1_ref[0])
    g = _dot(xb, w3_ref[0])
    hmid = (a * jax.nn.sigmoid(a) * g).astype(BF16)
    _rows_to_slabs(_dot(hmid, w2_ref[0]), y_ref)


def experts(blk_exp, row_tok, h2_slabs, w1, w3, w2):
    grid_spec = pltpu.PrefetchScalarGridSpec(
        num_scalar_prefetch=2,
        grid=(N_BLK,),
        in_specs=[pl.BlockSpec(memory_space=pl.ANY),
                  pl.BlockSpec((1, D, D_EXPERT), lambda b, be, rt: (be[b], 0, 0)),
                  pl.BlockSpec((1, D, D_EXPERT), lambda b, be, rt: (be[b], 0, 0)),
                  pl.BlockSpec((1, D_EXPERT, D), lambda b, be, rt: (be[b], 0, 0))],
        out_specs=pl.BlockSpec((MOE_BLOCK * ROW_SUB, LANE), lambda b, be, rt: (b, 0)),
        scratch_shapes=[pltpu.VMEM((2, MOE_BLOCK * ROW_SUB, LANE), F32),
                        pltpu.SemaphoreType.DMA((2,))],
    )
    return pl.pallas_call(
        _expert_kernel,
        grid_spec=grid_spec,
        out_shape=jax.ShapeDtypeStruct((N_ROWS * ROW_SUB, LANE), F32),
        compiler_params=_cparams(("arbitrary",)),
        name="experts",
    )(blk_exp, row_tok, h2_slabs, w1, w3, w2)


def _combine_kernel(pos_ref, y_hbm, gates_ref, x1_ref, mod_ref, g_ref, b_ref, nmod_ref,
                    x2_ref, hn_ref, ybuf, sem):
    i = pl.program_id(0)
    _row_gather(y_hbm, pos_ref, i * (RB * TOP_K), RB * TOP_K, ybuf, sem)
    _row_gather_wait(y_hbm, RB * TOP_K, ybuf, sem)
    y0 = _rows_from_slabs(ybuf, 0, RB, TOP_K * ROW_SUB)
    y1 = _rows_from_slabs(ybuf, ROW_SUB, RB, TOP_K * ROW_SUB)
    gates = gates_ref[...]
    f = gates[:, 0:1] * y0 + gates[:, 1:2] * y1
    x2 = _ln0(ALPHA * x1_ref[...] + _mod_row(mod_ref, 5) * f) * g_ref[...] + b_ref[...]
    x2_ref[...] = x2
    hn = _ln0(x2) * (1.0 + _mod_row(nmod_ref, 1)) + _mod_row(nmod_ref, 0)
    hn_ref[...] = hn.astype(BF16)


def combine(pos, y_slabs, gates, x1, mod, ln_g, ln_b, next_mod):
    row = lambda i, p: (i, 0)
    const = lambda i, p: (0, 0)
    grid_spec = pltpu.PrefetchScalarGridSpec(
        num_scalar_prefetch=1,
        grid=(NRB,),
        in_specs=[pl.BlockSpec(memory_space=pl.ANY),
                  pl.BlockSpec((RB, ROUTER_W), row),
                  pl.BlockSpec((RB, D), row),
                  pl.BlockSpec((8, 6 * D), const),
                  pl.BlockSpec((1, D), const),
                  pl.BlockSpec((1, D), const),
                  pl.BlockSpec((8, 6 * D), const)],
        out_specs=[pl.BlockSpec((RB, D), row), pl.BlockSpec((RB, D), row)],
        scratch_shapes=[pltpu.VMEM((RB * TOP_K * ROW_SUB, LANE), F32),
                        pltpu.SemaphoreType.DMA(())],
    )
    return pl.pallas_call(
        _combine_kernel,
        grid_spec=grid_spec,
        out_shape=[jax.ShapeDtypeStruct((T, D), F32), jax.ShapeDtypeStruct((T, D), BF16)],
        compiler_params=_cparams(("arbitrary",)),
        name="combine",
    )(pos, y_slabs, gates, x1, mod, ln_g, ln_b, next_mod)


def _routing_tables(ids):
    expert = ids[:, :TOP_K].reshape(-1)
    n = expert.shape[0]
    onehot = (expert[:, None] == jnp.arange(N_EXPERTS, dtype=jnp.int32)[None, :]).astype(jnp.int32)
    csum = jnp.cumsum(onehot, axis=0)
    counts = csum[-1]
    rank = jnp.sum(onehot * csum, axis=1) - 1
    padded = (counts + MOE_BLOCK - 1) // MOE_BLOCK * MOE_BLOCK
    pends = jnp.cumsum(padded)
    pstarts = pends - padded
    pos = (jnp.sum(onehot * pstarts[None, :], axis=1) + rank).astype(jnp.int32)
    tok = jnp.arange(n, dtype=jnp.int32) // TOP_K
    row_tok = jnp.zeros((N_ROWS,), jnp.int32).at[pos].set(tok)
    blk_start = jnp.arange(N_BLK, dtype=jnp.int32) * MOE_BLOCK
    blk_exp = jnp.minimum(jnp.sum(pends[None, :] <= blk_start[:, None], axis=1), N_EXPERTS - 1).astype(jnp.int32)
    return blk_exp, row_tok, pos


def _rope_tables():
    rows = SEQ // GRID_W
    row = jnp.repeat(jnp.arange(rows, dtype=F32), GRID_W)
    col = jnp.tile(jnp.arange(GRID_W, dtype=F32), rows)
    half = QK_ROPE // 2
    inv = ROPE_THETA ** (-jnp.arange(0, half, 2, dtype=F32) / half)
    ar = row[:, None] * inv
    ac = col[:, None] * inv
    ang = jnp.concatenate([ar, ar, ac, ac], -1)
    cos = jnp.concatenate([jnp.ones((N_CTX, QK_ROPE), F32), jnp.cos(ang)], 0)
    sin = jnp.concatenate([jnp.zeros((N_CTX, QK_ROPE), F32), jnp.sin(ang)], 0)
    one = jnp.ones((T, QK_NOPE), F32)
    zero_n = jnp.zeros((T, QK_NOPE), F32)
    zero_p = jnp.zeros((T, HEAD_PAD - QK_NOPE - QK_ROPE), F32)
    tq_c = jnp.concatenate([one, cos, zero_p], -1)
    tq_s = jnp.concatenate([zero_n, sin, zero_p], -1)
    tk = jnp.concatenate([cos, sin, jnp.zeros((T, LANE - 2 * QK_ROPE), F32)], -1)
    return tq_c, tq_s, tk


_ROT_SRC = np.concatenate([np.arange(8, 16), np.arange(0, 8), np.arange(24, 32), np.arange(16, 24)])
_ROT_SIGN = np.concatenate([-np.ones(8), np.ones(8), -np.ones(8), np.ones(8)]).astype(np.float32)


def _rot_cols(w):
    return w[..., _ROT_SRC] * _ROT_SIGN


def _layer_weights(l, w_in, b_in, d_w_uq, d_w_uk, d_w_uv, w_branch, w_o, r_w_grp, r_b_grp, r_w_exp, r_b_exp,
                   a_b_s, b_w_dw):
    wi, bi = w_in[l], b_in[l]
    p = {}
    p["wa"] = wi[:, COL_A:COL_B].astype(BF16)
    p["ba"] = bi[None, COL_A:COL_B]
    p["wb"] = wi[:, COL_B:COL_C].astype(BF16)
    p["bb"] = bi[None, COL_B:COL_C]
    p["wc"] = wi[:, COL_C:COL_D].astype(BF16)
    p["bc"] = bi[None, COL_C:COL_D]
    w_kr, b_kr = wi[:, COL_KR:COL_G], bi[COL_KR:COL_G]
    padw = D_PROJ_W - (COL_G - COL_D) - QK_ROPE
    p["wd"] = jnp.concatenate([wi[:, COL_D:COL_G], _rot_cols(w_kr), jnp.zeros((D, padw), F32)], -1).astype(BF16)
    p["bd"] = jnp.concatenate([bi[COL_D:COL_G], _rot_cols(b_kr), jnp.zeros((padw,), F32)])[None]
    wq = d_w_uq[l].reshape(Q_LORA, HEADS, QK_NOPE + QK_ROPE)
    zpad = jnp.zeros((Q_LORA, HEADS, HEAD_PAD - QK_NOPE - QK_ROPE), F32)
    p["wq"] = jnp.concatenate([wq, zpad], -1).reshape(Q_LORA, HEADS * HEAD_PAD).astype(BF16)
    p["wqr"] = jnp.concatenate([jnp.zeros((Q_LORA, HEADS, QK_NOPE), F32), _rot_cols(wq[..., QK_NOPE:]), zpad],
                               -1).reshape(Q_LORA, HEADS * HEAD_PAD).astype(BF16)
    wk = d_w_uk[l].reshape(KV_LORA, HEADS, QK_NOPE)
    p["wk"] = jnp.concatenate([wk, jnp.zeros((KV_LORA, HEADS, HEAD_PAD - QK_NOPE), F32)],
                              -1).reshape(KV_LORA, HEADS * HEAD_PAD).astype(BF16)
    p["wvt"] = d_w_uv[l].T.astype(BF16)
    e = np.zeros((LANE, HEADS, HEAD_PAD), np.float32)
    for j in range(QK_ROPE):
        e[j, :, QK_NOPE + j] = 1.0
        e[QK_ROPE + j, :, QK_NOPE + j] = 1.0
    p["e"] = jnp.asarray(e.reshape(LANE, HEADS * HEAD_PAD), dtype=BF16)
    p["wg"] = wi[:, COL_G:].reshape(D, 4, D).transpose(1, 0, 2).astype(BF16)
    p["bg"] = bi[COL_G:].reshape(4, 1, D)
    p["wbr"] = w_branch[l].astype(BF16)
    p["wo"] = w_o[l].astype(BF16)
    padr = ROUTER_W - N_GROUPS - N_EXPERTS
    p["wr"] = jnp.concatenate([r_w_grp[l], r_w_exp[l], jnp.zeros((D, padr), F32)], -1)
    p["br"] = jnp.concatenate([r_b_grp[l], r_b_exp[l], jnp.zeros((padr,), F32)])[None]
    p["bs"] = jnp.broadcast_to(a_b_s[l][:, :, None], (A_GROUPS, CHUNK, CHUNK))
    p["wdw"] = jnp.concatenate([b_w_dw[l], jnp.zeros((1, BW), F32)], 0)
    return p


def kernel(x, c, ctx, c_ctx, w_ada, b_ada, w_in, b_in, a_ln_g, a_ln_b, a_w_s, a_b_s, b_w_dw, b_b_dw, b_ln_g,
           b_ln_b, d_q_g, d_w_uq, d_kv_g, d_w_uk, d_w_uv, w_branch, w_o, ln1_g, ln1_b, r_w_grp, r_b_grp,
           r_w_exp, r_b_exp, e_w1, e_w3, e_w2, ln2_g, ln2_b):
    assert x.shape == (1, SEQ, D) and ctx.shape == (1, N_CTX, D)
    xs = jnp.concatenate([ctx[0], x[0]], axis=0)
    cvec = jnp.concatenate([c, c_ctx[None], jnp.zeros((6, D), F32)], axis=0)
    mods = ada_mod(cvec, w_ada, b_ada)
    tq_c, tq_s, tk = _rope_tables()
    fc = _fourier_consts()

    h = modln(xs, mods[0])
    for l in range(DEPTH):
        p = _layer_weights(l, w_in, b_in, d_w_uq, d_w_uk, d_w_uv, w_branch, w_o, r_w_grp, r_b_grp, r_w_exp,
                           r_b_exp, a_b_s, b_w_dw)
        mod = mods[l]
        br_a = mix_a(h, p["wa"], p["ba"], a_ln_g[l][None], a_ln_b[l][None], a_w_s[l].astype(BF16), p["bs"])
        br_b = conv_ln(glu(h, p["wb"], p["bb"]), p["wdw"], b_b_dw[l][None], b_ln_g[l][None], b_ln_b[l][None])
        br_c = fourier_mix(h, p["wc"], p["bc"], fc)
        q, k, vt = mla_proj(h, p["wd"], p["bd"], d_q_g[l][None], d_kv_g[l][None], p["wq"], p["wqr"], p["wk"],
                            p["wvt"], p["e"], tq_c, tq_s, tk)
        br_dt = attention(q, k, vt)
        y = merge(h, br_a, br_b, br_c, br_dt, p["wg"], p["bg"], p["wbr"])
        x1, h2, ids, gates = post_mix(y, xs, mod, p["wo"], ln1_g[l][None], ln1_b[l][None], p["wr"], p["br"])
        blk_exp, row_tok, pos = _routing_tables(ids)
        ys = experts(blk_exp, row_tok, h2,
                     e_w1[l].astype(BF16), e_w3[l].astype(BF16), e_w2[l].astype(BF16))
        xs, h = combine(pos, ys, gates, x1, mod, ln2_g[l][None], ln2_b[l][None],
                        mods[min(l + 1, DEPTH - 1)])
    return xs[N_CTX:][None]
```

```python
import functools
import math

import numpy as np
import jax
import jax.numpy as jnp
from jax import lax
from jax.experimental import pallas as pl
from jax.experimental.pallas import tpu as pltpu

F32 = jnp.float32
BF16 = jnp.bfloat16

D = 2048
SEQ = 8192
N_CTX = 256
T = N_CTX + SEQ
DEPTH = 2
GRID_W = 64
BW = 512
CHUNK = 128
A_GROUPS = 4
CONV_W = 31
F_GROUPS = 4
QK_NOPE = 64
QK_ROPE = 32
V_HEAD = 64
HEADS = 8
Q_LORA = 512
KV_LORA = 256
ROPE_THETA = 10000.0
MLA_SCALE = (QK_NOPE + QK_ROPE) ** -0.5
N_GROUPS = 4
EXP_PER_GROUP = 8
N_EXPERTS = 32
TOP_K = 2
D_EXPERT = 1024
MOE_BLOCK = 128
COL_A = 0
COL_B = 1024
COL_C = 2048
COL_D = 2560
COL_KV = 3072
COL_KR = 3328
COL_G = 3360
ALPHA = (2 * DEPTH) ** 0.25
LN_EPS = 1e-6

LANE = 128
HEAD_PAD = 128
RB = 256
NRB = T // RB
FFT_N1 = 64
FFT_N2 = 128
ROW_SUB = D // LANE
VMEM_LIMIT = 56 * 1024 * 1024


def _cparams(sem):
    return pltpu.CompilerParams(dimension_semantics=sem, vmem_limit_bytes=VMEM_LIMIT)


def _dot(a, b):
    return jnp.dot(a, b, preferred_element_type=F32)


def _dot_exact(a, b):
    return jnp.dot(a, b, preferred_element_type=F32, precision=lax.Precision.HIGHEST)


def _ln0(x):
    mu = jnp.mean(x, axis=-1, keepdims=True)
    xc = x - mu
    var = jnp.mean(xc * xc, axis=-1, keepdims=True)
    return xc * lax.rsqrt(var + LN_EPS)


def _mod_row(mod_ref, q):
    lat = mod_ref[0:1, q * D:(q + 1) * D]
    ctx = mod_ref[1:2, q * D:(q + 1) * D]
    return jnp.where(pl.program_id(0) == 0, ctx, lat)


def _ada_kernel(c_ref, w_ref, b_ref, o_ref):
    s = c_ref[...]
    s = s * jax.nn.sigmoid(s)
    o_ref[0] = _dot_exact(s, w_ref[0]) + b_ref[0]


def ada_mod(cvec, w_ada, b_ada):
    tn = 1536
    return pl.pallas_call(
        _ada_kernel,
        grid=(DEPTH, 6 * D // tn),
        in_specs=[pl.BlockSpec((8, D), lambda l, j: (0, 0)),
                  pl.BlockSpec((1, D, tn), lambda l, j: (l, 0, j)),
                  pl.BlockSpec((1, 1, tn), lambda l, j: (l, 0, j))],
        out_specs=pl.BlockSpec((1, 8, tn), lambda l, j: (l, 0, j)),
        out_shape=jax.ShapeDtypeStruct((DEPTH, 8, 6 * D), F32),
        compiler_params=_cparams(("arbitrary", "arbitrary")),
        name="ada_mod",
    )(cvec, w_ada, b_ada.reshape(DEPTH, 1, 6 * D))


def _modln_kernel(x_ref, mod_ref, h_ref):
    h = _ln0(x_ref[...]) * (1.0 + _mod_row(mod_ref, 1)) + _mod_row(mod_ref, 0)
    h_ref[...] = h.astype(BF16)


def modln(x, mod):
    return pl.pallas_call(
        _modln_kernel,
        grid=(NRB,),
        in_specs=[pl.BlockSpec((RB, D), lambda i: (i, 0)),
                  pl.BlockSpec((8, 6 * D), lambda i: (0, 0))],
        out_specs=pl.BlockSpec((RB, D), lambda i: (i, 0)),
        out_shape=jax.ShapeDtypeStruct((T, D), BF16),
        compiler_params=_cparams(("arbitrary",)),
        name="modln",
    )(x, mod)


def _mix_a_kernel(h_ref, w_ref, b_ref, g_ref, bb_ref, ws_ref, bs_ref, o_ref):
    uv = jax.nn.gelu(_dot(h_ref[...], w_ref[...]) + b_ref[...])
    u = uv[:, :BW]
    v = _ln0(uv[:, BW:]) * g_ref[...] + bb_ref[...]
    v = v.astype(BF16)
    gw = BW // A_GROUPS
    for ch in range(RB // CHUNK):
        rows = slice(ch * CHUNK, (ch + 1) * CHUNK)
        parts = [_dot(ws_ref[g], v[rows, g * gw:(g + 1) * gw]) + bs_ref[g] for g in range(A_GROUPS)]
        mixed = jnp.concatenate(parts, axis=-1)
        o_ref[rows, :] = (u[rows, :] * mixed).astype(BF16)


def mix_a(h, w, b, ln_g, ln_b, ws, bs):
    return pl.pallas_call(
        _mix_a_kernel,
        grid=(NRB,),
        in_specs=[pl.BlockSpec((RB, D), lambda i: (i, 0)),
                  pl.BlockSpec((D, 2 * BW), lambda i: (0, 0)),
                  pl.BlockSpec((1, 2 * BW), lambda i: (0, 0)),
                  pl.BlockSpec((1, BW), lambda i: (0, 0)),
                  pl.BlockSpec((1, BW), lambda i: (0, 0)),
                  pl.BlockSpec((A_GROUPS, CHUNK, CHUNK), lambda i: (0, 0, 0)),
                  pl.BlockSpec((A_GROUPS, CHUNK, CHUNK), lambda i: (0, 0, 0))],
        out_specs=pl.BlockSpec((RB, BW), lambda i: (i, 0)),
        out_shape=jax.ShapeDtypeStruct((T, BW), BF16),
        compiler_params=_cparams(("arbitrary",)),
        name="mix_a",
    )(h, w, b, ln_g, ln_b, ws, bs)


def _glu_kernel(h_ref, w_ref, b_ref, o_ref):
    ab = _dot(h_ref[...], w_ref[...]) + b_ref[...]
    o_ref[...] = ab[:, :BW] * jax.nn.sigmoid(ab[:, BW:])


def glu(h, w, b):
    return pl.pallas_call(
        _glu_kernel,
        grid=(NRB,),
        in_specs=[pl.BlockSpec((RB, D), lambda i: (i, 0)),
                  pl.BlockSpec((D, 2 * BW), lambda i: (0, 0)),
                  pl.BlockSpec((1, 2 * BW), lambda i: (0, 0))],
        out_specs=pl.BlockSpec((RB, BW), lambda i: (i, 0)),
        out_shape=jax.ShapeDtypeStruct((T, BW), F32),
        compiler_params=_cparams(("arbitrary",)),
        name="glu",
    )(h, w, b)


CONV_HALO = 16


def _conv_kernel(prev_ref, cur_ref, next_ref, w_ref, b_ref, g_ref, bb_ref, o_ref, ext_ref):
    i = pl.program_id(0)
    has_prev = i >= 2
    has_next = jnp.logical_and(i >= 1, i < NRB - 1)
    ext_ref[0:CONV_HALO, :] = jnp.where(has_prev, prev_ref[RB - CONV_HALO:RB, :], 0.0)
    ext_ref[CONV_HALO:CONV_HALO + RB, :] = cur_ref[...]
    ext_ref[CONV_HALO + RB:2 * CONV_HALO + RB, :] = jnp.where(has_next, next_ref[0:CONV_HALO, :], 0.0)
    off = CONV_HALO - CONV_W // 2
    acc = jnp.zeros((RB, BW), F32)
    for k in range(CONV_W):
        acc = acc + ext_ref[off + k:off + k + RB, :] * w_ref[k:k + 1, :]
    y = _ln0(acc + b_ref[...]) * g_ref[...] + bb_ref[...]
    o_ref[...] = (y * jax.nn.sigmoid(y)).astype(BF16)


def conv_ln(y, w_dw, b_dw, ln_g, ln_b):
    return pl.pallas_call(
        _conv_kernel,
        grid=(NRB,),
        in_specs=[pl.BlockSpec((RB, BW), lambda i: (jnp.maximum(i - 1, 0), 0)),
                  pl.BlockSpec((RB, BW), lambda i: (i, 0)),
                  pl.BlockSpec((RB, BW), lambda i: (jnp.minimum(i + 1, NRB - 1), 0)),
                  pl.BlockSpec((CONV_W + 1, BW), lambda i: (0, 0)),
                  pl.BlockSpec((1, BW), lambda i: (0, 0)),
                  pl.BlockSpec((1, BW), lambda i: (0, 0)),
                  pl.BlockSpec((1, BW), lambda i: (0, 0))],
        out_specs=pl.BlockSpec((RB, BW), lambda i: (i, 0)),
        out_shape=jax.ShapeDtypeStruct((T, BW), BF16),
        scratch_shapes=[pltpu.VMEM((RB + 2 * CONV_HALO, BW), F32)],
        compiler_params=_cparams(("arbitrary",)),
        name="conv_ln",
    )(y, y, y, w_dw, b_dw, ln_g, ln_b)


def _dft_tables(n):
    k = np.arange(n, dtype=np.int64)
    ang = 2.0 * np.pi * ((k[:, None] * k[None, :]) % n).astype(np.float64) / n
    return np.cos(ang), np.sin(ang)


def _fproj_kernel(h_ref, w_ref, b_ref, cs_ref, ar_ref, ai_ref):
    z = _dot(h_ref[...], w_ref[...]) + b_ref[...]
    a = _dot_exact(z, cs_ref[...])
    ar_ref[...] = a[:, :BW]
    ai_ref[...] = a[:, BW:]


def fourier_proj(h, w, b, cs):
    return pl.pallas_call(
        _fproj_kernel,
        grid=(NRB,),
        in_specs=[pl.BlockSpec((RB, D), lambda i: (i, 0)),
                  pl.BlockSpec((D, BW), lambda i: (0, 0)),
                  pl.BlockSpec((1, BW), lambda i: (0, 0)),
                  pl.BlockSpec((BW, 2 * BW), lambda i: (0, 0))],
        out_specs=[pl.BlockSpec((RB, BW), lambda i: (i, 0)),
                   pl.BlockSpec((RB, BW), lambda i: (i, 0))],
        out_shape=[jax.ShapeDtypeStruct((T, BW), F32), jax.ShapeDtypeStruct((T, BW), F32)],
        compiler_params=_cparams(("arbitrary",)),
        name="fourier_proj",
    )(h, w, b, cs)


FFT_TILE_N2 = 4


def _fft1_kernel(ar_ref, ai_ref, m_ref, ct_ref, st_ref, br_ref, bi_ref):
    x = jnp.concatenate([ar_ref[...], ai_ref[...]], axis=0)
    b = _dot_exact(m_ref[...], x)
    b_re, b_im = b[:FFT_N1], b[FFT_N1:]
    reps = BW // LANE
    for j in range(FFT_TILE_N2):
        ct = jnp.concatenate([ct_ref[j]] * reps, axis=-1)
        st = jnp.concatenate([st_ref[j]] * reps, axis=-1)
        lanes = slice(j * BW, (j + 1) * BW)
        br_ref[:, lanes] = b_re[:, lanes] * ct + b_im[:, lanes] * st
        bi_ref[:, lanes] = b_im[:, lanes] * ct - b_re[:, lanes] * st


def fft_stage1(ar2d, ai2d, m1, ct, st):
    tn = FFT_TILE_N2 * BW
    return pl.pallas_call(
        _fft1_kernel,
        grid=(FFT_N2 // FFT_TILE_N2,),
        in_specs=[pl.BlockSpec((FFT_N1, tn), lambda j: (0, j)),
                  pl.BlockSpec((FFT_N1, tn), lambda j: (0, j)),
                  pl.BlockSpec((2 * FFT_N1, 2 * FFT_N1), lambda j: (0, 0)),
                  pl.BlockSpec((FFT_TILE_N2, FFT_N1, LANE), lambda j: (j, 0, 0)),
                  pl.BlockSpec((FFT_TILE_N2, FFT_N1, LANE), lambda j: (j, 0, 0))],
        out_specs=[pl.BlockSpec((FFT_N1, tn), lambda j: (0, j)),
                   pl.BlockSpec((FFT_N1, tn), lambda j: (0, j))],
        out_shape=[jax.ShapeDtypeStruct((FFT_N1, FFT_N2 * BW), F32)] * 2,
        compiler_params=_cparams(("arbitrary",)),
        name="fft_stage1",
    )(ar2d, ai2d, m1, ct, st)


FFT_TILE_K1 = 8


def _fft2_kernel(br_ref, bi_ref, c_ref, s_ref, p_ref, o_ref, *, scale):
    ys = [(_dot_exact(c_ref[...], br_ref[a]) + _dot_exact(s_ref[...], bi_ref[a])) * scale
          for a in range(FFT_TILE_K1)]
    y = jnp.concatenate(ys, axis=0).astype(BF16)
    y = _dot(p_ref[...], y)
    o_ref[...] = y.reshape(FFT_N2, FFT_TILE_K1, BW)


def fft_stage2(br3d, bi3d, c128, s128, perm, scale):
    return pl.pallas_call(
        functools.partial(_fft2_kernel, scale=scale),
        grid=(FFT_N1 // FFT_TILE_K1,),
        in_specs=[pl.BlockSpec((FFT_TILE_K1, FFT_N2, BW), lambda a: (a, 0, 0)),
                  pl.BlockSpec((FFT_TILE_K1, FFT_N2, BW), lambda a: (a, 0, 0)),
                  pl.BlockSpec((FFT_N2, FFT_N2), lambda a: (0, 0)),
                  pl.BlockSpec((FFT_N2, FFT_N2), lambda a: (0, 0)),
                  pl.BlockSpec((FFT_TILE_K1 * FFT_N2, FFT_TILE_K1 * FFT_N2), lambda a: (0, 0))],
        out_specs=pl.BlockSpec((FFT_N2, FFT_TILE_K1, BW), lambda a: (0, a, 0)),
        out_shape=jax.ShapeDtypeStruct((FFT_N2, FFT_N1, BW), F32),
        compiler_params=_cparams(("arbitrary",)),
        name="fft_stage2",
    )(br3d, bi3d, c128, s128, perm)


def _dft_ctx_kernel(ar_ref, ai_ref, c_ref, s_ref, o_ref, *, scale):
    o_ref[...] = (_dot_exact(c_ref[...], ar_ref[...]) + _dot_exact(s_ref[...], ai_ref[...])) * scale


def dft_ctx(ar, ai, c, s, scale):
    return pl.pallas_call(
        functools.partial(_dft_ctx_kernel, scale=scale),
        grid=(1,),
        in_specs=[pl.BlockSpec((N_CTX, BW), lambda i: (0, 0)),
                  pl.BlockSpec((N_CTX, BW), lambda i: (0, 0)),
                  pl.BlockSpec((N_CTX, N_CTX), lambda i: (0, 0)),
                  pl.BlockSpec((N_CTX, N_CTX), lambda i: (0, 0))],
        out_specs=pl.BlockSpec((N_CTX, BW), lambda i: (0, 0)),
        out_shape=jax.ShapeDtypeStruct((N_CTX, BW), F32),
        compiler_params=_cparams(("arbitrary",)),
        name="dft_ctx",
    )(ar, ai, c, s)


def _fourier_consts():
    c64, s64 = _dft_tables(FFT_N1)
    m1 = np.block([[c64, s64], [-s64, c64]]).astype(np.float32)
    k1 = np.arange(FFT_N1, dtype=np.float64)
    n2 = np.arange(FFT_N2, dtype=np.float64)
    phi = 2.0 * np.pi * n2[:, None] * k1[None, :] / SEQ
    ct = np.repeat(np.cos(phi)[:, :, None], LANE, axis=2).astype(np.float32)
    st = np.repeat(np.sin(phi)[:, :, None], LANE, axis=2).astype(np.float32)
    c128, s128 = _dft_tables(FFT_N2)
    gw = BW // F_GROUPS
    cg, sg = _dft_tables(gw)
    cs = np.zeros((BW, 2 * BW), np.float32)
    for g in range(F_GROUPS):
        sl = slice(g * gw, (g + 1) * gw)
        cs[sl, sl] = cg
        cs[sl, BW + g * gw:BW + (g + 1) * gw] = -sg
    rows = np.arange(FFT_TILE_K1 * FFT_N2)
    perm = np.zeros((rows.size, rows.size), np.float32)
    perm[rows, (rows % FFT_TILE_K1) * FFT_N2 + rows // FFT_TILE_K1] = 1.0
    cc, sc = _dft_tables(N_CTX)
    return dict(m1=jnp.asarray(m1), ct=jnp.asarray(ct), st=jnp.asarray(st),
                c128=jnp.asarray(c128.astype(np.float32)), s128=jnp.asarray(s128.astype(np.float32)),
                cs=jnp.asarray(cs), perm=jnp.asarray(perm, dtype=BF16),
                cc=jnp.asarray(cc.astype(np.float32)), sc=jnp.asarray(sc.astype(np.float32)))


def fourier_mix(h, w, b, fc):
    gw = BW // F_GROUPS
    ar, ai = fourier_proj(h, w, b, fc["cs"])
    y_ctx = dft_ctx(ar, ai, fc["cc"], fc["sc"], 1.0 / math.sqrt(N_CTX * gw))
    ar2 = ar[N_CTX:].reshape(FFT_N1, FFT_N2 * BW)
    ai2 = ai[N_CTX:].reshape(FFT_N1, FFT_N2 * BW)
    br, bi = fft_stage1(ar2, ai2, fc["m1"], fc["ct"], fc["st"])
    y = fft_stage2(br.reshape(FFT_N1, FFT_N2, BW), bi.reshape(FFT_N1, FFT_N2, BW),
                   fc["c128"], fc["s128"], fc["perm"], 1.0 / math.sqrt(SEQ * gw))
    return jnp.concatenate([y_ctx, y.reshape(SEQ, BW)], axis=0)


D_PROJ_W = 896
VT_ROWS = 80
LOG2E = math.log2(math.e)


def _mla_proj_kernel(h_ref, w_ref, b_ref, qg_ref, kvg_ref, wq_ref, wqr_ref, wk_ref, wvt_ref, e_ref,
                     tq_c_ref, tq_s_ref, tk_ref, q_ref, k_ref, vt_ref):
    z = _dot(h_ref[...], w_ref[...]) + b_ref[...]
    cq = z[:, :Q_LORA]
    cq = cq * lax.rsqrt(jnp.mean(cq * cq, axis=-1, keepdims=True) + LN_EPS) * qg_ref[...]
    cq = cq.astype(BF16)
    cos_t = jnp.concatenate([tq_c_ref[...]] * HEADS, axis=-1)
    sin_t = jnp.concatenate([tq_s_ref[...]] * HEADS, axis=-1)
    q = (_dot(cq, wq_ref[...]) * cos_t + _dot(cq, wqr_ref[...]) * sin_t) * (MLA_SCALE * LOG2E)
    q_ref[...] = q.astype(BF16)
    ckv = z[:, Q_LORA:Q_LORA + KV_LORA]
    ckv = ckv * lax.rsqrt(jnp.mean(ckv * ckv, axis=-1, keepdims=True) + LN_EPS) * kvg_ref[...]
    ckv = ckv.astype(BF16)
    kr = (z[:, Q_LORA + KV_LORA:] * tk_ref[...]).astype(BF16)
    k_ref[...] = (_dot(ckv, wk_ref[...]) + _dot(kr, e_ref[...])).astype(BF16)
    vt = lax.dot_general(wvt_ref[...], ckv, (((1,), (1,)), ((), ())), preferred_element_type=F32)
    ones_rows = (lax.broadcasted_iota(jnp.int32, (VT_ROWS - V_HEAD, RB), 0) == 0).astype(BF16)
    for hd in range(HEADS):
        vt_ref[0, hd * VT_ROWS:hd * VT_ROWS + V_HEAD, :] = vt[hd * V_HEAD:(hd + 1) * V_HEAD, :].astype(BF16)
        vt_ref[0, hd * VT_ROWS + V_HEAD:(hd + 1) * VT_ROWS, :] = ones_rows


def mla_proj(h, w, b, qg, kvg, wq, wqr, wk, wvt, e, tq_c, tq_s, tk):
    const = lambda i: (0, 0)
    row = lambda i: (i, 0)
    return pl.pallas_call(
        _mla_proj_kernel,
        grid=(NRB,),
        in_specs=[pl.BlockSpec((RB, D), row),
                  pl.BlockSpec((D, D_PROJ_W), const),
                  pl.BlockSpec((1, D_PROJ_W), const),
                  pl.BlockSpec((1, Q_LORA), const),
                  pl.BlockSpec((1, KV_LORA), const),
                  pl.BlockSpec((Q_LORA, HEADS * HEAD_PAD), const),
                  pl.BlockSpec((Q_LORA, HEADS * HEAD_PAD), const),
                  pl.BlockSpec((KV_LORA, HEADS * HEAD_PAD), const),
                  pl.BlockSpec((HEADS * V_HEAD, KV_LORA), const),
                  pl.BlockSpec((LANE, HEADS * HEAD_PAD), const),
                  pl.BlockSpec((RB, HEAD_PAD), row),
                  pl.BlockSpec((RB, HEAD_PAD), row),
                  pl.BlockSpec((RB, LANE), row)],
        out_specs=[pl.BlockSpec((RB, HEADS * HEAD_PAD), row),
                   pl.BlockSpec((RB, HEADS * HEAD_PAD), row),
                   pl.BlockSpec((1, HEADS * VT_ROWS, RB), lambda i: (i, 0, 0))],
        out_shape=[jax.ShapeDtypeStruct((T, HEADS * HEAD_PAD), BF16),
                   jax.ShapeDtypeStruct((T, HEADS * HEAD_PAD), BF16),
                   jax.ShapeDtypeStruct((NRB, HEADS * VT_ROWS, RB), BF16)],
        compiler_params=_cparams(("arbitrary",)),
        name="mla_proj",
    )(h, w, b, qg, kvg, wq, wqr, wk, wvt, e, tq_c, tq_s, tk)


KV_TILE = RB
KV_STAGE = 1024
N_STAGE = SEQ // KV_STAGE
TILES_PER_STAGE = KV_STAGE // KV_TILE


HEAD_GROUP = 2


def _attn_kernel(q_ref, k_ref, vt_ref, o_ref, sa_ref, sb_ref, m_ref, acc_ref):
    i = pl.program_id(1)
    heads = range(HEAD_GROUP)
    qs = [q_ref[:, g * HEAD_PAD:(g + 1) * HEAD_PAD] for g in heads]

    def scores(g, start, n):
        kt = k_ref[pl.ds(start, n), g * HEAD_PAD:(g + 1) * HEAD_PAD]
        return lax.dot_general(kt, qs[g], (((1,), (1,)), ((), ())),
                               preferred_element_type=F32)

    def values(g, tile):
        return vt_ref[tile, g * VT_ROWS:(g + 1) * VT_ROWS, :]

    def softmax_pv(g, s_ref, tile0, m, acc):
        m_new = jnp.maximum(m, jnp.max(s_ref[g], axis=0, keepdims=True))
        p = jnp.exp2(s_ref[g] - m_new).astype(BF16)
        acc = jnp.exp2(m - m_new) * acc
        for u in range(TILES_PER_STAGE):
            acc = acc + _dot(values(g, tile0 + u), p[u * KV_TILE:(u + 1) * KV_TILE, :])
        return m_new, acc

    for g in heads:
        s0 = scores(g, 0, N_CTX)
        m0 = jnp.max(s0, axis=0, keepdims=True)
        m_ref[g] = m0
        acc_ref[g] = _dot(values(g, 0), jnp.exp2(s0 - m0).astype(BF16))

    @pl.when(i > 0)
    def _():
        first_tile = N_CTX // KV_TILE
        for g in heads:
            sa_ref[g] = scores(g, N_CTX, KV_STAGE)

        def body(j, carry):
            ms, accs = list(carry[0]), list(carry[1])
            base = pl.multiple_of(N_CTX + 2 * j * KV_STAGE, KV_TILE)
            tile = first_tile + 2 * j * TILES_PER_STAGE
            for g in heads:
                sb_ref[g] = scores(g, base + KV_STAGE, KV_STAGE)
            for g in heads:
                ms[g], accs[g] = softmax_pv(g, sa_ref, tile, ms[g], accs[g])
            for g in heads:
                sa_ref[g] = scores(g, base + 2 * KV_STAGE, KV_STAGE)
            for g in heads:
                ms[g], accs[g] = softmax_pv(g, sb_ref, tile + TILES_PER_STAGE, ms[g], accs[g])
            return tuple(ms), tuple(accs)

        init = (tuple(m_ref[g] for g in heads), tuple(acc_ref[g] for g in heads))
        ms, accs = lax.fori_loop(0, N_STAGE // 2 - 1, body, init)
        ms, accs = list(ms), list(accs)
        for g in heads:
            sb_ref[g] = scores(g, N_CTX + (N_STAGE - 1) * KV_STAGE, KV_STAGE)
        for g in heads:
            ms[g], accs[g] = softmax_pv(g, sa_ref, first_tile + (N_STAGE - 2) * TILES_PER_STAGE, ms[g], accs[g])
        for g in heads:
            ms[g], accs[g] = softmax_pv(g, sb_ref, first_tile + (N_STAGE - 1) * TILES_PER_STAGE, ms[g], accs[g])
            acc_ref[g] = accs[g]

    for g in heads:
        acc = acc_ref[g]
        o_ref[g * V_HEAD:(g + 1) * V_HEAD, :] = acc[:V_HEAD, :] / acc[V_HEAD:V_HEAD + 1, :]


def attention(q, k, vt):
    return pl.pallas_call(
        _attn_kernel,
        grid=(HEADS // HEAD_GROUP, NRB),
        in_specs=[pl.BlockSpec((RB, HEAD_GROUP * HEAD_PAD), lambda h, i: (i, h)),
                  pl.BlockSpec((T, HEAD_GROUP * HEAD_PAD), lambda h, i: (0, h)),
                  pl.BlockSpec((NRB, HEAD_GROUP * VT_ROWS, KV_TILE), lambda h, i: (0, h, 0))],
        out_specs=pl.BlockSpec((HEAD_GROUP * V_HEAD, RB), lambda h, i: (h, i)),
        out_shape=jax.ShapeDtypeStruct((HEADS * V_HEAD, T), F32),
        scratch_shapes=[pltpu.VMEM((HEAD_GROUP, KV_STAGE, RB), F32),
                        pltpu.VMEM((HEAD_GROUP, KV_STAGE, RB), F32),
                        pltpu.VMEM((HEAD_GROUP, 1, RB), F32),
                        pltpu.VMEM((HEAD_GROUP, VT_ROWS, RB), F32)],
        compiler_params=_cparams(("arbitrary", "arbitrary")),
        name="attention",
    )(q, k, vt)


MERGE_TN = 512


N_BRANCH = 4


def _merge_kernel(h_ref, a_ref, b_ref, c_ref, dt_ref, *refs):
    wg_refs, bg_refs = refs[:N_BRANCH], refs[N_BRANCH:2 * N_BRANCH]
    wb_ref, y_ref = refs[2 * N_BRANCH:]
    h = h_ref[...]
    branches = (a_ref[...], b_ref[...], c_ref[...].astype(BF16), dt_ref[...].T.astype(BF16))
    y = jnp.zeros((RB, MERGE_TN), F32)
    for n, br in enumerate(branches):
        gate = jax.nn.sigmoid(_dot(h, wg_refs[n][...]) + bg_refs[n][...])
        y = y + gate * _dot(br, wb_ref[n])
    y_ref[...] = y.astype(BF16)


def merge(h, br_a, br_b, br_c, br_dt, wg, bg, wb):
    row = lambda j, i: (i, 0)
    n_col = D // MERGE_TN
    gate_specs = [pl.BlockSpec((D, MERGE_TN), functools.partial(lambda j, i, n: (0, n * n_col + j), n=n))
                  for n in range(N_BRANCH)]
    bias_specs = [pl.BlockSpec((1, MERGE_TN), functools.partial(lambda j, i, n: (0, n * n_col + j), n=n))
                  for n in range(N_BRANCH)]
    return pl.pallas_call(
        _merge_kernel,
        grid=(n_col, NRB),
        in_specs=[pl.BlockSpec((RB, D), row),
                  pl.BlockSpec((RB, BW), row),
                  pl.BlockSpec((RB, BW), row),
                  pl.BlockSpec((RB, BW), row),
                  pl.BlockSpec((BW, RB), lambda j, i: (0, i))]
                 + gate_specs + bias_specs
                 + [pl.BlockSpec((N_BRANCH, BW, MERGE_TN), lambda j, i: (0, 0, j))],
        out_specs=pl.BlockSpec((RB, MERGE_TN), lambda j, i: (i, j)),
        out_shape=jax.ShapeDtypeStruct((T, D), BF16),
        compiler_params=_cparams(("arbitrary", "arbitrary")),
        name="merge",
    )(h, br_a, br_b, br_c, br_dt, *([wg] * N_BRANCH), *([bg] * N_BRANCH), wb)


ROUTER_W = 128


def _post_mix_kernel(y_ref, x_ref, mod_ref, wo_ref, g_ref, b_ref, wr_ref, br_ref,
                     x1_ref, h2_ref, ids_ref, gates_ref):
    mix = _dot(y_ref[...], wo_ref[...])
    x1 = _ln0(ALPHA * x_ref[...] + _mod_row(mod_ref, 2) * mix) * g_ref[...] + b_ref[...]
    x1_ref[...] = x1
    h2 = _ln0(x1) * (1.0 + _mod_row(mod_ref, 4)) + _mod_row(mod_ref, 3)
    _rows_to_slabs(h2, h2_ref)
    logits = _dot_exact(h2, wr_ref[...]) + br_ref[...]
    lane = lax.broadcasted_iota(jnp.int32, (RB, ROUTER_W), 1).astype(F32)
    neg = jnp.float32(-jnp.inf)
    big = jnp.float32(ROUTER_W)
    gl = jnp.where(lane < N_GROUPS, logits, neg)
    gmax = jnp.max(gl, axis=-1, keepdims=True)
    g_idx = jnp.min(jnp.where(gl == gmax, lane, big), axis=-1, keepdims=True)
    g_p = 1.0 / jnp.sum(jnp.exp(gl - gmax), axis=-1, keepdims=True)
    e_lane = lane - N_GROUPS
    in_grp = jnp.logical_and(e_lane >= g_idx * EXP_PER_GROUP, e_lane < (g_idx + 1) * EXP_PER_GROUP)
    el = jnp.where(in_grp, logits, neg)
    v1 = jnp.max(el, axis=-1, keepdims=True)
    i1 = jnp.min(jnp.where(el == v1, lane, big), axis=-1, keepdims=True)
    el2 = jnp.where(lane == i1, neg, el)
    v2 = jnp.max(el2, axis=-1, keepdims=True)
    i2 = jnp.min(jnp.where(el2 == v2, lane, big), axis=-1, keepdims=True)
    e21 = jnp.exp(v2 - v1)
    p1 = 1.0 / (1.0 + e21)
    p2 = e21 * p1
    ids_ref[...] = jnp.where(lane == 0, i1 - N_GROUPS, jnp.where(lane == 1, i2 - N_GROUPS, 0.0)).astype(jnp.int32)
    gates_ref[...] = jnp.where(lane == 0, p1 * g_p, jnp.where(lane == 1, p2 * g_p, 0.0))


def post_mix(y, x, mod, wo, ln_g, ln_b, wr, br):
    row = lambda i: (i, 0)
    const = lambda i: (0, 0)
    return pl.pallas_call(
        _post_mix_kernel,
        grid=(NRB,),
        in_specs=[pl.BlockSpec((RB, D), row),
                  pl.BlockSpec((RB, D), row),
                  pl.BlockSpec((8, 6 * D), const),
                  pl.BlockSpec((D, D), const),
                  pl.BlockSpec((1, D), const),
                  pl.BlockSpec((1, D), const),
                  pl.BlockSpec((D, ROUTER_W), const),
                  pl.BlockSpec((1, ROUTER_W), const)],
        out_specs=[pl.BlockSpec((RB, D), row),
                   pl.BlockSpec((RB * ROW_SUB, LANE), row),
                   pl.BlockSpec((RB, ROUTER_W), row),
                   pl.BlockSpec((RB, ROUTER_W), row)],
        out_shape=[jax.ShapeDtypeStruct((T, D), F32),
                   jax.ShapeDtypeStruct((T * ROW_SUB, LANE), F32),
                   jax.ShapeDtypeStruct((T, ROUTER_W), jnp.int32),
                   jax.ShapeDtypeStruct((T, ROUTER_W), F32)],
        compiler_params=_cparams(("arbitrary",)),
        name="post_mix",
    )(y, x, mod, wo, ln_g, ln_b, wr, br)


N_ROWS = -(-(T * TOP_K + N_EXPERTS * (MOE_BLOCK - 1)) // MOE_BLOCK) * MOE_BLOCK
N_BLK = N_ROWS // MOE_BLOCK


def _row_gather(src_hbm, idx_ref, base, n, dst_ref, sem):
    def issue(r, _):
        src_row = pl.multiple_of(idx_ref[base + r] * ROW_SUB, ROW_SUB)
        dst_row = pl.multiple_of(r * ROW_SUB, ROW_SUB)
        pltpu.make_async_copy(src_hbm.at[pl.ds(src_row, ROW_SUB), :],
                              dst_ref.at[pl.ds(dst_row, ROW_SUB), :], sem).start()
        return 0
    lax.fori_loop(0, n, issue, 0)


def _row_gather_wait(src_hbm, n, dst_ref, sem):
    pltpu.make_async_copy(src_hbm.at[pl.ds(0, n * ROW_SUB), :], dst_ref, sem).wait()


def _rows_from_slabs(buf_ref, start, n, stride):
    return jnp.concatenate([buf_ref[pl.ds(start + c, n, stride=stride), :] for c in range(ROW_SUB)], axis=-1)


def _rows_to_slabs(val, out_ref):
    n = val.shape[0]
    for c in range(ROW_SUB):
        out_ref[pl.ds(c, n, stride=ROW_SUB), :] = val[:, c * LANE:(c + 1) * LANE]


def _expert_kernel(blk_exp_ref, row_tok_ref, h_hbm, w1_ref, w3_ref, w2_ref, y_ref, xbuf, sems):
    b = pl.program_id(0)
    slot = b % 2

    @pl.when(b == 0)
    def _():
        _row_gather(h_hbm, row_tok_ref, 0, MOE_BLOCK, xbuf.at[0], sems.at[0])

    @pl.when(b + 1 < N_BLK)
    def _():
        _row_gather(h_hbm, row_tok_ref, (b + 1) * MOE_BLOCK, MOE_BLOCK, xbuf.at[1 - slot], sems.at[1 - slot])

    _row_gather_wait(h_hbm, MOE_BLOCK, xbuf.at[slot], sems.at[slot])
    xb = _rows_from_slabs(xbuf.at[slot], 0, MOE_BLOCK, ROW_SUB).astype(BF16)
    a = _dot(xb, w1_ref[0])
    g = _dot(xb, w3_ref[0])
    hmid = (a * jax.nn.sigmoid(a) * g).astype(BF16)
    _rows_to_slabs(_dot(hmid, w2_ref[0]), y_ref)


def experts(blk_exp, row_tok, h2_slabs, w1, w3, w2):
    grid_spec = pltpu.PrefetchScalarGridSpec(
        num_scalar_prefetch=2,
        grid=(N_BLK,),
        in_specs=[pl.BlockSpec(memory_space=pl.ANY),
                  pl.BlockSpec((1, D, D_EXPERT), lambda b, be, rt: (be[b], 0, 0)),
                  pl.BlockSpec((1, D, D_EXPERT), lambda b, be, rt: (be[b], 0, 0)),
                  pl.BlockSpec((1, D_EXPERT, D), lambda b, be, rt: (be[b], 0, 0))],
        out_specs=pl.BlockSpec((MOE_BLOCK * ROW_SUB, LANE), lambda b, be, rt: (b, 0)),
        scratch_shapes=[pltpu.VMEM((2, MOE_BLOCK * ROW_SUB, LANE), F32),
                        pltpu.SemaphoreType.DMA((2,))],
    )
    return pl.pallas_call(
        _expert_kernel,
        grid_spec=grid_spec,
        out_shape=jax.ShapeDtypeStruct((N_ROWS * ROW_SUB, LANE), F32),
        compiler_params=_cparams(("arbitrary",)),
        name="experts",
    )(blk_exp, row_tok, h2_slabs, w1, w3, w2)


def _combine_kernel(pos_ref, y_hbm, gates_ref, x1_ref, mod_ref, g_ref, b_ref, nmod_ref,
                    x2_ref, hn_ref, ybuf, sem):
    i = pl.program_id(0)
    _row_gather(y_hbm, pos_ref, i * (RB * TOP_K), RB * TOP_K, ybuf, sem)
    _row_gather_wait(y_hbm, RB * TOP_K, ybuf, sem)
    y0 = _rows_from_slabs(ybuf, 0, RB, TOP_K * ROW_SUB)
    y1 = _rows_from_slabs(ybuf, ROW_SUB, RB, TOP_K * ROW_SUB)
    gates = gates_ref[...]
    f = gates[:, 0:1] * y0 + gates[:, 1:2] * y1
    x2 = _ln0(ALPHA * x1_ref[...] + _mod_row(mod_ref, 5) * f) * g_ref[...] + b_ref[...]
    x2_ref[...] = x2
    hn = _ln0(x2) * (1.0 + _mod_row(nmod_ref, 1)) + _mod_row(nmod_ref, 0)
    hn_ref[...] = hn.astype(BF16)


def combine(pos, y_slabs, gates, x1, mod, ln_g, ln_b, next_mod):
    row = lambda i, p: (i, 0)
    const = lambda i, p: (0, 0)
    grid_spec = pltpu.PrefetchScalarGridSpec(
        num_scalar_prefetch=1,
        grid=(NRB,),
        in_specs=[pl.BlockSpec(memory_space=pl.ANY),
                  pl.BlockSpec((RB, ROUTER_W), row),
                  pl.BlockSpec((RB, D), row),
                  pl.BlockSpec((8, 6 * D), const),
                  pl.BlockSpec((1, D), const),
                  pl.BlockSpec((1, D), const),
                  pl.BlockSpec((8, 6 * D), const)],
        out_specs=[pl.BlockSpec((RB, D), row), pl.BlockSpec((RB, D), row)],
        scratch_shapes=[pltpu.VMEM((RB * TOP_K * ROW_SUB, LANE), F32),
                        pltpu.SemaphoreType.DMA(())],
    )
    return pl.pallas_call(
        _combine_kernel,
        grid_spec=grid_spec,
        out_shape=[jax.ShapeDtypeStruct((T, D), F32), jax.ShapeDtypeStruct((T, D), BF16)],
        compiler_params=_cparams(("arbitrary",)),
        name="combine",
    )(pos, y_slabs, gates, x1, mod, ln_g, ln_b, next_mod)


def _routing_tables(ids):
    expert = ids[:, :TOP_K].reshape(-1)
    n = expert.shape[0]
    onehot = (expert[:, None] == jnp.arange(N_EXPERTS, dtype=jnp.int32)[None, :]).astype(jnp.int32)
    csum = jnp.cumsum(onehot, axis=0)
    counts = csum[-1]
    rank = jnp.sum(onehot * csum, axis=1) - 1
    padded = (counts + MOE_BLOCK - 1) // MOE_BLOCK * MOE_BLOCK
    pends = jnp.cumsum(padded)
    pstarts = pends - padded
    pos = (jnp.sum(onehot * pstarts[None, :], axis=1) + rank).astype(jnp.int32)
    tok = jnp.arange(n, dtype=jnp.int32) // TOP_K
    row_tok = jnp.zeros((N_ROWS,), jnp.int32).at[pos].set(tok)
    blk_start = jnp.arange(N_BLK, dtype=jnp.int32) * MOE_BLOCK
    blk_exp = jnp.minimum(jnp.sum(pends[None, :] <= blk_start[:, None], axis=1), N_EXPERTS - 1).astype(jnp.int32)
    return blk_exp, row_tok, pos


def _rope_tables():
    rows = SEQ // GRID_W
    row = jnp.repeat(jnp.arange(rows, dtype=F32), GRID_W)
    col = jnp.tile(jnp.arange(GRID_W, dtype=F32), rows)
    half = QK_ROPE // 2
    inv = ROPE_THETA ** (-jnp.arange(0, half, 2, dtype=F32) / half)
    ar = row[:, None] * inv
    ac = col[:, None] * inv
    ang = jnp.concatenate([ar, ar, ac, ac], -1)
    cos = jnp.concatenate([jnp.ones((N_CTX, QK_ROPE), F32), jnp.cos(ang)], 0)
    sin = jnp.concatenate([jnp.zeros((N_CTX, QK_ROPE), F32), jnp.sin(ang)], 0)
    one = jnp.ones((T, QK_NOPE), F32)
    zero_n = jnp.zeros((T, QK_NOPE), F32)
    zero_p = jnp.zeros((T, HEAD_PAD - QK_NOPE - QK_ROPE), F32)
    tq_c = jnp.concatenate([one, cos, zero_p], -1)
    tq_s = jnp.concatenate([zero_n, sin, zero_p], -1)
    tk = jnp.concatenate([cos, sin, jnp.zeros((T, LANE - 2 * QK_ROPE), F32)], -1)
    return tq_c, tq_s, tk


_ROT_SRC = np.concatenate([np.arange(8, 16), np.arange(0, 8), np.arange(24, 32), np.arange(16, 24)])
_ROT_SIGN = np.concatenate([-np.ones(8), np.ones(8), -np.ones(8), np.ones(8)]).astype(np.float32)


def _rot_cols(w):
    return w[..., _ROT_SRC] * _ROT_SIGN


def _layer_weights(l, w_in, b_in, d_w_uq, d_w_uk, d_w_uv, w_branch, w_o, r_w_grp, r_b_grp, r_w_exp, r_b_exp,
                   a_b_s, b_w_dw):
    wi, bi = w_in[l], b_in[l]
    p = {}
    p["wa"] = wi[:, COL_A:COL_B].astype(BF16)
    p["ba"] = bi[None, COL_A:COL_B]
    p["wb"] = wi[:, COL_B:COL_C].astype(BF16)
    p["bb"] = bi[None, COL_B:COL_C]
    p["wc"] = wi[:, COL_C:COL_D].astype(BF16)
    p["bc"] = bi[None, COL_C:COL_D]
    w_kr, b_kr = wi[:, COL_KR:COL_G], bi[COL_KR:COL_G]
    padw = D_PROJ_W - (COL_G - COL_D) - QK_ROPE
    p["wd"] = jnp.concatenate([wi[:, COL_D:COL_G], _rot_cols(w_kr), jnp.zeros((D, padw), F32)], -1).astype(BF16)
    p["bd"] = jnp.concatenate([bi[COL_D:COL_G], _rot_cols(b_kr), jnp.zeros((padw,), F32)])[None]
    wq = d_w_uq[l].reshape(Q_LORA, HEADS, QK_NOPE + QK_ROPE)
    zpad = jnp.zeros((Q_LORA, HEADS, HEAD_PAD - QK_NOPE - QK_ROPE), F32)
    p["wq"] = jnp.concatenate([wq, zpad], -1).reshape(Q_LORA, HEADS * HEAD_PAD).astype(BF16)
    p["wqr"] = jnp.concatenate([jnp.zeros((Q_LORA, HEADS, QK_NOPE), F32), _rot_cols(wq[..., QK_NOPE:]), zpad],
                               -1).reshape(Q_LORA, HEADS * HEAD_PAD).astype(BF16)
    wk = d_w_uk[l].reshape(KV_LORA, HEADS, QK_NOPE)
    p["wk"] = jnp.concatenate([wk, jnp.zeros((KV_LORA, HEADS, HEAD_PAD - QK_NOPE), F32)],
                              -1).reshape(KV_LORA, HEADS * HEAD_PAD).astype(BF16)
    p["wvt"] = d_w_uv[l].T.astype(BF16)
    e = np.zeros((LANE, HEADS, HEAD_PAD), np.float32)
    for j in range(QK_ROPE):
        e[j, :, QK_NOPE + j] = 1.0
        e[QK_ROPE + j, :, QK_NOPE + j] = 1.0
    p["e"] = jnp.asarray(e.reshape(LANE, HEADS * HEAD_PAD), dtype=BF16)
    p["wg"] = wi[:, COL_G:].astype(BF16)
    p["bg"] = bi[None, COL_G:]
    p["wbr"] = w_branch[l].astype(BF16)
    p["wo"] = w_o[l].astype(BF16)
    padr = ROUTER_W - N_GROUPS - N_EXPERTS
    p["wr"] = jnp.concatenate([r_w_grp[l], r_w_exp[l], jnp.zeros((D, padr), F32)], -1)
    p["br"] = jnp.concatenate([r_b_grp[l], r_b_exp[l], jnp.zeros((padr,), F32)])[None]
    p["bs"] = jnp.broadcast_to(a_b_s[l][:, :, None], (A_GROUPS, CHUNK, CHUNK))
    p["wdw"] = jnp.concatenate([b_w_dw[l], jnp.zeros((1, BW), F32)], 0)
    return p


def kernel(x, c, ctx, c_ctx, w_ada, b_ada, w_in, b_in, a_ln_g, a_ln_b, a_w_s, a_b_s, b_w_dw, b_b_dw, b_ln_g,
           b_ln_b, d_q_g, d_w_uq, d_kv_g, d_w_uk, d_w_uv, w_branch, w_o, ln1_g, ln1_b, r_w_grp, r_b_grp,
           r_w_exp, r_b_exp, e_w1, e_w3, e_w2, ln2_g, ln2_b):
    assert x.shape == (1, SEQ, D) and ctx.shape == (1, N_CTX, D)
    xs = jnp.concatenate([ctx[0], x[0]], axis=0)
    cvec = jnp.concatenate([c, c_ctx[None], jnp.zeros((6, D), F32)], axis=0)
    mods = ada_mod(cvec, w_ada, b_ada)
    tq_c, tq_s, tk = _rope_tables()
    fc = _fourier_consts()

    h = modln(xs, mods[0])
    for l in range(DEPTH):
        p = _layer_weights(l, w_in, b_in, d_w_uq, d_w_uk, d_w_uv, w_branch, w_o, r_w_grp, r_b_grp, r_w_exp,
                           r_b_exp, a_b_s, b_w_dw)
        mod = mods[l]
        br_a = mix_a(h, p["wa"], p["ba"], a_ln_g[l][None], a_ln_b[l][None], a_w_s[l].astype(BF16), p["bs"])
        br_b = conv_ln(glu(h, p["wb"], p["bb"]), p["wdw"], b_b_dw[l][None], b_ln_g[l][None], b_ln_b[l][None])
        br_c = fourier_mix(h, p["wc"], p["bc"], fc)
        q, k, vt = mla_proj(h, p["wd"], p["bd"], d_q_g[l][None], d_kv_g[l][None], p["wq"], p["wqr"], p["wk"],
                            p["wvt"], p["e"], tq_c, tq_s, tk)
        br_dt = attention(q, k, vt)
        y = merge(h, br_a, br_b, br_c, br_dt, p["wg"], p["bg"], p["wbr"])
        x1, h2, ids, gates = post_mix(y, xs, mod, p["wo"], ln1_g[l][None], ln1_b[l][None], p["wr"], p["br"])
        blk_exp, row_tok, pos = _routing_tables(ids)
        ys = experts(blk_exp, row_tok, h2,
                     e_w1[l].astype(BF16), e_w3[l].astype(BF16), e_w2[l].astype(BF16))
        xs, h = combine(pos, ys, gates, x1, mod, ln2_g[l][None], ln2_b[l][None],
                        mods[min(l + 1, DEPTH - 1)])
    return xs[N_CTX:][None]
```

```python
import functools
import math

import numpy as np
import jax
import jax.numpy as jnp
from jax import lax
from jax.experimental import pallas as pl
from jax.experimental.pallas import tpu as pltpu

F32 = jnp.float32
BF16 = jnp.bfloat16

D = 2048
SEQ = 8192
N_CTX = 256
T = N_CTX + SEQ
DEPTH = 2
GRID_W = 64
BW = 512
CHUNK = 128
A_GROUPS = 4
CONV_W = 31
F_GROUPS = 4
QK_NOPE = 64
QK_ROPE = 32
V_HEAD = 64
HEADS = 8
Q_LORA = 512
KV_LORA = 256
ROPE_THETA = 10000.0
MLA_SCALE = (QK_NOPE + QK_ROPE) ** -0.5
N_GROUPS = 4
EXP_PER_GROUP = 8
N_EXPERTS = 32
TOP_K = 2
D_EXPERT = 1024
MOE_BLOCK = 256
COL_A = 0
COL_B = 1024
COL_C = 2048
COL_D = 2560
COL_KV = 3072
COL_KR = 3328
COL_G = 3360
ALPHA = (2 * DEPTH) ** 0.25
LN_EPS = 1e-6

LANE = 128
HEAD_PAD = 128
RB = 256
NRB = T // RB
FFT_N1 = 64
FFT_N2 = 128
ROW_SUB = D // LANE
VMEM_LIMIT = 56 * 1024 * 1024


def _cparams(sem):
    return pltpu.CompilerParams(dimension_semantics=sem, vmem_limit_bytes=VMEM_LIMIT)


def _dot(a, b):
    return jnp.dot(a, b, preferred_element_type=F32)


def _split(x):
    hi = lax.bitcast_convert_type(lax.bitcast_convert_type(x, jnp.int32) & jnp.int32(-65536), F32)
    return hi.astype(BF16), (x - hi).astype(BF16)


def _hilo(x):
    return jnp.stack(_split(jnp.asarray(x, F32)))


def _dot3(a, t_ref):
    a_hi, a_lo = _split(a)
    return _dot(a_hi, t_ref[0]) + (_dot(a_hi, t_ref[1]) + _dot(a_lo, t_ref[0]))


def _dot3_t(t_ref, b):
    b_hi, b_lo = _split(b)
    return _dot(t_ref[0], b_hi) + (_dot(t_ref[1], b_hi) + _dot(t_ref[0], b_lo))


def _ln0(x):
    mu = jnp.mean(x, axis=-1, keepdims=True)
    xc = x - mu
    var = jnp.mean(xc * xc, axis=-1, keepdims=True)
    return xc * lax.rsqrt(var + LN_EPS)


def _mod_row(mod_ref, q):
    lat = mod_ref[0:1, q * D:(q + 1) * D]
    ctx = mod_ref[1:2, q * D:(q + 1) * D]
    return jnp.where(pl.program_id(0) == 0, ctx, lat)


ADA_TN = 1536
ADA_TK = 256
N_COND = 2


def _ada_kernel(c_ref, w_ref, b_ref, o_ref):
    accs = [b_ref[0] for _ in range(N_COND)]
    for k0 in range(0, D, ADA_TK):
        w = w_ref[0, k0:k0 + ADA_TK, :]
        for r in range(N_COND):
            s = c_ref[r, k0:k0 + ADA_TK, :]
            s = s * jax.nn.sigmoid(s)
            s = jnp.concatenate([s] * (ADA_TN // LANE), axis=-1)
            accs[r] = accs[r] + jnp.sum(w * s, axis=0, keepdims=True)
    o_ref[0] = jnp.concatenate(accs + [jnp.zeros((8 - N_COND, ADA_TN), F32)], axis=0)


def ada_mod(c_rep, w_ada, b_ada):
    return pl.pallas_call(
        _ada_kernel,
        grid=(DEPTH, 6 * D // ADA_TN),
        in_specs=[pl.BlockSpec((N_COND, D, LANE), lambda l, j: (0, 0, 0)),
                  pl.BlockSpec((1, D, ADA_TN), lambda l, j: (l, 0, j)),
                  pl.BlockSpec((1, 1, ADA_TN), lambda l, j: (l, 0, j))],
        out_specs=pl.BlockSpec((1, 8, ADA_TN), lambda l, j: (l, 0, j)),
        out_shape=jax.ShapeDtypeStruct((DEPTH, 8, 6 * D), F32),
        compiler_params=_cparams(("arbitrary", "arbitrary")),
        name="ada_mod",
    )(c_rep, w_ada, b_ada.reshape(DEPTH, 1, 6 * D))


def _modln_kernel(x_ref, mod_ref, h_ref):
    h = _ln0(x_ref[...]) * (1.0 + _mod_row(mod_ref, 1)) + _mod_row(mod_ref, 0)
    h_ref[...] = h.astype(BF16)


def modln(x, mod):
    return pl.pallas_call(
        _modln_kernel,
        grid=(NRB,),
        in_specs=[pl.BlockSpec((RB, D), lambda i: (i, 0)),
                  pl.BlockSpec((8, 6 * D), lambda i: (0, 0))],
        out_specs=pl.BlockSpec((RB, D), lambda i: (i, 0)),
        out_shape=jax.ShapeDtypeStruct((T, D), BF16),
        compiler_params=_cparams(("arbitrary",)),
        name="modln",
    )(x, mod)


def _w_in_spec(l, width, col):
    return pl.BlockSpec((1, D, width), lambda i: (l, 0, col))


def _b_in_spec(l, width, col):
    return pl.BlockSpec((1, 1, width), lambda i: (l, 0, col))


def _cast_once(w_ref, wbf_ref):
    @pl.when(pl.program_id(0) == 0)
    def _():
        wbf_ref[...] = w_ref[0].astype(BF16)


def _mix_a_kernel(h_ref, w_ref, b_ref, g_ref, bb_ref, ws_ref, bs_ref, o_ref, wbf_ref):
    _cast_once(w_ref, wbf_ref)
    uv = jax.nn.gelu(_dot(h_ref[...], wbf_ref[...]) + b_ref[0])
    u = uv[:, :BW]
    v = _ln0(uv[:, BW:]) * g_ref[...] + bb_ref[...]
    v = v.astype(BF16)
    gw = BW // A_GROUPS
    for ch in range(RB // CHUNK):
        rows = slice(ch * CHUNK, (ch + 1) * CHUNK)
        parts = [_dot(ws_ref[g], v[rows, g * gw:(g + 1) * gw]) + bs_ref[g] for g in range(A_GROUPS)]
        mixed = jnp.concatenate(parts, axis=-1)
        o_ref[rows, :] = (u[rows, :] * mixed).astype(BF16)


def mix_a(l, h, w_in, b_in, ln_g, ln_b, ws, bs):
    return pl.pallas_call(
        _mix_a_kernel,
        grid=(NRB,),
        in_specs=[pl.BlockSpec((RB, D), lambda i: (i, 0)),
                  _w_in_spec(l, 2 * BW, COL_A // (2 * BW)),
                  _b_in_spec(l, 2 * BW, COL_A // (2 * BW)),
                  pl.BlockSpec((1, BW), lambda i: (0, 0)),
                  pl.BlockSpec((1, BW), lambda i: (0, 0)),
                  pl.BlockSpec((A_GROUPS, CHUNK, CHUNK), lambda i: (0, 0, 0)),
                  pl.BlockSpec((A_GROUPS, CHUNK, CHUNK), lambda i: (0, 0, 0))],
        out_specs=pl.BlockSpec((RB, BW), lambda i: (i, 0)),
        out_shape=jax.ShapeDtypeStruct((T, BW), BF16),
        scratch_shapes=[pltpu.VMEM((D, 2 * BW), BF16)],
        compiler_params=_cparams(("arbitrary",)),
        name="mix_a",
    )(h, w_in, b_in, ln_g, ln_b, ws, bs)


def _glu_kernel(h_ref, w_ref, b_ref, o_ref, wbf_ref):
    _cast_once(w_ref, wbf_ref)
    ab = _dot(h_ref[...], wbf_ref[...]) + b_ref[0]
    o_ref[...] = ab[:, :BW] * jax.nn.sigmoid(ab[:, BW:])


def glu(l, h, w_in, b_in):
    return pl.pallas_call(
        _glu_kernel,
        grid=(NRB,),
        in_specs=[pl.BlockSpec((RB, D), lambda i: (i, 0)),
                  _w_in_spec(l, 2 * BW, COL_B // (2 * BW)),
                  _b_in_spec(l, 2 * BW, COL_B // (2 * BW))],
        out_specs=pl.BlockSpec((RB, BW), lambda i: (i, 0)),
        out_shape=jax.ShapeDtypeStruct((T, BW), F32),
        scratch_shapes=[pltpu.VMEM((D, 2 * BW), BF16)],
        compiler_params=_cparams(("arbitrary",)),
        name="glu",
    )(h, w_in, b_in)


CONV_HALO = 16


def _conv_kernel(prev_ref, cur_ref, next_ref, w_ref, b_ref, g_ref, bb_ref, o_ref, ext_ref):
    i = pl.program_id(0)
    has_prev = i >= 2
    has_next = jnp.logical_and(i >= 1, i < NRB - 1)
    ext_ref[0:CONV_HALO, :] = jnp.where(has_prev, prev_ref[RB - CONV_HALO:RB, :], 0.0)
    ext_ref[CONV_HALO:CONV_HALO + RB, :] = cur_ref[...]
    ext_ref[CONV_HALO + RB:2 * CONV_HALO + RB, :] = jnp.where(has_next, next_ref[0:CONV_HALO, :], 0.0)
    off = CONV_HALO - CONV_W // 2
    acc = jnp.zeros((RB, BW), F32)
    for k in range(CONV_W):
        acc = acc + ext_ref[off + k:off + k + RB, :] * w_ref[k:k + 1, :]
    y = _ln0(acc + b_ref[...]) * g_ref[...] + bb_ref[...]
    o_ref[...] = (y * jax.nn.sigmoid(y)).astype(BF16)


def conv_ln(y, w_dw, b_dw, ln_g, ln_b):
    return pl.pallas_call(
        _conv_kernel,
        grid=(NRB,),
        in_specs=[pl.BlockSpec((RB, BW), lambda i: (jnp.maximum(i - 1, 0), 0)),
                  pl.BlockSpec((RB, BW), lambda i: (i, 0)),
                  pl.BlockSpec((RB, BW), lambda i: (jnp.minimum(i + 1, NRB - 1), 0)),
                  pl.BlockSpec((CONV_W + 1, BW), lambda i: (0, 0)),
                  pl.BlockSpec((1, BW), lambda i: (0, 0)),
                  pl.BlockSpec((1, BW), lambda i: (0, 0)),
                  pl.BlockSpec((1, BW), lambda i: (0, 0))],
        out_specs=pl.BlockSpec((RB, BW), lambda i: (i, 0)),
        out_shape=jax.ShapeDtypeStruct((T, BW), BF16),
        scratch_shapes=[pltpu.VMEM((RB + 2 * CONV_HALO, BW), F32)],
        compiler_params=_cparams(("arbitrary",)),
        name="conv_ln",
    )(y, y, y, w_dw, b_dw, ln_g, ln_b)


def _dft_tables(n):
    k = np.arange(n, dtype=np.int64)
    ang = 2.0 * np.pi * ((k[:, None] * k[None, :]) % n).astype(np.float64) / n
    return np.cos(ang), np.sin(ang)


def _fproj_kernel(h_ref, w_ref, b_ref, cs_ref, ar_ref, ai_ref, wbf_ref):
    _cast_once(w_ref, wbf_ref)
    z = _dot(h_ref[...], wbf_ref[...]) + b_ref[0]
    a = _dot3(z, cs_ref)
    ar_ref[...] = a[:, :BW]
    ai_ref[...] = a[:, BW:]


def fourier_proj(l, h, w_in, b_in, cs):
    return pl.pallas_call(
        _fproj_kernel,
        grid=(NRB,),
        in_specs=[pl.BlockSpec((RB, D), lambda i: (i, 0)),
                  _w_in_spec(l, BW, COL_C // BW),
                  _b_in_spec(l, BW, COL_C // BW),
                  pl.BlockSpec((2, BW, 2 * BW), lambda i: (0, 0, 0))],
        out_specs=[pl.BlockSpec((RB, BW), lambda i: (i, 0)),
                   pl.BlockSpec((RB, BW), lambda i: (i, 0))],
        out_shape=[jax.ShapeDtypeStruct((T, BW), F32), jax.ShapeDtypeStruct((T, BW), F32)],
        scratch_shapes=[pltpu.VMEM((D, BW), BF16)],
        compiler_params=_cparams(("arbitrary",)),
        name="fourier_proj",
    )(h, w_in, b_in, cs)


FFT_TILE_N2 = 4


def _fft1_kernel(ar_ref, ai_ref, m_ref, ct_ref, st_ref, br_ref, bi_ref):
    x = jnp.concatenate([ar_ref[...], ai_ref[...]], axis=0)
    b = _dot3_t(m_ref, x)
    b_re, b_im = b[:FFT_N1], b[FFT_N1:]
    reps = BW // LANE
    for j in range(FFT_TILE_N2):
        ct = jnp.concatenate([ct_ref[j]] * reps, axis=-1)
        st = jnp.concatenate([st_ref[j]] * reps, axis=-1)
        lanes = slice(j * BW, (j + 1) * BW)
        br_ref[:, lanes] = b_re[:, lanes] * ct + b_im[:, lanes] * st
        bi_ref[:, lanes] = b_im[:, lanes] * ct - b_re[:, lanes] * st


def fft_stage1(ar2d, ai2d, m1, ct, st):
    tn = FFT_TILE_N2 * BW
    return pl.pallas_call(
        _fft1_kernel,
        grid=(FFT_N2 // FFT_TILE_N2,),
        in_specs=[pl.BlockSpec((FFT_N1, tn), lambda j: (0, j)),
                  pl.BlockSpec((FFT_N1, tn), lambda j: (0, j)),
                  pl.BlockSpec((2, 2 * FFT_N1, 2 * FFT_N1), lambda j: (0, 0, 0)),
                  pl.BlockSpec((FFT_TILE_N2, FFT_N1, LANE), lambda j: (j, 0, 0)),
                  pl.BlockSpec((FFT_TILE_N2, FFT_N1, LANE), lambda j: (j, 0, 0))],
        out_specs=[pl.BlockSpec((FFT_N1, tn), lambda j: (0, j)),
                   pl.BlockSpec((FFT_N1, tn), lambda j: (0, j))],
        out_shape=[jax.ShapeDtypeStruct((FFT_N1, FFT_N2 * BW), F32)] * 2,
        compiler_params=_cparams(("arbitrary",)),
        name="fft_stage1",
    )(ar2d, ai2d, m1, ct, st)


FFT_TILE_K1 = 8


def _fft2_kernel(br_ref, bi_ref, c_ref, s_ref, p_ref, o_ref, *, scale):
    ys = [(_dot3_t(c_ref, br_ref[a]) + _dot3_t(s_ref, bi_ref[a])) * scale for a in range(FFT_TILE_K1)]
    y = jnp.concatenate(ys, axis=0).astype(BF16)
    y = _dot(p_ref[...], y)
    o_ref[...] = y.reshape(FFT_N2, FFT_TILE_K1, BW)


def fft_stage2(br3d, bi3d, c128, s128, perm, scale):
    return pl.pallas_call(
        functools.partial(_fft2_kernel, scale=scale),
        grid=(FFT_N1 // FFT_TILE_K1,),
        in_specs=[pl.BlockSpec((FFT_TILE_K1, FFT_N2, BW), lambda a: (a, 0, 0)),
                  pl.BlockSpec((FFT_TILE_K1, FFT_N2, BW), lambda a: (a, 0, 0)),
                  pl.BlockSpec((2, FFT_N2, FFT_N2), lambda a: (0, 0, 0)),
                  pl.BlockSpec((2, FFT_N2, FFT_N2), lambda a: (0, 0, 0)),
                  pl.BlockSpec((FFT_TILE_K1 * FFT_N2, FFT_TILE_K1 * FFT_N2), lambda a: (0, 0))],
        out_specs=pl.BlockSpec((FFT_N2, FFT_TILE_K1, BW), lambda a: (0, a, 0)),
        out_shape=jax.ShapeDtypeStruct((FFT_N2, FFT_N1, BW), F32),
        compiler_params=_cparams(("arbitrary",)),
        name="fft_stage2",
    )(br3d, bi3d, c128, s128, perm)


def _dft_ctx_kernel(ar_ref, ai_ref, c_ref, s_ref, o_ref, *, scale):
    o_ref[...] = (_dot3_t(c_ref, ar_ref[...]) + _dot3_t(s_ref, ai_ref[...])) * scale


def dft_ctx(ar, ai, c, s, scale):
    return pl.pallas_call(
        functools.partial(_dft_ctx_kernel, scale=scale),
        grid=(1,),
        in_specs=[pl.BlockSpec((N_CTX, BW), lambda i: (0, 0)),
                  pl.BlockSpec((N_CTX, BW), lambda i: (0, 0)),
                  pl.BlockSpec((2, N_CTX, N_CTX), lambda i: (0, 0, 0)),
                  pl.BlockSpec((2, N_CTX, N_CTX), lambda i: (0, 0, 0))],
        out_specs=pl.BlockSpec((N_CTX, BW), lambda i: (0, 0)),
        out_shape=jax.ShapeDtypeStruct((N_CTX, BW), F32),
        compiler_params=_cparams(("arbitrary",)),
        name="dft_ctx",
    )(ar, ai, c, s)


def _fourier_consts():
    c64, s64 = _dft_tables(FFT_N1)
    m1 = np.block([[c64, s64], [-s64, c64]]).astype(np.float32)
    k1 = np.arange(FFT_N1, dtype=np.float64)
    n2 = np.arange(FFT_N2, dtype=np.float64)
    phi = 2.0 * np.pi * n2[:, None] * k1[None, :] / SEQ
    ct = np.repeat(np.cos(phi)[:, :, None], LANE, axis=2).astype(np.float32)
    st = np.repeat(np.sin(phi)[:, :, None], LANE, axis=2).astype(np.float32)
    c128, s128 = _dft_tables(FFT_N2)
    gw = BW // F_GROUPS
    cg, sg = _dft_tables(gw)
    cs = np.zeros((BW, 2 * BW), np.float32)
    for g in range(F_GROUPS):
        sl = slice(g * gw, (g + 1) * gw)
        cs[sl, sl] = cg
        cs[sl, BW + g * gw:BW + (g + 1) * gw] = -sg
    rows = np.arange(FFT_TILE_K1 * FFT_N2)
    perm = np.zeros((rows.size, rows.size), np.float32)
    perm[rows, (rows % FFT_TILE_K1) * FFT_N2 + rows // FFT_TILE_K1] = 1.0
    cc, sc = _dft_tables(N_CTX)
    return dict(m1=_hilo(m1), ct=jnp.asarray(ct), st=jnp.asarray(st), c128=_hilo(c128), s128=_hilo(s128),
                cs=_hilo(cs), perm=jnp.asarray(perm, dtype=BF16), cc=_hilo(cc), sc=_hilo(sc))


def fourier_mix(l, h, w_in, b_in, fc):
    gw = BW // F_GROUPS
    ar, ai = fourier_proj(l, h, w_in, b_in, fc["cs"])
    y_ctx = dft_ctx(ar, ai, fc["cc"], fc["sc"], 1.0 / math.sqrt(N_CTX * gw))
    ar2 = ar[N_CTX:].reshape(FFT_N1, FFT_N2 * BW)
    ai2 = ai[N_CTX:].reshape(FFT_N1, FFT_N2 * BW)
    br, bi = fft_stage1(ar2, ai2, fc["m1"], fc["ct"], fc["st"])
    y = fft_stage2(br.reshape(FFT_N1, FFT_N2, BW), bi.reshape(FFT_N1, FFT_N2, BW),
                   fc["c128"], fc["s128"], fc["perm"], 1.0 / math.sqrt(SEQ * gw))
    return jnp.concatenate([y_ctx, y.reshape(SEQ, BW)], axis=0)


VT_ROWS = 80
LOG2E = math.log2(math.e)


def _mla_proj_kernel(h_ref, wcq_ref, bcq_ref, wkv_ref, bkv_ref, wkr_ref, bkr_ref, qg_ref, kvg_ref, wq_ref, wqr_ref,
                     wk_ref, wvt_ref, e_ref, tq_c_ref, tq_s_ref, tk_ref, q_ref, k_ref, vt_ref, wcq_bf, wkv_bf):
    _cast_once(wcq_ref, wcq_bf)
    _cast_once(wkv_ref, wkv_bf)
    h = h_ref[...]
    cq = _dot(h, wcq_bf[...]) + bcq_ref[0]
    cq = cq * lax.rsqrt(jnp.mean(cq * cq, axis=-1, keepdims=True) + LN_EPS) * qg_ref[...]
    cq = cq.astype(BF16)
    cos_t = jnp.concatenate([tq_c_ref[...]] * HEADS, axis=-1)
    sin_t = jnp.concatenate([tq_s_ref[...]] * HEADS, axis=-1)
    q = (_dot(cq, wq_ref[...]) * cos_t + _dot(cq, wqr_ref[...]) * sin_t) * (MLA_SCALE * LOG2E)
    q_ref[...] = q.astype(BF16)
    ckv = _dot(h, wkv_bf[...]) + bkv_ref[0]
    ckv = ckv * lax.rsqrt(jnp.mean(ckv * ckv, axis=-1, keepdims=True) + LN_EPS) * kvg_ref[...]
    ckv = ckv.astype(BF16)
    kr = ((_dot(h, wkr_ref[...]) + bkr_ref[...]) * tk_ref[...]).astype(BF16)
    k_ref[...] = (_dot(ckv, wk_ref[...]) + _dot(kr, e_ref[...])).astype(BF16)
    vt = lax.dot_general(wvt_ref[...], ckv, (((1,), (1,)), ((), ())), preferred_element_type=F32)
    ones_rows = (lax.broadcasted_iota(jnp.int32, (VT_ROWS - V_HEAD, RB), 0) == 0).astype(BF16)
    for hd in range(HEADS):
        vt_ref[0, hd * VT_ROWS:hd * VT_ROWS + V_HEAD, :] = vt[hd * V_HEAD:(hd + 1) * V_HEAD, :].astype(BF16)
        vt_ref[0, hd * VT_ROWS + V_HEAD:(hd + 1) * VT_ROWS, :] = ones_rows


def mla_proj(l, h, w_in, b_in, wkr, bkr, qg, kvg, wq, wqr, wk, wvt, e, tq_c, tq_s, tk):
    const = lambda i: (0, 0)
    row = lambda i: (i, 0)
    return pl.pallas_call(
        _mla_proj_kernel,
        grid=(NRB,),
        in_specs=[pl.BlockSpec((RB, D), row),
                  _w_in_spec(l, Q_LORA, COL_D // Q_LORA),
                  _b_in_spec(l, Q_LORA, COL_D // Q_LORA),
                  _w_in_spec(l, KV_LORA, COL_KV // KV_LORA),
                  _b_in_spec(l, KV_LORA, COL_KV // KV_LORA),
                  pl.BlockSpec((D, LANE), const),
                  pl.BlockSpec((1, LANE), const),
                  pl.BlockSpec((1, Q_LORA), const),
                  pl.BlockSpec((1, KV_LORA), const),
                  pl.BlockSpec((Q_LORA, HEADS * HEAD_PAD), const),
                  pl.BlockSpec((Q_LORA, HEADS * HEAD_PAD), const),
                  pl.BlockSpec((KV_LORA, HEADS * HEAD_PAD), const),
                  pl.BlockSpec((HEADS * V_HEAD, KV_LORA), const),
                  pl.BlockSpec((LANE, HEADS * HEAD_PAD), const),
                  pl.BlockSpec((RB, HEAD_PAD), row),
                  pl.BlockSpec((RB, HEAD_PAD), row),
                  pl.BlockSpec((RB, LANE), row)],
        out_specs=[pl.BlockSpec((RB, HEADS * HEAD_PAD), row),
                   pl.BlockSpec((RB, HEADS * HEAD_PAD), row),
                   pl.BlockSpec((1, HEADS * VT_ROWS, RB), lambda i: (i, 0, 0))],
        out_shape=[jax.ShapeDtypeStruct((T, HEADS * HEAD_PAD), BF16),
                   jax.ShapeDtypeStruct((T, HEADS * HEAD_PAD), BF16),
                   jax.ShapeDtypeStruct((NRB, HEADS * VT_ROWS, RB), BF16)],
        scratch_shapes=[pltpu.VMEM((D, Q_LORA), BF16), pltpu.VMEM((D, KV_LORA), BF16)],
        compiler_params=_cparams(("arbitrary",)),
        name="mla_proj",
    )(h, w_in, b_in, w_in, b_in, wkr, bkr, qg, kvg, wq, wqr, wk, wvt, e, tq_c, tq_s, tk)


KV_TILE = RB
KV_STAGE = 1024
N_STAGE = SEQ // KV_STAGE
TILES_PER_STAGE = KV_STAGE // KV_TILE


HEAD_GROUP = 2


def _attn_kernel(q_ref, k_ref, vt_ref, o_ref, sa_ref, sb_ref, m_ref, acc_ref):
    i = pl.program_id(1)
    heads = range(HEAD_GROUP)
    qs = [q_ref[:, g * HEAD_PAD:(g + 1) * HEAD_PAD] for g in heads]

    def scores(g, start, n):
        kt = k_ref[pl.ds(start, n), g * HEAD_PAD:(g + 1) * HEAD_PAD]
        return lax.dot_general(kt, qs[g], (((1,), (1,)), ((), ())),
                               preferred_element_type=F32)

    def values(g, tile):
        return vt_ref[tile, g * VT_ROWS:(g + 1) * VT_ROWS, :]

    def softmax_pv(g, s_ref, tile0, m, acc):
        m_new = jnp.maximum(m, jnp.max(s_ref[g], axis=0, keepdims=True))
        p = jnp.exp2(s_ref[g] - m_new).astype(BF16)
        acc = jnp.exp2(m - m_new) * acc
        for u in range(TILES_PER_STAGE):
            acc = acc + _dot(values(g, tile0 + u), p[u * KV_TILE:(u + 1) * KV_TILE, :])
        return m_new, acc

    for g in heads:
        s0 = scores(g, 0, N_CTX)
        m0 = jnp.max(s0, axis=0, keepdims=True)
        m_ref[g] = m0
        acc_ref[g] = _dot(values(g, 0), jnp.exp2(s0 - m0).astype(BF16))

    @pl.when(i > 0)
    def _():
        first_tile = N_CTX // KV_TILE
        for g in heads:
            sa_ref[g] = scores(g, N_CTX, KV_STAGE)

        def body(j, carry):
            ms, accs = list(carry[0]), list(carry[1])
            base = pl.multiple_of(N_CTX + 2 * j * KV_STAGE, KV_TILE)
            tile = first_tile + 2 * j * TILES_PER_STAGE
            for g in heads:
                sb_ref[g] = scores(g, base + KV_STAGE, KV_STAGE)
            for g in heads:
                ms[g], accs[g] = softmax_pv(g, sa_ref, tile, ms[g], accs[g])
            for g in heads:
                sa_ref[g] = scores(g, base + 2 * KV_STAGE, KV_STAGE)
            for g in heads:
                ms[g], accs[g] = softmax_pv(g, sb_ref, tile + TILES_PER_STAGE, ms[g], accs[g])
            return tuple(ms), tuple(accs)

        init = (tuple(m_ref[g] for g in heads), tuple(acc_ref[g] for g in heads))
        ms, accs = lax.fori_loop(0, N_STAGE // 2 - 1, body, init)
        ms, accs = list(ms), list(accs)
        for g in heads:
            sb_ref[g] = scores(g, N_CTX + (N_STAGE - 1) * KV_STAGE, KV_STAGE)
        for g in heads:
            ms[g], accs[g] = softmax_pv(g, sa_ref, first_tile + (N_STAGE - 2) * TILES_PER_STAGE, ms[g], accs[g])
        for g in heads:
            ms[g], accs[g] = softmax_pv(g, sb_ref, first_tile + (N_STAGE - 1) * TILES_PER_STAGE, ms[g], accs[g])
            acc_ref[g] = accs[g]

    for g in heads:
        acc = acc_ref[g]
        o_ref[g * V_HEAD:(g + 1) * V_HEAD, :] = acc[:V_HEAD, :] / acc[V_HEAD:V_HEAD + 1, :]


def attention(q, k, vt):
    return pl.pallas_call(
        _attn_kernel,
        grid=(HEADS // HEAD_GROUP, NRB),
        in_specs=[pl.BlockSpec((RB, HEAD_GROUP * HEAD_PAD), lambda h, i: (i, h)),
                  pl.BlockSpec((T, HEAD_GROUP * HEAD_PAD), lambda h, i: (0, h)),
                  pl.BlockSpec((NRB, HEAD_GROUP * VT_ROWS, KV_TILE), lambda h, i: (0, h, 0))],
        out_specs=pl.BlockSpec((HEAD_GROUP * V_HEAD, RB), lambda h, i: (h, i)),
        out_shape=jax.ShapeDtypeStruct((HEADS * V_HEAD, T), F32),
        scratch_shapes=[pltpu.VMEM((HEAD_GROUP, KV_STAGE, RB), F32),
                        pltpu.VMEM((HEAD_GROUP, KV_STAGE, RB), F32),
                        pltpu.VMEM((HEAD_GROUP, 1, RB), F32),
                        pltpu.VMEM((HEAD_GROUP, VT_ROWS, RB), F32)],
        compiler_params=_cparams(("arbitrary", "arbitrary")),
        name="attention",
    )(q, k, vt)


MERGE_TN = 512


N_BRANCH = 4


def _merge_kernel(h_ref, a_ref, b_ref, c_ref, dt_ref, *refs):
    wg_refs, bg_refs = refs[:N_BRANCH], refs[N_BRANCH:2 * N_BRANCH]
    wb_ref, y_ref = refs[2 * N_BRANCH:]
    h = h_ref[...]
    branches = (a_ref[...], b_ref[...], c_ref[...].astype(BF16), dt_ref[...].T.astype(BF16))
    y = jnp.zeros((RB, MERGE_TN), F32)
    for n, br in enumerate(branches):
        gate = jax.nn.sigmoid(_dot(h, wg_refs[n][...]) + bg_refs[n][...])
        y = y + gate * _dot(br, wb_ref[0, n])
    y_ref[...] = y.astype(BF16)


def merge(l, h, br_a, br_b, br_c, br_dt, wg, bg, wb):
    row = lambda j, i: (i, 0)
    n_col = D // MERGE_TN
    gate_specs = [pl.BlockSpec((D, MERGE_TN), functools.partial(lambda j, i, n: (0, n * n_col + j), n=n))
                  for n in range(N_BRANCH)]
    bias_specs = [pl.BlockSpec((1, MERGE_TN), functools.partial(lambda j, i, n: (0, n * n_col + j), n=n))
                  for n in range(N_BRANCH)]
    return pl.pallas_call(
        _merge_kernel,
        grid=(n_col, NRB),
        in_specs=[pl.BlockSpec((RB, D), row),
                  pl.BlockSpec((RB, BW), row),
                  pl.BlockSpec((RB, BW), row),
                  pl.BlockSpec((RB, BW), row),
                  pl.BlockSpec((BW, RB), lambda j, i: (0, i))]
                 + gate_specs + bias_specs
                 + [pl.BlockSpec((1, N_BRANCH, BW, MERGE_TN), lambda j, i: (l, 0, 0, j))],
        out_specs=pl.BlockSpec((RB, MERGE_TN), lambda j, i: (i, j)),
        out_shape=jax.ShapeDtypeStruct((T, D), BF16),
        compiler_params=_cparams(("arbitrary", "arbitrary")),
        name="merge",
    )(h, br_a, br_b, br_c, br_dt, *([wg] * N_BRANCH), *([bg] * N_BRANCH), wb)


ROUTER_W = 128


def _post_mix_kernel(y_ref, x_ref, mod_ref, wo_ref, g_ref, b_ref, wr_ref, br_ref,
                     x1_ref, h2_ref, ids_ref, gates_ref):
    mix = _dot(y_ref[...], wo_ref[0])
    x1 = _ln0(ALPHA * x_ref[...] + _mod_row(mod_ref, 2) * mix) * g_ref[...] + b_ref[...]
    x1_ref[...] = x1
    h2 = _ln0(x1) * (1.0 + _mod_row(mod_ref, 4)) + _mod_row(mod_ref, 3)
    _rows_to_slabs(h2, h2_ref)
    logits = _dot3(h2, wr_ref) + br_ref[...]
    lane = lax.broadcasted_iota(jnp.int32, (RB, ROUTER_W), 1).astype(F32)
    neg = jnp.float32(-jnp.inf)
    big = jnp.float32(ROUTER_W)
    gl = jnp.where(lane < N_GROUPS, logits, neg)
    gmax = jnp.max(gl, axis=-1, keepdims=True)
    g_idx = jnp.min(jnp.where(gl == gmax, lane, big), axis=-1, keepdims=True)
    g_p = 1.0 / jnp.sum(jnp.exp(gl - gmax), axis=-1, keepdims=True)
    e_lane = lane - N_GROUPS
    in_grp = jnp.logical_and(e_lane >= g_idx * EXP_PER_GROUP, e_lane < (g_idx + 1) * EXP_PER_GROUP)
    el = jnp.where(in_grp, logits, neg)
    v1 = jnp.max(el, axis=-1, keepdims=True)
    i1 = jnp.min(jnp.where(el == v1, lane, big), axis=-1, keepdims=True)
    el2 = jnp.where(lane == i1, neg, el)
    v2 = jnp.max(el2, axis=-1, keepdims=True)
    i2 = jnp.min(jnp.where(el2 == v2, lane, big), axis=-1, keepdims=True)
    e21 = jnp.exp(v2 - v1)
    p1 = 1.0 / (1.0 + e21)
    p2 = e21 * p1
    ids_ref[...] = jnp.where(lane == 0, i1 - N_GROUPS, jnp.where(lane == 1, i2 - N_GROUPS, 0.0)).astype(jnp.int32)
    gates_ref[...] = jnp.where(lane == 0, p1 * g_p, jnp.where(lane == 1, p2 * g_p, 0.0))


def post_mix(l, y, x, mod, wo, ln_g, ln_b, wr, br):
    row = lambda i: (i, 0)
    const = lambda i: (0, 0)
    return pl.pallas_call(
        _post_mix_kernel,
        grid=(NRB,),
        in_specs=[pl.BlockSpec((RB, D), row),
                  pl.BlockSpec((RB, D), row),
                  pl.BlockSpec((8, 6 * D), const),
                  pl.BlockSpec((1, D, D), lambda i: (l, 0, 0)),
                  pl.BlockSpec((1, D), const),
                  pl.BlockSpec((1, D), const),
                  pl.BlockSpec((2, D, ROUTER_W), lambda i: (0, 0, 0)),
                  pl.BlockSpec((1, ROUTER_W), const)],
        out_specs=[pl.BlockSpec((RB, D), row),
                   pl.BlockSpec((RB * ROW_SUB, LANE), row),
                   pl.BlockSpec((RB, ROUTER_W), row),
                   pl.BlockSpec((RB, ROUTER_W), row)],
        out_shape=[jax.ShapeDtypeStruct((T, D), F32),
                   jax.ShapeDtypeStruct((T * ROW_SUB, LANE), F32),
                   jax.ShapeDtypeStruct((T, ROUTER_W), jnp.int32),
                   jax.ShapeDtypeStruct((T, ROUTER_W), F32)],
        compiler_params=_cparams(("arbitrary",)),
        name="post_mix",
    )(y, x, mod, wo, ln_g, ln_b, wr, br)


N_ROWS = -(-(T * TOP_K + N_EXPERTS * (MOE_BLOCK - 1)) // MOE_BLOCK) * MOE_BLOCK
N_BLK = N_ROWS // MOE_BLOCK


def _row_gather(src_hbm, idx_ref, base, n, dst_ref, sem):
    def issue(r, _):
        src_row = pl.multiple_of(idx_ref[base + r] * ROW_SUB, ROW_SUB)
        dst_row = pl.multiple_of(r * ROW_SUB, ROW_SUB)
        pltpu.make_async_copy(src_hbm.at[pl.ds(src_row, ROW_SUB), :],
                              dst_ref.at[pl.ds(dst_row, ROW_SUB), :], sem).start()
        return 0
    lax.fori_loop(0, n, issue, 0)


def _row_gather_wait(src_hbm, n, dst_ref, sem):
    pltpu.make_async_copy(src_hbm.at[pl.ds(0, n * ROW_SUB), :], dst_ref, sem).wait()


def _rows_from_slabs(buf_ref, start, n, stride):
    return jnp.concatenate([buf_ref[pl.ds(start + c, n, stride=stride), :] for c in range(ROW_SUB)], axis=-1)


def _rows_to_slabs(val, out_ref):
    n = val.shape[0]
    for c in range(ROW_SUB):
        out_ref[pl.ds(c, n, stride=ROW_SUB), :] = val[:, c * LANE:(c + 1) * LANE]


def _expert_kernel(blk_exp_ref, row_tok_ref, h_hbm, w1_ref, w3_ref, w2_ref, y_ref, xbuf, sems):
    b = pl.program_id(0)
    slot = b % 2

    @pl.when(b == 0)
    def _():
        _row_gather(h_hbm, row_tok_ref, 0, MOE_BLOCK, xbuf.at[0], sems.at[0])

    @pl.when(b + 1 < N_BLK)
    def _():
        _row_gather(h_hbm, row_tok_ref, (b + 1) * MOE_BLOCK, MOE_BLOCK, xbuf.at[1 - slot], sems.at[1 - slot])

    _row_gather_wait(h_hbm, MOE_BLOCK, xbuf.at[slot], sems.at[slot])
    xb = _rows_from_slabs(xbuf.at[slot], 0, MOE_BLOCK, ROW_SUB).astype(BF16)
    a = _dot(xb, w1_ref[0, 0])
    g = _dot(xb, w3_ref[0, 0])
    hmid = (a * jax.nn.sigmoid(a) * g).astype(BF16)
    _rows_to_slabs(_dot(hmid, w2_ref[0, 0]), y_ref)


def experts(l, blk_exp, row_tok, h2_slabs, w1, w3, w2):
    grid_spec = pltpu.PrefetchScalarGridSpec(
        num_scalar_prefetch=2,
        grid=(N_BLK,),
        in_specs=[pl.BlockSpec(memory_space=pl.ANY),
                  pl.BlockSpec((1, 1, D, D_EXPERT), lambda b, be, rt: (l, be[b], 0, 0)),
                  pl.BlockSpec((1, 1, D, D_EXPERT), lambda b, be, rt: (l, be[b], 0, 0)),
                  pl.BlockSpec((1, 1, D_EXPERT, D), lambda b, be, rt: (l, be[b], 0, 0))],
        out_specs=pl.BlockSpec((MOE_BLOCK * ROW_SUB, LANE), lambda b, be, rt: (b, 0)),
        scratch_shapes=[pltpu.VMEM((2, MOE_BLOCK * ROW_SUB, LANE), F32),
                        pltpu.SemaphoreType.DMA((2,))],
    )
    return pl.pallas_call(
        _expert_kernel,
        grid_spec=grid_spec,
        out_shape=jax.ShapeDtypeStruct((N_ROWS * ROW_SUB, LANE), F32),
        compiler_params=_cparams(("arbitrary",)),
        name="experts",
    )(blk_exp, row_tok, h2_slabs, w1, w3, w2)


def _combine_kernel(pos_ref, y_hbm, gates_ref, x1_ref, mod_ref, g_ref, b_ref, nmod_ref,
                    x2_ref, hn_ref, ybuf, sems):
    i = pl.program_id(0)
    slot = i % 2
    n = RB * TOP_K

    @pl.when(i == 0)
    def _():
        _row_gather(y_hbm, pos_ref, 0, n, ybuf.at[0], sems.at[0])

    @pl.when(i + 1 < NRB)
    def _():
        _row_gather(y_hbm, pos_ref, (i + 1) * n, n, ybuf.at[1 - slot], sems.at[1 - slot])

    _row_gather_wait(y_hbm, n, ybuf.at[slot], sems.at[slot])
    y0 = _rows_from_slabs(ybuf.at[slot], 0, RB, TOP_K * ROW_SUB)
    y1 = _rows_from_slabs(ybuf.at[slot], ROW_SUB, RB, TOP_K * ROW_SUB)
    gates = gates_ref[...]
    f = gates[:, 0:1] * y0 + gates[:, 1:2] * y1
    x2 = _ln0(ALPHA * x1_ref[...] + _mod_row(mod_ref, 5) * f) * g_ref[...] + b_ref[...]
    x2_ref[...] = x2
    hn = _ln0(x2) * (1.0 + _mod_row(nmod_ref, 1)) + _mod_row(nmod_ref, 0)
    hn_ref[...] = hn.astype(BF16)


def combine(pos, y_slabs, gates, x1, mod, ln_g, ln_b, next_mod):
    row = lambda i, p: (i, 0)
    const = lambda i, p: (0, 0)
    grid_spec = pltpu.PrefetchScalarGridSpec(
        num_scalar_prefetch=1,
        grid=(NRB,),
        in_specs=[pl.BlockSpec(memory_space=pl.ANY),
                  pl.BlockSpec((RB, ROUTER_W), row),
                  pl.BlockSpec((RB, D), row),
                  pl.BlockSpec((8, 6 * D), const),
                  pl.BlockSpec((1, D), const),
                  pl.BlockSpec((1, D), const),
                  pl.BlockSpec((8, 6 * D), const)],
        out_specs=[pl.BlockSpec((RB, D), row), pl.BlockSpec((RB, D), row)],
        scratch_shapes=[pltpu.VMEM((2, RB * TOP_K * ROW_SUB, LANE), F32),
                        pltpu.SemaphoreType.DMA((2,))],
    )
    return pl.pallas_call(
        _combine_kernel,
        grid_spec=grid_spec,
        out_shape=[jax.ShapeDtypeStruct((T, D), F32), jax.ShapeDtypeStruct((T, D), BF16)],
        compiler_params=_cparams(("arbitrary",)),
        name="combine",
    )(pos, y_slabs, gates, x1, mod, ln_g, ln_b, next_mod)


def _routing_tables(ids):
    expert = ids[:, :TOP_K].reshape(-1)
    n = expert.shape[0]
    onehot = (expert[:, None] == jnp.arange(N_EXPERTS, dtype=jnp.int32)[None, :]).astype(jnp.int32)
    csum = jnp.cumsum(onehot, axis=0)
    counts = csum[-1]
    rank = jnp.sum(onehot * csum, axis=1) - 1
    padded = (counts + MOE_BLOCK - 1) // MOE_BLOCK * MOE_BLOCK
    pends = jnp.cumsum(padded)
    pstarts = pends - padded
    pos = (jnp.sum(onehot * pstarts[None, :], axis=1) + rank).astype(jnp.int32)
    tok = jnp.arange(n, dtype=jnp.int32) // TOP_K
    row_tok = jnp.zeros((N_ROWS,), jnp.int32).at[pos].set(tok)
    blk_start = jnp.arange(N_BLK, dtype=jnp.int32) * MOE_BLOCK
    blk_exp = jnp.minimum(jnp.sum(pends[None, :] <= blk_start[:, None], axis=1), N_EXPERTS - 1).astype(jnp.int32)
    return blk_exp, row_tok, pos


def _rope_tables():
    rows = SEQ // GRID_W
    row = jnp.repeat(jnp.arange(rows, dtype=F32), GRID_W)
    col = jnp.tile(jnp.arange(GRID_W, dtype=F32), rows)
    half = QK_ROPE // 2
    inv = ROPE_THETA ** (-jnp.arange(0, half, 2, dtype=F32) / half)
    ar = row[:, None] * inv
    ac = col[:, None] * inv
    ang = jnp.concatenate([ar, ar, ac, ac], -1)
    cos = jnp.concatenate([jnp.ones((N_CTX, QK_ROPE), F32), jnp.cos(ang)], 0)
    sin = jnp.concatenate([jnp.zeros((N_CTX, QK_ROPE), F32), jnp.sin(ang)], 0)
    one = jnp.ones((T, QK_NOPE), F32)
    zero_n = jnp.zeros((T, QK_NOPE), F32)
    zero_p = jnp.zeros((T, HEAD_PAD - QK_NOPE - QK_ROPE), F32)
    tq_c = jnp.concatenate([one, cos, zero_p], -1)
    tq_s = jnp.concatenate([zero_n, sin, zero_p], -1)
    tk = jnp.concatenate([cos, sin, jnp.zeros((T, LANE - 2 * QK_ROPE), F32)], -1)
    return tq_c, tq_s, tk


_ROT_SRC = np.concatenate([np.arange(8, 16), np.arange(0, 8), np.arange(24, 32), np.arange(16, 24)])
_ROT_SIGN = np.concatenate([-np.ones(8), np.ones(8), -np.ones(8), np.ones(8)]).astype(np.float32)


def _rot_cols(w):
    return w[..., _ROT_SRC] * _ROT_SIGN


def _layer_weights(l, w_in, b_in, d_w_uq, d_w_uk, d_w_uv, r_w_grp, r_b_grp, r_w_exp, r_b_exp, a_b_s, b_w_dw):
    wi, bi = w_in[l], b_in[l]
    p = {}
    w_kr, b_kr = wi[:, COL_KR:COL_G], bi[COL_KR:COL_G]
    padw = LANE - 2 * QK_ROPE
    p["wkr"] = jnp.concatenate([w_kr, _rot_cols(w_kr), jnp.zeros((D, padw), F32)], -1).astype(BF16)
    p["bkr"] = jnp.concatenate([b_kr, _rot_cols(b_kr), jnp.zeros((padw,), F32)])[None]
    wq = d_w_uq[l].reshape(Q_LORA, HEADS, QK_NOPE + QK_ROPE)
    zpad = jnp.zeros((Q_LORA, HEADS, HEAD_PAD - QK_NOPE - QK_ROPE), F32)
    p["wq"] = jnp.concatenate([wq, zpad], -1).reshape(Q_LORA, HEADS * HEAD_PAD).astype(BF16)
    p["wqr"] = jnp.concatenate([jnp.zeros((Q_LORA, HEADS, QK_NOPE), F32), _rot_cols(wq[..., QK_NOPE:]), zpad],
                               -1).reshape(Q_LORA, HEADS * HEAD_PAD).astype(BF16)
    wk = d_w_uk[l].reshape(KV_LORA, HEADS, QK_NOPE)
    p["wk"] = jnp.concatenate([wk, jnp.zeros((KV_LORA, HEADS, HEAD_PAD - QK_NOPE), F32)],
                              -1).reshape(KV_LORA, HEADS * HEAD_PAD).astype(BF16)
    p["wvt"] = d_w_uv[l].T.astype(BF16)
    e = np.zeros((LANE, HEADS, HEAD_PAD), np.float32)
    for j in range(QK_ROPE):
        e[j, :, QK_NOPE + j] = 1.0
        e[QK_ROPE + j, :, QK_NOPE + j] = 1.0
    p["e"] = jnp.asarray(e.reshape(LANE, HEADS * HEAD_PAD), dtype=BF16)
    p["wg"] = wi[:, COL_G:].astype(BF16)
    p["bg"] = bi[None, COL_G:]
    padr = ROUTER_W - N_GROUPS - N_EXPERTS
    p["wr"] = _hilo(jnp.concatenate([r_w_grp[l], r_w_exp[l], jnp.zeros((D, padr), F32)], -1))
    p["br"] = jnp.concatenate([r_b_grp[l], r_b_exp[l], jnp.zeros((padr,), F32)])[None]
    p["bs"] = jnp.broadcast_to(a_b_s[l][:, :, None], (A_GROUPS, CHUNK, CHUNK))
    p["wdw"] = jnp.concatenate([b_w_dw[l], jnp.zeros((1, BW), F32)], 0)
    return p


def kernel(x, c, ctx, c_ctx, w_ada, b_ada, w_in, b_in, a_ln_g, a_ln_b, a_w_s, a_b_s, b_w_dw, b_b_dw, b_ln_g,
           b_ln_b, d_q_g, d_w_uq, d_kv_g, d_w_uk, d_w_uv, w_branch, w_o, ln1_g, ln1_b, r_w_grp, r_b_grp,
           r_w_exp, r_b_exp, e_w1, e_w3, e_w2, ln2_g, ln2_b):
    assert x.shape == (1, SEQ, D) and ctx.shape == (1, N_CTX, D)
    xs = jnp.concatenate([ctx[0], x[0]], axis=0)
    c_rep = jnp.broadcast_to(jnp.concatenate([c, c_ctx[None]], axis=0)[:, :, None], (N_COND, D, LANE))
    mods = ada_mod(c_rep, w_ada, b_ada)
    tq_c, tq_s, tk = _rope_tables()
    fc = _fourier_consts()
    b_in3 = b_in.reshape(DEPTH, 1, -1)
    w_br, w_ob = w_branch.astype(BF16), w_o.astype(BF16)
    e1, e3, e2 = e_w1.astype(BF16), e_w3.astype(BF16), e_w2.astype(BF16)

    h = modln(xs, mods[0])
    for l in range(DEPTH):
        p = _layer_weights(l, w_in, b_in, d_w_uq, d_w_uk, d_w_uv, r_w_grp, r_b_grp, r_w_exp, r_b_exp, a_b_s, b_w_dw)
        mod = mods[l]
        br_a = mix_a(l, h, w_in, b_in3, a_ln_g[l][None], a_ln_b[l][None], a_w_s[l].astype(BF16), p["bs"])
        br_b = conv_ln(glu(l, h, w_in, b_in3), p["wdw"], b_b_dw[l][None], b_ln_g[l][None], b_ln_b[l][None])
        br_c = fourier_mix(l, h, w_in, b_in3, fc)
        q, k, vt = mla_proj(l, h, w_in, b_in3, p["wkr"], p["bkr"], d_q_g[l][None], d_kv_g[l][None], p["wq"],
                            p["wqr"], p["wk"], p["wvt"], p["e"], tq_c, tq_s, tk)
        br_dt = attention(q, k, vt)
        y = merge(l, h, br_a, br_b, br_c, br_dt, p["wg"], p["bg"], w_br)
        x1, h2, ids, gates = post_mix(l, y, xs, mod, w_ob, ln1_g[l][None], ln1_b[l][None], p["wr"], p["br"])
        blk_exp, row_tok, pos = _routing_tables(ids)
        ys = experts(l, blk_exp, row_tok, h2, e1, e3, e2)
        xs, h = combine(pos, ys, gates, x1, mod, ln2_g[l][None], ln2_b[l][None],
                        mods[min(l + 1, DEPTH - 1)])
    return xs[N_CTX:][None]
```

```python
import functools
import math

import numpy as np
import jax
import jax.numpy as jnp
from jax import lax
from jax.experimental import pallas as pl
from jax.experimental.pallas import tpu as pltpu

F32 = jnp.float32
BF16 = jnp.bfloat16

D = 2048
SEQ = 8192
N_CTX = 256
T = N_CTX + SEQ
DEPTH = 2
GRID_W = 64
BW = 512
CHUNK = 128
A_GROUPS = 4
CONV_W = 31
F_GROUPS = 4
QK_NOPE = 64
QK_ROPE = 32
V_HEAD = 64
HEADS = 8
Q_LORA = 512
KV_LORA = 256
ROPE_THETA = 10000.0
MLA_SCALE = (QK_NOPE + QK_ROPE) ** -0.5
N_GROUPS = 4
EXP_PER_GROUP = 8
N_EXPERTS = 32
TOP_K = 2
D_EXPERT = 1024
MOE_BLOCK = 256
COL_A = 0
COL_B = 1024
COL_C = 2048
COL_D = 2560
COL_KV = 3072
COL_KR = 3328
COL_G = 3360
ALPHA = (2 * DEPTH) ** 0.25
LN_EPS = 1e-6

LANE = 128
HEAD_PAD = 128
RB = 256
NRB = T // RB
FFT_N1 = 64
FFT_N2 = 128
ROW_SUB = D // LANE
VMEM_LIMIT = 56 * 1024 * 1024


def _cparams(sem):
    return pltpu.CompilerParams(dimension_semantics=sem, vmem_limit_bytes=VMEM_LIMIT)


def _dot(a, b):
    return jnp.dot(a, b, preferred_element_type=F32)


def _split(x):
    hi = lax.bitcast_convert_type(lax.bitcast_convert_type(x, jnp.int32) & jnp.int32(-65536), F32)
    return hi.astype(BF16), (x - hi).astype(BF16)


def _hilo(x):
    return jnp.stack(_split(jnp.asarray(x, F32)))


def _dot3(a, t_ref):
    a_hi, a_lo = _split(a)
    return _dot(a_hi, t_ref[0]) + (_dot(a_hi, t_ref[1]) + _dot(a_lo, t_ref[0]))


def _dot3_t(t_ref, b):
    b_hi, b_lo = _split(b)
    return _dot(t_ref[0], b_hi) + (_dot(t_ref[1], b_hi) + _dot(t_ref[0], b_lo))


def _ln0(x):
    mu = jnp.mean(x, axis=-1, keepdims=True)
    xc = x - mu
    var = jnp.mean(xc * xc, axis=-1, keepdims=True)
    return xc * lax.rsqrt(var + LN_EPS)


def _mod_row(mod_ref, q):
    lat = mod_ref[0:1, q * D:(q + 1) * D]
    ctx = mod_ref[1:2, q * D:(q + 1) * D]
    return jnp.where(pl.program_id(0) == 0, ctx, lat)


ADA_TN = 1536
ADA_TK = 256
N_COND = 2


def _ada_kernel(c_ref, w_ref, b_ref, o_ref):
    accs = [b_ref[0] for _ in range(N_COND)]
    for k0 in range(0, D, ADA_TK):
        w = w_ref[0, k0:k0 + ADA_TK, :]
        for r in range(N_COND):
            s = c_ref[r, k0:k0 + ADA_TK, :]
            s = s * jax.nn.sigmoid(s)
            s = jnp.concatenate([s] * (ADA_TN // LANE), axis=-1)
            accs[r] = accs[r] + jnp.sum(w * s, axis=0, keepdims=True)
    o_ref[0] = jnp.concatenate(accs + [jnp.zeros((8 - N_COND, ADA_TN), F32)], axis=0)


def ada_mod(c_rep, w_ada, b_ada):
    return pl.pallas_call(
        _ada_kernel,
        grid=(DEPTH, 6 * D // ADA_TN),
        in_specs=[pl.BlockSpec((N_COND, D, LANE), lambda l, j: (0, 0, 0)),
                  pl.BlockSpec((1, D, ADA_TN), lambda l, j: (l, 0, j)),
                  pl.BlockSpec((1, 1, ADA_TN), lambda l, j: (l, 0, j))],
        out_specs=pl.BlockSpec((1, 8, ADA_TN), lambda l, j: (l, 0, j)),
        out_shape=jax.ShapeDtypeStruct((DEPTH, 8, 6 * D), F32),
        compiler_params=_cparams(("arbitrary", "arbitrary")),
        name="ada_mod",
    )(c_rep, w_ada, b_ada.reshape(DEPTH, 1, 6 * D))


def _stream_specs():
    return [pl.BlockSpec((RB, D), lambda i, *_: (0, 0)),
            pl.BlockSpec((RB, D), lambda i, *_: (jnp.maximum(i - 1, 0), 0))]


def _stream_block(ctx_ref, lat_ref):
    return jnp.where(pl.program_id(0) == 0, ctx_ref[...], lat_ref[...])


def _modln_kernel(xc_ref, xl_ref, mod_ref, h_ref):
    h = _ln0(_stream_block(xc_ref, xl_ref)) * (1.0 + _mod_row(mod_ref, 1)) + _mod_row(mod_ref, 0)
    h_ref[...] = h.astype(BF16)


def modln(x_ctx, x_lat, mod):
    return pl.pallas_call(
        _modln_kernel,
        grid=(NRB,),
        in_specs=_stream_specs() + [pl.BlockSpec((8, 6 * D), lambda i: (0, 0))],
        out_specs=pl.BlockSpec((RB, D), lambda i: (i, 0)),
        out_shape=jax.ShapeDtypeStruct((T, D), BF16),
        compiler_params=_cparams(("arbitrary",)),
        name="modln",
    )(x_ctx, x_lat, mod)


def _w_in_spec(l, width, col):
    return pl.BlockSpec((1, width, D), lambda i: (l, col, 0))


def _b_in_spec(l, width, col):
    return pl.BlockSpec((1, 1, width), lambda i: (l, 0, col))


CAST_CHUNK = 256


def _cast_transposed(w_ref, wbf_ref):
    width = w_ref.shape[0]
    for c0 in range(0, width, CAST_CHUNK):
        wbf_ref[:, c0:c0 + CAST_CHUNK] = w_ref[c0:c0 + CAST_CHUNK, :].T.astype(BF16)


def _cast_once(w_ref, wbf_ref):
    @pl.when(pl.program_id(0) == 0)
    def _():
        _cast_transposed(w_ref.at[0], wbf_ref)


def _mix_a_kernel(h_ref, w_ref, b_ref, g_ref, bb_ref, ws_ref, bs_ref, o_ref, wbf_ref):
    _cast_once(w_ref, wbf_ref)
    uv = jax.nn.gelu(_dot(h_ref[...], wbf_ref[...]) + b_ref[0])
    u = uv[:, :BW]
    v = _ln0(uv[:, BW:]) * g_ref[...] + bb_ref[...]
    v = v.astype(BF16)
    gw = BW // A_GROUPS
    for ch in range(RB // CHUNK):
        rows = slice(ch * CHUNK, (ch + 1) * CHUNK)
        parts = [_dot(ws_ref[g], v[rows, g * gw:(g + 1) * gw]) + bs_ref[g] for g in range(A_GROUPS)]
        mixed = jnp.concatenate(parts, axis=-1)
        o_ref[rows, :] = (u[rows, :] * mixed).astype(BF16)


def mix_a(l, h, w_in, b_in, ln_g, ln_b, ws, bs):
    return pl.pallas_call(
        _mix_a_kernel,
        grid=(NRB,),
        in_specs=[pl.BlockSpec((RB, D), lambda i: (i, 0)),
                  _w_in_spec(l, 2 * BW, COL_A // (2 * BW)),
                  _b_in_spec(l, 2 * BW, COL_A // (2 * BW)),
                  pl.BlockSpec((1, BW), lambda i: (0, 0)),
                  pl.BlockSpec((1, BW), lambda i: (0, 0)),
                  pl.BlockSpec((A_GROUPS, CHUNK, CHUNK), lambda i: (0, 0, 0)),
                  pl.BlockSpec((A_GROUPS, CHUNK, CHUNK), lambda i: (0, 0, 0))],
        out_specs=pl.BlockSpec((RB, BW), lambda i: (i, 0)),
        out_shape=jax.ShapeDtypeStruct((T, BW), BF16),
        scratch_shapes=[pltpu.VMEM((D, 2 * BW), BF16)],
        compiler_params=_cparams(("arbitrary",)),
        name="mix_a",
    )(h, w_in, b_in, ln_g, ln_b, ws, bs)


def _glu_kernel(h_ref, w_ref, b_ref, o_ref, wbf_ref):
    _cast_once(w_ref, wbf_ref)
    ab = _dot(h_ref[...], wbf_ref[...]) + b_ref[0]
    o_ref[...] = ab[:, :BW] * jax.nn.sigmoid(ab[:, BW:])


def glu(l, h, w_in, b_in):
    return pl.pallas_call(
        _glu_kernel,
        grid=(NRB,),
        in_specs=[pl.BlockSpec((RB, D), lambda i: (i, 0)),
                  _w_in_spec(l, 2 * BW, COL_B // (2 * BW)),
                  _b_in_spec(l, 2 * BW, COL_B // (2 * BW))],
        out_specs=pl.BlockSpec((RB, BW), lambda i: (i, 0)),
        out_shape=jax.ShapeDtypeStruct((T, BW), F32),
        scratch_shapes=[pltpu.VMEM((D, 2 * BW), BF16)],
        compiler_params=_cparams(("arbitrary",)),
        name="glu",
    )(h, w_in, b_in)


CONV_HALO = 16


def _conv_kernel(prev_ref, cur_ref, next_ref, w_ref, b_ref, g_ref, bb_ref, o_ref, ext_ref):
    i = pl.program_id(0)
    has_prev = i >= 2
    has_next = jnp.logical_and(i >= 1, i < NRB - 1)
    ext_ref[0:CONV_HALO, :] = jnp.where(has_prev, prev_ref[RB - CONV_HALO:RB, :], 0.0)
    ext_ref[CONV_HALO:CONV_HALO + RB, :] = cur_ref[...]
    ext_ref[CONV_HALO + RB:2 * CONV_HALO + RB, :] = jnp.where(has_next, next_ref[0:CONV_HALO, :], 0.0)
    off = CONV_HALO - CONV_W // 2
    acc = jnp.zeros((RB, BW), F32)
    for k in range(CONV_W):
        acc = acc + ext_ref[off + k:off + k + RB, :] * w_ref[k:k + 1, :]
    y = _ln0(acc + b_ref[...]) * g_ref[...] + bb_ref[...]
    o_ref[...] = (y * jax.nn.sigmoid(y)).astype(BF16)


def conv_ln(y, w_dw, b_dw, ln_g, ln_b):
    return pl.pallas_call(
        _conv_kernel,
        grid=(NRB,),
        in_specs=[pl.BlockSpec((RB, BW), lambda i: (jnp.maximum(i - 1, 0), 0)),
                  pl.BlockSpec((RB, BW), lambda i: (i, 0)),
                  pl.BlockSpec((RB, BW), lambda i: (jnp.minimum(i + 1, NRB - 1), 0)),
                  pl.BlockSpec((CONV_W + 1, BW), lambda i: (0, 0)),
                  pl.BlockSpec((1, BW), lambda i: (0, 0)),
                  pl.BlockSpec((1, BW), lambda i: (0, 0)),
                  pl.BlockSpec((1, BW), lambda i: (0, 0))],
        out_specs=pl.BlockSpec((RB, BW), lambda i: (i, 0)),
        out_shape=jax.ShapeDtypeStruct((T, BW), BF16),
        scratch_shapes=[pltpu.VMEM((RB + 2 * CONV_HALO, BW), F32)],
        compiler_params=_cparams(("arbitrary",)),
        name="conv_ln",
    )(y, y, y, w_dw, b_dw, ln_g, ln_b)


def _dft_tables(n):
    k = np.arange(n, dtype=np.int64)
    ang = 2.0 * np.pi * ((k[:, None] * k[None, :]) % n).astype(np.float64) / n
    return np.cos(ang), np.sin(ang)


def _fproj_kernel(h_ref, w_ref, b_ref, cs_ref, ar_ref, ai_ref, wbf_ref):
    _cast_once(w_ref, wbf_ref)
    z = _dot(h_ref[...], wbf_ref[...]) + b_ref[0]
    a = _dot3(z, cs_ref)
    ar_ref[...] = a[:, :BW]
    ai_ref[...] = a[:, BW:]


def fourier_proj(l, h, w_in, b_in, cs):
    return pl.pallas_call(
        _fproj_kernel,
        grid=(NRB,),
        in_specs=[pl.BlockSpec((RB, D), lambda i: (i, 0)),
                  _w_in_spec(l, BW, COL_C // BW),
                  _b_in_spec(l, BW, COL_C // BW),
                  pl.BlockSpec((2, BW, 2 * BW), lambda i: (0, 0, 0))],
        out_specs=[pl.BlockSpec((RB, BW), lambda i: (i, 0)),
                   pl.BlockSpec((RB, BW), lambda i: (i, 0))],
        out_shape=[jax.ShapeDtypeStruct((T, BW), F32), jax.ShapeDtypeStruct((T, BW), F32)],
        scratch_shapes=[pltpu.VMEM((D, BW), BF16)],
        compiler_params=_cparams(("arbitrary",)),
        name="fourier_proj",
    )(h, w_in, b_in, cs)


FFT_TILE_N2 = 4


def _fft1_kernel(ar_ref, ai_ref, m_ref, ct_ref, st_ref, br_ref, bi_ref):
    x = jnp.concatenate([ar_ref[...], ai_ref[...]], axis=0)
    b = _dot3_t(m_ref, x)
    b_re, b_im = b[:FFT_N1], b[FFT_N1:]
    reps = BW // LANE
    for j in range(FFT_TILE_N2):
        ct = jnp.concatenate([ct_ref[j]] * reps, axis=-1)
        st = jnp.concatenate([st_ref[j]] * reps, axis=-1)
        lanes = slice(j * BW, (j + 1) * BW)
        br_ref[:, lanes] = b_re[:, lanes] * ct + b_im[:, lanes] * st
        bi_ref[:, lanes] = b_im[:, lanes] * ct - b_re[:, lanes] * st


def fft_stage1(ar2d, ai2d, m1, ct, st):
    tn = FFT_TILE_N2 * BW
    return pl.pallas_call(
        _fft1_kernel,
        grid=(FFT_N2 // FFT_TILE_N2,),
        in_specs=[pl.BlockSpec((FFT_N1, tn), lambda j: (0, j)),
                  pl.BlockSpec((FFT_N1, tn), lambda j: (0, j)),
                  pl.BlockSpec((2, 2 * FFT_N1, 2 * FFT_N1), lambda j: (0, 0, 0)),
                  pl.BlockSpec((FFT_TILE_N2, FFT_N1, LANE), lambda j: (j, 0, 0)),
                  pl.BlockSpec((FFT_TILE_N2, FFT_N1, LANE), lambda j: (j, 0, 0))],
        out_specs=[pl.BlockSpec((FFT_N1, tn), lambda j: (0, j)),
                   pl.BlockSpec((FFT_N1, tn), lambda j: (0, j))],
        out_shape=[jax.ShapeDtypeStruct((FFT_N1, FFT_N2 * BW), F32)] * 2,
        compiler_params=_cparams(("arbitrary",)),
        name="fft_stage1",
    )(ar2d, ai2d, m1, ct, st)


FFT_TILE_K1 = 8


def _fft2_kernel(br_ref, bi_ref, c_ref, s_ref, p_ref, o_ref, *, scale):
    ys = [(_dot3_t(c_ref, br_ref[a]) + _dot3_t(s_ref, bi_ref[a])) * scale for a in range(FFT_TILE_K1)]
    y = jnp.concatenate(ys, axis=0).astype(BF16)
    y = _dot(p_ref[...], y)
    o_ref[...] = y.reshape(FFT_N2, FFT_TILE_K1, BW)


def fft_stage2(br3d, bi3d, c128, s128, perm, scale):
    return pl.pallas_call(
        functools.partial(_fft2_kernel, scale=scale),
        grid=(FFT_N1 // FFT_TILE_K1,),
        in_specs=[pl.BlockSpec((FFT_TILE_K1, FFT_N2, BW), lambda a: (a, 0, 0)),
                  pl.BlockSpec((FFT_TILE_K1, FFT_N2, BW), lambda a: (a, 0, 0)),
                  pl.BlockSpec((2, FFT_N2, FFT_N2), lambda a: (0, 0, 0)),
                  pl.BlockSpec((2, FFT_N2, FFT_N2), lambda a: (0, 0, 0)),
                  pl.BlockSpec((FFT_TILE_K1 * FFT_N2, FFT_TILE_K1 * FFT_N2), lambda a: (0, 0))],
        out_specs=pl.BlockSpec((FFT_N2, FFT_TILE_K1, BW), lambda a: (0, a, 0)),
        out_shape=jax.ShapeDtypeStruct((FFT_N2, FFT_N1, BW), F32),
        compiler_params=_cparams(("arbitrary",)),
        name="fft_stage2",
    )(br3d, bi3d, c128, s128, perm)


def _dft_ctx_kernel(ar_ref, ai_ref, c_ref, s_ref, o_ref, *, scale):
    o_ref[...] = (_dot3_t(c_ref, ar_ref[...]) + _dot3_t(s_ref, ai_ref[...])) * scale


def dft_ctx(ar, ai, c, s, scale):
    return pl.pallas_call(
        functools.partial(_dft_ctx_kernel, scale=scale),
        grid=(1,),
        in_specs=[pl.BlockSpec((N_CTX, BW), lambda i: (0, 0)),
                  pl.BlockSpec((N_CTX, BW), lambda i: (0, 0)),
                  pl.BlockSpec((2, N_CTX, N_CTX), lambda i: (0, 0, 0)),
                  pl.BlockSpec((2, N_CTX, N_CTX), lambda i: (0, 0, 0))],
        out_specs=pl.BlockSpec((N_CTX, BW), lambda i: (0, 0)),
        out_shape=jax.ShapeDtypeStruct((N_CTX, BW), F32),
        compiler_params=_cparams(("arbitrary",)),
        name="dft_ctx",
    )(ar, ai, c, s)


def _fourier_consts():
    c64, s64 = _dft_tables(FFT_N1)
    m1 = np.block([[c64, s64], [-s64, c64]]).astype(np.float32)
    k1 = np.arange(FFT_N1, dtype=np.float64)
    n2 = np.arange(FFT_N2, dtype=np.float64)
    phi = 2.0 * np.pi * n2[:, None] * k1[None, :] / SEQ
    ct = np.repeat(np.cos(phi)[:, :, None], LANE, axis=2).astype(np.float32)
    st = np.repeat(np.sin(phi)[:, :, None], LANE, axis=2).astype(np.float32)
    c128, s128 = _dft_tables(FFT_N2)
    gw = BW // F_GROUPS
    cg, sg = _dft_tables(gw)
    cs = np.zeros((BW, 2 * BW), np.float32)
    for g in range(F_GROUPS):
        sl = slice(g * gw, (g + 1) * gw)
        cs[sl, sl] = cg
        cs[sl, BW + g * gw:BW + (g + 1) * gw] = -sg
    rows = np.arange(FFT_TILE_K1 * FFT_N2)
    perm = np.zeros((rows.size, rows.size), np.float32)
    perm[rows, (rows % FFT_TILE_K1) * FFT_N2 + rows // FFT_TILE_K1] = 1.0
    cc, sc = _dft_tables(N_CTX)
    return dict(m1=_hilo(m1), ct=jnp.asarray(ct), st=jnp.asarray(st), c128=_hilo(c128), s128=_hilo(s128),
                cs=_hilo(cs), perm=jnp.asarray(perm, dtype=BF16), cc=_hilo(cc), sc=_hilo(sc))


def fourier_mix(l, h, w_in, b_in, fc):
    gw = BW // F_GROUPS
    ar, ai = fourier_proj(l, h, w_in, b_in, fc["cs"])
    y_ctx = dft_ctx(ar, ai, fc["cc"], fc["sc"], 1.0 / math.sqrt(N_CTX * gw))
    ar2 = ar[N_CTX:].reshape(FFT_N1, FFT_N2 * BW)
    ai2 = ai[N_CTX:].reshape(FFT_N1, FFT_N2 * BW)
    br, bi = fft_stage1(ar2, ai2, fc["m1"], fc["ct"], fc["st"])
    y = fft_stage2(br.reshape(FFT_N1, FFT_N2, BW), bi.reshape(FFT_N1, FFT_N2, BW),
                   fc["c128"], fc["s128"], fc["perm"], 1.0 / math.sqrt(SEQ * gw))
    return jnp.concatenate([y_ctx, y.reshape(SEQ, BW)], axis=0)


VT_ROWS = 80
LOG2E = math.log2(math.e)


def _mla_proj_kernel(h_ref, wcq_ref, bcq_ref, wkv_ref, bkv_ref, wkr_ref, bkr_ref, qg_ref, kvg_ref, wq_ref, wqr_ref,
                     wk_ref, wvt_ref, e_ref, tq_c_ref, tq_s_ref, tk_ref, q_ref, k_ref, vt_ref, wcq_bf, wkv_bf):
    _cast_once(wcq_ref, wcq_bf)
    _cast_once(wkv_ref, wkv_bf)
    h = h_ref[...]
    cq = _dot(h, wcq_bf[...]) + bcq_ref[0]
    cq = cq * lax.rsqrt(jnp.mean(cq * cq, axis=-1, keepdims=True) + LN_EPS) * qg_ref[...]
    cq = cq.astype(BF16)
    cos_t = jnp.concatenate([tq_c_ref[...]] * HEADS, axis=-1)
    sin_t = jnp.concatenate([tq_s_ref[...]] * HEADS, axis=-1)
    q = (_dot(cq, wq_ref[...]) * cos_t + _dot(cq, wqr_ref[...]) * sin_t) * (MLA_SCALE * LOG2E)
    q_ref[...] = q.astype(BF16)
    ckv = _dot(h, wkv_bf[...]) + bkv_ref[0]
    ckv = ckv * lax.rsqrt(jnp.mean(ckv * ckv, axis=-1, keepdims=True) + LN_EPS) * kvg_ref[...]
    ckv = ckv.astype(BF16)
    kr = ((_dot(h, wkr_ref[...]) + bkr_ref[...]) * tk_ref[...]).astype(BF16)
    k_ref[...] = (_dot(ckv, wk_ref[...]) + _dot(kr, e_ref[...])).astype(BF16)
    vt = lax.dot_general(wvt_ref[...], ckv, (((1,), (1,)), ((), ())), preferred_element_type=F32)
    ones_rows = (lax.broadcasted_iota(jnp.int32, (VT_ROWS - V_HEAD, RB), 0) == 0).astype(BF16)
    for hd in range(HEADS):
        vt_ref[0, hd * VT_ROWS:hd * VT_ROWS + V_HEAD, :] = vt[hd * V_HEAD:(hd + 1) * V_HEAD, :].astype(BF16)
        vt_ref[0, hd * VT_ROWS + V_HEAD:(hd + 1) * VT_ROWS, :] = ones_rows


def mla_proj(l, h, w_in, b_in, wkr, bkr, qg, kvg, wq, wqr, wk, wvt, e, tq_c, tq_s, tk):
    const = lambda i: (0, 0)
    row = lambda i: (i, 0)
    return pl.pallas_call(
        _mla_proj_kernel,
        grid=(NRB,),
        in_specs=[pl.BlockSpec((RB, D), row),
                  _w_in_spec(l, Q_LORA, COL_D // Q_LORA),
                  _b_in_spec(l, Q_LORA, COL_D // Q_LORA),
                  _w_in_spec(l, KV_LORA, COL_KV // KV_LORA),
                  _b_in_spec(l, KV_LORA, COL_KV // KV_LORA),
                  pl.BlockSpec((D, LANE), const),
                  pl.BlockSpec((1, LANE), const),
                  pl.BlockSpec((1, Q_LORA), const),
                  pl.BlockSpec((1, KV_LORA), const),
                  pl.BlockSpec((Q_LORA, HEADS * HEAD_PAD), const),
                  pl.BlockSpec((Q_LORA, HEADS * HEAD_PAD), const),
                  pl.BlockSpec((KV_LORA, HEADS * HEAD_PAD), const),
                  pl.BlockSpec((HEADS * V_HEAD, KV_LORA), const),
                  pl.BlockSpec((LANE, HEADS * HEAD_PAD), const),
                  pl.BlockSpec((RB, HEAD_PAD), row),
                  pl.BlockSpec((RB, HEAD_PAD), row),
                  pl.BlockSpec((RB, LANE), row)],
        out_specs=[pl.BlockSpec((RB, HEADS * HEAD_PAD), row),
                   pl.BlockSpec((RB, HEADS * HEAD_PAD), row),
                   pl.BlockSpec((1, HEADS * VT_ROWS, RB), lambda i: (i, 0, 0))],
        out_shape=[jax.ShapeDtypeStruct((T, HEADS * HEAD_PAD), BF16),
                   jax.ShapeDtypeStruct((T, HEADS * HEAD_PAD), BF16),
                   jax.ShapeDtypeStruct((NRB, HEADS * VT_ROWS, RB), BF16)],
        scratch_shapes=[pltpu.VMEM((D, Q_LORA), BF16), pltpu.VMEM((D, KV_LORA), BF16)],
        compiler_params=_cparams(("arbitrary",)),
        name="mla_proj",
    )(h, w_in, b_in, w_in, b_in, wkr, bkr, qg, kvg, wq, wqr, wk, wvt, e, tq_c, tq_s, tk)


KV_TILE = RB
KV_STAGE = 1024
N_STAGE = SEQ // KV_STAGE
TILES_PER_STAGE = KV_STAGE // KV_TILE


HEAD_GROUP = 2


def _attn_kernel(q_ref, k_ref, vt_ref, o_ref, sa_ref, sb_ref, m_ref, acc_ref):
    i = pl.program_id(1)
    heads = range(HEAD_GROUP)
    qs = [q_ref[:, g * HEAD_PAD:(g + 1) * HEAD_PAD] for g in heads]

    def scores(g, start, n):
        kt = k_ref[pl.ds(start, n), g * HEAD_PAD:(g + 1) * HEAD_PAD]
        return lax.dot_general(kt, qs[g], (((1,), (1,)), ((), ())),
                               preferred_element_type=F32)

    def values(g, tile):
        return vt_ref[tile, g * VT_ROWS:(g + 1) * VT_ROWS, :]

    def softmax_pv(g, s_ref, tile0, m, acc):
        m_new = jnp.maximum(m, jnp.max(s_ref[g], axis=0, keepdims=True))
        p = jnp.exp2(s_ref[g] - m_new).astype(BF16)
        acc = jnp.exp2(m - m_new) * acc
        for u in range(TILES_PER_STAGE):
            acc = acc + _dot(values(g, tile0 + u), p[u * KV_TILE:(u + 1) * KV_TILE, :])
        return m_new, acc

    for g in heads:
        s0 = scores(g, 0, N_CTX)
        m0 = jnp.max(s0, axis=0, keepdims=True)
        m_ref[g] = m0
        acc_ref[g] = _dot(values(g, 0), jnp.exp2(s0 - m0).astype(BF16))

    @pl.when(i > 0)
    def _():
        first_tile = N_CTX // KV_TILE
        for g in heads:
            sa_ref[g] = scores(g, N_CTX, KV_STAGE)

        def body(j, carry):
            ms, accs = list(carry[0]), list(carry[1])
            base = pl.multiple_of(N_CTX + 2 * j * KV_STAGE, KV_TILE)
            tile = first_tile + 2 * j * TILES_PER_STAGE
            for g in heads:
                sb_ref[g] = scores(g, base + KV_STAGE, KV_STAGE)
            for g in heads:
                ms[g], accs[g] = softmax_pv(g, sa_ref, tile, ms[g], accs[g])
            for g in heads:
                sa_ref[g] = scores(g, base + 2 * KV_STAGE, KV_STAGE)
            for g in heads:
                ms[g], accs[g] = softmax_pv(g, sb_ref, tile + TILES_PER_STAGE, ms[g], accs[g])
            return tuple(ms), tuple(accs)

        init = (tuple(m_ref[g] for g in heads), tuple(acc_ref[g] for g in heads))
        ms, accs = lax.fori_loop(0, N_STAGE // 2 - 1, body, init)
        ms, accs = list(ms), list(accs)
        for g in heads:
            sb_ref[g] = scores(g, N_CTX + (N_STAGE - 1) * KV_STAGE, KV_STAGE)
        for g in heads:
            ms[g], accs[g] = softmax_pv(g, sa_ref, first_tile + (N_STAGE - 2) * TILES_PER_STAGE, ms[g], accs[g])
        for g in heads:
            ms[g], accs[g] = softmax_pv(g, sb_ref, first_tile + (N_STAGE - 1) * TILES_PER_STAGE, ms[g], accs[g])
            acc_ref[g] = accs[g]

    for g in heads:
        acc = acc_ref[g]
        o_ref[g * V_HEAD:(g + 1) * V_HEAD, :] = acc[:V_HEAD, :] / acc[V_HEAD:V_HEAD + 1, :]


def attention(q, k, vt):
    return pl.pallas_call(
        _attn_kernel,
        grid=(HEADS // HEAD_GROUP, NRB),
        in_specs=[pl.BlockSpec((RB, HEAD_GROUP * HEAD_PAD), lambda h, i: (i, h)),
                  pl.BlockSpec((T, HEAD_GROUP * HEAD_PAD), lambda h, i: (0, h)),
                  pl.BlockSpec((NRB, HEAD_GROUP * VT_ROWS, KV_TILE), lambda h, i: (0, h, 0))],
        out_specs=pl.BlockSpec((HEAD_GROUP * V_HEAD, RB), lambda h, i: (h, i)),
        out_shape=jax.ShapeDtypeStruct((HEADS * V_HEAD, T), F32),
        scratch_shapes=[pltpu.VMEM((HEAD_GROUP, KV_STAGE, RB), F32),
                        pltpu.VMEM((HEAD_GROUP, KV_STAGE, RB), F32),
                        pltpu.VMEM((HEAD_GROUP, 1, RB), F32),
                        pltpu.VMEM((HEAD_GROUP, VT_ROWS, RB), F32)],
        compiler_params=_cparams(("arbitrary", "arbitrary")),
        name="attention",
    )(q, k, vt)


MERGE_TN = 512


N_BRANCH = 4


MERGE_RB = 768


def _merge_kernel(h_ref, a_ref, b_ref, c_ref, dt_ref, *refs):
    wg_refs, bg_refs = refs[:N_BRANCH], refs[N_BRANCH:2 * N_BRANCH]
    wb_ref, y_ref, wg_bf = refs[2 * N_BRANCH:]

    @pl.when(pl.program_id(1) == 0)
    def _():
        for n in range(N_BRANCH):
            _cast_transposed(wg_refs[n], wg_bf.at[n])

    h = h_ref[...]
    branches = (a_ref[...], b_ref[...], c_ref[...].astype(BF16), dt_ref[...].T.astype(BF16))
    y = jnp.zeros((MERGE_RB, MERGE_TN), F32)
    for n, br in enumerate(branches):
        gate = jax.nn.sigmoid(_dot(h, wg_bf[n]) + bg_refs[n][...])
        y = y + gate * _dot(br, wb_ref[0, n])
    y_ref[...] = y.astype(BF16)


def merge(l, h, br_a, br_b, br_c, br_dt, w_in_t, bg, wb):
    row = lambda j, i: (i, 0)
    n_col = D // MERGE_TN
    gate_specs = [pl.BlockSpec((pl.Squeezed(), pl.Element(MERGE_TN), pl.Element(D)),
                               functools.partial(
                                   lambda j, i, n: (l, pl.multiple_of(COL_G + n * D + j * MERGE_TN, QK_ROPE), 0), n=n),
                               pipeline_mode=pl.Buffered(1))
                  for n in range(N_BRANCH)]
    bias_specs = [pl.BlockSpec((1, MERGE_TN), functools.partial(lambda j, i, n: (0, n * n_col + j), n=n))
                  for n in range(N_BRANCH)]
    return pl.pallas_call(
        _merge_kernel,
        grid=(n_col, T // MERGE_RB),
        in_specs=[pl.BlockSpec((MERGE_RB, D), row),
                  pl.BlockSpec((MERGE_RB, BW), row),
                  pl.BlockSpec((MERGE_RB, BW), row),
                  pl.BlockSpec((MERGE_RB, BW), row),
                  pl.BlockSpec((BW, MERGE_RB), lambda j, i: (0, i))]
                 + gate_specs + bias_specs
                 + [pl.BlockSpec((1, N_BRANCH, BW, MERGE_TN), lambda j, i: (l, 0, 0, j))],
        out_specs=pl.BlockSpec((MERGE_RB, MERGE_TN), lambda j, i: (i, j)),
        out_shape=jax.ShapeDtypeStruct((T, D), BF16),
        scratch_shapes=[pltpu.VMEM((N_BRANCH, D, MERGE_TN), BF16)],
        compiler_params=_cparams(("arbitrary", "arbitrary")),
        name="merge",
    )(h, br_a, br_b, br_c, br_dt, *([w_in_t] * N_BRANCH), *([bg] * N_BRANCH), wb)


ROUTER_W = 128


def _post_mix_kernel(y_ref, xc_ref, xl_ref, mod_ref, wo_ref, g_ref, b_ref, wr_ref, br_ref,
                     x1_ref, h2_ref, ids_ref, gates_ref):
    mix = _dot(y_ref[...], wo_ref[0])
    x1 = _ln0(ALPHA * _stream_block(xc_ref, xl_ref) + _mod_row(mod_ref, 2) * mix) * g_ref[...] + b_ref[...]
    x1_ref[...] = x1
    h2 = _ln0(x1) * (1.0 + _mod_row(mod_ref, 4)) + _mod_row(mod_ref, 3)
    _rows_to_slabs(h2, h2_ref)
    logits = _dot3(h2, wr_ref) + br_ref[...]
    lane = lax.broadcasted_iota(jnp.int32, (RB, ROUTER_W), 1).astype(F32)
    neg = jnp.float32(-jnp.inf)
    big = jnp.float32(ROUTER_W)
    gl = jnp.where(lane < N_GROUPS, logits, neg)
    gmax = jnp.max(gl, axis=-1, keepdims=True)
    g_idx = jnp.min(jnp.where(gl == gmax, lane, big), axis=-1, keepdims=True)
    g_p = 1.0 / jnp.sum(jnp.exp(gl - gmax), axis=-1, keepdims=True)
    e_lane = lane - N_GROUPS
    in_grp = jnp.logical_and(e_lane >= g_idx * EXP_PER_GROUP, e_lane < (g_idx + 1) * EXP_PER_GROUP)
    el = jnp.where(in_grp, logits, neg)
    v1 = jnp.max(el, axis=-1, keepdims=True)
    i1 = jnp.min(jnp.where(el == v1, lane, big), axis=-1, keepdims=True)
    el2 = jnp.where(lane == i1, neg, el)
    v2 = jnp.max(el2, axis=-1, keepdims=True)
    i2 = jnp.min(jnp.where(el2 == v2, lane, big), axis=-1, keepdims=True)
    e21 = jnp.exp(v2 - v1)
    p1 = 1.0 / (1.0 + e21)
    p2 = e21 * p1
    ids_ref[...] = jnp.where(lane == 0, i1 - N_GROUPS, jnp.where(lane == 1, i2 - N_GROUPS, 0.0)).astype(jnp.int32)
    gates_ref[...] = jnp.where(lane == 0, p1 * g_p, jnp.where(lane == 1, p2 * g_p, 0.0))


def post_mix(l, y, x_ctx, x_lat, mod, wo, ln_g, ln_b, wr, br):
    row = lambda i: (i, 0)
    const = lambda i: (0, 0)
    return pl.pallas_call(
        _post_mix_kernel,
        grid=(NRB,),
        in_specs=[pl.BlockSpec((RB, D), row)] + _stream_specs()
                 + [pl.BlockSpec((8, 6 * D), const),
                  pl.BlockSpec((1, D, D), lambda i: (l, 0, 0)),
                  pl.BlockSpec((1, D), const),
                  pl.BlockSpec((1, D), const),
                  pl.BlockSpec((2, D, ROUTER_W), lambda i: (0, 0, 0)),
                  pl.BlockSpec((1, ROUTER_W), const)],
        out_specs=[pl.BlockSpec((RB, D), row),
                   pl.BlockSpec((RB * ROW_SUB, LANE), row),
                   pl.BlockSpec((RB, ROUTER_W), row),
                   pl.BlockSpec((RB, ROUTER_W), row)],
        out_shape=[jax.ShapeDtypeStruct((T, D), F32),
                   jax.ShapeDtypeStruct((T * ROW_SUB, LANE), F32),
                   jax.ShapeDtypeStruct((T, ROUTER_W), jnp.int32),
                   jax.ShapeDtypeStruct((T, ROUTER_W), F32)],
        compiler_params=_cparams(("arbitrary",)),
        name="post_mix",
    )(y, x_ctx, x_lat, mod, wo, ln_g, ln_b, wr, br)


N_ROWS = -(-(T * TOP_K + N_EXPERTS * (MOE_BLOCK - 1)) // MOE_BLOCK) * MOE_BLOCK
N_BLK = N_ROWS // MOE_BLOCK


GATHER_PRIORITY = 1


def _row_gather(src_hbm, idx_ref, base, n, dst_ref, sem, first=0):
    def issue(r, _):
        src_row = pl.multiple_of(idx_ref[base + r] * ROW_SUB, ROW_SUB)
        dst_row = pl.multiple_of(r * ROW_SUB, ROW_SUB)
        pltpu.make_async_copy(src_hbm.at[pl.ds(src_row, ROW_SUB), :],
                              dst_ref.at[pl.ds(dst_row, ROW_SUB), :], sem).start(priority=GATHER_PRIORITY)
        return 0
    lax.fori_loop(first, first + n, issue, 0)


def _row_gather_wait(src_hbm, n, dst_ref, sem):
    pltpu.make_async_copy(src_hbm.at[pl.ds(0, n * ROW_SUB), :], dst_ref, sem).wait()


def _rows_from_slabs(buf_ref, start, n, stride):
    return jnp.concatenate([buf_ref[pl.ds(start + c, n, stride=stride), :] for c in range(ROW_SUB)], axis=-1)


def _rows_to_slabs(val, out_ref):
    n = val.shape[0]
    for c in range(ROW_SUB):
        out_ref[pl.ds(c, n, stride=ROW_SUB), :] = val[:, c * LANE:(c + 1) * LANE]


def _expert_kernel(blk_exp_ref, row_tok_ref, h_hbm, w1_ref, w3_ref, w2_ref, y_ref, xbuf, sems):
    b = pl.program_id(0)
    slot = b % 2

    @pl.when(b == 0)
    def _():
        _row_gather(h_hbm, row_tok_ref, 0, MOE_BLOCK, xbuf.at[0], sems.at[0])

    @pl.when(b + 1 < N_BLK)
    def _():
        _row_gather(h_hbm, row_tok_ref, (b + 1) * MOE_BLOCK, MOE_BLOCK, xbuf.at[1 - slot], sems.at[1 - slot])

    _row_gather_wait(h_hbm, MOE_BLOCK, xbuf.at[slot], sems.at[slot])
    xb = _rows_from_slabs(xbuf.at[slot], 0, MOE_BLOCK, ROW_SUB).astype(BF16)
    a = _dot(xb, w1_ref[0, 0])
    g = _dot(xb, w3_ref[0, 0])
    hmid = (a * jax.nn.sigmoid(a) * g).astype(BF16)
    _rows_to_slabs(_dot(hmid, w2_ref[0, 0]), y_ref)


def experts(l, blk_exp, row_tok, h2_slabs, w1, w3, w2):
    grid_spec = pltpu.PrefetchScalarGridSpec(
        num_scalar_prefetch=2,
        grid=(N_BLK,),
        in_specs=[pl.BlockSpec(memory_space=pl.ANY),
                  pl.BlockSpec((1, 1, D, D_EXPERT), lambda b, be, rt: (l, be[b], 0, 0)),
                  pl.BlockSpec((1, 1, D, D_EXPERT), lambda b, be, rt: (l, be[b], 0, 0)),
                  pl.BlockSpec((1, 1, D_EXPERT, D), lambda b, be, rt: (l, be[b], 0, 0))],
        out_specs=pl.BlockSpec((MOE_BLOCK * ROW_SUB, LANE), lambda b, be, rt: (b, 0)),
        scratch_shapes=[pltpu.VMEM((2, MOE_BLOCK * ROW_SUB, LANE), F32),
                        pltpu.SemaphoreType.DMA((2,))],
    )
    return pl.pallas_call(
        _expert_kernel,
        grid_spec=grid_spec,
        out_shape=jax.ShapeDtypeStruct((N_ROWS * ROW_SUB, LANE), F32),
        compiler_params=_cparams(("arbitrary",)),
        name="experts",
    )(blk_exp, row_tok, h2_slabs, w1, w3, w2)


COMBINE_SUB = 64


def _combine_kernel(pos_ref, y_hbm, gates_ref, x1_ref, mod_ref, g_ref, b_ref, nmod_ref,
                    x2c_ref, x2l_ref, hn_ref, ybuf, sems):
    i = pl.program_id(0)
    slot = i % 2
    n = RB * TOP_K
    n_sub = RB // COMBINE_SUB

    @pl.when(i == 0)
    def _():
        _row_gather(y_hbm, pos_ref, 0, n, ybuf.at[0], sems.at[0])

    _row_gather_wait(y_hbm, n, ybuf.at[slot], sems.at[slot])
    shift5, shift0, scale1 = _mod_row(mod_ref, 5), _mod_row(nmod_ref, 0), 1.0 + _mod_row(nmod_ref, 1)
    for s in range(n_sub):
        @pl.when(i + 1 < NRB)
        def _():
            _row_gather(y_hbm, pos_ref, (i + 1) * n, n // n_sub, ybuf.at[1 - slot], sems.at[1 - slot],
                        first=s * (n // n_sub))

        rows = slice(s * COMBINE_SUB, (s + 1) * COMBINE_SUB)
        base = s * COMBINE_SUB * TOP_K * ROW_SUB
        y0 = _rows_from_slabs(ybuf.at[slot], base, COMBINE_SUB, TOP_K * ROW_SUB)
        y1 = _rows_from_slabs(ybuf.at[slot], base + ROW_SUB, COMBINE_SUB, TOP_K * ROW_SUB)
        gates = gates_ref[rows, :]
        f = gates[:, 0:1] * y0 + gates[:, 1:2] * y1
        x2 = _ln0(ALPHA * x1_ref[rows, :] + shift5 * f) * g_ref[...] + b_ref[...]
        x2l_ref[rows, :] = x2

        @pl.when(i == 0)
        def _():
            x2c_ref[rows, :] = x2

        hn_ref[rows, :] = (_ln0(x2) * scale1 + shift0).astype(BF16)


def combine(pos, y_slabs, gates, x1, mod, ln_g, ln_b, next_mod):
    row = lambda i, p: (i, 0)
    const = lambda i, p: (0, 0)
    grid_spec = pltpu.PrefetchScalarGridSpec(
        num_scalar_prefetch=1,
        grid=(NRB,),
        in_specs=[pl.BlockSpec(memory_space=pl.ANY),
                  pl.BlockSpec((RB, ROUTER_W), row),
                  pl.BlockSpec((RB, D), row),
                  pl.BlockSpec((8, 6 * D), const),
                  pl.BlockSpec((1, D), const),
                  pl.BlockSpec((1, D), const),
                  pl.BlockSpec((8, 6 * D), const)],
        out_specs=[pl.BlockSpec((RB, D), const),
                   pl.BlockSpec((RB, D), lambda i, p: (jnp.maximum(i - 1, 0), 0)),
                   pl.BlockSpec((RB, D), row)],
        scratch_shapes=[pltpu.VMEM((2, RB * TOP_K * ROW_SUB, LANE), F32),
                        pltpu.SemaphoreType.DMA((2,))],
    )
    return pl.pallas_call(
        _combine_kernel,
        grid_spec=grid_spec,
        out_shape=[jax.ShapeDtypeStruct((N_CTX, D), F32), jax.ShapeDtypeStruct((SEQ, D), F32),
                   jax.ShapeDtypeStruct((T, D), BF16)],
        compiler_params=_cparams(("arbitrary",)),
        name="combine",
    )(pos, y_slabs, gates, x1, mod, ln_g, ln_b, next_mod)


def _routing_tables(ids):
    expert = ids[:, :TOP_K].reshape(-1)
    n = expert.shape[0]
    onehot = (expert[:, None] == jnp.arange(N_EXPERTS, dtype=jnp.int32)[None, :]).astype(jnp.int32)
    csum = jnp.cumsum(onehot, axis=0)
    counts = csum[-1]
    rank = jnp.sum(onehot * csum, axis=1) - 1
    padded = (counts + MOE_BLOCK - 1) // MOE_BLOCK * MOE_BLOCK
    pends = jnp.cumsum(padded)
    pstarts = pends - padded
    pos = (jnp.sum(onehot * pstarts[None, :], axis=1) + rank).astype(jnp.int32)
    tok = jnp.arange(n, dtype=jnp.int32) // TOP_K
    row_tok = jnp.zeros((N_ROWS,), jnp.int32).at[pos].set(tok)
    blk_start = jnp.arange(N_BLK, dtype=jnp.int32) * MOE_BLOCK
    blk_exp = jnp.minimum(jnp.sum(pends[None, :] <= blk_start[:, None], axis=1), N_EXPERTS - 1).astype(jnp.int32)
    return blk_exp, row_tok, pos


def _rope_tables():
    rows = SEQ // GRID_W
    row = jnp.repeat(jnp.arange(rows, dtype=F32), GRID_W)
    col = jnp.tile(jnp.arange(GRID_W, dtype=F32), rows)
    half = QK_ROPE // 2
    inv = ROPE_THETA ** (-jnp.arange(0, half, 2, dtype=F32) / half)
    ar = row[:, None] * inv
    ac = col[:, None] * inv
    ang = jnp.concatenate([ar, ar, ac, ac], -1)
    cos = jnp.concatenate([jnp.ones((N_CTX, QK_ROPE), F32), jnp.cos(ang)], 0)
    sin = jnp.concatenate([jnp.zeros((N_CTX, QK_ROPE), F32), jnp.sin(ang)], 0)
    one = jnp.ones((T, QK_NOPE), F32)
    zero_n = jnp.zeros((T, QK_NOPE), F32)
    zero_p = jnp.zeros((T, HEAD_PAD - QK_NOPE - QK_ROPE), F32)
    tq_c = jnp.concatenate([one, cos, zero_p], -1)
    tq_s = jnp.concatenate([zero_n, sin, zero_p], -1)
    tk = jnp.concatenate([cos, sin, jnp.zeros((T, LANE - 2 * QK_ROPE), F32)], -1)
    return tq_c, tq_s, tk


_ROT_SRC = np.concatenate([np.arange(8, 16), np.arange(0, 8), np.arange(24, 32), np.arange(16, 24)])
_ROT_SIGN = np.concatenate([-np.ones(8), np.ones(8), -np.ones(8), np.ones(8)]).astype(np.float32)


def _rot_cols(w):
    return w[..., _ROT_SRC] * _ROT_SIGN


def _layer_weights(l, w_in_t, b_in, d_w_uq, d_w_uk, d_w_uv, r_w_grp, r_b_grp, r_w_exp, r_b_exp, a_b_s, b_w_dw):
    bi = b_in[l]
    p = {}
    w_kr, b_kr = w_in_t[l, COL_KR:COL_G, :].T, bi[COL_KR:COL_G]
    padw = LANE - 2 * QK_ROPE
    p["wkr"] = jnp.concatenate([w_kr, _rot_cols(w_kr), jnp.zeros((D, padw), F32)], -1).astype(BF16)
    p["bkr"] = jnp.concatenate([b_kr, _rot_cols(b_kr), jnp.zeros((padw,), F32)])[None]
    wq = d_w_uq[l].reshape(Q_LORA, HEADS, QK_NOPE + QK_ROPE)
    zpad = jnp.zeros((Q_LORA, HEADS, HEAD_PAD - QK_NOPE - QK_ROPE), F32)
    p["wq"] = jnp.concatenate([wq, zpad], -1).reshape(Q_LORA, HEADS * HEAD_PAD).astype(BF16)
    p["wqr"] = jnp.concatenate([jnp.zeros((Q_LORA, HEADS, QK_NOPE), F32), _rot_cols(wq[..., QK_NOPE:]), zpad],
                               -1).reshape(Q_LORA, HEADS * HEAD_PAD).astype(BF16)
    wk = d_w_uk[l].reshape(KV_LORA, HEADS, QK_NOPE)
    p["wk"] = jnp.concatenate([wk, jnp.zeros((KV_LORA, HEADS, HEAD_PAD - QK_NOPE), F32)],
                              -1).reshape(KV_LORA, HEADS * HEAD_PAD).astype(BF16)
    p["wvt"] = d_w_uv[l].T.astype(BF16)
    e = np.zeros((LANE, HEADS, HEAD_PAD), np.float32)
    for j in range(QK_ROPE):
        e[j, :, QK_NOPE + j] = 1.0
        e[QK_ROPE + j, :, QK_NOPE + j] = 1.0
    p["e"] = jnp.asarray(e.reshape(LANE, HEADS * HEAD_PAD), dtype=BF16)
    p["bg"] = bi[None, COL_G:]
    padr = ROUTER_W - N_GROUPS - N_EXPERTS
    p["wr"] = _hilo(jnp.concatenate([r_w_grp[l], r_w_exp[l], jnp.zeros((D, padr), F32)], -1))
    p["br"] = jnp.concatenate([r_b_grp[l], r_b_exp[l], jnp.zeros((padr,), F32)])[None]
    p["bs"] = jnp.broadcast_to(a_b_s[l][:, :, None], (A_GROUPS, CHUNK, CHUNK))
    p["wdw"] = jnp.concatenate([b_w_dw[l], jnp.zeros((1, BW), F32)], 0)
    return p


def kernel(x, c, ctx, c_ctx, w_ada, b_ada, w_in, b_in, a_ln_g, a_ln_b, a_w_s, a_b_s, b_w_dw, b_b_dw, b_ln_g,
           b_ln_b, d_q_g, d_w_uq, d_kv_g, d_w_uk, d_w_uv, w_branch, w_o, ln1_g, ln1_b, r_w_grp, r_b_grp,
           r_w_exp, r_b_exp, e_w1, e_w3, e_w2, ln2_g, ln2_b):
    assert x.shape == (1, SEQ, D) and ctx.shape == (1, N_CTX, D)
    x_ctx, x_lat = ctx[0], x[0]
    w_in_t = jnp.transpose(w_in, (0, 2, 1))
    c_rep = jnp.broadcast_to(jnp.concatenate([c, c_ctx[None]], axis=0)[:, :, None], (N_COND, D, LANE))
    mods = ada_mod(c_rep, w_ada, b_ada)
    tq_c, tq_s, tk = _rope_tables()
    fc = _fourier_consts()
    b_in3 = b_in.reshape(DEPTH, 1, -1)
    w_br, w_ob = w_branch.astype(BF16), w_o.astype(BF16)
    e1, e3, e2 = e_w1.astype(BF16), e_w3.astype(BF16), e_w2.astype(BF16)

    h = modln(x_ctx, x_lat, mods[0])
    for l in range(DEPTH):
        p = _layer_weights(l, w_in_t, b_in, d_w_uq, d_w_uk, d_w_uv, r_w_grp, r_b_grp, r_w_exp, r_b_exp, a_b_s,
                           b_w_dw)
        mod = mods[l]
        br_a = mix_a(l, h, w_in_t, b_in3, a_ln_g[l][None], a_ln_b[l][None], a_w_s[l].astype(BF16), p["bs"])
        br_b = conv_ln(glu(l, h, w_in_t, b_in3), p["wdw"], b_b_dw[l][None], b_ln_g[l][None], b_ln_b[l][None])
        br_c = fourier_mix(l, h, w_in_t, b_in3, fc)
        q, k, vt = mla_proj(l, h, w_in_t, b_in3, p["wkr"], p["bkr"], d_q_g[l][None], d_kv_g[l][None], p["wq"],
                            p["wqr"], p["wk"], p["wvt"], p["e"], tq_c, tq_s, tk)
        br_dt = attention(q, k, vt)
        y = merge(l, h, br_a, br_b, br_c, br_dt, w_in_t, p["bg"], w_br)
        x1, h2, ids, gates = post_mix(l, y, x_ctx, x_lat, mod, w_ob, ln1_g[l][None], ln1_b[l][None], p["wr"],
                                      p["br"])
        blk_exp, row_tok, pos = _routing_tables(ids)
        ys = experts(l, blk_exp, row_tok, h2, e1, e3, e2)
        x_ctx, x_lat, h = combine(pos, ys, gates, x1, mod, ln2_g[l][None], ln2_b[l][None],
                                  mods[min(l + 1, DEPTH - 1)])
    return x_lat[None]
```

```python
import functools
import math

import numpy as np
import jax
import jax.numpy as jnp
from jax import lax
from jax.experimental import pallas as pl
from jax.experimental.pallas import tpu as pltpu

F32 = jnp.float32
BF16 = jnp.bfloat16

D = 2048
SEQ = 8192
N_CTX = 256
T = N_CTX + SEQ
DEPTH = 2
GRID_W = 64
BW = 512
CHUNK = 128
A_GROUPS = 4
CONV_W = 31
F_GROUPS = 4
QK_NOPE = 64
QK_ROPE = 32
V_HEAD = 64
HEADS = 8
Q_LORA = 512
KV_LORA = 256
ROPE_THETA = 10000.0
MLA_SCALE = (QK_NOPE + QK_ROPE) ** -0.5
N_GROUPS = 4
EXP_PER_GROUP = 8
N_EXPERTS = 32
TOP_K = 2
D_EXPERT = 1024
MOE_BLOCK = 256
COL_A = 0
COL_B = 1024
COL_C = 2048
COL_D = 2560
COL_KV = 3072
COL_KR = 3328
COL_G = 3360
ALPHA = (2 * DEPTH) ** 0.25
LN_EPS = 1e-6

LANE = 128
HEAD_PAD = 128
RB = 256
NRB = T // RB
FFT_N1 = 64
FFT_N2 = 128
ROW_SUB = D // LANE
VMEM_LIMIT = 56 * 1024 * 1024


def _cparams(sem):
    return pltpu.CompilerParams(dimension_semantics=sem, vmem_limit_bytes=VMEM_LIMIT)


def _dot(a, b):
    return jnp.dot(a, b, preferred_element_type=F32)


def _split(x):
    hi = lax.bitcast_convert_type(lax.bitcast_convert_type(x, jnp.int32) & jnp.int32(-65536), F32)
    return hi.astype(BF16), (x - hi).astype(BF16)


def _hilo(x):
    return jnp.stack(_split(jnp.asarray(x, F32)))


def _dot3(a, t_ref):
    a_hi, a_lo = _split(a)
    return _dot(a_hi, t_ref[0]) + (_dot(a_hi, t_ref[1]) + _dot(a_lo, t_ref[0]))


def _dot3_t(t_ref, b):
    b_hi, b_lo = _split(b)
    return _dot(t_ref[0], b_hi) + (_dot(t_ref[1], b_hi) + _dot(t_ref[0], b_lo))


def _ln0(x):
    mu = jnp.mean(x, axis=-1, keepdims=True)
    xc = x - mu
    var = jnp.mean(xc * xc, axis=-1, keepdims=True)
    return xc * lax.rsqrt(var + LN_EPS)


def _mod_row(mod_ref, q):
    lat = mod_ref[0:1, q * D:(q + 1) * D]
    ctx = mod_ref[1:2, q * D:(q + 1) * D]
    return jnp.where(pl.program_id(0) == 0, ctx, lat)


ADA_TN = 1536
ADA_TK = 256
N_COND = 2


def _ada_kernel(c_ref, w_ref, b_ref, o_ref):
    accs = [b_ref[0] for _ in range(N_COND)]
    for k0 in range(0, D, ADA_TK):
        w = w_ref[0, k0:k0 + ADA_TK, :]
        for r in range(N_COND):
            s = c_ref[r, k0:k0 + ADA_TK, :]
            s = s * jax.nn.sigmoid(s)
            s = jnp.concatenate([s] * (ADA_TN // LANE), axis=-1)
            accs[r] = accs[r] + jnp.sum(w * s, axis=0, keepdims=True)
    o_ref[0] = jnp.concatenate(accs + [jnp.zeros((8 - N_COND, ADA_TN), F32)], axis=0)


def ada_mod(c_rep, w_ada, b_ada):
    return pl.pallas_call(
        _ada_kernel,
        grid=(DEPTH, 6 * D // ADA_TN),
        in_specs=[pl.BlockSpec((N_COND, D, LANE), lambda l, j: (0, 0, 0)),
                  pl.BlockSpec((1, D, ADA_TN), lambda l, j: (l, 0, j)),
                  pl.BlockSpec((1, 1, ADA_TN), lambda l, j: (l, 0, j))],
        out_specs=pl.BlockSpec((1, 8, ADA_TN), lambda l, j: (l, 0, j)),
        out_shape=jax.ShapeDtypeStruct((DEPTH, 8, 6 * D), F32),
        compiler_params=_cparams(("arbitrary", "arbitrary")),
        name="ada_mod",
    )(c_rep, w_ada, b_ada.reshape(DEPTH, 1, 6 * D))


def _stream_specs():
    return [pl.BlockSpec((RB, D), lambda i, *_: (0, 0)),
            pl.BlockSpec((RB, D), lambda i, *_: (jnp.maximum(i - 1, 0), 0))]


def _stream_block(ctx_ref, lat_ref):
    return jnp.where(pl.program_id(0) == 0, ctx_ref[...], lat_ref[...])


def _modln_kernel(xc_ref, xl_ref, mod_ref, h_ref):
    h = _ln0(_stream_block(xc_ref, xl_ref)) * (1.0 + _mod_row(mod_ref, 1)) + _mod_row(mod_ref, 0)
    h_ref[...] = h.astype(BF16)


def modln(x_ctx, x_lat, mod):
    return pl.pallas_call(
        _modln_kernel,
        grid=(NRB,),
        in_specs=_stream_specs() + [pl.BlockSpec((8, 6 * D), lambda i: (0, 0))],
        out_specs=pl.BlockSpec((RB, D), lambda i: (i, 0)),
        out_shape=jax.ShapeDtypeStruct((T, D), BF16),
        compiler_params=_cparams(("arbitrary",)),
        name="modln",
    )(x_ctx, x_lat, mod)


def _w_in_spec(l, width, col):
    return pl.BlockSpec((1, width, D), lambda i: (l, col, 0))


def _b_in_spec(l, width, col):
    return pl.BlockSpec((1, 1, width), lambda i: (l, 0, col))


CAST_CHUNK = 256


def _cast_transposed(w_ref, wbf_ref):
    width = w_ref.shape[0]
    for c0 in range(0, width, CAST_CHUNK):
        wbf_ref[:, c0:c0 + CAST_CHUNK] = w_ref[c0:c0 + CAST_CHUNK, :].T.astype(BF16)


def _cast_once(w_ref, wbf_ref):
    @pl.when(pl.program_id(0) == 0)
    def _():
        _cast_transposed(w_ref.at[0], wbf_ref)


def _mix_a_kernel(h_ref, w_ref, b_ref, g_ref, bb_ref, ws_ref, bs_ref, o_ref, wbf_ref):
    _cast_once(w_ref, wbf_ref)
    uv = jax.nn.gelu(_dot(h_ref[...], wbf_ref[...]) + b_ref[0])
    u = uv[:, :BW]
    v = _ln0(uv[:, BW:]) * g_ref[...] + bb_ref[...]
    v = v.astype(BF16)
    gw = BW // A_GROUPS
    for ch in range(RB // CHUNK):
        rows = slice(ch * CHUNK, (ch + 1) * CHUNK)
        parts = [_dot(ws_ref[g], v[rows, g * gw:(g + 1) * gw]) + bs_ref[g] for g in range(A_GROUPS)]
        mixed = jnp.concatenate(parts, axis=-1)
        o_ref[rows, :] = (u[rows, :] * mixed).astype(BF16)


def mix_a(l, h, w_in, b_in, ln_g, ln_b, ws, bs):
    return pl.pallas_call(
        _mix_a_kernel,
        grid=(NRB,),
        in_specs=[pl.BlockSpec((RB, D), lambda i: (i, 0)),
                  _w_in_spec(l, 2 * BW, COL_A // (2 * BW)),
                  _b_in_spec(l, 2 * BW, COL_A // (2 * BW)),
                  pl.BlockSpec((1, BW), lambda i: (0, 0)),
                  pl.BlockSpec((1, BW), lambda i: (0, 0)),
                  pl.BlockSpec((A_GROUPS, CHUNK, CHUNK), lambda i: (0, 0, 0)),
                  pl.BlockSpec((A_GROUPS, CHUNK, CHUNK), lambda i: (0, 0, 0))],
        out_specs=pl.BlockSpec((RB, BW), lambda i: (i, 0)),
        out_shape=jax.ShapeDtypeStruct((T, BW), BF16),
        scratch_shapes=[pltpu.VMEM((D, 2 * BW), BF16)],
        compiler_params=_cparams(("arbitrary",)),
        name="mix_a",
    )(h, w_in, b_in, ln_g, ln_b, ws, bs)


def _glu_kernel(h_ref, w_ref, b_ref, o_ref, wbf_ref):
    _cast_once(w_ref, wbf_ref)
    ab = _dot(h_ref[...], wbf_ref[...]) + b_ref[0]
    o_ref[...] = ab[:, :BW] * jax.nn.sigmoid(ab[:, BW:])


def glu(l, h, w_in, b_in):
    return pl.pallas_call(
        _glu_kernel,
        grid=(NRB,),
        in_specs=[pl.BlockSpec((RB, D), lambda i: (i, 0)),
                  _w_in_spec(l, 2 * BW, COL_B // (2 * BW)),
                  _b_in_spec(l, 2 * BW, COL_B // (2 * BW))],
        out_specs=pl.BlockSpec((RB, BW), lambda i: (i, 0)),
        out_shape=jax.ShapeDtypeStruct((T, BW), F32),
        scratch_shapes=[pltpu.VMEM((D, 2 * BW), BF16)],
        compiler_params=_cparams(("arbitrary",)),
        name="glu",
    )(h, w_in, b_in)


CONV_HALO = 16


def _conv_kernel(prev_ref, cur_ref, next_ref, w_ref, b_ref, g_ref, bb_ref, o_ref, ext_ref):
    i = pl.program_id(0)
    has_prev = i >= 2
    has_next = jnp.logical_and(i >= 1, i < NRB - 1)
    ext_ref[0:CONV_HALO, :] = jnp.where(has_prev, prev_ref[RB - CONV_HALO:RB, :], 0.0)
    ext_ref[CONV_HALO:CONV_HALO + RB, :] = cur_ref[...]
    ext_ref[CONV_HALO + RB:2 * CONV_HALO + RB, :] = jnp.where(has_next, next_ref[0:CONV_HALO, :], 0.0)
    off = CONV_HALO - CONV_W // 2
    acc = jnp.zeros((RB, BW), F32)
    for k in range(CONV_W):
        acc = acc + ext_ref[off + k:off + k + RB, :] * w_ref[k:k + 1, :]
    y = _ln0(acc + b_ref[...]) * g_ref[...] + bb_ref[...]
    o_ref[...] = (y * jax.nn.sigmoid(y)).astype(BF16)


def conv_ln(y, w_dw, b_dw, ln_g, ln_b):
    return pl.pallas_call(
        _conv_kernel,
        grid=(NRB,),
        in_specs=[pl.BlockSpec((RB, BW), lambda i: (jnp.maximum(i - 1, 0), 0)),
                  pl.BlockSpec((RB, BW), lambda i: (i, 0)),
                  pl.BlockSpec((RB, BW), lambda i: (jnp.minimum(i + 1, NRB - 1), 0)),
                  pl.BlockSpec((CONV_W + 1, BW), lambda i: (0, 0)),
                  pl.BlockSpec((1, BW), lambda i: (0, 0)),
                  pl.BlockSpec((1, BW), lambda i: (0, 0)),
                  pl.BlockSpec((1, BW), lambda i: (0, 0))],
        out_specs=pl.BlockSpec((RB, BW), lambda i: (i, 0)),
        out_shape=jax.ShapeDtypeStruct((T, BW), BF16),
        scratch_shapes=[pltpu.VMEM((RB + 2 * CONV_HALO, BW), F32)],
        compiler_params=_cparams(("arbitrary",)),
        name="conv_ln",
    )(y, y, y, w_dw, b_dw, ln_g, ln_b)


def _dft_tables(n):
    k = np.arange(n, dtype=np.int64)
    ang = 2.0 * np.pi * ((k[:, None] * k[None, :]) % n).astype(np.float64) / n
    return np.cos(ang), np.sin(ang)


def _fproj_kernel(h_ref, w_ref, b_ref, cs_ref, ar_ref, ai_ref, wbf_ref):
    _cast_once(w_ref, wbf_ref)
    z = _dot(h_ref[...], wbf_ref[...]) + b_ref[0]
    a = _dot3(z, cs_ref)
    ar_ref[...] = a[:, :BW]
    ai_ref[...] = a[:, BW:]


def fourier_proj(l, h, w_in, b_in, cs):
    return pl.pallas_call(
        _fproj_kernel,
        grid=(NRB,),
        in_specs=[pl.BlockSpec((RB, D), lambda i: (i, 0)),
                  _w_in_spec(l, BW, COL_C // BW),
                  _b_in_spec(l, BW, COL_C // BW),
                  pl.BlockSpec((2, BW, 2 * BW), lambda i: (0, 0, 0))],
        out_specs=[pl.BlockSpec((RB, BW), lambda i: (i, 0)),
                   pl.BlockSpec((RB, BW), lambda i: (i, 0))],
        out_shape=[jax.ShapeDtypeStruct((T, BW), F32), jax.ShapeDtypeStruct((T, BW), F32)],
        scratch_shapes=[pltpu.VMEM((D, BW), BF16)],
        compiler_params=_cparams(("arbitrary",)),
        name="fourier_proj",
    )(h, w_in, b_in, cs)


FFT_TILE_N2 = 4


def _fft1_kernel(ar_ref, ai_ref, m_ref, ct_ref, st_ref, br_ref, bi_ref):
    x = jnp.concatenate([ar_ref[...], ai_ref[...]], axis=0)
    b = _dot3_t(m_ref, x)
    b_re, b_im = b[:FFT_N1], b[FFT_N1:]
    reps = BW // LANE
    for j in range(FFT_TILE_N2):
        ct = jnp.concatenate([ct_ref[j]] * reps, axis=-1)
        st = jnp.concatenate([st_ref[j]] * reps, axis=-1)
        lanes = slice(j * BW, (j + 1) * BW)
        br_ref[:, lanes] = b_re[:, lanes] * ct + b_im[:, lanes] * st
        bi_ref[:, lanes] = b_im[:, lanes] * ct - b_re[:, lanes] * st


def fft_stage1(ar2d, ai2d, m1, ct, st):
    tn = FFT_TILE_N2 * BW
    return pl.pallas_call(
        _fft1_kernel,
        grid=(FFT_N2 // FFT_TILE_N2,),
        in_specs=[pl.BlockSpec((FFT_N1, tn), lambda j: (0, j)),
                  pl.BlockSpec((FFT_N1, tn), lambda j: (0, j)),
                  pl.BlockSpec((2, 2 * FFT_N1, 2 * FFT_N1), lambda j: (0, 0, 0)),
                  pl.BlockSpec((FFT_TILE_N2, FFT_N1, LANE), lambda j: (j, 0, 0)),
                  pl.BlockSpec((FFT_TILE_N2, FFT_N1, LANE), lambda j: (j, 0, 0))],
        out_specs=[pl.BlockSpec((FFT_N1, tn), lambda j: (0, j)),
                   pl.BlockSpec((FFT_N1, tn), lambda j: (0, j))],
        out_shape=[jax.ShapeDtypeStruct((FFT_N1, FFT_N2 * BW), F32)] * 2,
        compiler_params=_cparams(("arbitrary",)),
        name="fft_stage1",
    )(ar2d, ai2d, m1, ct, st)


FFT_TILE_K1 = 8


def _fft2_kernel(br_ref, bi_ref, c_ref, s_ref, p_ref, o_ref, *, scale):
    ys = [(_dot3_t(c_ref, br_ref[a]) + _dot3_t(s_ref, bi_ref[a])) * scale for a in range(FFT_TILE_K1)]
    y = jnp.concatenate(ys, axis=0).astype(BF16)
    y = _dot(p_ref[...], y)
    o_ref[...] = y.reshape(FFT_N2, FFT_TILE_K1, BW)


def fft_stage2(br3d, bi3d, c128, s128, perm, scale):
    return pl.pallas_call(
        functools.partial(_fft2_kernel, scale=scale),
        grid=(FFT_N1 // FFT_TILE_K1,),
        in_specs=[pl.BlockSpec((FFT_TILE_K1, FFT_N2, BW), lambda a: (a, 0, 0)),
                  pl.BlockSpec((FFT_TILE_K1, FFT_N2, BW), lambda a: (a, 0, 0)),
                  pl.BlockSpec((2, FFT_N2, FFT_N2), lambda a: (0, 0, 0)),
                  pl.BlockSpec((2, FFT_N2, FFT_N2), lambda a: (0, 0, 0)),
                  pl.BlockSpec((FFT_TILE_K1 * FFT_N2, FFT_TILE_K1 * FFT_N2), lambda a: (0, 0))],
        out_specs=pl.BlockSpec((FFT_N2, FFT_TILE_K1, BW), lambda a: (0, a, 0)),
        out_shape=jax.ShapeDtypeStruct((FFT_N2, FFT_N1, BW), F32),
        compiler_params=_cparams(("arbitrary",)),
        name="fft_stage2",
    )(br3d, bi3d, c128, s128, perm)


def _dft_ctx_kernel(ar_ref, ai_ref, c_ref, s_ref, o_ref, *, scale):
    o_ref[...] = (_dot3_t(c_ref, ar_ref[...]) + _dot3_t(s_ref, ai_ref[...])) * scale


def dft_ctx(ar, ai, c, s, scale):
    return pl.pallas_call(
        functools.partial(_dft_ctx_kernel, scale=scale),
        grid=(1,),
        in_specs=[pl.BlockSpec((N_CTX, BW), lambda i: (0, 0)),
                  pl.BlockSpec((N_CTX, BW), lambda i: (0, 0)),
                  pl.BlockSpec((2, N_CTX, N_CTX), lambda i: (0, 0, 0)),
                  pl.BlockSpec((2, N_CTX, N_CTX), lambda i: (0, 0, 0))],
        out_specs=pl.BlockSpec((N_CTX, BW), lambda i: (0, 0)),
        out_shape=jax.ShapeDtypeStruct((N_CTX, BW), F32),
        compiler_params=_cparams(("arbitrary",)),
        name="dft_ctx",
    )(ar, ai, c, s)


def _fourier_consts():
    c64, s64 = _dft_tables(FFT_N1)
    m1 = np.block([[c64, s64], [-s64, c64]]).astype(np.float32)
    k1 = np.arange(FFT_N1, dtype=np.float64)
    n2 = np.arange(FFT_N2, dtype=np.float64)
    phi = 2.0 * np.pi * n2[:, None] * k1[None, :] / SEQ
    ct = np.repeat(np.cos(phi)[:, :, None], LANE, axis=2).astype(np.float32)
    st = np.repeat(np.sin(phi)[:, :, None], LANE, axis=2).astype(np.float32)
    c128, s128 = _dft_tables(FFT_N2)
    gw = BW // F_GROUPS
    cg, sg = _dft_tables(gw)
    cs = np.zeros((BW, 2 * BW), np.float32)
    for g in range(F_GROUPS):
        sl = slice(g * gw, (g + 1) * gw)
        cs[sl, sl] = cg
        cs[sl, BW + g * gw:BW + (g + 1) * gw] = -sg
    rows = np.arange(FFT_TILE_K1 * FFT_N2)
    perm = np.zeros((rows.size, rows.size), np.float32)
    perm[rows, (rows % FFT_TILE_K1) * FFT_N2 + rows // FFT_TILE_K1] = 1.0
    cc, sc = _dft_tables(N_CTX)
    return dict(m1=_hilo(m1), ct=jnp.asarray(ct), st=jnp.asarray(st), c128=_hilo(c128), s128=_hilo(s128),
                cs=_hilo(cs), perm=jnp.asarray(perm, dtype=BF16), cc=_hilo(cc), sc=_hilo(sc))


def fourier_mix(l, h, w_in, b_in, fc):
    gw = BW // F_GROUPS
    ar, ai = fourier_proj(l, h, w_in, b_in, fc["cs"])
    y_ctx = dft_ctx(ar, ai, fc["cc"], fc["sc"], 1.0 / math.sqrt(N_CTX * gw))
    ar2 = ar[N_CTX:].reshape(FFT_N1, FFT_N2 * BW)
    ai2 = ai[N_CTX:].reshape(FFT_N1, FFT_N2 * BW)
    br, bi = fft_stage1(ar2, ai2, fc["m1"], fc["ct"], fc["st"])
    y = fft_stage2(br.reshape(FFT_N1, FFT_N2, BW), bi.reshape(FFT_N1, FFT_N2, BW),
                   fc["c128"], fc["s128"], fc["perm"], 1.0 / math.sqrt(SEQ * gw))
    return jnp.concatenate([y_ctx, y.reshape(SEQ, BW)], axis=0)


VT_ROWS = 80
LOG2E = math.log2(math.e)


def _mla_proj_kernel(h_ref, wcq_ref, bcq_ref, wkv_ref, bkv_ref, wkr_ref, bkr_ref, qg_ref, kvg_ref, wq_ref, wqr_ref,
                     wk_ref, wvt_ref, e_ref, tq_c_ref, tq_s_ref, tk_ref, q_ref, k_ref, vt_ref, wcq_bf, wkv_bf):
    _cast_once(wcq_ref, wcq_bf)
    _cast_once(wkv_ref, wkv_bf)
    h = h_ref[...]
    cq = _dot(h, wcq_bf[...]) + bcq_ref[0]
    cq = cq * lax.rsqrt(jnp.mean(cq * cq, axis=-1, keepdims=True) + LN_EPS) * qg_ref[...]
    cq = cq.astype(BF16)
    cos_t = jnp.concatenate([tq_c_ref[...]] * HEADS, axis=-1)
    sin_t = jnp.concatenate([tq_s_ref[...]] * HEADS, axis=-1)
    q = (_dot(cq, wq_ref[...]) * cos_t + _dot(cq, wqr_ref[...]) * sin_t) * (MLA_SCALE * LOG2E)
    q_ref[...] = q.astype(BF16)
    ckv = _dot(h, wkv_bf[...]) + bkv_ref[0]
    ckv = ckv * lax.rsqrt(jnp.mean(ckv * ckv, axis=-1, keepdims=True) + LN_EPS) * kvg_ref[...]
    ckv = ckv.astype(BF16)
    kr = ((_dot(h, wkr_ref[...]) + bkr_ref[...]) * tk_ref[...]).astype(BF16)
    k_ref[...] = (_dot(ckv, wk_ref[...]) + _dot(kr, e_ref[...])).astype(BF16)
    vt = lax.dot_general(wvt_ref[...], ckv, (((1,), (1,)), ((), ())), preferred_element_type=F32)
    ones_rows = (lax.broadcasted_iota(jnp.int32, (VT_ROWS - V_HEAD, RB), 0) == 0).astype(BF16)
    for hd in range(HEADS):
        vt_ref[0, hd * VT_ROWS:hd * VT_ROWS + V_HEAD, :] = vt[hd * V_HEAD:(hd + 1) * V_HEAD, :].astype(BF16)
        vt_ref[0, hd * VT_ROWS + V_HEAD:(hd + 1) * VT_ROWS, :] = ones_rows


def mla_proj(l, h, w_in, b_in, wkr, bkr, qg, kvg, wq, wqr, wk, wvt, e, tq_c, tq_s, tk):
    const = lambda i: (0, 0)
    row = lambda i: (i, 0)
    return pl.pallas_call(
        _mla_proj_kernel,
        grid=(NRB,),
        in_specs=[pl.BlockSpec((RB, D), row),
                  _w_in_spec(l, Q_LORA, COL_D // Q_LORA),
                  _b_in_spec(l, Q_LORA, COL_D // Q_LORA),
                  _w_in_spec(l, KV_LORA, COL_KV // KV_LORA),
                  _b_in_spec(l, KV_LORA, COL_KV // KV_LORA),
                  pl.BlockSpec((D, LANE), const),
                  pl.BlockSpec((1, LANE), const),
                  pl.BlockSpec((1, Q_LORA), const),
                  pl.BlockSpec((1, KV_LORA), const),
                  pl.BlockSpec((Q_LORA, HEADS * HEAD_PAD), const),
                  pl.BlockSpec((Q_LORA, HEADS * HEAD_PAD), const),
                  pl.BlockSpec((KV_LORA, HEADS * HEAD_PAD), const),
                  pl.BlockSpec((HEADS * V_HEAD, KV_LORA), const),
                  pl.BlockSpec((LANE, HEADS * HEAD_PAD), const),
                  pl.BlockSpec((RB, HEAD_PAD), row),
                  pl.BlockSpec((RB, HEAD_PAD), row),
                  pl.BlockSpec((RB, LANE), row)],
        out_specs=[pl.BlockSpec((RB, HEADS * HEAD_PAD), row),
                   pl.BlockSpec((RB, HEADS * HEAD_PAD), row),
                   pl.BlockSpec((1, HEADS * VT_ROWS, RB), lambda i: (i, 0, 0))],
        out_shape=[jax.ShapeDtypeStruct((T, HEADS * HEAD_PAD), BF16),
                   jax.ShapeDtypeStruct((T, HEADS * HEAD_PAD), BF16),
                   jax.ShapeDtypeStruct((NRB, HEADS * VT_ROWS, RB), BF16)],
        scratch_shapes=[pltpu.VMEM((D, Q_LORA), BF16), pltpu.VMEM((D, KV_LORA), BF16)],
        compiler_params=_cparams(("arbitrary",)),
        name="mla_proj",
    )(h, w_in, b_in, w_in, b_in, wkr, bkr, qg, kvg, wq, wqr, wk, wvt, e, tq_c, tq_s, tk)


KV_TILE = RB
KV_STAGE = 1024
N_STAGE = SEQ // KV_STAGE
TILES_PER_STAGE = KV_STAGE // KV_TILE


HEAD_GROUP = 2


def _attn_kernel(q_ref, k_ref, vt_ref, o_ref, sa_ref, sb_ref, m_ref, acc_ref):
    i = pl.program_id(1)
    heads = range(HEAD_GROUP)
    qs = [q_ref[:, g * HEAD_PAD:(g + 1) * HEAD_PAD] for g in heads]

    def scores(g, start, n):
        kt = k_ref[pl.ds(start, n), g * HEAD_PAD:(g + 1) * HEAD_PAD]
        return lax.dot_general(kt, qs[g], (((1,), (1,)), ((), ())),
                               preferred_element_type=F32)

    def values(g, tile):
        return vt_ref[tile, g * VT_ROWS:(g + 1) * VT_ROWS, :]

    def softmax_pv(g, s_ref, tile0, m, acc):
        m_new = jnp.maximum(m, jnp.max(s_ref[g], axis=0, keepdims=True))
        p = jnp.exp2(s_ref[g] - m_new).astype(BF16)
        acc = jnp.exp2(m - m_new) * acc
        for u in range(TILES_PER_STAGE):
            acc = acc + _dot(values(g, tile0 + u), p[u * KV_TILE:(u + 1) * KV_TILE, :])
        return m_new, acc

    for g in heads:
        s0 = scores(g, 0, N_CTX)
        m0 = jnp.max(s0, axis=0, keepdims=True)
        m_ref[g] = m0
        acc_ref[g] = _dot(values(g, 0), jnp.exp2(s0 - m0).astype(BF16))

    @pl.when(i > 0)
    def _():
        first_tile = N_CTX // KV_TILE
        for g in heads:
            sa_ref[g] = scores(g, N_CTX, KV_STAGE)

        def body(j, carry):
            ms, accs = list(carry[0]), list(carry[1])
            base = pl.multiple_of(N_CTX + 2 * j * KV_STAGE, KV_TILE)
            tile = first_tile + 2 * j * TILES_PER_STAGE
            for g in heads:
                sb_ref[g] = scores(g, base + KV_STAGE, KV_STAGE)
            for g in heads:
                ms[g], accs[g] = softmax_pv(g, sa_ref, tile, ms[g], accs[g])
            for g in heads:
                sa_ref[g] = scores(g, base + 2 * KV_STAGE, KV_STAGE)
            for g in heads:
                ms[g], accs[g] = softmax_pv(g, sb_ref, tile + TILES_PER_STAGE, ms[g], accs[g])
            return tuple(ms), tuple(accs)

        init = (tuple(m_ref[g] for g in heads), tuple(acc_ref[g] for g in heads))
        ms, accs = lax.fori_loop(0, N_STAGE // 2 - 1, body, init)
        ms, accs = list(ms), list(accs)
        for g in heads:
            sb_ref[g] = scores(g, N_CTX + (N_STAGE - 1) * KV_STAGE, KV_STAGE)
        for g in heads:
            ms[g], accs[g] = softmax_pv(g, sa_ref, first_tile + (N_STAGE - 2) * TILES_PER_STAGE, ms[g], accs[g])
        for g in heads:
            ms[g], accs[g] = softmax_pv(g, sb_ref, first_tile + (N_STAGE - 1) * TILES_PER_STAGE, ms[g], accs[g])
            acc_ref[g] = accs[g]

    for g in heads:
        acc = acc_ref[g]
        o_ref[g * V_HEAD:(g + 1) * V_HEAD, :] = acc[:V_HEAD, :] / acc[V_HEAD:V_HEAD + 1, :]


def attention(q, k, vt):
    return pl.pallas_call(
        _attn_kernel,
        grid=(HEADS // HEAD_GROUP, NRB),
        in_specs=[pl.BlockSpec((RB, HEAD_GROUP * HEAD_PAD), lambda h, i: (i, h)),
                  pl.BlockSpec((T, HEAD_GROUP * HEAD_PAD), lambda h, i: (0, h)),
                  pl.BlockSpec((NRB, HEAD_GROUP * VT_ROWS, KV_TILE), lambda h, i: (0, h, 0))],
        out_specs=pl.BlockSpec((HEAD_GROUP * V_HEAD, RB), lambda h, i: (h, i)),
        out_shape=jax.ShapeDtypeStruct((HEADS * V_HEAD, T), F32),
        scratch_shapes=[pltpu.VMEM((HEAD_GROUP, KV_STAGE, RB), F32),
                        pltpu.VMEM((HEAD_GROUP, KV_STAGE, RB), F32),
                        pltpu.VMEM((HEAD_GROUP, 1, RB), F32),
                        pltpu.VMEM((HEAD_GROUP, VT_ROWS, RB), F32)],
        compiler_params=_cparams(("arbitrary", "arbitrary")),
        name="attention",
    )(q, k, vt)


MERGE_TN = 512


N_BRANCH = 4


MERGE_RB = 768


def _merge_kernel(h_ref, a_ref, b_ref, c_ref, dt_ref, *refs):
    wg_refs, bg_refs = refs[:N_BRANCH], refs[N_BRANCH:2 * N_BRANCH]
    wb_ref, y_ref, wg_bf = refs[2 * N_BRANCH:]

    @pl.when(pl.program_id(1) == 0)
    def _():
        for n in range(N_BRANCH):
            _cast_transposed(wg_refs[n], wg_bf.at[n])

    h = h_ref[...]
    branches = (a_ref[...], b_ref[...], c_ref[...].astype(BF16), dt_ref[...].T.astype(BF16))
    y = jnp.zeros((MERGE_RB, MERGE_TN), F32)
    for n, br in enumerate(branches):
        gate = jax.nn.sigmoid(_dot(h, wg_bf[n]) + bg_refs[n][...])
        y = y + gate * _dot(br, wb_ref[0, n])
    y_ref[...] = y.astype(BF16)


def merge(l, h, br_a, br_b, br_c, br_dt, w_in_t, bg, wb):
    row = lambda j, i: (i, 0)
    n_col = D // MERGE_TN
    gate_specs = [pl.BlockSpec((pl.Squeezed(), pl.Element(MERGE_TN), pl.Element(D)),
                               functools.partial(
                                   lambda j, i, n: (l, pl.multiple_of(COL_G + n * D + j * MERGE_TN, QK_ROPE), 0), n=n),
                               pipeline_mode=pl.Buffered(1))
                  for n in range(N_BRANCH)]
    bias_specs = [pl.BlockSpec((1, MERGE_TN), functools.partial(lambda j, i, n: (0, n * n_col + j), n=n))
                  for n in range(N_BRANCH)]
    return pl.pallas_call(
        _merge_kernel,
        grid=(n_col, T // MERGE_RB),
        in_specs=[pl.BlockSpec((MERGE_RB, D), row),
                  pl.BlockSpec((MERGE_RB, BW), row),
                  pl.BlockSpec((MERGE_RB, BW), row),
                  pl.BlockSpec((MERGE_RB, BW), row),
                  pl.BlockSpec((BW, MERGE_RB), lambda j, i: (0, i))]
                 + gate_specs + bias_specs
                 + [pl.BlockSpec((1, N_BRANCH, BW, MERGE_TN), lambda j, i: (l, 0, 0, j))],
        out_specs=pl.BlockSpec((MERGE_RB, MERGE_TN), lambda j, i: (i, j)),
        out_shape=jax.ShapeDtypeStruct((T, D), BF16),
        scratch_shapes=[pltpu.VMEM((N_BRANCH, D, MERGE_TN), BF16)],
        compiler_params=_cparams(("arbitrary", "arbitrary")),
        name="merge",
    )(h, br_a, br_b, br_c, br_dt, *([w_in_t] * N_BRANCH), *([bg] * N_BRANCH), wb)


ROUTER_W = 128


def _post_mix_kernel(y_ref, xc_ref, xl_ref, mod_ref, wo_ref, g_ref, b_ref, wr_ref, br_ref,
                     x1_ref, h2_ref, ids_ref, gates_ref):
    mix = _dot(y_ref[...], wo_ref[0])
    x1 = _ln0(ALPHA * _stream_block(xc_ref, xl_ref) + _mod_row(mod_ref, 2) * mix) * g_ref[...] + b_ref[...]
    x1_ref[...] = x1
    h2 = _ln0(x1) * (1.0 + _mod_row(mod_ref, 4)) + _mod_row(mod_ref, 3)
    _rows_to_slabs(h2, h2_ref)
    logits = _dot3(h2, wr_ref) + br_ref[...]
    lane = lax.broadcasted_iota(jnp.int32, (RB, ROUTER_W), 1).astype(F32)
    neg = jnp.float32(-jnp.inf)
    big = jnp.float32(ROUTER_W)
    gl = jnp.where(lane < N_GROUPS, logits, neg)
    gmax = jnp.max(gl, axis=-1, keepdims=True)
    g_idx = jnp.min(jnp.where(gl == gmax, lane, big), axis=-1, keepdims=True)
    g_p = 1.0 / jnp.sum(jnp.exp(gl - gmax), axis=-1, keepdims=True)
    e_lane = lane - N_GROUPS
    in_grp = jnp.logical_and(e_lane >= g_idx * EXP_PER_GROUP, e_lane < (g_idx + 1) * EXP_PER_GROUP)
    el = jnp.where(in_grp, logits, neg)
    v1 = jnp.max(el, axis=-1, keepdims=True)
    i1 = jnp.min(jnp.where(el == v1, lane, big), axis=-1, keepdims=True)
    el2 = jnp.where(lane == i1, neg, el)
    v2 = jnp.max(el2, axis=-1, keepdims=True)
    i2 = jnp.min(jnp.where(el2 == v2, lane, big), axis=-1, keepdims=True)
    e21 = jnp.exp(v2 - v1)
    p1 = 1.0 / (1.0 + e21)
    p2 = e21 * p1
    ids_ref[...] = jnp.where(lane == 0, i1 - N_GROUPS, jnp.where(lane == 1, i2 - N_GROUPS, 0.0)).astype(jnp.int32)
    gates_ref[...] = jnp.where(lane == 0, p1 * g_p, jnp.where(lane == 1, p2 * g_p, 0.0))


def post_mix(l, y, x_ctx, x_lat, mod, wo, ln_g, ln_b, wr, br):
    row = lambda i: (i, 0)
    const = lambda i: (0, 0)
    return pl.pallas_call(
        _post_mix_kernel,
        grid=(NRB,),
        in_specs=[pl.BlockSpec((RB, D), row)] + _stream_specs()
                 + [pl.BlockSpec((8, 6 * D), const),
                  pl.BlockSpec((1, D, D), lambda i: (l, 0, 0)),
                  pl.BlockSpec((1, D), const),
                  pl.BlockSpec((1, D), const),
                  pl.BlockSpec((2, D, ROUTER_W), lambda i: (0, 0, 0)),
                  pl.BlockSpec((1, ROUTER_W), const)],
        out_specs=[pl.BlockSpec((RB, D), row),
                   pl.BlockSpec((RB * ROW_SUB, LANE), row),
                   pl.BlockSpec((RB, ROUTER_W), row),
                   pl.BlockSpec((RB, ROUTER_W), row)],
        out_shape=[jax.ShapeDtypeStruct((T, D), F32),
                   jax.ShapeDtypeStruct((T * ROW_SUB, LANE), F32),
                   jax.ShapeDtypeStruct((T, ROUTER_W), jnp.int32),
                   jax.ShapeDtypeStruct((T, ROUTER_W), F32)],
        compiler_params=_cparams(("arbitrary",)),
        name="post_mix",
    )(y, x_ctx, x_lat, mod, wo, ln_g, ln_b, wr, br)


N_ROWS = -(-(T * TOP_K + N_EXPERTS * (MOE_BLOCK - 1)) // MOE_BLOCK) * MOE_BLOCK
N_BLK = N_ROWS // MOE_BLOCK


GATHER_PRIORITY = 1


GATHER_UNROLL = 8


def _row_gather(src_hbm, idx_ref, base, n, dst_ref, sem, first=0):
    def issue(t, _):
        for u in range(GATHER_UNROLL):
            r = first + t * GATHER_UNROLL + u
            src_row = pl.multiple_of(idx_ref[base + r] * ROW_SUB, ROW_SUB)
            dst_row = pl.multiple_of(r * ROW_SUB, ROW_SUB)
            pltpu.make_async_copy(src_hbm.at[pl.ds(src_row, ROW_SUB), :],
                                  dst_ref.at[pl.ds(dst_row, ROW_SUB), :], sem).start(priority=GATHER_PRIORITY)
        return 0
    lax.fori_loop(0, n // GATHER_UNROLL, issue, 0)


def _row_gather_wait(src_hbm, n, dst_ref, sem):
    pltpu.make_async_copy(src_hbm.at[pl.ds(0, n * ROW_SUB), :], dst_ref, sem).wait()


def _rows_from_slabs(buf_ref, start, n, stride):
    return jnp.concatenate([buf_ref[pl.ds(start + c, n, stride=stride), :] for c in range(ROW_SUB)], axis=-1)


def _rows_to_slabs(val, out_ref):
    n = val.shape[0]
    for c in range(ROW_SUB):
        out_ref[pl.ds(c, n, stride=ROW_SUB), :] = val[:, c * LANE:(c + 1) * LANE]


ITEM_BLKS = 4
ITEM_ROWS = ITEM_BLKS * MOE_BLOCK
N_ITEMS = N_BLK // ITEM_BLKS + N_EXPERTS
EXP_CHUNK = 256
N_CHUNK = D_EXPERT // EXP_CHUNK
BLK_SLABS = MOE_BLOCK * ROW_SUB


def _expert_kernel(item_exp_ref, item_blk0_ref, item_nblk_ref, n_used_ref, row_tok_ref, h_hbm, w1_ref, w3_ref, w2_ref,
                   y_hbm, xslab, x_bf, yacc, stage, sem_x, sem_y):
    i, j = pl.program_id(0), pl.program_id(1)
    nblk = item_nblk_ref[i]
    blk0 = item_blk0_ref[i]

    def gather_item(item, first, n):
        _row_gather(h_hbm, row_tok_ref, item_blk0_ref[item] * MOE_BLOCK, n, xslab, sem_x, first=first)

    @pl.when(jnp.logical_and(i == 0, j == 0))
    def _():
        gather_item(0, 0, nblk * MOE_BLOCK)

    @pl.when(j == 0)
    def _():
        def wait_block(b, _):
            pltpu.make_async_copy(h_hbm.at[pl.ds(0, BLK_SLABS), :], xslab.at[pl.ds(0, BLK_SLABS), :], sem_x).wait()
            return 0
        lax.fori_loop(0, nblk, wait_block, 0)

        def unpack(b, _):
            r0 = pl.multiple_of(b * MOE_BLOCK, MOE_BLOCK)
            x_bf[pl.ds(r0, MOE_BLOCK), :] = _rows_from_slabs(xslab, b * BLK_SLABS, MOE_BLOCK, ROW_SUB).astype(BF16)
            return 0
        lax.fori_loop(0, nblk, unpack, 0)

    @pl.when(i + 1 < N_ITEMS)
    def _():
        quarter = item_nblk_ref[i + 1] * (MOE_BLOCK // N_CHUNK)
        gather_item(i + 1, j * quarter, quarter)

    w1c = w1_ref[0, 0].astype(BF16)
    w3c = w3_ref[0, 0].astype(BF16)
    w2c = w2_ref[0, 0].astype(BF16)
    last = j == N_CHUNK - 1

    def y_copy(b, slot):
        dst = pl.multiple_of((blk0 + b) * BLK_SLABS, BLK_SLABS)
        return pltpu.make_async_copy(stage.at[slot], y_hbm.at[pl.ds(dst, BLK_SLABS), :], sem_y.at[slot])

    def block(b, _):
        r0 = pl.multiple_of(b * MOE_BLOCK, MOE_BLOCK)
        xb = x_bf[pl.ds(r0, MOE_BLOCK), :]
        a = _dot(xb, w1c)
        g = _dot(xb, w3c)
        hmid = (a * jax.nn.sigmoid(a) * g).astype(BF16)
        @pl.when(j == 0)
        def _():
            yacc[pl.ds(r0, MOE_BLOCK), :] = jnp.zeros((MOE_BLOCK, D), F32)

        yacc[pl.ds(r0, MOE_BLOCK), :] += _dot(hmid, w2c)

        @pl.when(last)
        def _():
            slot = b % 2

            @pl.when(b >= 2)
            def _():
                y_copy(b - 2, slot).wait()
            _rows_to_slabs(yacc[pl.ds(r0, MOE_BLOCK), :], stage.at[slot])
            y_copy(b, slot).start()
        return 0

    lax.fori_loop(0, nblk, block, 0)

    @pl.when(last)
    def _():
        for back in (1, 2):
            @pl.when(nblk >= back)
            def _():
                b = nblk - back
                y_copy(b, b % 2).wait()

    @pl.when(jnp.logical_and(i == N_ITEMS - 1, last))
    def _():
        stage[0] = jnp.zeros((BLK_SLABS, LANE), F32)

        def fill(b, _):
            dst = pl.multiple_of(b * BLK_SLABS, BLK_SLABS)
            cp = pltpu.make_async_copy(stage.at[0], y_hbm.at[pl.ds(dst, BLK_SLABS), :], sem_y.at[0])
            cp.start()
            cp.wait()
            return 0
        lax.fori_loop(n_used_ref[0], N_BLK, fill, 0)


def experts(l, item_exp, item_blk0, item_nblk, n_used, row_tok, h2_slabs, w1, w3, w2):
    wspec = lambda blk, imap: pl.BlockSpec(blk, imap)
    grid_spec = pltpu.PrefetchScalarGridSpec(
        num_scalar_prefetch=5,
        grid=(N_ITEMS, N_CHUNK),
        in_specs=[pl.BlockSpec(memory_space=pl.ANY),
                  wspec((1, 1, D, EXP_CHUNK), lambda i, j, ie, *_: (l, ie[i], 0, j)),
                  wspec((1, 1, D, EXP_CHUNK), lambda i, j, ie, *_: (l, ie[i], 0, j)),
                  wspec((1, 1, EXP_CHUNK, D), lambda i, j, ie, *_: (l, ie[i], j, 0))],
        out_specs=pl.BlockSpec(memory_space=pl.ANY),
        scratch_shapes=[pltpu.VMEM((ITEM_BLKS * BLK_SLABS, LANE), F32),
                        pltpu.VMEM((ITEM_ROWS, D), BF16),
                        pltpu.VMEM((ITEM_ROWS, D), F32),
                        pltpu.VMEM((2, BLK_SLABS, LANE), F32),
                        pltpu.SemaphoreType.DMA(()),
                        pltpu.SemaphoreType.DMA((2,))],
    )
    return pl.pallas_call(
        _expert_kernel,
        grid_spec=grid_spec,
        out_shape=jax.ShapeDtypeStruct((N_ROWS * ROW_SUB, LANE), F32),
        compiler_params=_cparams(("arbitrary", "arbitrary")),
        name="experts",
    )(item_exp, item_blk0, item_nblk, n_used, row_tok, h2_slabs, w1, w3, w2)


COMBINE_SUB = 64


def _combine_kernel(pos_ref, y_hbm, gates_ref, x1_ref, mod_ref, g_ref, b_ref, nmod_ref,
                    x2c_ref, x2l_ref, hn_ref, ybuf, sems):
    i = pl.program_id(0)
    slot = i % 2
    n = RB * TOP_K
    n_sub = RB // COMBINE_SUB

    @pl.when(i == 0)
    def _():
        _row_gather(y_hbm, pos_ref, 0, n, ybuf.at[0], sems.at[0])

    _row_gather_wait(y_hbm, n, ybuf.at[slot], sems.at[slot])
    shift5, shift0, scale1 = _mod_row(mod_ref, 5), _mod_row(nmod_ref, 0), 1.0 + _mod_row(nmod_ref, 1)
    for s in range(n_sub):
        @pl.when(i + 1 < NRB)
        def _():
            _row_gather(y_hbm, pos_ref, (i + 1) * n, n // n_sub, ybuf.at[1 - slot], sems.at[1 - slot],
                        first=s * (n // n_sub))

        rows = slice(s * COMBINE_SUB, (s + 1) * COMBINE_SUB)
        base = s * COMBINE_SUB * TOP_K * ROW_SUB
        y0 = _rows_from_slabs(ybuf.at[slot], base, COMBINE_SUB, TOP_K * ROW_SUB)
        y1 = _rows_from_slabs(ybuf.at[slot], base + ROW_SUB, COMBINE_SUB, TOP_K * ROW_SUB)
        gates = gates_ref[rows, :]
        f = gates[:, 0:1] * y0 + gates[:, 1:2] * y1
        x2 = _ln0(ALPHA * x1_ref[rows, :] + shift5 * f) * g_ref[...] + b_ref[...]
        x2l_ref[rows, :] = x2

        @pl.when(i == 0)
        def _():
            x2c_ref[rows, :] = x2

        hn_ref[rows, :] = (_ln0(x2) * scale1 + shift0).astype(BF16)


def combine(pos, y_slabs, gates, x1, mod, ln_g, ln_b, next_mod):
    row = lambda i, p: (i, 0)
    const = lambda i, p: (0, 0)
    grid_spec = pltpu.PrefetchScalarGridSpec(
        num_scalar_prefetch=1,
        grid=(NRB,),
        in_specs=[pl.BlockSpec(memory_space=pl.ANY),
                  pl.BlockSpec((RB, ROUTER_W), row),
                  pl.BlockSpec((RB, D), row),
                  pl.BlockSpec((8, 6 * D), const),
                  pl.BlockSpec((1, D), const),
                  pl.BlockSpec((1, D), const),
                  pl.BlockSpec((8, 6 * D), const)],
        out_specs=[pl.BlockSpec((RB, D), const),
                   pl.BlockSpec((RB, D), lambda i, p: (jnp.maximum(i - 1, 0), 0)),
                   pl.BlockSpec((RB, D), row)],
        scratch_shapes=[pltpu.VMEM((2, RB * TOP_K * ROW_SUB, LANE), F32),
                        pltpu.SemaphoreType.DMA((2,))],
    )
    return pl.pallas_call(
        _combine_kernel,
        grid_spec=grid_spec,
        out_shape=[jax.ShapeDtypeStruct((N_CTX, D), F32), jax.ShapeDtypeStruct((SEQ, D), F32),
                   jax.ShapeDtypeStruct((T, D), BF16)],
        compiler_params=_cparams(("arbitrary",)),
        name="combine",
    )(pos, y_slabs, gates, x1, mod, ln_g, ln_b, next_mod)


def _routing_tables(ids):
    expert = ids[:, :TOP_K].reshape(-1)
    n = expert.shape[0]
    onehot = (expert[:, None] == jnp.arange(N_EXPERTS, dtype=jnp.int32)[None, :]).astype(jnp.int32)
    csum = jnp.cumsum(onehot, axis=0)
    counts = csum[-1]
    rank = jnp.sum(onehot * csum, axis=1) - 1
    nb = (counts + MOE_BLOCK - 1) // MOE_BLOCK
    bstart = jnp.cumsum(nb) - nb
    pos = (jnp.sum(onehot * (bstart * MOE_BLOCK)[None, :], axis=1) + rank).astype(jnp.int32)
    tok = jnp.arange(n, dtype=jnp.int32) // TOP_K
    row_tok = jnp.zeros((N_ROWS,), jnp.int32).at[pos].set(tok)
    n_it = (nb + ITEM_BLKS - 1) // ITEM_BLKS
    it_end = jnp.cumsum(n_it)
    it_start = it_end - n_it
    idx = jnp.arange(N_ITEMS, dtype=jnp.int32)
    e_of = jnp.minimum(jnp.sum(it_end[None, :] <= idx[:, None], axis=1), N_EXPERTS - 1).astype(jnp.int32)
    k = idx - it_start[e_of]
    valid = idx < it_end[-1]
    item_exp = e_of
    item_blk0 = jnp.where(valid, bstart[e_of] + ITEM_BLKS * k, 0).astype(jnp.int32)
    item_nblk = jnp.where(valid, jnp.clip(nb[e_of] - ITEM_BLKS * k, 0, ITEM_BLKS), 0).astype(jnp.int32)
    n_used = jnp.sum(nb).astype(jnp.int32)[None]
    return item_exp, item_blk0, item_nblk, n_used, row_tok, pos


def _rope_tables():
    rows = SEQ // GRID_W
    row = jnp.repeat(jnp.arange(rows, dtype=F32), GRID_W)
    col = jnp.tile(jnp.arange(GRID_W, dtype=F32), rows)
    half = QK_ROPE // 2
    inv = ROPE_THETA ** (-jnp.arange(0, half, 2, dtype=F32) / half)
    ar = row[:, None] * inv
    ac = col[:, None] * inv
    ang = jnp.concatenate([ar, ar, ac, ac], -1)
    cos = jnp.concatenate([jnp.ones((N_CTX, QK_ROPE), F32), jnp.cos(ang)], 0)
    sin = jnp.concatenate([jnp.zeros((N_CTX, QK_ROPE), F32), jnp.sin(ang)], 0)
    one = jnp.ones((T, QK_NOPE), F32)
    zero_n = jnp.zeros((T, QK_NOPE), F32)
    zero_p = jnp.zeros((T, HEAD_PAD - QK_NOPE - QK_ROPE), F32)
    tq_c = jnp.concatenate([one, cos, zero_p], -1)
    tq_s = jnp.concatenate([zero_n, sin, zero_p], -1)
    tk = jnp.concatenate([cos, sin, jnp.zeros((T, LANE - 2 * QK_ROPE), F32)], -1)
    return tq_c, tq_s, tk


_ROT_SRC = np.concatenate([np.arange(8, 16), np.arange(0, 8), np.arange(24, 32), np.arange(16, 24)])
_ROT_SIGN = np.concatenate([-np.ones(8), np.ones(8), -np.ones(8), np.ones(8)]).astype(np.float32)


def _rot_cols(w):
    return w[..., _ROT_SRC] * _ROT_SIGN


def _layer_weights(l, w_in_t, b_in, d_w_uq, d_w_uk, d_w_uv, r_w_grp, r_b_grp, r_w_exp, r_b_exp, a_b_s, b_w_dw):
    bi = b_in[l]
    p = {}
    w_kr, b_kr = w_in_t[l, COL_KR:COL_G, :].T, bi[COL_KR:COL_G]
    padw = LANE - 2 * QK_ROPE
    p["wkr"] = jnp.concatenate([w_kr, _rot_cols(w_kr), jnp.zeros((D, padw), F32)], -1).astype(BF16)
    p["bkr"] = jnp.concatenate([b_kr, _rot_cols(b_kr), jnp.zeros((padw,), F32)])[None]
    wq = d_w_uq[l].reshape(Q_LORA, HEADS, QK_NOPE + QK_ROPE)
    zpad = jnp.zeros((Q_LORA, HEADS, HEAD_PAD - QK_NOPE - QK_ROPE), F32)
    p["wq"] = jnp.concatenate([wq, zpad], -1).reshape(Q_LORA, HEADS * HEAD_PAD).astype(BF16)
    p["wqr"] = jnp.concatenate([jnp.zeros((Q_LORA, HEADS, QK_NOPE), F32), _rot_cols(wq[..., QK_NOPE:]), zpad],
                               -1).reshape(Q_LORA, HEADS * HEAD_PAD).astype(BF16)
    wk = d_w_uk[l].reshape(KV_LORA, HEADS, QK_NOPE)
    p["wk"] = jnp.concatenate([wk, jnp.zeros((KV_LORA, HEADS, HEAD_PAD - QK_NOPE), F32)],
                              -1).reshape(KV_LORA, HEADS * HEAD_PAD).astype(BF16)
    p["wvt"] = d_w_uv[l].T.astype(BF16)
    e = np.zeros((LANE, HEADS, HEAD_PAD), np.float32)
    for j in range(QK_ROPE):
        e[j, :, QK_NOPE + j] = 1.0
        e[QK_ROPE + j, :, QK_NOPE + j] = 1.0
    p["e"] = jnp.asarray(e.reshape(LANE, HEADS * HEAD_PAD), dtype=BF16)
    p["bg"] = bi[None, COL_G:]
    padr = ROUTER_W - N_GROUPS - N_EXPERTS
    p["wr"] = _hilo(jnp.concatenate([r_w_grp[l], r_w_exp[l], jnp.zeros((D, padr), F32)], -1))
    p["br"] = jnp.concatenate([r_b_grp[l], r_b_exp[l], jnp.zeros((padr,), F32)])[None]
    p["bs"] = jnp.broadcast_to(a_b_s[l][:, :, None], (A_GROUPS, CHUNK, CHUNK))
    p["wdw"] = jnp.concatenate([b_w_dw[l], jnp.zeros((1, BW), F32)], 0)
    return p


def kernel(x, c, ctx, c_ctx, w_ada, b_ada, w_in, b_in, a_ln_g, a_ln_b, a_w_s, a_b_s, b_w_dw, b_b_dw, b_ln_g,
           b_ln_b, d_q_g, d_w_uq, d_kv_g, d_w_uk, d_w_uv, w_branch, w_o, ln1_g, ln1_b, r_w_grp, r_b_grp,
           r_w_exp, r_b_exp, e_w1, e_w3, e_w2, ln2_g, ln2_b):
    assert x.shape == (1, SEQ, D) and ctx.shape == (1, N_CTX, D)
    x_ctx, x_lat = ctx[0], x[0]
    w_in_t = jnp.transpose(w_in, (0, 2, 1))
    c_rep = jnp.broadcast_to(jnp.concatenate([c, c_ctx[None]], axis=0)[:, :, None], (N_COND, D, LANE))
    mods = ada_mod(c_rep, w_ada, b_ada)
    tq_c, tq_s, tk = _rope_tables()
    fc = _fourier_consts()
    b_in3 = b_in.reshape(DEPTH, 1, -1)
    w_br, w_ob = w_branch.astype(BF16), w_o.astype(BF16)

    h = modln(x_ctx, x_lat, mods[0])
    for l in range(DEPTH):
        p = _layer_weights(l, w_in_t, b_in, d_w_uq, d_w_uk, d_w_uv, r_w_grp, r_b_grp, r_w_exp, r_b_exp, a_b_s,
                           b_w_dw)
        mod = mods[l]
        br_a = mix_a(l, h, w_in_t, b_in3, a_ln_g[l][None], a_ln_b[l][None], a_w_s[l].astype(BF16), p["bs"])
        br_b = conv_ln(glu(l, h, w_in_t, b_in3), p["wdw"], b_b_dw[l][None], b_ln_g[l][None], b_ln_b[l][None])
        br_c = fourier_mix(l, h, w_in_t, b_in3, fc)
        q, k, vt = mla_proj(l, h, w_in_t, b_in3, p["wkr"], p["bkr"], d_q_g[l][None], d_kv_g[l][None], p["wq"],
                            p["wqr"], p["wk"], p["wvt"], p["e"], tq_c, tq_s, tk)
        br_dt = attention(q, k, vt)
        y = merge(l, h, br_a, br_b, br_c, br_dt, w_in_t, p["bg"], w_br)
        x1, h2, ids, gates = post_mix(l, y, x_ctx, x_lat, mod, w_ob, ln1_g[l][None], ln1_b[l][None], p["wr"],
                                      p["br"])
        item_exp, item_blk0, item_nblk, n_used, row_tok, pos = _routing_tables(ids)
        ys = experts(l, item_exp, item_blk0, item_nblk, n_used, row_tok, h2, e_w1, e_w3, e_w2)
        x_ctx, x_lat, h = combine(pos, ys, gates, x1, mod, ln2_g[l][None], ln2_b[l][None],
                                  mods[min(l + 1, DEPTH - 1)])
    return x_lat[None]
```

```python
import functools
import math

import numpy as np
import jax
import jax.numpy as jnp
from jax import lax
from jax.experimental import pallas as pl
from jax.experimental.pallas import tpu as pltpu

F32 = jnp.float32
BF16 = jnp.bfloat16

D = 2048
SEQ = 8192
N_CTX = 256
T = N_CTX + SEQ
DEPTH = 2
GRID_W = 64
BW = 512
CHUNK = 128
A_GROUPS = 4
CONV_W = 31
F_GROUPS = 4
QK_NOPE = 64
QK_ROPE = 32
V_HEAD = 64
HEADS = 8
Q_LORA = 512
KV_LORA = 256
ROPE_THETA = 10000.0
MLA_SCALE = (QK_NOPE + QK_ROPE) ** -0.5
N_GROUPS = 4
EXP_PER_GROUP = 8
N_EXPERTS = 32
TOP_K = 2
D_EXPERT = 1024
MOE_BLOCK = 256
COL_A = 0
COL_B = 1024
COL_C = 2048
COL_D = 2560
COL_KV = 3072
COL_KR = 3328
COL_G = 3360
ALPHA = (2 * DEPTH) ** 0.25
LN_EPS = 1e-6

LANE = 128
HEAD_PAD = 128
RB = 256
NRB = T // RB
FFT_N1 = 64
FFT_N2 = 128
ROW_SUB = D // LANE
VMEM_LIMIT = 56 * 1024 * 1024


def _cparams(sem):
    return pltpu.CompilerParams(dimension_semantics=sem, vmem_limit_bytes=VMEM_LIMIT)


def _dot(a, b):
    return jnp.dot(a, b, preferred_element_type=F32)


def _split(x):
    hi = lax.bitcast_convert_type(lax.bitcast_convert_type(x, jnp.int32) & jnp.int32(-65536), F32)
    return hi.astype(BF16), (x - hi).astype(BF16)


def _hilo(x):
    return jnp.stack(_split(jnp.asarray(x, F32)))


def _dot3(a, t_ref):
    a_hi, a_lo = _split(a)
    return _dot(a_hi, t_ref[0]) + (_dot(a_hi, t_ref[1]) + _dot(a_lo, t_ref[0]))


def _dot3_t(t_ref, b):
    b_hi, b_lo = _split(b)
    return _dot(t_ref[0], b_hi) + (_dot(t_ref[1], b_hi) + _dot(t_ref[0], b_lo))


def _ln0(x):
    mu = jnp.mean(x, axis=-1, keepdims=True)
    xc = x - mu
    var = jnp.mean(xc * xc, axis=-1, keepdims=True)
    return xc * lax.rsqrt(var + LN_EPS)


def _mod_row(mod_ref, q):
    lat = mod_ref[0:1, q * D:(q + 1) * D]
    ctx = mod_ref[1:2, q * D:(q + 1) * D]
    return jnp.where(pl.program_id(0) == 0, ctx, lat)


ADA_TN = 1536
ADA_TK = 256
N_COND = 2


def _ada_kernel(c_ref, w_ref, b_ref, o_ref):
    accs = [b_ref[0] for _ in range(N_COND)]
    for k0 in range(0, D, ADA_TK):
        w = w_ref[0, k0:k0 + ADA_TK, :]
        for r in range(N_COND):
            s = c_ref[r, k0:k0 + ADA_TK, :]
            s = s * jax.nn.sigmoid(s)
            s = jnp.concatenate([s] * (ADA_TN // LANE), axis=-1)
            accs[r] = accs[r] + jnp.sum(w * s, axis=0, keepdims=True)
    o_ref[0] = jnp.concatenate(accs + [jnp.zeros((8 - N_COND, ADA_TN), F32)], axis=0)


def ada_mod(c_rep, w_ada, b_ada):
    return pl.pallas_call(
        _ada_kernel,
        grid=(DEPTH, 6 * D // ADA_TN),
        in_specs=[pl.BlockSpec((N_COND, D, LANE), lambda l, j: (0, 0, 0)),
                  pl.BlockSpec((1, D, ADA_TN), lambda l, j: (l, 0, j)),
                  pl.BlockSpec((1, 1, ADA_TN), lambda l, j: (l, 0, j))],
        out_specs=pl.BlockSpec((1, 8, ADA_TN), lambda l, j: (l, 0, j)),
        out_shape=jax.ShapeDtypeStruct((DEPTH, 8, 6 * D), F32),
        compiler_params=_cparams(("arbitrary", "arbitrary")),
        name="ada_mod",
    )(c_rep, w_ada, b_ada.reshape(DEPTH, 1, 6 * D))


def _stream_specs():
    return [pl.BlockSpec((RB, D), lambda i, *_: (0, 0)),
            pl.BlockSpec((RB, D), lambda i, *_: (jnp.maximum(i - 1, 0), 0))]


def _stream_block(ctx_ref, lat_ref):
    return jnp.where(pl.program_id(0) == 0, ctx_ref[...], lat_ref[...])


def _modln_kernel(xc_ref, xl_ref, mod_ref, h_ref):
    h = _ln0(_stream_block(xc_ref, xl_ref)) * (1.0 + _mod_row(mod_ref, 1)) + _mod_row(mod_ref, 0)
    h_ref[...] = h.astype(BF16)


def modln(x_ctx, x_lat, mod):
    return pl.pallas_call(
        _modln_kernel,
        grid=(NRB,),
        in_specs=_stream_specs() + [pl.BlockSpec((8, 6 * D), lambda i: (0, 0))],
        out_specs=pl.BlockSpec((RB, D), lambda i: (i, 0)),
        out_shape=jax.ShapeDtypeStruct((T, D), BF16),
        compiler_params=_cparams(("arbitrary",)),
        name="modln",
    )(x_ctx, x_lat, mod)


def _w_in_spec(l, width, col):
    return pl.BlockSpec((1, width, D), lambda i: (l, col, 0))


def _b_in_spec(l, width, col):
    return pl.BlockSpec((1, 1, width), lambda i: (l, 0, col))


CAST_CHUNK = 256


def _cast_transposed(w_ref, wbf_ref):
    width = w_ref.shape[0]
    for c0 in range(0, width, CAST_CHUNK):
        wbf_ref[:, c0:c0 + CAST_CHUNK] = w_ref[c0:c0 + CAST_CHUNK, :].T.astype(BF16)


def _cast_once(w_ref, wbf_ref):
    @pl.when(pl.program_id(0) == 0)
    def _():
        _cast_transposed(w_ref.at[0], wbf_ref)


def _mix_a_kernel(h_ref, w_ref, b_ref, g_ref, bb_ref, ws_ref, bs_ref, o_ref, wbf_ref):
    _cast_once(w_ref, wbf_ref)
    uv = jax.nn.gelu(_dot(h_ref[...], wbf_ref[...]) + b_ref[0])
    u = uv[:, :BW]
    v = _ln0(uv[:, BW:]) * g_ref[...] + bb_ref[...]
    v = v.astype(BF16)
    gw = BW // A_GROUPS
    for ch in range(RB // CHUNK):
        rows = slice(ch * CHUNK, (ch + 1) * CHUNK)
        parts = [_dot(ws_ref[g], v[rows, g * gw:(g + 1) * gw]) + bs_ref[g] for g in range(A_GROUPS)]
        mixed = jnp.concatenate(parts, axis=-1)
        o_ref[rows, :] = (u[rows, :] * mixed).astype(BF16)


def mix_a(l, h, w_in, b_in, ln_g, ln_b, ws, bs):
    return pl.pallas_call(
        _mix_a_kernel,
        grid=(NRB,),
        in_specs=[pl.BlockSpec((RB, D), lambda i: (i, 0)),
                  _w_in_spec(l, 2 * BW, COL_A // (2 * BW)),
                  _b_in_spec(l, 2 * BW, COL_A // (2 * BW)),
                  pl.BlockSpec((1, BW), lambda i: (0, 0)),
                  pl.BlockSpec((1, BW), lambda i: (0, 0)),
                  pl.BlockSpec((A_GROUPS, CHUNK, CHUNK), lambda i: (0, 0, 0)),
                  pl.BlockSpec((A_GROUPS, CHUNK, CHUNK), lambda i: (0, 0, 0))],
        out_specs=pl.BlockSpec((RB, BW), lambda i: (i, 0)),
        out_shape=jax.ShapeDtypeStruct((T, BW), BF16),
        scratch_shapes=[pltpu.VMEM((D, 2 * BW), BF16)],
        compiler_params=_cparams(("arbitrary",)),
        name="mix_a",
    )(h, w_in, b_in, ln_g, ln_b, ws, bs)


def _glu_kernel(h_ref, w_ref, b_ref, o_ref, wbf_ref):
    _cast_once(w_ref, wbf_ref)
    ab = _dot(h_ref[...], wbf_ref[...]) + b_ref[0]
    o_ref[...] = ab[:, :BW] * jax.nn.sigmoid(ab[:, BW:])


def glu(l, h, w_in, b_in):
    return pl.pallas_call(
        _glu_kernel,
        grid=(NRB,),
        in_specs=[pl.BlockSpec((RB, D), lambda i: (i, 0)),
                  _w_in_spec(l, 2 * BW, COL_B // (2 * BW)),
                  _b_in_spec(l, 2 * BW, COL_B // (2 * BW))],
        out_specs=pl.BlockSpec((RB, BW), lambda i: (i, 0)),
        out_shape=jax.ShapeDtypeStruct((T, BW), F32),
        scratch_shapes=[pltpu.VMEM((D, 2 * BW), BF16)],
        compiler_params=_cparams(("arbitrary",)),
        name="glu",
    )(h, w_in, b_in)


CONV_HALO = 16


def _conv_kernel(prev_ref, cur_ref, next_ref, w_ref, b_ref, g_ref, bb_ref, o_ref, ext_ref):
    i = pl.program_id(0)
    has_prev = i >= 2
    has_next = jnp.logical_and(i >= 1, i < NRB - 1)
    ext_ref[0:CONV_HALO, :] = jnp.where(has_prev, prev_ref[RB - CONV_HALO:RB, :], 0.0)
    ext_ref[CONV_HALO:CONV_HALO + RB, :] = cur_ref[...]
    ext_ref[CONV_HALO + RB:2 * CONV_HALO + RB, :] = jnp.where(has_next, next_ref[0:CONV_HALO, :], 0.0)
    off = CONV_HALO - CONV_W // 2
    acc = jnp.zeros((RB, BW), F32)
    for k in range(CONV_W):
        acc = acc + ext_ref[off + k:off + k + RB, :] * w_ref[k:k + 1, :]
    y = _ln0(acc + b_ref[...]) * g_ref[...] + bb_ref[...]
    o_ref[...] = (y * jax.nn.sigmoid(y)).astype(BF16)


def conv_ln(y, w_dw, b_dw, ln_g, ln_b):
    return pl.pallas_call(
        _conv_kernel,
        grid=(NRB,),
        in_specs=[pl.BlockSpec((RB, BW), lambda i: (jnp.maximum(i - 1, 0), 0)),
                  pl.BlockSpec((RB, BW), lambda i: (i, 0)),
                  pl.BlockSpec((RB, BW), lambda i: (jnp.minimum(i + 1, NRB - 1), 0)),
                  pl.BlockSpec((CONV_W + 1, BW), lambda i: (0, 0)),
                  pl.BlockSpec((1, BW), lambda i: (0, 0)),
                  pl.BlockSpec((1, BW), lambda i: (0, 0)),
                  pl.BlockSpec((1, BW), lambda i: (0, 0))],
        out_specs=pl.BlockSpec((RB, BW), lambda i: (i, 0)),
        out_shape=jax.ShapeDtypeStruct((T, BW), BF16),
        scratch_shapes=[pltpu.VMEM((RB + 2 * CONV_HALO, BW), F32)],
        compiler_params=_cparams(("arbitrary",)),
        name="conv_ln",
    )(y, y, y, w_dw, b_dw, ln_g, ln_b)


def _dft_tables(n):
    k = np.arange(n, dtype=np.int64)
    ang = 2.0 * np.pi * ((k[:, None] * k[None, :]) % n).astype(np.float64) / n
    return np.cos(ang), np.sin(ang)


def _fproj_kernel(h_ref, w_ref, b_ref, cs_ref, ar_ref, ai_ref, wbf_ref):
    _cast_once(w_ref, wbf_ref)
    z = _dot(h_ref[...], wbf_ref[...]) + b_ref[0]
    a = _dot3(z, cs_ref)
    ar_ref[...] = a[:, :BW]
    ai_ref[...] = a[:, BW:]


def fourier_proj(l, h, w_in, b_in, cs):
    return pl.pallas_call(
        _fproj_kernel,
        grid=(NRB,),
        in_specs=[pl.BlockSpec((RB, D), lambda i: (i, 0)),
                  _w_in_spec(l, BW, COL_C // BW),
                  _b_in_spec(l, BW, COL_C // BW),
                  pl.BlockSpec((2, BW, 2 * BW), lambda i: (0, 0, 0))],
        out_specs=[pl.BlockSpec((RB, BW), lambda i: (i, 0)),
                   pl.BlockSpec((RB, BW), lambda i: (i, 0))],
        out_shape=[jax.ShapeDtypeStruct((T, BW), F32), jax.ShapeDtypeStruct((T, BW), F32)],
        scratch_shapes=[pltpu.VMEM((D, BW), BF16)],
        compiler_params=_cparams(("arbitrary",)),
        name="fourier_proj",
    )(h, w_in, b_in, cs)


FFT_TILE_N2 = 4


def _fft1_kernel(ar_ref, ai_ref, m_ref, ct_ref, st_ref, br_ref, bi_ref):
    x = jnp.concatenate([ar_ref[...], ai_ref[...]], axis=0)
    b = _dot3_t(m_ref, x)
    b_re, b_im = b[:FFT_N1], b[FFT_N1:]
    reps = BW // LANE
    for j in range(FFT_TILE_N2):
        ct = jnp.concatenate([ct_ref[j]] * reps, axis=-1)
        st = jnp.concatenate([st_ref[j]] * reps, axis=-1)
        lanes = slice(j * BW, (j + 1) * BW)
        br_ref[:, lanes] = b_re[:, lanes] * ct + b_im[:, lanes] * st
        bi_ref[:, lanes] = b_im[:, lanes] * ct - b_re[:, lanes] * st


def fft_stage1(ar2d, ai2d, m1, ct, st):
    tn = FFT_TILE_N2 * BW
    return pl.pallas_call(
        _fft1_kernel,
        grid=(FFT_N2 // FFT_TILE_N2,),
        in_specs=[pl.BlockSpec((FFT_N1, tn), lambda j: (0, j)),
                  pl.BlockSpec((FFT_N1, tn), lambda j: (0, j)),
                  pl.BlockSpec((2, 2 * FFT_N1, 2 * FFT_N1), lambda j: (0, 0, 0)),
                  pl.BlockSpec((FFT_TILE_N2, FFT_N1, LANE), lambda j: (j, 0, 0)),
                  pl.BlockSpec((FFT_TILE_N2, FFT_N1, LANE), lambda j: (j, 0, 0))],
        out_specs=[pl.BlockSpec((FFT_N1, tn), lambda j: (0, j)),
                   pl.BlockSpec((FFT_N1, tn), lambda j: (0, j))],
        out_shape=[jax.ShapeDtypeStruct((FFT_N1, FFT_N2 * BW), F32)] * 2,
        compiler_params=_cparams(("arbitrary",)),
        name="fft_stage1",
    )(ar2d, ai2d, m1, ct, st)


FFT_TILE_K1 = 8


def _fft2_kernel(br_ref, bi_ref, c_ref, s_ref, p_ref, o_ref, *, scale):
    ys = [(_dot3_t(c_ref, br_ref[a]) + _dot3_t(s_ref, bi_ref[a])) * scale for a in range(FFT_TILE_K1)]
    y = jnp.concatenate(ys, axis=0).astype(BF16)
    y = _dot(p_ref[...], y)
    o_ref[...] = y.reshape(FFT_N2, FFT_TILE_K1, BW)


def fft_stage2(br3d, bi3d, c128, s128, perm, scale):
    return pl.pallas_call(
        functools.partial(_fft2_kernel, scale=scale),
        grid=(FFT_N1 // FFT_TILE_K1,),
        in_specs=[pl.BlockSpec((FFT_TILE_K1, FFT_N2, BW), lambda a: (a, 0, 0)),
                  pl.BlockSpec((FFT_TILE_K1, FFT_N2, BW), lambda a: (a, 0, 0)),
                  pl.BlockSpec((2, FFT_N2, FFT_N2), lambda a: (0, 0, 0)),
                  pl.BlockSpec((2, FFT_N2, FFT_N2), lambda a: (0, 0, 0)),
                  pl.BlockSpec((FFT_TILE_K1 * FFT_N2, FFT_TILE_K1 * FFT_N2), lambda a: (0, 0))],
        out_specs=pl.BlockSpec((FFT_N2, FFT_TILE_K1, BW), lambda a: (0, a, 0)),
        out_shape=jax.ShapeDtypeStruct((FFT_N2, FFT_N1, BW), F32),
        compiler_params=_cparams(("arbitrary",)),
        name="fft_stage2",
    )(br3d, bi3d, c128, s128, perm)


def _dft_ctx_kernel(ar_ref, ai_ref, c_ref, s_ref, o_ref, *, scale):
    o_ref[...] = (_dot3_t(c_ref, ar_ref[...]) + _dot3_t(s_ref, ai_ref[...])) * scale


def dft_ctx(ar, ai, c, s, scale):
    return pl.pallas_call(
        functools.partial(_dft_ctx_kernel, scale=scale),
        grid=(1,),
        in_specs=[pl.BlockSpec((N_CTX, BW), lambda i: (0, 0)),
                  pl.BlockSpec((N_CTX, BW), lambda i: (0, 0)),
                  pl.BlockSpec((2, N_CTX, N_CTX), lambda i: (0, 0, 0)),
                  pl.BlockSpec((2, N_CTX, N_CTX), lambda i: (0, 0, 0))],
        out_specs=pl.BlockSpec((N_CTX, BW), lambda i: (0, 0)),
        out_shape=jax.ShapeDtypeStruct((N_CTX, BW), F32),
        compiler_params=_cparams(("arbitrary",)),
        name="dft_ctx",
    )(ar, ai, c, s)


def _fourier_consts():
    c64, s64 = _dft_tables(FFT_N1)
    m1 = np.block([[c64, s64], [-s64, c64]]).astype(np.float32)
    k1 = np.arange(FFT_N1, dtype=np.float64)
    n2 = np.arange(FFT_N2, dtype=np.float64)
    phi = 2.0 * np.pi * n2[:, None] * k1[None, :] / SEQ
    ct = np.repeat(np.cos(phi)[:, :, None], LANE, axis=2).astype(np.float32)
    st = np.repeat(np.sin(phi)[:, :, None], LANE, axis=2).astype(np.float32)
    c128, s128 = _dft_tables(FFT_N2)
    gw = BW // F_GROUPS
    cg, sg = _dft_tables(gw)
    cs = np.zeros((BW, 2 * BW), np.float32)
    for g in range(F_GROUPS):
        sl = slice(g * gw, (g + 1) * gw)
        cs[sl, sl] = cg
        cs[sl, BW + g * gw:BW + (g + 1) * gw] = -sg
    rows = np.arange(FFT_TILE_K1 * FFT_N2)
    perm = np.zeros((rows.size, rows.size), np.float32)
    perm[rows, (rows % FFT_TILE_K1) * FFT_N2 + rows // FFT_TILE_K1] = 1.0
    cc, sc = _dft_tables(N_CTX)
    return dict(m1=_hilo(m1), ct=jnp.asarray(ct), st=jnp.asarray(st), c128=_hilo(c128), s128=_hilo(s128),
                cs=_hilo(cs), perm=jnp.asarray(perm, dtype=BF16), cc=_hilo(cc), sc=_hilo(sc))


def fourier_mix(l, h, w_in, b_in, fc):
    gw = BW // F_GROUPS
    ar, ai = fourier_proj(l, h, w_in, b_in, fc["cs"])
    y_ctx = dft_ctx(ar, ai, fc["cc"], fc["sc"], 1.0 / math.sqrt(N_CTX * gw))
    ar2 = ar[N_CTX:].reshape(FFT_N1, FFT_N2 * BW)
    ai2 = ai[N_CTX:].reshape(FFT_N1, FFT_N2 * BW)
    br, bi = fft_stage1(ar2, ai2, fc["m1"], fc["ct"], fc["st"])
    y = fft_stage2(br.reshape(FFT_N1, FFT_N2, BW), bi.reshape(FFT_N1, FFT_N2, BW),
                   fc["c128"], fc["s128"], fc["perm"], 1.0 / math.sqrt(SEQ * gw))
    return jnp.concatenate([y_ctx, y.reshape(SEQ, BW)], axis=0)


VT_ROWS = 80
LOG2E = math.log2(math.e)


def _mla_proj_kernel(h_ref, wcq_ref, bcq_ref, wkv_ref, bkv_ref, wkr_ref, bkr_ref, qg_ref, kvg_ref, wq_ref, wqr_ref,
                     wk_ref, wvt_ref, e_ref, tq_c_ref, tq_s_ref, tk_ref, q_ref, k_ref, vt_ref, wcq_bf, wkv_bf):
    _cast_once(wcq_ref, wcq_bf)
    _cast_once(wkv_ref, wkv_bf)
    h = h_ref[...]
    cq = _dot(h, wcq_bf[...]) + bcq_ref[0]
    cq = cq * lax.rsqrt(jnp.mean(cq * cq, axis=-1, keepdims=True) + LN_EPS) * qg_ref[...]
    cq = cq.astype(BF16)
    cos_t = jnp.concatenate([tq_c_ref[...]] * HEADS, axis=-1)
    sin_t = jnp.concatenate([tq_s_ref[...]] * HEADS, axis=-1)
    q = (_dot(cq, wq_ref[...]) * cos_t + _dot(cq, wqr_ref[...]) * sin_t) * (MLA_SCALE * LOG2E)
    q_ref[...] = q.astype(BF16)
    ckv = _dot(h, wkv_bf[...]) + bkv_ref[0]
    ckv = ckv * lax.rsqrt(jnp.mean(ckv * ckv, axis=-1, keepdims=True) + LN_EPS) * kvg_ref[...]
    ckv = ckv.astype(BF16)
    kr = ((_dot(h, wkr_ref[...]) + bkr_ref[...]) * tk_ref[...]).astype(BF16)
    k_ref[...] = (_dot(ckv, wk_ref[...]) + _dot(kr, e_ref[...])).astype(BF16)
    vt = lax.dot_general(wvt_ref[...], ckv, (((1,), (1,)), ((), ())), preferred_element_type=F32)
    ones_rows = (lax.broadcasted_iota(jnp.int32, (VT_ROWS - V_HEAD, RB), 0) == 0).astype(BF16)
    for hd in range(HEADS):
        vt_ref[0, hd * VT_ROWS:hd * VT_ROWS + V_HEAD, :] = vt[hd * V_HEAD:(hd + 1) * V_HEAD, :].astype(BF16)
        vt_ref[0, hd * VT_ROWS + V_HEAD:(hd + 1) * VT_ROWS, :] = ones_rows


def mla_proj(l, h, w_in, b_in, wkr, bkr, qg, kvg, wq, wqr, wk, wvt, e, tq_c, tq_s, tk):
    const = lambda i: (0, 0)
    row = lambda i: (i, 0)
    return pl.pallas_call(
        _mla_proj_kernel,
        grid=(NRB,),
        in_specs=[pl.BlockSpec((RB, D), row),
                  _w_in_spec(l, Q_LORA, COL_D // Q_LORA),
                  _b_in_spec(l, Q_LORA, COL_D // Q_LORA),
                  _w_in_spec(l, KV_LORA, COL_KV // KV_LORA),
                  _b_in_spec(l, KV_LORA, COL_KV // KV_LORA),
                  pl.BlockSpec((D, LANE), const),
                  pl.BlockSpec((1, LANE), const),
                  pl.BlockSpec((1, Q_LORA), const),
                  pl.BlockSpec((1, KV_LORA), const),
                  pl.BlockSpec((Q_LORA, HEADS * HEAD_PAD), const),
                  pl.BlockSpec((Q_LORA, HEADS * HEAD_PAD), const),
                  pl.BlockSpec((KV_LORA, HEADS * HEAD_PAD), const),
                  pl.BlockSpec((HEADS * V_HEAD, KV_LORA), const),
                  pl.BlockSpec((LANE, HEADS * HEAD_PAD), const),
                  pl.BlockSpec((RB, HEAD_PAD), row),
                  pl.BlockSpec((RB, HEAD_PAD), row),
                  pl.BlockSpec((RB, LANE), row)],
        out_specs=[pl.BlockSpec((RB, HEADS * HEAD_PAD), row),
                   pl.BlockSpec((RB, HEADS * HEAD_PAD), row),
                   pl.BlockSpec((1, HEADS * VT_ROWS, RB), lambda i: (i, 0, 0))],
        out_shape=[jax.ShapeDtypeStruct((T, HEADS * HEAD_PAD), BF16),
                   jax.ShapeDtypeStruct((T, HEADS * HEAD_PAD), BF16),
                   jax.ShapeDtypeStruct((NRB, HEADS * VT_ROWS, RB), BF16)],
        scratch_shapes=[pltpu.VMEM((D, Q_LORA), BF16), pltpu.VMEM((D, KV_LORA), BF16)],
        compiler_params=_cparams(("arbitrary",)),
        name="mla_proj",
    )(h, w_in, b_in, w_in, b_in, wkr, bkr, qg, kvg, wq, wqr, wk, wvt, e, tq_c, tq_s, tk)


KV_TILE = RB
KV_STAGE = 1024
N_STAGE = SEQ // KV_STAGE
TILES_PER_STAGE = KV_STAGE // KV_TILE


HEAD_GROUP = 2


def _attn_kernel(q_ref, k_ref, vt_ref, o_ref, sa_ref, sb_ref, m_ref, acc_ref):
    i = pl.program_id(1)
    heads = range(HEAD_GROUP)
    qs = [q_ref[:, g * HEAD_PAD:(g + 1) * HEAD_PAD] for g in heads]

    def scores(g, start, n):
        kt = k_ref[pl.ds(start, n), g * HEAD_PAD:(g + 1) * HEAD_PAD]
        return lax.dot_general(kt, qs[g], (((1,), (1,)), ((), ())),
                               preferred_element_type=F32)

    def values(g, tile):
        return vt_ref[tile, g * VT_ROWS:(g + 1) * VT_ROWS, :]

    def softmax_pv(g, s_ref, tile0, m, acc):
        m_new = jnp.maximum(m, jnp.max(s_ref[g], axis=0, keepdims=True))
        p = jnp.exp2(s_ref[g] - m_new).astype(BF16)
        acc = jnp.exp2(m - m_new) * acc
        for u in range(TILES_PER_STAGE):
            acc = acc + _dot(values(g, tile0 + u), p[u * KV_TILE:(u + 1) * KV_TILE, :])
        return m_new, acc

    for g in heads:
        s0 = scores(g, 0, N_CTX)
        m0 = jnp.max(s0, axis=0, keepdims=True)
        m_ref[g] = m0
        acc_ref[g] = _dot(values(g, 0), jnp.exp2(s0 - m0).astype(BF16))

    @pl.when(i > 0)
    def _():
        first_tile = N_CTX // KV_TILE
        for g in heads:
            sa_ref[g] = scores(g, N_CTX, KV_STAGE)

        def body(j, carry):
            ms, accs = list(carry[0]), list(carry[1])
            base = pl.multiple_of(N_CTX + 2 * j * KV_STAGE, KV_TILE)
            tile = first_tile + 2 * j * TILES_PER_STAGE
            for g in heads:
                sb_ref[g] = scores(g, base + KV_STAGE, KV_STAGE)
            for g in heads:
                ms[g], accs[g] = softmax_pv(g, sa_ref, tile, ms[g], accs[g])
            for g in heads:
                sa_ref[g] = scores(g, base + 2 * KV_STAGE, KV_STAGE)
            for g in heads:
                ms[g], accs[g] = softmax_pv(g, sb_ref, tile + TILES_PER_STAGE, ms[g], accs[g])
            return tuple(ms), tuple(accs)

        init = (tuple(m_ref[g] for g in heads), tuple(acc_ref[g] for g in heads))
        ms, accs = lax.fori_loop(0, N_STAGE // 2 - 1, body, init)
        ms, accs = list(ms), list(accs)
        for g in heads:
            sb_ref[g] = scores(g, N_CTX + (N_STAGE - 1) * KV_STAGE, KV_STAGE)
        for g in heads:
            ms[g], accs[g] = softmax_pv(g, sa_ref, first_tile + (N_STAGE - 2) * TILES_PER_STAGE, ms[g], accs[g])
        for g in heads:
            ms[g], accs[g] = softmax_pv(g, sb_ref, first_tile + (N_STAGE - 1) * TILES_PER_STAGE, ms[g], accs[g])
            acc_ref[g] = accs[g]

    for g in heads:
        acc = acc_ref[g]
        o_ref[g * V_HEAD:(g + 1) * V_HEAD, :] = acc[:V_HEAD, :] / acc[V_HEAD:V_HEAD + 1, :]


def attention(q, k, vt):
    return pl.pallas_call(
        _attn_kernel,
        grid=(HEADS // HEAD_GROUP, NRB),
        in_specs=[pl.BlockSpec((RB, HEAD_GROUP * HEAD_PAD), lambda h, i: (i, h)),
                  pl.BlockSpec((T, HEAD_GROUP * HEAD_PAD), lambda h, i: (0, h)),
                  pl.BlockSpec((NRB, HEAD_GROUP * VT_ROWS, KV_TILE), lambda h, i: (0, h, 0))],
        out_specs=pl.BlockSpec((HEAD_GROUP * V_HEAD, RB), lambda h, i: (h, i)),
        out_shape=jax.ShapeDtypeStruct((HEADS * V_HEAD, T), F32),
        scratch_shapes=[pltpu.VMEM((HEAD_GROUP, KV_STAGE, RB), F32),
                        pltpu.VMEM((HEAD_GROUP, KV_STAGE, RB), F32),
                        pltpu.VMEM((HEAD_GROUP, 1, RB), F32),
                        pltpu.VMEM((HEAD_GROUP, VT_ROWS, RB), F32)],
        compiler_params=_cparams(("arbitrary", "arbitrary")),
        name="attention",
    )(q, k, vt)


MERGE_TN = 512


N_BRANCH = 4


MERGE_RB = 768


def _merge_kernel(h_ref, a_ref, b_ref, c_ref, dt_ref, *refs):
    wg_refs, bg_refs = refs[:N_BRANCH], refs[N_BRANCH:2 * N_BRANCH]
    wb_ref, y_ref, wg_bf = refs[2 * N_BRANCH:]

    @pl.when(pl.program_id(1) == 0)
    def _():
        for n in range(N_BRANCH):
            _cast_transposed(wg_refs[n], wg_bf.at[n])

    h = h_ref[...]
    branches = (a_ref[...], b_ref[...], c_ref[...].astype(BF16), dt_ref[...].T.astype(BF16))
    y = jnp.zeros((MERGE_RB, MERGE_TN), F32)
    for n, br in enumerate(branches):
        gate = jax.nn.sigmoid(_dot(h, wg_bf[n]) + bg_refs[n][...])
        y = y + gate * _dot(br, wb_ref[0, n])
    y_ref[...] = y.astype(BF16)


def merge(l, h, br_a, br_b, br_c, br_dt, w_in_t, bg, wb):
    row = lambda j, i: (i, 0)
    n_col = D // MERGE_TN
    gate_specs = [pl.BlockSpec((pl.Squeezed(), pl.Element(MERGE_TN), pl.Element(D)),
                               functools.partial(
                                   lambda j, i, n: (l, pl.multiple_of(COL_G + n * D + j * MERGE_TN, QK_ROPE), 0), n=n),
                               pipeline_mode=pl.Buffered(1))
                  for n in range(N_BRANCH)]
    bias_specs = [pl.BlockSpec((1, MERGE_TN), functools.partial(lambda j, i, n: (0, n * n_col + j), n=n))
                  for n in range(N_BRANCH)]
    return pl.pallas_call(
        _merge_kernel,
        grid=(n_col, T // MERGE_RB),
        in_specs=[pl.BlockSpec((MERGE_RB, D), row),
                  pl.BlockSpec((MERGE_RB, BW), row),
                  pl.BlockSpec((MERGE_RB, BW), row),
                  pl.BlockSpec((MERGE_RB, BW), row),
                  pl.BlockSpec((BW, MERGE_RB), lambda j, i: (0, i))]
                 + gate_specs + bias_specs
                 + [pl.BlockSpec((1, N_BRANCH, BW, MERGE_TN), lambda j, i: (l, 0, 0, j))],
        out_specs=pl.BlockSpec((MERGE_RB, MERGE_TN), lambda j, i: (i, j)),
        out_shape=jax.ShapeDtypeStruct((T, D), BF16),
        scratch_shapes=[pltpu.VMEM((N_BRANCH, D, MERGE_TN), BF16)],
        compiler_params=_cparams(("arbitrary", "arbitrary")),
        name="merge",
    )(h, br_a, br_b, br_c, br_dt, *([w_in_t] * N_BRANCH), *([bg] * N_BRANCH), wb)


ROUTER_W = 128


def _post_mix_kernel(y_ref, xc_ref, xl_ref, mod_ref, wo_ref, g_ref, b_ref, wr_ref, br_ref,
                     x1_ref, h2_ref, ids_ref, gates_ref):
    mix = _dot(y_ref[...], wo_ref[0])
    x1 = _ln0(ALPHA * _stream_block(xc_ref, xl_ref) + _mod_row(mod_ref, 2) * mix) * g_ref[...] + b_ref[...]
    x1_ref[...] = x1
    h2 = _ln0(x1) * (1.0 + _mod_row(mod_ref, 4)) + _mod_row(mod_ref, 3)
    _rows_to_slabs(h2, h2_ref)
    logits = _dot3(h2, wr_ref) + br_ref[...]
    lane = lax.broadcasted_iota(jnp.int32, (RB, ROUTER_W), 1).astype(F32)
    neg = jnp.float32(-jnp.inf)
    big = jnp.float32(ROUTER_W)
    gl = jnp.where(lane < N_GROUPS, logits, neg)
    gmax = jnp.max(gl, axis=-1, keepdims=True)
    g_idx = jnp.min(jnp.where(gl == gmax, lane, big), axis=-1, keepdims=True)
    g_p = 1.0 / jnp.sum(jnp.exp(gl - gmax), axis=-1, keepdims=True)
    e_lane = lane - N_GROUPS
    in_grp = jnp.logical_and(e_lane >= g_idx * EXP_PER_GROUP, e_lane < (g_idx + 1) * EXP_PER_GROUP)
    el = jnp.where(in_grp, logits, neg)
    v1 = jnp.max(el, axis=-1, keepdims=True)
    i1 = jnp.min(jnp.where(el == v1, lane, big), axis=-1, keepdims=True)
    el2 = jnp.where(lane == i1, neg, el)
    v2 = jnp.max(el2, axis=-1, keepdims=True)
    i2 = jnp.min(jnp.where(el2 == v2, lane, big), axis=-1, keepdims=True)
    e21 = jnp.exp(v2 - v1)
    p1 = 1.0 / (1.0 + e21)
    p2 = e21 * p1
    ids_ref[...] = jnp.where(lane == 0, i1 - N_GROUPS, jnp.where(lane == 1, i2 - N_GROUPS, 0.0)).astype(jnp.int32)
    gates_ref[...] = jnp.where(lane == 0, p1 * g_p, jnp.where(lane == 1, p2 * g_p, 0.0))


def post_mix(l, y, x_ctx, x_lat, mod, wo, ln_g, ln_b, wr, br):
    row = lambda i: (i, 0)
    const = lambda i: (0, 0)
    return pl.pallas_call(
        _post_mix_kernel,
        grid=(NRB,),
        in_specs=[pl.BlockSpec((RB, D), row)] + _stream_specs()
                 + [pl.BlockSpec((8, 6 * D), const),
                  pl.BlockSpec((1, D, D), lambda i: (l, 0, 0)),
                  pl.BlockSpec((1, D), const),
                  pl.BlockSpec((1, D), const),
                  pl.BlockSpec((2, D, ROUTER_W), lambda i: (0, 0, 0)),
                  pl.BlockSpec((1, ROUTER_W), const)],
        out_specs=[pl.BlockSpec((RB, D), row),
                   pl.BlockSpec((RB * ROW_SUB, LANE), row),
                   pl.BlockSpec((RB, ROUTER_W), row),
                   pl.BlockSpec((RB, ROUTER_W), row)],
        out_shape=[jax.ShapeDtypeStruct((T, D), F32),
                   jax.ShapeDtypeStruct((T * ROW_SUB, LANE), F32),
                   jax.ShapeDtypeStruct((T, ROUTER_W), jnp.int32),
                   jax.ShapeDtypeStruct((T, ROUTER_W), F32)],
        compiler_params=_cparams(("arbitrary",)),
        name="post_mix",
    )(y, x_ctx, x_lat, mod, wo, ln_g, ln_b, wr, br)


N_ROWS = -(-(T * TOP_K + N_EXPERTS * (MOE_BLOCK - 1)) // MOE_BLOCK) * MOE_BLOCK
N_BLK = N_ROWS // MOE_BLOCK


GATHER_PRIORITY = 1


GATHER_UNROLL = 8


def _row_gather(src_hbm, idx_ref, base, n, dst_ref, sem, first=0):
    def issue(t, _):
        for u in range(GATHER_UNROLL):
            r = first + t * GATHER_UNROLL + u
            src_row = pl.multiple_of(idx_ref[base + r] * ROW_SUB, ROW_SUB)
            dst_row = pl.multiple_of(r * ROW_SUB, ROW_SUB)
            pltpu.make_async_copy(src_hbm.at[pl.ds(src_row, ROW_SUB), :],
                                  dst_ref.at[pl.ds(dst_row, ROW_SUB), :], sem).start(priority=GATHER_PRIORITY)
        return 0
    lax.fori_loop(0, n // GATHER_UNROLL, issue, 0)


def _row_gather_wait(src_hbm, n, dst_ref, sem):
    pltpu.make_async_copy(src_hbm.at[pl.ds(0, n * ROW_SUB), :], dst_ref, sem).wait()


def _rows_from_slabs(buf_ref, start, n, stride):
    return jnp.concatenate([buf_ref[pl.ds(start + c, n, stride=stride), :] for c in range(ROW_SUB)], axis=-1)


def _rows_to_slabs(val, out_ref):
    n = val.shape[0]
    for c in range(ROW_SUB):
        out_ref[pl.ds(c, n, stride=ROW_SUB), :] = val[:, c * LANE:(c + 1) * LANE]


ITEM_BLKS = 3
ITEM_ROWS = ITEM_BLKS * MOE_BLOCK
N_ITEMS = N_BLK // ITEM_BLKS + N_EXPERTS
EXP_CHUNK = 512
N_CHUNK = D_EXPERT // EXP_CHUNK
BLK_SLABS = MOE_BLOCK * ROW_SUB


def _expert_kernel(item_exp_ref, item_blk0_ref, item_nblk_ref, n_used_ref, row_tok_ref, h_hbm, w1_ref, w3_ref, w2_ref,
                   y_hbm, xslab, x_bf, yacc, stage, w1c, w3c, w2c, sem_x, sem_y):
    i, j = pl.program_id(0), pl.program_id(1)
    nblk = item_nblk_ref[i]
    blk0 = item_blk0_ref[i]

    def gather_item(item, first, n):
        _row_gather(h_hbm, row_tok_ref, item_blk0_ref[item] * MOE_BLOCK, n, xslab, sem_x, first=first)

    @pl.when(jnp.logical_and(i == 0, j == 0))
    def _():
        gather_item(0, 0, nblk * MOE_BLOCK)

    w1c[...] = w1_ref[0, 0].astype(BF16)
    w3c[...] = w3_ref[0, 0].astype(BF16)
    w2c[...] = w2_ref[0, 0].astype(BF16)
    last = j == N_CHUNK - 1

    def rows_of(b):
        return pl.ds(pl.multiple_of(b * MOE_BLOCK, MOE_BLOCK), MOE_BLOCK)

    def partial_out(xb):
        a = _dot(xb, w1c[...])
        g = _dot(xb, w3c[...])
        return _dot((a * jax.nn.sigmoid(a) * g).astype(BF16), w2c[...])

    def y_copy(b, slot):
        dst = pl.multiple_of((blk0 + b) * BLK_SLABS, BLK_SLABS)
        return pltpu.make_async_copy(stage.at[slot], y_hbm.at[pl.ds(dst, BLK_SLABS), :], sem_y.at[slot])

    @pl.when(j == 0)
    def _():
        def wait_block(b, _):
            pltpu.make_async_copy(h_hbm.at[pl.ds(0, BLK_SLABS), :], xslab.at[pl.ds(0, BLK_SLABS), :], sem_x).wait()
            return 0
        lax.fori_loop(0, nblk, wait_block, 0)

        def first_block(b, _):
            xb = _rows_from_slabs(xslab, b * BLK_SLABS, MOE_BLOCK, ROW_SUB).astype(BF16)
            x_bf[rows_of(b), :] = xb
            yacc[rows_of(b), :] = partial_out(xb)
            return 0
        lax.fori_loop(0, nblk, first_block, 0)

    @pl.when(jnp.logical_and(j > 0, jnp.logical_not(last)))
    def _():
        def mid_block(b, _):
            yacc[rows_of(b), :] += partial_out(x_bf[rows_of(b), :])
            return 0
        lax.fori_loop(0, nblk, mid_block, 0)

    @pl.when(last)
    def _():
        def last_block(b, _):
            slot = b % 2

            @pl.when(b >= 2)
            def _():
                y_copy(b - 2, slot).wait()
            _rows_to_slabs(yacc[rows_of(b), :] + partial_out(x_bf[rows_of(b), :]), stage.at[slot])
            y_copy(b, slot).start()
            return 0
        lax.fori_loop(0, nblk, last_block, 0)

        for back in (1, 2):
            @pl.when(nblk >= back)
            def _():
                b = nblk - back
                y_copy(b, b % 2).wait()

    @pl.when(i + 1 < N_ITEMS)
    def _():
        share = item_nblk_ref[i + 1] * (MOE_BLOCK // N_CHUNK)
        gather_item(i + 1, j * share, share)

    @pl.when(jnp.logical_and(i == N_ITEMS - 1, last))
    def _():
        stage[0] = jnp.zeros((BLK_SLABS, LANE), F32)

        def fill(b, _):
            dst = pl.multiple_of(b * BLK_SLABS, BLK_SLABS)
            cp = pltpu.make_async_copy(stage.at[0], y_hbm.at[pl.ds(dst, BLK_SLABS), :], sem_y.at[0])
            cp.start()
            cp.wait()
            return 0
        lax.fori_loop(n_used_ref[0], N_BLK, fill, 0)


def experts(l, item_exp, item_blk0, item_nblk, n_used, row_tok, h2_slabs, w1, w3, w2):
    wspec = lambda blk, imap: pl.BlockSpec(blk, imap)
    grid_spec = pltpu.PrefetchScalarGridSpec(
        num_scalar_prefetch=5,
        grid=(N_ITEMS, N_CHUNK),
        in_specs=[pl.BlockSpec(memory_space=pl.ANY),
                  wspec((1, 1, D, EXP_CHUNK), lambda i, j, ie, *_: (l, ie[i], 0, j)),
                  wspec((1, 1, D, EXP_CHUNK), lambda i, j, ie, *_: (l, ie[i], 0, j)),
                  wspec((1, 1, EXP_CHUNK, D), lambda i, j, ie, *_: (l, ie[i], j, 0))],
        out_specs=pl.BlockSpec(memory_space=pl.ANY),
        scratch_shapes=[pltpu.VMEM((ITEM_BLKS * BLK_SLABS, LANE), F32),
                        pltpu.VMEM((ITEM_ROWS, D), BF16),
                        pltpu.VMEM((ITEM_ROWS, D), F32),
                        pltpu.VMEM((2, BLK_SLABS, LANE), F32),
                        pltpu.VMEM((D, EXP_CHUNK), BF16),
                        pltpu.VMEM((D, EXP_CHUNK), BF16),
                        pltpu.VMEM((EXP_CHUNK, D), BF16),
                        pltpu.SemaphoreType.DMA(()),
                        pltpu.SemaphoreType.DMA((2,))],
    )
    return pl.pallas_call(
        _expert_kernel,
        grid_spec=grid_spec,
        out_shape=jax.ShapeDtypeStruct((N_ROWS * ROW_SUB, LANE), F32),
        compiler_params=_cparams(("arbitrary", "arbitrary")),
        name="experts",
    )(item_exp, item_blk0, item_nblk, n_used, row_tok, h2_slabs, w1, w3, w2)


COMBINE_SUB = 64


def _combine_kernel(pos_ref, y_hbm, gates_ref, x1_ref, mod_ref, g_ref, b_ref, nmod_ref,
                    x2c_ref, x2l_ref, hn_ref, ybuf, sems):
    i = pl.program_id(0)
    slot = i % 2
    n = RB * TOP_K
    n_sub = RB // COMBINE_SUB

    @pl.when(i == 0)
    def _():
        _row_gather(y_hbm, pos_ref, 0, n, ybuf.at[0], sems.at[0])

    _row_gather_wait(y_hbm, n, ybuf.at[slot], sems.at[slot])
    shift5, shift0, scale1 = _mod_row(mod_ref, 5), _mod_row(nmod_ref, 0), 1.0 + _mod_row(nmod_ref, 1)
    for s in range(n_sub):
        @pl.when(i + 1 < NRB)
        def _():
            _row_gather(y_hbm, pos_ref, (i + 1) * n, n // n_sub, ybuf.at[1 - slot], sems.at[1 - slot],
                        first=s * (n // n_sub))

        rows = slice(s * COMBINE_SUB, (s + 1) * COMBINE_SUB)
        base = s * COMBINE_SUB * TOP_K * ROW_SUB
        y0 = _rows_from_slabs(ybuf.at[slot], base, COMBINE_SUB, TOP_K * ROW_SUB)
        y1 = _rows_from_slabs(ybuf.at[slot], base + ROW_SUB, COMBINE_SUB, TOP_K * ROW_SUB)
        gates = gates_ref[rows, :]
        f = gates[:, 0:1] * y0 + gates[:, 1:2] * y1
        x2 = _ln0(ALPHA * x1_ref[rows, :] + shift5 * f) * g_ref[...] + b_ref[...]
        x2l_ref[rows, :] = x2

        @pl.when(i == 0)
        def _():
            x2c_ref[rows, :] = x2

        hn_ref[rows, :] = (_ln0(x2) * scale1 + shift0).astype(BF16)


def combine(pos, y_slabs, gates, x1, mod, ln_g, ln_b, next_mod):
    row = lambda i, p: (i, 0)
    const = lambda i, p: (0, 0)
    grid_spec = pltpu.PrefetchScalarGridSpec(
        num_scalar_prefetch=1,
        grid=(NRB,),
        in_specs=[pl.BlockSpec(memory_space=pl.ANY),
                  pl.BlockSpec((RB, ROUTER_W), row),
                  pl.BlockSpec((RB, D), row),
                  pl.BlockSpec((8, 6 * D), const),
                  pl.BlockSpec((1, D), const),
                  pl.BlockSpec((1, D), const),
                  pl.BlockSpec((8, 6 * D), const)],
        out_specs=[pl.BlockSpec((RB, D), const),
                   pl.BlockSpec((RB, D), lambda i, p: (jnp.maximum(i - 1, 0), 0)),
                   pl.BlockSpec((RB, D), row)],
        scratch_shapes=[pltpu.VMEM((2, RB * TOP_K * ROW_SUB, LANE), F32),
                        pltpu.SemaphoreType.DMA((2,))],
    )
    return pl.pallas_call(
        _combine_kernel,
        grid_spec=grid_spec,
        out_shape=[jax.ShapeDtypeStruct((N_CTX, D), F32), jax.ShapeDtypeStruct((SEQ, D), F32),
                   jax.ShapeDtypeStruct((T, D), BF16)],
        compiler_params=_cparams(("arbitrary",)),
        name="combine",
    )(pos, y_slabs, gates, x1, mod, ln_g, ln_b, next_mod)


def _routing_tables(ids):
    expert = ids[:, :TOP_K].reshape(-1)
    n = expert.shape[0]
    onehot = (expert[:, None] == jnp.arange(N_EXPERTS, dtype=jnp.int32)[None, :]).astype(jnp.int32)
    csum = jnp.cumsum(onehot, axis=0)
    counts = csum[-1]
    rank = jnp.sum(onehot * csum, axis=1) - 1
    nb = (counts + MOE_BLOCK - 1) // MOE_BLOCK
    bstart = jnp.cumsum(nb) - nb
    pos = (jnp.sum(onehot * (bstart * MOE_BLOCK)[None, :], axis=1) + rank).astype(jnp.int32)
    tok = jnp.arange(n, dtype=jnp.int32) // TOP_K
    row_tok = jnp.zeros((N_ROWS,), jnp.int32).at[pos].set(tok)
    n_it = (nb + ITEM_BLKS - 1) // ITEM_BLKS
    it_end = jnp.cumsum(n_it)
    it_start = it_end - n_it
    idx = jnp.arange(N_ITEMS, dtype=jnp.int32)
    e_of = jnp.minimum(jnp.sum(it_end[None, :] <= idx[:, None], axis=1), N_EXPERTS - 1).astype(jnp.int32)
    k = idx - it_start[e_of]
    valid = idx < it_end[-1]
    item_exp = e_of
    item_blk0 = jnp.where(valid, bstart[e_of] + ITEM_BLKS * k, 0).astype(jnp.int32)
    item_nblk = jnp.where(valid, jnp.clip(nb[e_of] - ITEM_BLKS * k, 0, ITEM_BLKS), 0).astype(jnp.int32)
    n_used = jnp.sum(nb).astype(jnp.int32)[None]
    return item_exp, item_blk0, item_nblk, n_used, row_tok, pos


def _rope_tables():
    rows = SEQ // GRID_W
    row = jnp.repeat(jnp.arange(rows, dtype=F32), GRID_W)
    col = jnp.tile(jnp.arange(GRID_W, dtype=F32), rows)
    half = QK_ROPE // 2
    inv = ROPE_THETA ** (-jnp.arange(0, half, 2, dtype=F32) / half)
    ar = row[:, None] * inv
    ac = col[:, None] * inv
    ang = jnp.concatenate([ar, ar, ac, ac], -1)
    cos = jnp.concatenate([jnp.ones((N_CTX, QK_ROPE), F32), jnp.cos(ang)], 0)
    sin = jnp.concatenate([jnp.zeros((N_CTX, QK_ROPE), F32), jnp.sin(ang)], 0)
    one = jnp.ones((T, QK_NOPE), F32)
    zero_n = jnp.zeros((T, QK_NOPE), F32)
    zero_p = jnp.zeros((T, HEAD_PAD - QK_NOPE - QK_ROPE), F32)
    tq_c = jnp.concatenate([one, cos, zero_p], -1)
    tq_s = jnp.concatenate([zero_n, sin, zero_p], -1)
    tk = jnp.concatenate([cos, sin, jnp.zeros((T, LANE - 2 * QK_ROPE), F32)], -1)
    return tq_c, tq_s, tk


_ROT_SRC = np.concatenate([np.arange(8, 16), np.arange(0, 8), np.arange(24, 32), np.arange(16, 24)])
_ROT_SIGN = np.concatenate([-np.ones(8), np.ones(8), -np.ones(8), np.ones(8)]).astype(np.float32)


def _rot_cols(w):
    return w[..., _ROT_SRC] * _ROT_SIGN


def _layer_weights(l, w_in_t, b_in, d_w_uq, d_w_uk, d_w_uv, r_w_grp, r_b_grp, r_w_exp, r_b_exp, a_b_s, b_w_dw):
    bi = b_in[l]
    p = {}
    w_kr, b_kr = w_in_t[l, COL_KR:COL_G, :].T, bi[COL_KR:COL_G]
    padw = LANE - 2 * QK_ROPE
    p["wkr"] = jnp.concatenate([w_kr, _rot_cols(w_kr), jnp.zeros((D, padw), F32)], -1).astype(BF16)
    p["bkr"] = jnp.concatenate([b_kr, _rot_cols(b_kr), jnp.zeros((padw,), F32)])[None]
    wq = d_w_uq[l].reshape(Q_LORA, HEADS, QK_NOPE + QK_ROPE)
    zpad = jnp.zeros((Q_LORA, HEADS, HEAD_PAD - QK_NOPE - QK_ROPE), F32)
    p["wq"] = jnp.concatenate([wq, zpad], -1).reshape(Q_LORA, HEADS * HEAD_PAD).astype(BF16)
    p["wqr"] = jnp.concatenate([jnp.zeros((Q_LORA, HEADS, QK_NOPE), F32), _rot_cols(wq[..., QK_NOPE:]), zpad],
                               -1).reshape(Q_LORA, HEADS * HEAD_PAD).astype(BF16)
    wk = d_w_uk[l].reshape(KV_LORA, HEADS, QK_NOPE)
    p["wk"] = jnp.concatenate([wk, jnp.zeros((KV_LORA, HEADS, HEAD_PAD - QK_NOPE), F32)],
                              -1).reshape(KV_LORA, HEADS * HEAD_PAD).astype(BF16)
    p["wvt"] = d_w_uv[l].T.astype(BF16)
    e = np.zeros((LANE, HEADS, HEAD_PAD), np.float32)
    for j in range(QK_ROPE):
        e[j, :, QK_NOPE + j] = 1.0
        e[QK_ROPE + j, :, QK_NOPE + j] = 1.0
    p["e"] = jnp.asarray(e.reshape(LANE, HEADS * HEAD_PAD), dtype=BF16)
    p["bg"] = bi[None, COL_G:]
    padr = ROUTER_W - N_GROUPS - N_EXPERTS
    p["wr"] = _hilo(jnp.concatenate([r_w_grp[l], r_w_exp[l], jnp.zeros((D, padr), F32)], -1))
    p["br"] = jnp.concatenate([r_b_grp[l], r_b_exp[l], jnp.zeros((padr,), F32)])[None]
    p["bs"] = jnp.broadcast_to(a_b_s[l][:, :, None], (A_GROUPS, CHUNK, CHUNK))
    p["wdw"] = jnp.concatenate([b_w_dw[l], jnp.zeros((1, BW), F32)], 0)
    return p


def kernel(x, c, ctx, c_ctx, w_ada, b_ada, w_in, b_in, a_ln_g, a_ln_b, a_w_s, a_b_s, b_w_dw, b_b_dw, b_ln_g,
           b_ln_b, d_q_g, d_w_uq, d_kv_g, d_w_uk, d_w_uv, w_branch, w_o, ln1_g, ln1_b, r_w_grp, r_b_grp,
           r_w_exp, r_b_exp, e_w1, e_w3, e_w2, ln2_g, ln2_b):
    assert x.shape == (1, SEQ, D) and ctx.shape == (1, N_CTX, D)
    x_ctx, x_lat = ctx[0], x[0]
    w_in_t = jnp.transpose(w_in, (0, 2, 1))
    c_rep = jnp.broadcast_to(jnp.concatenate([c, c_ctx[None]], axis=0)[:, :, None], (N_COND, D, LANE))
    mods = ada_mod(c_rep, w_ada, b_ada)
    tq_c, tq_s, tk = _rope_tables()
    fc = _fourier_consts()
    b_in3 = b_in.reshape(DEPTH, 1, -1)
    w_br, w_ob = w_branch.astype(BF16), w_o.astype(BF16)

    h = modln(x_ctx, x_lat, mods[0])
    for l in range(DEPTH):
        p = _layer_weights(l, w_in_t, b_in, d_w_uq, d_w_uk, d_w_uv, r_w_grp, r_b_grp, r_w_exp, r_b_exp, a_b_s,
                           b_w_dw)
        mod = mods[l]
        br_a = mix_a(l, h, w_in_t, b_in3, a_ln_g[l][None], a_ln_b[l][None], a_w_s[l].astype(BF16), p["bs"])
        br_b = conv_ln(glu(l, h, w_in_t, b_in3), p["wdw"], b_b_dw[l][None], b_ln_g[l][None], b_ln_b[l][None])
        br_c = fourier_mix(l, h, w_in_t, b_in3, fc)
        q, k, vt = mla_proj(l, h, w_in_t, b_in3, p["wkr"], p["bkr"], d_q_g[l][None], d_kv_g[l][None], p["wq"],
                            p["wqr"], p["wk"], p["wvt"], p["e"], tq_c, tq_s, tk)
        br_dt = attention(q, k, vt)
        y = merge(l, h, br_a, br_b, br_c, br_dt, w_in_t, p["bg"], w_br)
        x1, h2, ids, gates = post_mix(l, y, x_ctx, x_lat, mod, w_ob, ln1_g[l][None], ln1_b[l][None], p["wr"],
                                      p["br"])
        item_exp, item_blk0, item_nblk, n_used, row_tok, pos = _routing_tables(ids)
        ys = experts(l, item_exp, item_blk0, item_nblk, n_used, row_tok, h2, e_w1, e_w3, e_w2)
        x_ctx, x_lat, h = combine(pos, ys, gates, x1, mod, ln2_g[l][None], ln2_b[l][None],
                                  mods[min(l + 1, DEPTH - 1)])
    return x_lat[None]
```

```python
import functools
import math

import numpy as np
import jax
import jax.numpy as jnp
from jax import lax
from jax.experimental import pallas as pl
from jax.experimental.pallas import tpu as pltpu

F32 = jnp.float32
BF16 = jnp.bfloat16

D = 2048
SEQ = 8192
N_CTX = 256
T = N_CTX + SEQ
DEPTH = 2
GRID_W = 64
BW = 512
CHUNK = 128
A_GROUPS = 4
CONV_W = 31
F_GROUPS = 4
QK_NOPE = 64
QK_ROPE = 32
V_HEAD = 64
HEADS = 8
Q_LORA = 512
KV_LORA = 256
ROPE_THETA = 10000.0
MLA_SCALE = (QK_NOPE + QK_ROPE) ** -0.5
N_GROUPS = 4
EXP_PER_GROUP = 8
N_EXPERTS = 32
TOP_K = 2
D_EXPERT = 1024
MOE_BLOCK = 256
COL_A = 0
COL_B = 1024
COL_C = 2048
COL_D = 2560
COL_KV = 3072
COL_KR = 3328
COL_G = 3360
ALPHA = (2 * DEPTH) ** 0.25
LN_EPS = 1e-6

LANE = 128
HEAD_PAD = 128
RB = 256
NRB = T // RB
FFT_N1 = 64
FFT_N2 = 128
ROW_SUB = D // LANE
VMEM_LIMIT = 56 * 1024 * 1024


def _cparams(sem):
    return pltpu.CompilerParams(dimension_semantics=sem, vmem_limit_bytes=VMEM_LIMIT)


def _dot(a, b):
    return jnp.dot(a, b, preferred_element_type=F32)


def _split(x):
    hi = lax.bitcast_convert_type(lax.bitcast_convert_type(x, jnp.int32) & jnp.int32(-65536), F32)
    return hi.astype(BF16), (x - hi).astype(BF16)


def _hilo(x):
    return jnp.stack(_split(jnp.asarray(x, F32)))


def _dot3(a, t_ref):
    a_hi, a_lo = _split(a)
    return _dot(a_hi, t_ref[0]) + (_dot(a_hi, t_ref[1]) + _dot(a_lo, t_ref[0]))


def _dot3_t(t_ref, b):
    b_hi, b_lo = _split(b)
    return _dot(t_ref[0], b_hi) + (_dot(t_ref[1], b_hi) + _dot(t_ref[0], b_lo))


def _ln0(x):
    mu = jnp.mean(x, axis=-1, keepdims=True)
    xc = x - mu
    var = jnp.mean(xc * xc, axis=-1, keepdims=True)
    return xc * lax.rsqrt(var + LN_EPS)


def _mod_row(mod_ref, q):
    lat = mod_ref[0:1, q * D:(q + 1) * D]
    ctx = mod_ref[1:2, q * D:(q + 1) * D]
    return jnp.where(pl.program_id(0) == 0, ctx, lat)


ADA_TN = 1536
ADA_TK = 256
N_COND = 2


def _ada_kernel(c_ref, w_ref, b_ref, o_ref):
    accs = [b_ref[0] for _ in range(N_COND)]
    for k0 in range(0, D, ADA_TK):
        w = w_ref[0, k0:k0 + ADA_TK, :]
        for r in range(N_COND):
            s = c_ref[r, k0:k0 + ADA_TK, :]
            s = s * jax.nn.sigmoid(s)
            s = jnp.concatenate([s] * (ADA_TN // LANE), axis=-1)
            accs[r] = accs[r] + jnp.sum(w * s, axis=0, keepdims=True)
    o_ref[0] = jnp.concatenate(accs + [jnp.zeros((8 - N_COND, ADA_TN), F32)], axis=0)


def ada_mod(c_rep, w_ada, b_ada):
    return pl.pallas_call(
        _ada_kernel,
        grid=(DEPTH, 6 * D // ADA_TN),
        in_specs=[pl.BlockSpec((N_COND, D, LANE), lambda l, j: (0, 0, 0)),
                  pl.BlockSpec((1, D, ADA_TN), lambda l, j: (l, 0, j)),
                  pl.BlockSpec((1, 1, ADA_TN), lambda l, j: (l, 0, j))],
        out_specs=pl.BlockSpec((1, 8, ADA_TN), lambda l, j: (l, 0, j)),
        out_shape=jax.ShapeDtypeStruct((DEPTH, 8, 6 * D), F32),
        compiler_params=_cparams(("arbitrary", "arbitrary")),
        name="ada_mod",
    )(c_rep, w_ada, b_ada.reshape(DEPTH, 1, 6 * D))


def _stream_specs():
    return [pl.BlockSpec((RB, D), lambda i, *_: (0, 0)),
            pl.BlockSpec((RB, D), lambda i, *_: (jnp.maximum(i - 1, 0), 0))]


def _stream_block(ctx_ref, lat_ref):
    return jnp.where(pl.program_id(0) == 0, ctx_ref[...], lat_ref[...])


def _modln_kernel(xc_ref, xl_ref, mod_ref, h_ref):
    h = _ln0(_stream_block(xc_ref, xl_ref)) * (1.0 + _mod_row(mod_ref, 1)) + _mod_row(mod_ref, 0)
    h_ref[...] = h.astype(BF16)


def modln(x_ctx, x_lat, mod):
    return pl.pallas_call(
        _modln_kernel,
        grid=(NRB,),
        in_specs=_stream_specs() + [pl.BlockSpec((8, 6 * D), lambda i: (0, 0))],
        out_specs=pl.BlockSpec((RB, D), lambda i: (i, 0)),
        out_shape=jax.ShapeDtypeStruct((T, D), BF16),
        compiler_params=_cparams(("arbitrary",)),
        name="modln",
    )(x_ctx, x_lat, mod)


def _w_in_spec(l, width, col):
    return pl.BlockSpec((1, width, D), lambda i: (l, col, 0))


def _b_in_spec(l, width, col):
    return pl.BlockSpec((1, 1, width), lambda i: (l, 0, col))


CAST_CHUNK = 256


def _cast_transposed(w_ref, wbf_ref):
    width = w_ref.shape[0]
    for c0 in range(0, width, CAST_CHUNK):
        wbf_ref[:, c0:c0 + CAST_CHUNK] = w_ref[c0:c0 + CAST_CHUNK, :].T.astype(BF16)


def _cast_once(w_ref, wbf_ref):
    @pl.when(pl.program_id(0) == 0)
    def _():
        _cast_transposed(w_ref.at[0], wbf_ref)


def _mix_a_kernel(h_ref, w_ref, b_ref, g_ref, bb_ref, ws_ref, bs_ref, o_ref, wbf_ref):
    _cast_once(w_ref, wbf_ref)
    uv = jax.nn.gelu(_dot(h_ref[...], wbf_ref[...]) + b_ref[0])
    u = uv[:, :BW]
    v = _ln0(uv[:, BW:]) * g_ref[...] + bb_ref[...]
    v = v.astype(BF16)
    gw = BW // A_GROUPS
    for ch in range(RB // CHUNK):
        rows = slice(ch * CHUNK, (ch + 1) * CHUNK)
        parts = [_dot(ws_ref[g], v[rows, g * gw:(g + 1) * gw]) + bs_ref[g] for g in range(A_GROUPS)]
        mixed = jnp.concatenate(parts, axis=-1)
        o_ref[rows, :] = (u[rows, :] * mixed).astype(BF16)


def mix_a(l, h, w_in, b_in, ln_g, ln_b, ws, bs):
    return pl.pallas_call(
        _mix_a_kernel,
        grid=(NRB,),
        in_specs=[pl.BlockSpec((RB, D), lambda i: (i, 0)),
                  _w_in_spec(l, 2 * BW, COL_A // (2 * BW)),
                  _b_in_spec(l, 2 * BW, COL_A // (2 * BW)),
                  pl.BlockSpec((1, BW), lambda i: (0, 0)),
                  pl.BlockSpec((1, BW), lambda i: (0, 0)),
                  pl.BlockSpec((A_GROUPS, CHUNK, CHUNK), lambda i: (0, 0, 0)),
                  pl.BlockSpec((A_GROUPS, CHUNK, CHUNK), lambda i: (0, 0, 0))],
        out_specs=pl.BlockSpec((RB, BW), lambda i: (i, 0)),
        out_shape=jax.ShapeDtypeStruct((T, BW), BF16),
        scratch_shapes=[pltpu.VMEM((D, 2 * BW), BF16)],
        compiler_params=_cparams(("arbitrary",)),
        name="mix_a",
    )(h, w_in, b_in, ln_g, ln_b, ws, bs)


def _glu_kernel(h_ref, w_ref, b_ref, o_ref, wbf_ref):
    _cast_once(w_ref, wbf_ref)
    ab = _dot(h_ref[...], wbf_ref[...]) + b_ref[0]
    o_ref[...] = ab[:, :BW] * jax.nn.sigmoid(ab[:, BW:])


def glu(l, h, w_in, b_in):
    return pl.pallas_call(
        _glu_kernel,
        grid=(NRB,),
        in_specs=[pl.BlockSpec((RB, D), lambda i: (i, 0)),
                  _w_in_spec(l, 2 * BW, COL_B // (2 * BW)),
                  _b_in_spec(l, 2 * BW, COL_B // (2 * BW))],
        out_specs=pl.BlockSpec((RB, BW), lambda i: (i, 0)),
        out_shape=jax.ShapeDtypeStruct((T, BW), F32),
        scratch_shapes=[pltpu.VMEM((D, 2 * BW), BF16)],
        compiler_params=_cparams(("arbitrary",)),
        name="glu",
    )(h, w_in, b_in)


CONV_HALO = 16


def _conv_kernel(prev_ref, cur_ref, next_ref, w_ref, b_ref, g_ref, bb_ref, o_ref, ext_ref):
    i = pl.program_id(0)
    has_prev = i >= 2
    has_next = jnp.logical_and(i >= 1, i < NRB - 1)
    ext_ref[0:CONV_HALO, :] = jnp.where(has_prev, prev_ref[RB - CONV_HALO:RB, :], 0.0)
    ext_ref[CONV_HALO:CONV_HALO + RB, :] = cur_ref[...]
    ext_ref[CONV_HALO + RB:2 * CONV_HALO + RB, :] = jnp.where(has_next, next_ref[0:CONV_HALO, :], 0.0)
    off = CONV_HALO - CONV_W // 2
    acc = jnp.zeros((RB, BW), F32)
    for k in range(CONV_W):
        acc = acc + ext_ref[off + k:off + k + RB, :] * w_ref[k:k + 1, :]
    y = _ln0(acc + b_ref[...]) * g_ref[...] + bb_ref[...]
    o_ref[...] = (y * jax.nn.sigmoid(y)).astype(BF16)


def conv_ln(y, w_dw, b_dw, ln_g, ln_b):
    return pl.pallas_call(
        _conv_kernel,
        grid=(NRB,),
        in_specs=[pl.BlockSpec((RB, BW), lambda i: (jnp.maximum(i - 1, 0), 0)),
                  pl.BlockSpec((RB, BW), lambda i: (i, 0)),
                  pl.BlockSpec((RB, BW), lambda i: (jnp.minimum(i + 1, NRB - 1), 0)),
                  pl.BlockSpec((CONV_W + 1, BW), lambda i: (0, 0)),
                  pl.BlockSpec((1, BW), lambda i: (0, 0)),
                  pl.BlockSpec((1, BW), lambda i: (0, 0)),
                  pl.BlockSpec((1, BW), lambda i: (0, 0))],
        out_specs=pl.BlockSpec((RB, BW), lambda i: (i, 0)),
        out_shape=jax.ShapeDtypeStruct((T, BW), BF16),
        scratch_shapes=[pltpu.VMEM((RB + 2 * CONV_HALO, BW), F32)],
        compiler_params=_cparams(("arbitrary",)),
        name="conv_ln",
    )(y, y, y, w_dw, b_dw, ln_g, ln_b)


def _dft_tables(n):
    k = np.arange(n, dtype=np.int64)
    ang = 2.0 * np.pi * ((k[:, None] * k[None, :]) % n).astype(np.float64) / n
    return np.cos(ang), np.sin(ang)


def _fproj_kernel(h_ref, w_ref, b_ref, cs_ref, ar_ref, ai_ref, wbf_ref):
    _cast_once(w_ref, wbf_ref)
    z = _dot(h_ref[...], wbf_ref[...]) + b_ref[0]
    a = _dot3(z, cs_ref)
    ar_ref[...] = a[:, :BW]
    ai_ref[...] = a[:, BW:]


def fourier_proj(l, h, w_in, b_in, cs):
    return pl.pallas_call(
        _fproj_kernel,
        grid=(NRB,),
        in_specs=[pl.BlockSpec((RB, D), lambda i: (i, 0)),
                  _w_in_spec(l, BW, COL_C // BW),
                  _b_in_spec(l, BW, COL_C // BW),
                  pl.BlockSpec((2, BW, 2 * BW), lambda i: (0, 0, 0))],
        out_specs=[pl.BlockSpec((RB, BW), lambda i: (i, 0)),
                   pl.BlockSpec((RB, BW), lambda i: (i, 0))],
        out_shape=[jax.ShapeDtypeStruct((T, BW), F32), jax.ShapeDtypeStruct((T, BW), F32)],
        scratch_shapes=[pltpu.VMEM((D, BW), BF16)],
        compiler_params=_cparams(("arbitrary",)),
        name="fourier_proj",
    )(h, w_in, b_in, cs)


FFT_TILE_N2 = 4


def _fft1_kernel(ar_ref, ai_ref, m_ref, ct_ref, st_ref, br_ref, bi_ref):
    x = jnp.concatenate([ar_ref[...], ai_ref[...]], axis=0)
    b = _dot3_t(m_ref, x)
    b_re, b_im = b[:FFT_N1], b[FFT_N1:]
    reps = BW // LANE
    for j in range(FFT_TILE_N2):
        ct = jnp.concatenate([ct_ref[j]] * reps, axis=-1)
        st = jnp.concatenate([st_ref[j]] * reps, axis=-1)
        lanes = slice(j * BW, (j + 1) * BW)
        br_ref[:, lanes] = b_re[:, lanes] * ct + b_im[:, lanes] * st
        bi_ref[:, lanes] = b_im[:, lanes] * ct - b_re[:, lanes] * st


def fft_stage1(ar2d, ai2d, m1, ct, st):
    tn = FFT_TILE_N2 * BW
    return pl.pallas_call(
        _fft1_kernel,
        grid=(FFT_N2 // FFT_TILE_N2,),
        in_specs=[pl.BlockSpec((FFT_N1, tn), lambda j: (0, j)),
                  pl.BlockSpec((FFT_N1, tn), lambda j: (0, j)),
                  pl.BlockSpec((2, 2 * FFT_N1, 2 * FFT_N1), lambda j: (0, 0, 0)),
                  pl.BlockSpec((FFT_TILE_N2, FFT_N1, LANE), lambda j: (j, 0, 0)),
                  pl.BlockSpec((FFT_TILE_N2, FFT_N1, LANE), lambda j: (j, 0, 0))],
        out_specs=[pl.BlockSpec((FFT_N1, tn), lambda j: (0, j)),
                   pl.BlockSpec((FFT_N1, tn), lambda j: (0, j))],
        out_shape=[jax.ShapeDtypeStruct((FFT_N1, FFT_N2 * BW), F32)] * 2,
        compiler_params=_cparams(("arbitrary",)),
        name="fft_stage1",
    )(ar2d, ai2d, m1, ct, st)


FFT_TILE_K1 = 8


def _fft2_kernel(br_ref, bi_ref, c_ref, s_ref, p_ref, o_ref, *, scale):
    ys = [(_dot3_t(c_ref, br_ref[a]) + _dot3_t(s_ref, bi_ref[a])) * scale for a in range(FFT_TILE_K1)]
    y = jnp.concatenate(ys, axis=0).astype(BF16)
    y = _dot(p_ref[...], y)
    o_ref[...] = y.reshape(FFT_N2, FFT_TILE_K1, BW)


def fft_stage2(br3d, bi3d, c128, s128, perm, scale):
    return pl.pallas_call(
        functools.partial(_fft2_kernel, scale=scale),
        grid=(FFT_N1 // FFT_TILE_K1,),
        in_specs=[pl.BlockSpec((FFT_TILE_K1, FFT_N2, BW), lambda a: (a, 0, 0)),
                  pl.BlockSpec((FFT_TILE_K1, FFT_N2, BW), lambda a: (a, 0, 0)),
                  pl.BlockSpec((2, FFT_N2, FFT_N2), lambda a: (0, 0, 0)),
                  pl.BlockSpec((2, FFT_N2, FFT_N2), lambda a: (0, 0, 0)),
                  pl.BlockSpec((FFT_TILE_K1 * FFT_N2, FFT_TILE_K1 * FFT_N2), lambda a: (0, 0))],
        out_specs=pl.BlockSpec((FFT_N2, FFT_TILE_K1, BW), lambda a: (0, a, 0)),
        out_shape=jax.ShapeDtypeStruct((FFT_N2, FFT_N1, BW), F32),
        compiler_params=_cparams(("arbitrary",)),
        name="fft_stage2",
    )(br3d, bi3d, c128, s128, perm)


def _dft_ctx_kernel(ar_ref, ai_ref, c_ref, s_ref, o_ref, *, scale):
    o_ref[...] = (_dot3_t(c_ref, ar_ref[...]) + _dot3_t(s_ref, ai_ref[...])) * scale


def dft_ctx(ar, ai, c, s, scale):
    return pl.pallas_call(
        functools.partial(_dft_ctx_kernel, scale=scale),
        grid=(1,),
        in_specs=[pl.BlockSpec((N_CTX, BW), lambda i: (0, 0)),
                  pl.BlockSpec((N_CTX, BW), lambda i: (0, 0)),
                  pl.BlockSpec((2, N_CTX, N_CTX), lambda i: (0, 0, 0)),
                  pl.BlockSpec((2, N_CTX, N_CTX), lambda i: (0, 0, 0))],
        out_specs=pl.BlockSpec((N_CTX, BW), lambda i: (0, 0)),
        out_shape=jax.ShapeDtypeStruct((N_CTX, BW), F32),
        compiler_params=_cparams(("arbitrary",)),
        name="dft_ctx",
    )(ar, ai, c, s)


def _fourier_consts():
    c64, s64 = _dft_tables(FFT_N1)
    m1 = np.block([[c64, s64], [-s64, c64]]).astype(np.float32)
    k1 = np.arange(FFT_N1, dtype=np.float64)
    n2 = np.arange(FFT_N2, dtype=np.float64)
    phi = 2.0 * np.pi * n2[:, None] * k1[None, :] / SEQ
    ct = np.repeat(np.cos(phi)[:, :, None], LANE, axis=2).astype(np.float32)
    st = np.repeat(np.sin(phi)[:, :, None], LANE, axis=2).astype(np.float32)
    c128, s128 = _dft_tables(FFT_N2)
    gw = BW // F_GROUPS
    cg, sg = _dft_tables(gw)
    cs = np.zeros((BW, 2 * BW), np.float32)
    for g in range(F_GROUPS):
        sl = slice(g * gw, (g + 1) * gw)
        cs[sl, sl] = cg
        cs[sl, BW + g * gw:BW + (g + 1) * gw] = -sg
    rows = np.arange(FFT_TILE_K1 * FFT_N2)
    perm = np.zeros((rows.size, rows.size), np.float32)
    perm[rows, (rows % FFT_TILE_K1) * FFT_N2 + rows // FFT_TILE_K1] = 1.0
    cc, sc = _dft_tables(N_CTX)
    return dict(m1=_hilo(m1), ct=jnp.asarray(ct), st=jnp.asarray(st), c128=_hilo(c128), s128=_hilo(s128),
                cs=_hilo(cs), perm=jnp.asarray(perm, dtype=BF16), cc=_hilo(cc), sc=_hilo(sc))


def fourier_mix(l, h, w_in, b_in, fc):
    gw = BW // F_GROUPS
    ar, ai = fourier_proj(l, h, w_in, b_in, fc["cs"])
    y_ctx = dft_ctx(ar, ai, fc["cc"], fc["sc"], 1.0 / math.sqrt(N_CTX * gw))
    ar2 = ar[N_CTX:].reshape(FFT_N1, FFT_N2 * BW)
    ai2 = ai[N_CTX:].reshape(FFT_N1, FFT_N2 * BW)
    br, bi = fft_stage1(ar2, ai2, fc["m1"], fc["ct"], fc["st"])
    y = fft_stage2(br.reshape(FFT_N1, FFT_N2, BW), bi.reshape(FFT_N1, FFT_N2, BW),
                   fc["c128"], fc["s128"], fc["perm"], 1.0 / math.sqrt(SEQ * gw))
    return jnp.concatenate([y_ctx, y.reshape(SEQ, BW)], axis=0)


VT_ROWS = 80
LOG2E = math.log2(math.e)


def _mla_proj_kernel(h_ref, wcq_ref, bcq_ref, wkv_ref, bkv_ref, wkr_ref, bkr_ref, qg_ref, kvg_ref, wq_ref, wqr_ref,
                     wk_ref, wvt_ref, e_ref, tq_c_ref, tq_s_ref, tk_ref, q_ref, k_ref, vt_ref, wcq_bf, wkv_bf):
    _cast_once(wcq_ref, wcq_bf)
    _cast_once(wkv_ref, wkv_bf)
    h = h_ref[...]
    cq = _dot(h, wcq_bf[...]) + bcq_ref[0]
    cq = cq * lax.rsqrt(jnp.mean(cq * cq, axis=-1, keepdims=True) + LN_EPS) * qg_ref[...]
    cq = cq.astype(BF16)
    cos_t = jnp.concatenate([tq_c_ref[...]] * HEADS, axis=-1)
    sin_t = jnp.concatenate([tq_s_ref[...]] * HEADS, axis=-1)
    q = (_dot(cq, wq_ref[...]) * cos_t + _dot(cq, wqr_ref[...]) * sin_t) * (MLA_SCALE * LOG2E)
    q_ref[...] = q.astype(BF16)
    ckv = _dot(h, wkv_bf[...]) + bkv_ref[0]
    ckv = ckv * lax.rsqrt(jnp.mean(ckv * ckv, axis=-1, keepdims=True) + LN_EPS) * kvg_ref[...]
    ckv = ckv.astype(BF16)
    kr = ((_dot(h, wkr_ref[...]) + bkr_ref[...]) * tk_ref[...]).astype(BF16)
    k_ref[...] = (_dot(ckv, wk_ref[...]) + _dot(kr, e_ref[...])).astype(BF16)
    vt = lax.dot_general(wvt_ref[...], ckv, (((1,), (1,)), ((), ())), preferred_element_type=F32)
    ones_rows = (lax.broadcasted_iota(jnp.int32, (VT_ROWS - V_HEAD, RB), 0) == 0).astype(BF16)
    for hd in range(HEADS):
        vt_ref[0, hd * VT_ROWS:hd * VT_ROWS + V_HEAD, :] = vt[hd * V_HEAD:(hd + 1) * V_HEAD, :].astype(BF16)
        vt_ref[0, hd * VT_ROWS + V_HEAD:(hd + 1) * VT_ROWS, :] = ones_rows


def mla_proj(l, h, w_in, b_in, wkr, bkr, qg, kvg, wq, wqr, wk, wvt, e, tq_c, tq_s, tk):
    const = lambda i: (0, 0)
    row = lambda i: (i, 0)
    return pl.pallas_call(
        _mla_proj_kernel,
        grid=(NRB,),
        in_specs=[pl.BlockSpec((RB, D), row),
                  _w_in_spec(l, Q_LORA, COL_D // Q_LORA),
                  _b_in_spec(l, Q_LORA, COL_D // Q_LORA),
                  _w_in_spec(l, KV_LORA, COL_KV // KV_LORA),
                  _b_in_spec(l, KV_LORA, COL_KV // KV_LORA),
                  pl.BlockSpec((D, LANE), const),
                  pl.BlockSpec((1, LANE), const),
                  pl.BlockSpec((1, Q_LORA), const),
                  pl.BlockSpec((1, KV_LORA), const),
                  pl.BlockSpec((Q_LORA, HEADS * HEAD_PAD), const),
                  pl.BlockSpec((Q_LORA, HEADS * HEAD_PAD), const),
                  pl.BlockSpec((KV_LORA, HEADS * HEAD_PAD), const),
                  pl.BlockSpec((HEADS * V_HEAD, KV_LORA), const),
                  pl.BlockSpec((LANE, HEADS * HEAD_PAD), const),
                  pl.BlockSpec((RB, HEAD_PAD), row),
                  pl.BlockSpec((RB, HEAD_PAD), row),
                  pl.BlockSpec((RB, LANE), row)],
        out_specs=[pl.BlockSpec((RB, HEADS * HEAD_PAD), row),
                   pl.BlockSpec((RB, HEADS * HEAD_PAD), row),
                   pl.BlockSpec((1, HEADS * VT_ROWS, RB), lambda i: (i, 0, 0))],
        out_shape=[jax.ShapeDtypeStruct((T, HEADS * HEAD_PAD), BF16),
                   jax.ShapeDtypeStruct((T, HEADS * HEAD_PAD), BF16),
                   jax.ShapeDtypeStruct((NRB, HEADS * VT_ROWS, RB), BF16)],
        scratch_shapes=[pltpu.VMEM((D, Q_LORA), BF16), pltpu.VMEM((D, KV_LORA), BF16)],
        compiler_params=_cparams(("arbitrary",)),
        name="mla_proj",
    )(h, w_in, b_in, w_in, b_in, wkr, bkr, qg, kvg, wq, wqr, wk, wvt, e, tq_c, tq_s, tk)


KV_TILE = RB
KV_STAGE = 1024
N_STAGE = SEQ // KV_STAGE
TILES_PER_STAGE = KV_STAGE // KV_TILE


HEAD_GROUP = 2


def _attn_kernel(q_ref, k_ref, vt_ref, o_ref, sa_ref, sb_ref, m_ref, acc_ref):
    i = pl.program_id(1)
    heads = range(HEAD_GROUP)
    qs = [q_ref[:, g * HEAD_PAD:(g + 1) * HEAD_PAD] for g in heads]

    def scores(g, start, n):
        kt = k_ref[pl.ds(start, n), g * HEAD_PAD:(g + 1) * HEAD_PAD]
        return lax.dot_general(kt, qs[g], (((1,), (1,)), ((), ())),
                               preferred_element_type=F32)

    def values(g, tile):
        return vt_ref[tile, g * VT_ROWS:(g + 1) * VT_ROWS, :]

    def softmax_pv(g, s_ref, tile0, m, acc):
        m_new = jnp.maximum(m, jnp.max(s_ref[g], axis=0, keepdims=True))
        p = jnp.exp2(s_ref[g] - m_new).astype(BF16)
        acc = jnp.exp2(m - m_new) * acc
        for u in range(TILES_PER_STAGE):
            acc = acc + _dot(values(g, tile0 + u), p[u * KV_TILE:(u + 1) * KV_TILE, :])
        return m_new, acc

    for g in heads:
        s0 = scores(g, 0, N_CTX)
        m0 = jnp.max(s0, axis=0, keepdims=True)
        m_ref[g] = m0
        acc_ref[g] = _dot(values(g, 0), jnp.exp2(s0 - m0).astype(BF16))

    @pl.when(i > 0)
    def _():
        first_tile = N_CTX // KV_TILE
        for g in heads:
            sa_ref[g] = scores(g, N_CTX, KV_STAGE)

        def body(j, carry):
            ms, accs = list(carry[0]), list(carry[1])
            base = pl.multiple_of(N_CTX + 2 * j * KV_STAGE, KV_TILE)
            tile = first_tile + 2 * j * TILES_PER_STAGE
            for g in heads:
                sb_ref[g] = scores(g, base + KV_STAGE, KV_STAGE)
            for g in heads:
                ms[g], accs[g] = softmax_pv(g, sa_ref, tile, ms[g], accs[g])
            for g in heads:
                sa_ref[g] = scores(g, base + 2 * KV_STAGE, KV_STAGE)
            for g in heads:
                ms[g], accs[g] = softmax_pv(g, sb_ref, tile + TILES_PER_STAGE, ms[g], accs[g])
            return tuple(ms), tuple(accs)

        init = (tuple(m_ref[g] for g in heads), tuple(acc_ref[g] for g in heads))
        ms, accs = lax.fori_loop(0, N_STAGE // 2 - 1, body, init)
        ms, accs = list(ms), list(accs)
        for g in heads:
            sb_ref[g] = scores(g, N_CTX + (N_STAGE - 1) * KV_STAGE, KV_STAGE)
        for g in heads:
            ms[g], accs[g] = softmax_pv(g, sa_ref, first_tile + (N_STAGE - 2) * TILES_PER_STAGE, ms[g], accs[g])
        for g in heads:
            ms[g], accs[g] = softmax_pv(g, sb_ref, first_tile + (N_STAGE - 1) * TILES_PER_STAGE, ms[g], accs[g])
            acc_ref[g] = accs[g]

    for g in heads:
        acc = acc_ref[g]
        o_ref[g * V_HEAD:(g + 1) * V_HEAD, :] = acc[:V_HEAD, :] / acc[V_HEAD:V_HEAD + 1, :]


def attention(q, k, vt):
    return pl.pallas_call(
        _attn_kernel,
        grid=(HEADS // HEAD_GROUP, NRB),
        in_specs=[pl.BlockSpec((RB, HEAD_GROUP * HEAD_PAD), lambda h, i: (i, h)),
                  pl.BlockSpec((T, HEAD_GROUP * HEAD_PAD), lambda h, i: (0, h)),
                  pl.BlockSpec((NRB, HEAD_GROUP * VT_ROWS, KV_TILE), lambda h, i: (0, h, 0))],
        out_specs=pl.BlockSpec((HEAD_GROUP * V_HEAD, RB), lambda h, i: (h, i)),
        out_shape=jax.ShapeDtypeStruct((HEADS * V_HEAD, T), F32),
        scratch_shapes=[pltpu.VMEM((HEAD_GROUP, KV_STAGE, RB), F32),
                        pltpu.VMEM((HEAD_GROUP, KV_STAGE, RB), F32),
                        pltpu.VMEM((HEAD_GROUP, 1, RB), F32),
                        pltpu.VMEM((HEAD_GROUP, VT_ROWS, RB), F32)],
        compiler_params=_cparams(("arbitrary", "arbitrary")),
        name="attention",
    )(q, k, vt)


MERGE_TN = 512


N_BRANCH = 4


MERGE_RB = 768


def _merge_kernel(h_ref, a_ref, b_ref, c_ref, dt_ref, *refs):
    wg_refs, bg_refs = refs[:N_BRANCH], refs[N_BRANCH:2 * N_BRANCH]
    wb_ref, y_ref, wg_bf = refs[2 * N_BRANCH:]

    @pl.when(pl.program_id(1) == 0)
    def _():
        for n in range(N_BRANCH):
            _cast_transposed(wg_refs[n], wg_bf.at[n])

    h = h_ref[...]
    branches = (a_ref[...], b_ref[...], c_ref[...].astype(BF16), dt_ref[...].T.astype(BF16))
    y = jnp.zeros((MERGE_RB, MERGE_TN), F32)
    for n, br in enumerate(branches):
        gate = jax.nn.sigmoid(_dot(h, wg_bf[n]) + bg_refs[n][...])
        y = y + gate * _dot(br, wb_ref[0, n])
    y_ref[...] = y.astype(BF16)


def merge(l, h, br_a, br_b, br_c, br_dt, w_in_t, bg, wb):
    row = lambda j, i: (i, 0)
    n_col = D // MERGE_TN
    gate_specs = [pl.BlockSpec((pl.Squeezed(), pl.Element(MERGE_TN), pl.Element(D)),
                               functools.partial(
                                   lambda j, i, n: (l, pl.multiple_of(COL_G + n * D + j * MERGE_TN, QK_ROPE), 0), n=n),
                               pipeline_mode=pl.Buffered(1))
                  for n in range(N_BRANCH)]
    bias_specs = [pl.BlockSpec((1, MERGE_TN), functools.partial(lambda j, i, n: (0, n * n_col + j), n=n))
                  for n in range(N_BRANCH)]
    return pl.pallas_call(
        _merge_kernel,
        grid=(n_col, T // MERGE_RB),
        in_specs=[pl.BlockSpec((MERGE_RB, D), row),
                  pl.BlockSpec((MERGE_RB, BW), row),
                  pl.BlockSpec((MERGE_RB, BW), row),
                  pl.BlockSpec((MERGE_RB, BW), row),
                  pl.BlockSpec((BW, MERGE_RB), lambda j, i: (0, i))]
                 + gate_specs + bias_specs
                 + [pl.BlockSpec((1, N_BRANCH, BW, MERGE_TN), lambda j, i: (l, 0, 0, j))],
        out_specs=pl.BlockSpec((MERGE_RB, MERGE_TN), lambda j, i: (i, j)),
        out_shape=jax.ShapeDtypeStruct((T, D), BF16),
        scratch_shapes=[pltpu.VMEM((N_BRANCH, D, MERGE_TN), BF16)],
        compiler_params=_cparams(("arbitrary", "arbitrary")),
        name="merge",
    )(h, br_a, br_b, br_c, br_dt, *([w_in_t] * N_BRANCH), *([bg] * N_BRANCH), wb)


ROUTER_W = 128


def _post_mix_kernel(y_ref, xc_ref, xl_ref, mod_ref, wo_ref, g_ref, b_ref, wr_ref, br_ref,
                     x1_ref, h2_ref, ids_ref, gates_ref):
    mix = _dot(y_ref[...], wo_ref[0])
    x1 = _ln0(ALPHA * _stream_block(xc_ref, xl_ref) + _mod_row(mod_ref, 2) * mix) * g_ref[...] + b_ref[...]
    x1_ref[...] = x1
    h2 = _ln0(x1) * (1.0 + _mod_row(mod_ref, 4)) + _mod_row(mod_ref, 3)
    _rows_to_slabs(h2, h2_ref)
    logits = _dot3(h2, wr_ref) + br_ref[...]
    lane = lax.broadcasted_iota(jnp.int32, (RB, ROUTER_W), 1).astype(F32)
    neg = jnp.float32(-jnp.inf)
    big = jnp.float32(ROUTER_W)
    gl = jnp.where(lane < N_GROUPS, logits, neg)
    gmax = jnp.max(gl, axis=-1, keepdims=True)
    g_idx = jnp.min(jnp.where(gl == gmax, lane, big), axis=-1, keepdims=True)
    g_p = 1.0 / jnp.sum(jnp.exp(gl - gmax), axis=-1, keepdims=True)
    e_lane = lane - N_GROUPS
    in_grp = jnp.logical_and(e_lane >= g_idx * EXP_PER_GROUP, e_lane < (g_idx + 1) * EXP_PER_GROUP)
    el = jnp.where(in_grp, logits, neg)
    v1 = jnp.max(el, axis=-1, keepdims=True)
    i1 = jnp.min(jnp.where(el == v1, lane, big), axis=-1, keepdims=True)
    el2 = jnp.where(lane == i1, neg, el)
    v2 = jnp.max(el2, axis=-1, keepdims=True)
    i2 = jnp.min(jnp.where(el2 == v2, lane, big), axis=-1, keepdims=True)
    e21 = jnp.exp(v2 - v1)
    p1 = 1.0 / (1.0 + e21)
    p2 = e21 * p1
    ids_ref[...] = jnp.where(lane == 0, i1 - N_GROUPS, jnp.where(lane == 1, i2 - N_GROUPS, 0.0)).astype(jnp.int32)
    gates_ref[...] = jnp.where(lane == 0, p1 * g_p, jnp.where(lane == 1, p2 * g_p, 0.0))


def post_mix(l, y, x_ctx, x_lat, mod, wo, ln_g, ln_b, wr, br):
    row = lambda i: (i, 0)
    const = lambda i: (0, 0)
    return pl.pallas_call(
        _post_mix_kernel,
        grid=(NRB,),
        in_specs=[pl.BlockSpec((RB, D), row)] + _stream_specs()
                 + [pl.BlockSpec((8, 6 * D), const),
                  pl.BlockSpec((1, D, D), lambda i: (l, 0, 0)),
                  pl.BlockSpec((1, D), const),
                  pl.BlockSpec((1, D), const),
                  pl.BlockSpec((2, D, ROUTER_W), lambda i: (0, 0, 0)),
                  pl.BlockSpec((1, ROUTER_W), const)],
        out_specs=[pl.BlockSpec((RB, D), row),
                   pl.BlockSpec((RB * ROW_SUB, LANE), row),
                   pl.BlockSpec((RB, ROUTER_W), row),
                   pl.BlockSpec((RB, ROUTER_W), row)],
        out_shape=[jax.ShapeDtypeStruct((T, D), F32),
                   jax.ShapeDtypeStruct((T * ROW_SUB, LANE), F32),
                   jax.ShapeDtypeStruct((T, ROUTER_W), jnp.int32),
                   jax.ShapeDtypeStruct((T, ROUTER_W), F32)],
        compiler_params=_cparams(("arbitrary",)),
        name="post_mix",
    )(y, x_ctx, x_lat, mod, wo, ln_g, ln_b, wr, br)


N_ROWS = -(-(T * TOP_K + N_EXPERTS * (MOE_BLOCK - 1)) // MOE_BLOCK) * MOE_BLOCK
N_BLK = N_ROWS // MOE_BLOCK


N_DMA_PRIORITIES = 2


GATHER_UNROLL = 8


def _row_gather(src_hbm, idx_ref, base, n, dst_ref, sem, first=0):
    def issue(t, _):
        for u in range(GATHER_UNROLL):
            r = first + t * GATHER_UNROLL + u
            src_row = pl.multiple_of(idx_ref[base + r] * ROW_SUB, ROW_SUB)
            dst_row = pl.multiple_of(r * ROW_SUB, ROW_SUB)
            pltpu.make_async_copy(src_hbm.at[pl.ds(src_row, ROW_SUB), :],
                                  dst_ref.at[pl.ds(dst_row, ROW_SUB), :], sem).start(priority=u % N_DMA_PRIORITIES)
        return 0
    lax.fori_loop(0, n // GATHER_UNROLL, issue, 0)


def _row_gather_wait(src_hbm, n, dst_ref, sem):
    pltpu.make_async_copy(src_hbm.at[pl.ds(0, n * ROW_SUB), :], dst_ref, sem).wait()


def _rows_from_slabs(buf_ref, start, n, stride):
    return jnp.concatenate([buf_ref[pl.ds(start + c, n, stride=stride), :] for c in range(ROW_SUB)], axis=-1)


def _rows_to_slabs(val, out_ref):
    n = val.shape[0]
    for c in range(ROW_SUB):
        out_ref[pl.ds(c, n, stride=ROW_SUB), :] = val[:, c * LANE:(c + 1) * LANE]


ITEM_BLKS = 3
ITEM_ROWS = ITEM_BLKS * MOE_BLOCK
N_ITEMS = N_BLK // ITEM_BLKS + N_EXPERTS
EXP_CHUNK = 512
N_CHUNK = D_EXPERT // EXP_CHUNK
BLK_SLABS = MOE_BLOCK * ROW_SUB


def _expert_kernel(item_exp_ref, item_blk0_ref, item_nblk_ref, n_used_ref, row_tok_ref, h_hbm, w1_ref, w3_ref, w2_ref,
                   y_hbm, xslab, x_bf, yacc, stage, w1c, w3c, w2c, sem_x, sem_y):
    i, j = pl.program_id(0), pl.program_id(1)
    nblk = item_nblk_ref[i]
    blk0 = item_blk0_ref[i]

    def gather_item(item, first, n):
        _row_gather(h_hbm, row_tok_ref, item_blk0_ref[item] * MOE_BLOCK, n, xslab, sem_x, first=first)

    @pl.when(jnp.logical_and(i == 0, j == 0))
    def _():
        gather_item(0, 0, nblk * MOE_BLOCK)

    w1c[...] = w1_ref[0, 0].astype(BF16)
    w3c[...] = w3_ref[0, 0].astype(BF16)
    w2c[...] = w2_ref[0, 0].astype(BF16)
    last = j == N_CHUNK - 1

    def rows_of(b):
        return pl.ds(pl.multiple_of(b * MOE_BLOCK, MOE_BLOCK), MOE_BLOCK)

    def partial_out(xb):
        a = _dot(xb, w1c[...])
        g = _dot(xb, w3c[...])
        return _dot((a * jax.nn.sigmoid(a) * g).astype(BF16), w2c[...])

    def y_copy(b, slot):
        dst = pl.multiple_of((blk0 + b) * BLK_SLABS, BLK_SLABS)
        return pltpu.make_async_copy(stage.at[slot], y_hbm.at[pl.ds(dst, BLK_SLABS), :], sem_y.at[slot])

    def gather_next():
        @pl.when(i + 1 < N_ITEMS)
        def _():
            share = item_nblk_ref[i + 1] * (MOE_BLOCK // N_CHUNK)
            gather_item(i + 1, j * share, share)

    @pl.when(j == 0)
    def _():
        def wait_block(b, _):
            pltpu.make_async_copy(h_hbm.at[pl.ds(0, BLK_SLABS), :], xslab.at[pl.ds(0, BLK_SLABS), :], sem_x).wait()
            return 0
        lax.fori_loop(0, nblk, wait_block, 0)

        def first_block(b, _):
            xb = _rows_from_slabs(xslab, b * BLK_SLABS, MOE_BLOCK, ROW_SUB).astype(BF16)
            x_bf[rows_of(b), :] = xb
            yacc[rows_of(b), :] = partial_out(xb)
            return 0
        lax.fori_loop(0, nblk, first_block, 0)
        gather_next()

    @pl.when(jnp.logical_and(j > 0, jnp.logical_not(last)))
    def _():
        gather_next()

        def mid_block(b, _):
            yacc[rows_of(b), :] += partial_out(x_bf[rows_of(b), :])
            return 0
        lax.fori_loop(0, nblk, mid_block, 0)

    @pl.when(last)
    def _():
        gather_next()

        def last_block(b, _):
            slot = b % 2

            @pl.when(b >= 2)
            def _():
                y_copy(b - 2, slot).wait()
            _rows_to_slabs(yacc[rows_of(b), :] + partial_out(x_bf[rows_of(b), :]), stage.at[slot])
            y_copy(b, slot).start()
            return 0
        lax.fori_loop(0, nblk, last_block, 0)

        for back in (1, 2):
            @pl.when(nblk >= back)
            def _():
                b = nblk - back
                y_copy(b, b % 2).wait()

    @pl.when(jnp.logical_and(i == N_ITEMS - 1, last))
    def _():
        stage[0] = jnp.zeros((BLK_SLABS, LANE), F32)

        def fill(b, _):
            dst = pl.multiple_of(b * BLK_SLABS, BLK_SLABS)
            cp = pltpu.make_async_copy(stage.at[0], y_hbm.at[pl.ds(dst, BLK_SLABS), :], sem_y.at[0])
            cp.start()
            cp.wait()
            return 0
        lax.fori_loop(n_used_ref[0], N_BLK, fill, 0)


def experts(l, item_exp, item_blk0, item_nblk, n_used, row_tok, h2_slabs, w1, w3, w2):
    wspec = lambda blk, imap: pl.BlockSpec(blk, imap)
    grid_spec = pltpu.PrefetchScalarGridSpec(
        num_scalar_prefetch=5,
        grid=(N_ITEMS, N_CHUNK),
        in_specs=[pl.BlockSpec(memory_space=pl.ANY),
                  wspec((1, 1, D, EXP_CHUNK), lambda i, j, ie, *_: (l, ie[i], 0, j)),
                  wspec((1, 1, D, EXP_CHUNK), lambda i, j, ie, *_: (l, ie[i], 0, j)),
                  wspec((1, 1, EXP_CHUNK, D), lambda i, j, ie, *_: (l, ie[i], j, 0))],
        out_specs=pl.BlockSpec(memory_space=pl.ANY),
        scratch_shapes=[pltpu.VMEM((ITEM_BLKS * BLK_SLABS, LANE), F32),
                        pltpu.VMEM((ITEM_ROWS, D), BF16),
                        pltpu.VMEM((ITEM_ROWS, D), F32),
                        pltpu.VMEM((2, BLK_SLABS, LANE), F32),
                        pltpu.VMEM((D, EXP_CHUNK), BF16),
                        pltpu.VMEM((D, EXP_CHUNK), BF16),
                        pltpu.VMEM((EXP_CHUNK, D), BF16),
                        pltpu.SemaphoreType.DMA(()),
                        pltpu.SemaphoreType.DMA((2,))],
    )
    return pl.pallas_call(
        _expert_kernel,
        grid_spec=grid_spec,
        out_shape=jax.ShapeDtypeStruct((N_ROWS * ROW_SUB, LANE), F32),
        compiler_params=_cparams(("arbitrary", "arbitrary")),
        name="experts",
    )(item_exp, item_blk0, item_nblk, n_used, row_tok, h2_slabs, w1, w3, w2)


COMBINE_SUB = 64


def _combine_kernel(pos_ref, y_hbm, gates_ref, x1_ref, mod_ref, g_ref, b_ref, nmod_ref,
                    x2c_ref, x2l_ref, hn_ref, ybuf, sems):
    i = pl.program_id(0)
    slot = i % 2
    n = RB * TOP_K
    n_sub = RB // COMBINE_SUB

    @pl.when(i == 0)
    def _():
        _row_gather(y_hbm, pos_ref, 0, n, ybuf.at[0], sems.at[0])

    _row_gather_wait(y_hbm, n, ybuf.at[slot], sems.at[slot])
    shift5, shift0, scale1 = _mod_row(mod_ref, 5), _mod_row(nmod_ref, 0), 1.0 + _mod_row(nmod_ref, 1)
    for s in range(n_sub):
        @pl.when(i + 1 < NRB)
        def _():
            _row_gather(y_hbm, pos_ref, (i + 1) * n, n // n_sub, ybuf.at[1 - slot], sems.at[1 - slot],
                        first=s * (n // n_sub))

        rows = slice(s * COMBINE_SUB, (s + 1) * COMBINE_SUB)
        base = s * COMBINE_SUB * TOP_K * ROW_SUB
        y0 = _rows_from_slabs(ybuf.at[slot], base, COMBINE_SUB, TOP_K * ROW_SUB)
        y1 = _rows_from_slabs(ybuf.at[slot], base + ROW_SUB, COMBINE_SUB, TOP_K * ROW_SUB)
        gates = gates_ref[rows, :]
        f = gates[:, 0:1] * y0 + gates[:, 1:2] * y1
        x2 = _ln0(ALPHA * x1_ref[rows, :] + shift5 * f) * g_ref[...] + b_ref[...]
        x2l_ref[rows, :] = x2

        @pl.when(i == 0)
        def _():
            x2c_ref[rows, :] = x2

        hn_ref[rows, :] = (_ln0(x2) * scale1 + shift0).astype(BF16)


def combine(pos, y_slabs, gates, x1, mod, ln_g, ln_b, next_mod):
    row = lambda i, p: (i, 0)
    const = lambda i, p: (0, 0)
    grid_spec = pltpu.PrefetchScalarGridSpec(
        num_scalar_prefetch=1,
        grid=(NRB,),
        in_specs=[pl.BlockSpec(memory_space=pl.ANY),
                  pl.BlockSpec((RB, ROUTER_W), row),
                  pl.BlockSpec((RB, D), row),
                  pl.BlockSpec((8, 6 * D), const),
                  pl.BlockSpec((1, D), const),
                  pl.BlockSpec((1, D), const),
                  pl.BlockSpec((8, 6 * D), const)],
        out_specs=[pl.BlockSpec((RB, D), const),
                   pl.BlockSpec((RB, D), lambda i, p: (jnp.maximum(i - 1, 0), 0)),
                   pl.BlockSpec((RB, D), row)],
        scratch_shapes=[pltpu.VMEM((2, RB * TOP_K * ROW_SUB, LANE), F32),
                        pltpu.SemaphoreType.DMA((2,))],
    )
    return pl.pallas_call(
        _combine_kernel,
        grid_spec=grid_spec,
        out_shape=[jax.ShapeDtypeStruct((N_CTX, D), F32), jax.ShapeDtypeStruct((SEQ, D), F32),
                   jax.ShapeDtypeStruct((T, D), BF16)],
        compiler_params=_cparams(("arbitrary",)),
        name="combine",
    )(pos, y_slabs, gates, x1, mod, ln_g, ln_b, next_mod)


def _routing_tables(ids):
    expert = ids[:, :TOP_K].reshape(-1)
    n = expert.shape[0]
    onehot = (expert[:, None] == jnp.arange(N_EXPERTS, dtype=jnp.int32)[None, :]).astype(jnp.int32)
    csum = jnp.cumsum(onehot, axis=0)
    counts = csum[-1]
    rank = jnp.sum(onehot * csum, axis=1) - 1
    nb = (counts + MOE_BLOCK - 1) // MOE_BLOCK
    bstart = jnp.cumsum(nb) - nb
    pos = (jnp.sum(onehot * (bstart * MOE_BLOCK)[None, :], axis=1) + rank).astype(jnp.int32)
    tok = jnp.arange(n, dtype=jnp.int32) // TOP_K
    row_tok = jnp.zeros((N_ROWS,), jnp.int32).at[pos].set(tok)
    n_it = (nb + ITEM_BLKS - 1) // ITEM_BLKS
    it_end = jnp.cumsum(n_it)
    it_start = it_end - n_it
    idx = jnp.arange(N_ITEMS, dtype=jnp.int32)
    e_of = jnp.minimum(jnp.sum(it_end[None, :] <= idx[:, None], axis=1), N_EXPERTS - 1).astype(jnp.int32)
    k = idx - it_start[e_of]
    valid = idx < it_end[-1]
    item_exp = e_of
    item_blk0 = jnp.where(valid, bstart[e_of] + ITEM_BLKS * k, 0).astype(jnp.int32)
    item_nblk = jnp.where(valid, jnp.clip(nb[e_of] - ITEM_BLKS * k, 0, ITEM_BLKS), 0).astype(jnp.int32)
    n_used = jnp.sum(nb).astype(jnp.int32)[None]
    return item_exp, item_blk0, item_nblk, n_used, row_tok, pos


def _rope_tables():
    rows = SEQ // GRID_W
    row = jnp.repeat(jnp.arange(rows, dtype=F32), GRID_W)
    col = jnp.tile(jnp.arange(GRID_W, dtype=F32), rows)
    half = QK_ROPE // 2
    inv = ROPE_THETA ** (-jnp.arange(0, half, 2, dtype=F32) / half)
    ar = row[:, None] * inv
    ac = col[:, None] * inv
    ang = jnp.concatenate([ar, ar, ac, ac], -1)
    cos = jnp.concatenate([jnp.ones((N_CTX, QK_ROPE), F32), jnp.cos(ang)], 0)
    sin = jnp.concatenate([jnp.zeros((N_CTX, QK_ROPE), F32), jnp.sin(ang)], 0)
    one = jnp.ones((T, QK_NOPE), F32)
    zero_n = jnp.zeros((T, QK_NOPE), F32)
    zero_p = jnp.zeros((T, HEAD_PAD - QK_NOPE - QK_ROPE), F32)
    tq_c = jnp.concatenate([one, cos, zero_p], -1)
    tq_s = jnp.concatenate([zero_n, sin, zero_p], -1)
    tk = jnp.concatenate([cos, sin, jnp.zeros((T, LANE - 2 * QK_ROPE), F32)], -1)
    return tq_c, tq_s, tk


_ROT_SRC = np.concatenate([np.arange(8, 16), np.arange(0, 8), np.arange(24, 32), np.arange(16, 24)])
_ROT_SIGN = np.concatenate([-np.ones(8), np.ones(8), -np.ones(8), np.ones(8)]).astype(np.float32)


def _rot_cols(w):
    return w[..., _ROT_SRC] * _ROT_SIGN


def _layer_weights(l, w_in_t, b_in, d_w_uq, d_w_uk, d_w_uv, r_w_grp, r_b_grp, r_w_exp, r_b_exp, a_b_s, b_w_dw):
    bi = b_in[l]
    p = {}
    w_kr, b_kr = w_in_t[l, COL_KR:COL_G, :].T, bi[COL_KR:COL_G]
    padw = LANE - 2 * QK_ROPE
    p["wkr"] = jnp.concatenate([w_kr, _rot_cols(w_kr), jnp.zeros((D, padw), F32)], -1).astype(BF16)
    p["bkr"] = jnp.concatenate([b_kr, _rot_cols(b_kr), jnp.zeros((padw,), F32)])[None]
    wq = d_w_uq[l].reshape(Q_LORA, HEADS, QK_NOPE + QK_ROPE)
    zpad = jnp.zeros((Q_LORA, HEADS, HEAD_PAD - QK_NOPE - QK_ROPE), F32)
    p["wq"] = jnp.concatenate([wq, zpad], -1).reshape(Q_LORA, HEADS * HEAD_PAD).astype(BF16)
    p["wqr"] = jnp.concatenate([jnp.zeros((Q_LORA, HEADS, QK_NOPE), F32), _rot_cols(wq[..., QK_NOPE:]), zpad],
                               -1).reshape(Q_LORA, HEADS * HEAD_PAD).astype(BF16)
    wk = d_w_uk[l].reshape(KV_LORA, HEADS, QK_NOPE)
    p["wk"] = jnp.concatenate([wk, jnp.zeros((KV_LORA, HEADS, HEAD_PAD - QK_NOPE), F32)],
                              -1).reshape(KV_LORA, HEADS * HEAD_PAD).astype(BF16)
    p["wvt"] = d_w_uv[l].T.astype(BF16)
    e = np.zeros((LANE, HEADS, HEAD_PAD), np.float32)
    for j in range(QK_ROPE):
        e[j, :, QK_NOPE + j] = 1.0
        e[QK_ROPE + j, :, QK_NOPE + j] = 1.0
    p["e"] = jnp.asarray(e.reshape(LANE, HEADS * HEAD_PAD), dtype=BF16)
    p["bg"] = bi[None, COL_G:]
    padr = ROUTER_W - N_GROUPS - N_EXPERTS
    p["wr"] = _hilo(jnp.concatenate([r_w_grp[l], r_w_exp[l], jnp.zeros((D, padr), F32)], -1))
    p["br"] = jnp.concatenate([r_b_grp[l], r_b_exp[l], jnp.zeros((padr,), F32)])[None]
    p["bs"] = jnp.broadcast_to(a_b_s[l][:, :, None], (A_GROUPS, CHUNK, CHUNK))
    p["wdw"] = jnp.concatenate([b_w_dw[l], jnp.zeros((1, BW), F32)], 0)
    return p


def kernel(x, c, ctx, c_ctx, w_ada, b_ada, w_in, b_in, a_ln_g, a_ln_b, a_w_s, a_b_s, b_w_dw, b_b_dw, b_ln_g,
           b_ln_b, d_q_g, d_w_uq, d_kv_g, d_w_uk, d_w_uv, w_branch, w_o, ln1_g, ln1_b, r_w_grp, r_b_grp,
           r_w_exp, r_b_exp, e_w1, e_w3, e_w2, ln2_g, ln2_b):
    assert x.shape == (1, SEQ, D) and ctx.shape == (1, N_CTX, D)
    x_ctx, x_lat = ctx[0], x[0]
    w_in_t = jnp.transpose(w_in, (0, 2, 1))
    c_rep = jnp.broadcast_to(jnp.concatenate([c, c_ctx[None]], axis=0)[:, :, None], (N_COND, D, LANE))
    mods = ada_mod(c_rep, w_ada, b_ada)
    tq_c, tq_s, tk = _rope_tables()
    fc = _fourier_consts()
    b_in3 = b_in.reshape(DEPTH, 1, -1)
    w_br, w_ob = w_branch.astype(BF16), w_o.astype(BF16)

    h = modln(x_ctx, x_lat, mods[0])
    for l in range(DEPTH):
        p = _layer_weights(l, w_in_t, b_in, d_w_uq, d_w_uk, d_w_uv, r_w_grp, r_b_grp, r_w_exp, r_b_exp, a_b_s,
                           b_w_dw)
        mod = mods[l]
        br_a = mix_a(l, h, w_in_t, b_in3, a_ln_g[l][None], a_ln_b[l][None], a_w_s[l].astype(BF16), p["bs"])
        br_b = conv_ln(glu(l, h, w_in_t, b_in3), p["wdw"], b_b_dw[l][None], b_ln_g[l][None], b_ln_b[l][None])
        br_c = fourier_mix(l, h, w_in_t, b_in3, fc)
        q, k, vt = mla_proj(l, h, w_in_t, b_in3, p["wkr"], p["bkr"], d_q_g[l][None], d_kv_g[l][None], p["wq"],
                            p["wqr"], p["wk"], p["wvt"], p["e"], tq_c, tq_s, tk)
        br_dt = attention(q, k, vt)
        y = merge(l, h, br_a, br_b, br_c, br_dt, w_in_t, p["bg"], w_br)
        x1, h2, ids, gates = post_mix(l, y, x_ctx, x_lat, mod, w_ob, ln1_g[l][None], ln1_b[l][None], p["wr"],
                                      p["br"])
        item_exp, item_blk0, item_nblk, n_used, row_tok, pos = _routing_tables(ids)
        ys = experts(l, item_exp, item_blk0, item_nblk, n_used, row_tok, h2, e_w1, e_w3, e_w2)
        x_ctx, x_lat, h = combine(pos, ys, gates, x1, mod, ln2_g[l][None], ln2_b[l][None],
                                  mods[min(l + 1, DEPTH - 1)])
    return x_lat[None]
```

```python
import functools
import math

import numpy as np
import jax
import jax.numpy as jnp
from jax import lax
from jax.experimental import pallas as pl
from jax.experimental.pallas import tpu as pltpu

F32 = jnp.float32
BF16 = jnp.bfloat16

D = 2048
SEQ = 8192
N_CTX = 256
T = N_CTX + SEQ
DEPTH = 2
GRID_W = 64
BW = 512
CHUNK = 128
A_GROUPS = 4
CONV_W = 31
F_GROUPS = 4
QK_NOPE = 64
QK_ROPE = 32
V_HEAD = 64
HEADS = 8
Q_LORA = 512
KV_LORA = 256
ROPE_THETA = 10000.0
MLA_SCALE = (QK_NOPE + QK_ROPE) ** -0.5
N_GROUPS = 4
EXP_PER_GROUP = 8
N_EXPERTS = 32
TOP_K = 2
D_EXPERT = 1024
MOE_BLOCK = 256
COL_A = 0
COL_B = 1024
COL_C = 2048
COL_D = 2560
COL_KV = 3072
COL_KR = 3328
COL_G = 3360
ALPHA = (2 * DEPTH) ** 0.25
LN_EPS = 1e-6

LANE = 128
HEAD_PAD = 128
RB = 256
NRB = T // RB
FFT_N1 = 64
FFT_N2 = 128
ROW_SUB = D // LANE
VMEM_LIMIT = 56 * 1024 * 1024


def _cparams(sem):
    return pltpu.CompilerParams(dimension_semantics=sem, vmem_limit_bytes=VMEM_LIMIT)


def _dot(a, b):
    return jnp.dot(a, b, preferred_element_type=F32)


def _split(x):
    hi = lax.bitcast_convert_type(lax.bitcast_convert_type(x, jnp.int32) & jnp.int32(-65536), F32)
    return hi.astype(BF16), (x - hi).astype(BF16)


def _hilo(x):
    return jnp.stack(_split(jnp.asarray(x, F32)))


def _dot3(a, t_ref):
    a_hi, a_lo = _split(a)
    return _dot(a_hi, t_ref[0]) + (_dot(a_hi, t_ref[1]) + _dot(a_lo, t_ref[0]))


def _dot3_t(t_ref, b):
    b_hi, b_lo = _split(b)
    return _dot(t_ref[0], b_hi) + (_dot(t_ref[1], b_hi) + _dot(t_ref[0], b_lo))


def _ln0(x):
    mu = jnp.mean(x, axis=-1, keepdims=True)
    xc = x - mu
    var = jnp.mean(xc * xc, axis=-1, keepdims=True)
    return xc * lax.rsqrt(var + LN_EPS)


def _mod_row(mod_ref, q):
    lat = mod_ref[0:1, q * D:(q + 1) * D]
    ctx = mod_ref[1:2, q * D:(q + 1) * D]
    return jnp.where(pl.program_id(0) == 0, ctx, lat)


ADA_TN = 1536
ADA_TK = 256
N_COND = 2


def _ada_kernel(c_ref, w_ref, b_ref, o_ref):
    accs = [b_ref[0] for _ in range(N_COND)]
    for k0 in range(0, D, ADA_TK):
        w = w_ref[0, k0:k0 + ADA_TK, :]
        for r in range(N_COND):
            s = c_ref[r, k0:k0 + ADA_TK, :]
            s = s * jax.nn.sigmoid(s)
            s = jnp.concatenate([s] * (ADA_TN // LANE), axis=-1)
            accs[r] = accs[r] + jnp.sum(w * s, axis=0, keepdims=True)
    o_ref[0] = jnp.concatenate(accs + [jnp.zeros((8 - N_COND, ADA_TN), F32)], axis=0)


def ada_mod(c_rep, w_ada, b_ada):
    return pl.pallas_call(
        _ada_kernel,
        grid=(DEPTH, 6 * D // ADA_TN),
        in_specs=[pl.BlockSpec((N_COND, D, LANE), lambda l, j: (0, 0, 0)),
                  pl.BlockSpec((1, D, ADA_TN), lambda l, j: (l, 0, j)),
                  pl.BlockSpec((1, 1, ADA_TN), lambda l, j: (l, 0, j))],
        out_specs=pl.BlockSpec((1, 8, ADA_TN), lambda l, j: (l, 0, j)),
        out_shape=jax.ShapeDtypeStruct((DEPTH, 8, 6 * D), F32),
        compiler_params=_cparams(("arbitrary", "arbitrary")),
        name="ada_mod",
    )(c_rep, w_ada, b_ada.reshape(DEPTH, 1, 6 * D))


def _stream_specs():
    return [pl.BlockSpec((RB, D), lambda i, *_: (0, 0)),
            pl.BlockSpec((RB, D), lambda i, *_: (jnp.maximum(i - 1, 0), 0))]


def _stream_block(ctx_ref, lat_ref):
    return jnp.where(pl.program_id(0) == 0, ctx_ref[...], lat_ref[...])


def _modln_kernel(xc_ref, xl_ref, mod_ref, h_ref):
    h = _ln0(_stream_block(xc_ref, xl_ref)) * (1.0 + _mod_row(mod_ref, 1)) + _mod_row(mod_ref, 0)
    h_ref[...] = h.astype(BF16)


def modln(x_ctx, x_lat, mod):
    return pl.pallas_call(
        _modln_kernel,
        grid=(NRB,),
        in_specs=_stream_specs() + [pl.BlockSpec((8, 6 * D), lambda i: (0, 0))],
        out_specs=pl.BlockSpec((RB, D), lambda i: (i, 0)),
        out_shape=jax.ShapeDtypeStruct((T, D), BF16),
        compiler_params=_cparams(("arbitrary",)),
        name="modln",
    )(x_ctx, x_lat, mod)


def _w_in_spec(l, width, col):
    return pl.BlockSpec((1, width, D), lambda i: (l, col, 0))


def _b_in_spec(l, width, col):
    return pl.BlockSpec((1, 1, width), lambda i: (l, 0, col))


CAST_CHUNK = 256


def _cast_transposed(w_ref, wbf_ref):
    width = w_ref.shape[0]
    for c0 in range(0, width, CAST_CHUNK):
        wbf_ref[:, c0:c0 + CAST_CHUNK] = w_ref[c0:c0 + CAST_CHUNK, :].T.astype(BF16)


def _cast_once(w_ref, wbf_ref):
    @pl.when(pl.program_id(0) == 0)
    def _():
        _cast_transposed(w_ref.at[0], wbf_ref)


def _mix_a_kernel(h_ref, w_ref, b_ref, g_ref, bb_ref, ws_ref, bs_ref, o_ref, wbf_ref):
    _cast_once(w_ref, wbf_ref)
    uv = jax.nn.gelu(_dot(h_ref[...], wbf_ref[...]) + b_ref[0])
    u = uv[:, :BW]
    v = _ln0(uv[:, BW:]) * g_ref[...] + bb_ref[...]
    v = v.astype(BF16)
    gw = BW // A_GROUPS
    for ch in range(RB // CHUNK):
        rows = slice(ch * CHUNK, (ch + 1) * CHUNK)
        parts = [_dot(ws_ref[g], v[rows, g * gw:(g + 1) * gw]) + bs_ref[g] for g in range(A_GROUPS)]
        mixed = jnp.concatenate(parts, axis=-1)
        o_ref[rows, :] = (u[rows, :] * mixed).astype(BF16)


def mix_a(l, h, w_in, b_in, ln_g, ln_b, ws, bs):
    return pl.pallas_call(
        _mix_a_kernel,
        grid=(NRB,),
        in_specs=[pl.BlockSpec((RB, D), lambda i: (i, 0)),
                  _w_in_spec(l, 2 * BW, COL_A // (2 * BW)),
                  _b_in_spec(l, 2 * BW, COL_A // (2 * BW)),
                  pl.BlockSpec((1, BW), lambda i: (0, 0)),
                  pl.BlockSpec((1, BW), lambda i: (0, 0)),
                  pl.BlockSpec((A_GROUPS, CHUNK, CHUNK), lambda i: (0, 0, 0)),
                  pl.BlockSpec((A_GROUPS, CHUNK, CHUNK), lambda i: (0, 0, 0))],
        out_specs=pl.BlockSpec((RB, BW), lambda i: (i, 0)),
        out_shape=jax.ShapeDtypeStruct((T, BW), BF16),
        scratch_shapes=[pltpu.VMEM((D, 2 * BW), BF16)],
        compiler_params=_cparams(("arbitrary",)),
        name="mix_a",
    )(h, w_in, b_in, ln_g, ln_b, ws, bs)


def _glu_kernel(h_ref, w_ref, b_ref, o_ref, wbf_ref):
    _cast_once(w_ref, wbf_ref)
    ab = _dot(h_ref[...], wbf_ref[...]) + b_ref[0]
    o_ref[...] = ab[:, :BW] * jax.nn.sigmoid(ab[:, BW:])


def glu(l, h, w_in, b_in):
    return pl.pallas_call(
        _glu_kernel,
        grid=(NRB,),
        in_specs=[pl.BlockSpec((RB, D), lambda i: (i, 0)),
                  _w_in_spec(l, 2 * BW, COL_B // (2 * BW)),
                  _b_in_spec(l, 2 * BW, COL_B // (2 * BW))],
        out_specs=pl.BlockSpec((RB, BW), lambda i: (i, 0)),
        out_shape=jax.ShapeDtypeStruct((T, BW), F32),
        scratch_shapes=[pltpu.VMEM((D, 2 * BW), BF16)],
        compiler_params=_cparams(("arbitrary",)),
        name="glu",
    )(h, w_in, b_in)


CONV_HALO = 16


def _conv_kernel(prev_ref, cur_ref, next_ref, w_ref, b_ref, g_ref, bb_ref, o_ref, ext_ref):
    i = pl.program_id(0)
    has_prev = i >= 2
    has_next = jnp.logical_and(i >= 1, i < NRB - 1)
    ext_ref[0:CONV_HALO, :] = jnp.where(has_prev, prev_ref[RB - CONV_HALO:RB, :], 0.0)
    ext_ref[CONV_HALO:CONV_HALO + RB, :] = cur_ref[...]
    ext_ref[CONV_HALO + RB:2 * CONV_HALO + RB, :] = jnp.where(has_next, next_ref[0:CONV_HALO, :], 0.0)
    off = CONV_HALO - CONV_W // 2
    acc = jnp.zeros((RB, BW), F32)
    for k in range(CONV_W):
        acc = acc + ext_ref[off + k:off + k + RB, :] * w_ref[k:k + 1, :]
    y = _ln0(acc + b_ref[...]) * g_ref[...] + bb_ref[...]
    o_ref[...] = (y * jax.nn.sigmoid(y)).astype(BF16)


def conv_ln(y, w_dw, b_dw, ln_g, ln_b):
    return pl.pallas_call(
        _conv_kernel,
        grid=(NRB,),
        in_specs=[pl.BlockSpec((RB, BW), lambda i: (jnp.maximum(i - 1, 0), 0)),
                  pl.BlockSpec((RB, BW), lambda i: (i, 0)),
                  pl.BlockSpec((RB, BW), lambda i: (jnp.minimum(i + 1, NRB - 1), 0)),
                  pl.BlockSpec((CONV_W + 1, BW), lambda i: (0, 0)),
                  pl.BlockSpec((1, BW), lambda i: (0, 0)),
                  pl.BlockSpec((1, BW), lambda i: (0, 0)),
                  pl.BlockSpec((1, BW), lambda i: (0, 0))],
        out_specs=pl.BlockSpec((RB, BW), lambda i: (i, 0)),
        out_shape=jax.ShapeDtypeStruct((T, BW), BF16),
        scratch_shapes=[pltpu.VMEM((RB + 2 * CONV_HALO, BW), F32)],
        compiler_params=_cparams(("arbitrary",)),
        name="conv_ln",
    )(y, y, y, w_dw, b_dw, ln_g, ln_b)


def _dft_tables(n):
    k = np.arange(n, dtype=np.int64)
    ang = 2.0 * np.pi * ((k[:, None] * k[None, :]) % n).astype(np.float64) / n
    return np.cos(ang), np.sin(ang)


def _fproj_kernel(h_ref, w_ref, b_ref, cs_ref, ar_ref, ai_ref, wbf_ref):
    _cast_once(w_ref, wbf_ref)
    z = _dot(h_ref[...], wbf_ref[...]) + b_ref[0]
    a = _dot3(z, cs_ref)
    ar_ref[...] = a[:, :BW]
    ai_ref[...] = a[:, BW:]


def fourier_proj(l, h, w_in, b_in, cs):
    return pl.pallas_call(
        _fproj_kernel,
        grid=(NRB,),
        in_specs=[pl.BlockSpec((RB, D), lambda i: (i, 0)),
                  _w_in_spec(l, BW, COL_C // BW),
                  _b_in_spec(l, BW, COL_C // BW),
                  pl.BlockSpec((2, BW, 2 * BW), lambda i: (0, 0, 0))],
        out_specs=[pl.BlockSpec((RB, BW), lambda i: (i, 0)),
                   pl.BlockSpec((RB, BW), lambda i: (i, 0))],
        out_shape=[jax.ShapeDtypeStruct((T, BW), F32), jax.ShapeDtypeStruct((T, BW), F32)],
        scratch_shapes=[pltpu.VMEM((D, BW), BF16)],
        compiler_params=_cparams(("arbitrary",)),
        name="fourier_proj",
    )(h, w_in, b_in, cs)


FFT_TILE_N2 = 4


def _fft1_kernel(ar_ref, ai_ref, m_ref, ct_ref, st_ref, br_ref, bi_ref):
    x = jnp.concatenate([ar_ref[...], ai_ref[...]], axis=0)
    b = _dot3_t(m_ref, x)
    b_re, b_im = b[:FFT_N1], b[FFT_N1:]
    reps = BW // LANE
    for j in range(FFT_TILE_N2):
        ct = jnp.concatenate([ct_ref[j]] * reps, axis=-1)
        st = jnp.concatenate([st_ref[j]] * reps, axis=-1)
        lanes = slice(j * BW, (j + 1) * BW)
        br_ref[:, lanes] = b_re[:, lanes] * ct + b_im[:, lanes] * st
        bi_ref[:, lanes] = b_im[:, lanes] * ct - b_re[:, lanes] * st


def fft_stage1(ar2d, ai2d, m1, ct, st):
    tn = FFT_TILE_N2 * BW
    return pl.pallas_call(
        _fft1_kernel,
        grid=(FFT_N2 // FFT_TILE_N2,),
        in_specs=[pl.BlockSpec((FFT_N1, tn), lambda j: (0, j)),
                  pl.BlockSpec((FFT_N1, tn), lambda j: (0, j)),
                  pl.BlockSpec((2, 2 * FFT_N1, 2 * FFT_N1), lambda j: (0, 0, 0)),
                  pl.BlockSpec((FFT_TILE_N2, FFT_N1, LANE), lambda j: (j, 0, 0)),
                  pl.BlockSpec((FFT_TILE_N2, FFT_N1, LANE), lambda j: (j, 0, 0))],
        out_specs=[pl.BlockSpec((FFT_N1, tn), lambda j: (0, j)),
                   pl.BlockSpec((FFT_N1, tn), lambda j: (0, j))],
        out_shape=[jax.ShapeDtypeStruct((FFT_N1, FFT_N2 * BW), F32)] * 2,
        compiler_params=_cparams(("arbitrary",)),
        name="fft_stage1",
    )(ar2d, ai2d, m1, ct, st)


FFT_TILE_K1 = 8


def _fft2_kernel(br_ref, bi_ref, c_ref, s_ref, p_ref, o_ref, *, scale):
    ys = [(_dot3_t(c_ref, br_ref[a]) + _dot3_t(s_ref, bi_ref[a])) * scale for a in range(FFT_TILE_K1)]
    y = jnp.concatenate(ys, axis=0).astype(BF16)
    y = _dot(p_ref[...], y)
    o_ref[...] = y.reshape(FFT_N2, FFT_TILE_K1, BW)


def fft_stage2(br3d, bi3d, c128, s128, perm, scale):
    return pl.pallas_call(
        functools.partial(_fft2_kernel, scale=scale),
        grid=(FFT_N1 // FFT_TILE_K1,),
        in_specs=[pl.BlockSpec((FFT_TILE_K1, FFT_N2, BW), lambda a: (a, 0, 0)),
                  pl.BlockSpec((FFT_TILE_K1, FFT_N2, BW), lambda a: (a, 0, 0)),
                  pl.BlockSpec((2, FFT_N2, FFT_N2), lambda a: (0, 0, 0)),
                  pl.BlockSpec((2, FFT_N2, FFT_N2), lambda a: (0, 0, 0)),
                  pl.BlockSpec((FFT_TILE_K1 * FFT_N2, FFT_TILE_K1 * FFT_N2), lambda a: (0, 0))],
        out_specs=pl.BlockSpec((FFT_N2, FFT_TILE_K1, BW), lambda a: (0, a, 0)),
        out_shape=jax.ShapeDtypeStruct((FFT_N2, FFT_N1, BW), F32),
        compiler_params=_cparams(("arbitrary",)),
        name="fft_stage2",
    )(br3d, bi3d, c128, s128, perm)


def _dft_ctx_kernel(ar_ref, ai_ref, c_ref, s_ref, o_ref, *, scale):
    o_ref[...] = (_dot3_t(c_ref, ar_ref[...]) + _dot3_t(s_ref, ai_ref[...])) * scale


def dft_ctx(ar, ai, c, s, scale):
    return pl.pallas_call(
        functools.partial(_dft_ctx_kernel, scale=scale),
        grid=(1,),
        in_specs=[pl.BlockSpec((N_CTX, BW), lambda i: (0, 0)),
                  pl.BlockSpec((N_CTX, BW), lambda i: (0, 0)),
                  pl.BlockSpec((2, N_CTX, N_CTX), lambda i: (0, 0, 0)),
                  pl.BlockSpec((2, N_CTX, N_CTX), lambda i: (0, 0, 0))],
        out_specs=pl.BlockSpec((N_CTX, BW), lambda i: (0, 0)),
        out_shape=jax.ShapeDtypeStruct((N_CTX, BW), F32),
        compiler_params=_cparams(("arbitrary",)),
        name="dft_ctx",
    )(ar, ai, c, s)


def _fourier_consts():
    c64, s64 = _dft_tables(FFT_N1)
    m1 = np.block([[c64, s64], [-s64, c64]]).astype(np.float32)
    k1 = np.arange(FFT_N1, dtype=np.float64)
    n2 = np.arange(FFT_N2, dtype=np.float64)
    phi = 2.0 * np.pi * n2[:, None] * k1[None, :] / SEQ
    ct = np.repeat(np.cos(phi)[:, :, None], LANE, axis=2).astype(np.float32)
    st = np.repeat(np.sin(phi)[:, :, None], LANE, axis=2).astype(np.float32)
    c128, s128 = _dft_tables(FFT_N2)
    gw = BW // F_GROUPS
    cg, sg = _dft_tables(gw)
    cs = np.zeros((BW, 2 * BW), np.float32)
    for g in range(F_GROUPS):
        sl = slice(g * gw, (g + 1) * gw)
        cs[sl, sl] = cg
        cs[sl, BW + g * gw:BW + (g + 1) * gw] = -sg
    rows = np.arange(FFT_TILE_K1 * FFT_N2)
    perm = np.zeros((rows.size, rows.size), np.float32)
    perm[rows, (rows % FFT_TILE_K1) * FFT_N2 + rows // FFT_TILE_K1] = 1.0
    cc, sc = _dft_tables(N_CTX)
    return dict(m1=_hilo(m1), ct=jnp.asarray(ct), st=jnp.asarray(st), c128=_hilo(c128), s128=_hilo(s128),
                cs=_hilo(cs), perm=jnp.asarray(perm, dtype=BF16), cc=_hilo(cc), sc=_hilo(sc))


def fourier_mix(l, h, w_in, b_in, fc):
    gw = BW // F_GROUPS
    ar, ai = fourier_proj(l, h, w_in, b_in, fc["cs"])
    y_ctx = dft_ctx(ar, ai, fc["cc"], fc["sc"], 1.0 / math.sqrt(N_CTX * gw))
    ar2 = ar[N_CTX:].reshape(FFT_N1, FFT_N2 * BW)
    ai2 = ai[N_CTX:].reshape(FFT_N1, FFT_N2 * BW)
    br, bi = fft_stage1(ar2, ai2, fc["m1"], fc["ct"], fc["st"])
    y = fft_stage2(br.reshape(FFT_N1, FFT_N2, BW), bi.reshape(FFT_N1, FFT_N2, BW),
                   fc["c128"], fc["s128"], fc["perm"], 1.0 / math.sqrt(SEQ * gw))
    return jnp.concatenate([y_ctx, y.reshape(SEQ, BW)], axis=0)


VT_ROWS = 80
LOG2E = math.log2(math.e)


def _mla_proj_kernel(h_ref, wcq_ref, bcq_ref, wkv_ref, bkv_ref, wkr_ref, bkr_ref, qg_ref, kvg_ref, wq_ref, wqr_ref,
                     wk_ref, wvt_ref, e_ref, tq_c_ref, tq_s_ref, tk_ref, q_ref, k_ref, vt_ref, wcq_bf, wkv_bf):
    _cast_once(wcq_ref, wcq_bf)
    _cast_once(wkv_ref, wkv_bf)
    h = h_ref[...]
    cq = _dot(h, wcq_bf[...]) + bcq_ref[0]
    cq = cq * lax.rsqrt(jnp.mean(cq * cq, axis=-1, keepdims=True) + LN_EPS) * qg_ref[...]
    cq = cq.astype(BF16)
    cos_t = jnp.concatenate([tq_c_ref[...]] * HEADS, axis=-1)
    sin_t = jnp.concatenate([tq_s_ref[...]] * HEADS, axis=-1)
    q = (_dot(cq, wq_ref[...]) * cos_t + _dot(cq, wqr_ref[...]) * sin_t) * (MLA_SCALE * LOG2E)
    q_ref[...] = q.astype(BF16)
    ckv = _dot(h, wkv_bf[...]) + bkv_ref[0]
    ckv = ckv * lax.rsqrt(jnp.mean(ckv * ckv, axis=-1, keepdims=True) + LN_EPS) * kvg_ref[...]
    ckv = ckv.astype(BF16)
    kr = ((_dot(h, wkr_ref[...]) + bkr_ref[...]) * tk_ref[...]).astype(BF16)
    k_ref[...] = (_dot(ckv, wk_ref[...]) + _dot(kr, e_ref[...])).astype(BF16)
    vt = lax.dot_general(wvt_ref[...], ckv, (((1,), (1,)), ((), ())), preferred_element_type=F32)
    ones_rows = (lax.broadcasted_iota(jnp.int32, (VT_ROWS - V_HEAD, RB), 0) == 0).astype(BF16)
    for hd in range(HEADS):
        vt_ref[0, hd * VT_ROWS:hd * VT_ROWS + V_HEAD, :] = vt[hd * V_HEAD:(hd + 1) * V_HEAD, :].astype(BF16)
        vt_ref[0, hd * VT_ROWS + V_HEAD:(hd + 1) * VT_ROWS, :] = ones_rows


def mla_proj(l, h, w_in, b_in, wkr, bkr, qg, kvg, wq, wqr, wk, wvt, e, tq_c, tq_s, tk):
    const = lambda i: (0, 0)
    row = lambda i: (i, 0)
    return pl.pallas_call(
        _mla_proj_kernel,
        grid=(NRB,),
        in_specs=[pl.BlockSpec((RB, D), row),
                  _w_in_spec(l, Q_LORA, COL_D // Q_LORA),
                  _b_in_spec(l, Q_LORA, COL_D // Q_LORA),
                  _w_in_spec(l, KV_LORA, COL_KV // KV_LORA),
                  _b_in_spec(l, KV_LORA, COL_KV // KV_LORA),
                  pl.BlockSpec((D, LANE), const),
                  pl.BlockSpec((1, LANE), const),
                  pl.BlockSpec((1, Q_LORA), const),
                  pl.BlockSpec((1, KV_LORA), const),
                  pl.BlockSpec((Q_LORA, HEADS * HEAD_PAD), const),
                  pl.BlockSpec((Q_LORA, HEADS * HEAD_PAD), const),
                  pl.BlockSpec((KV_LORA, HEADS * HEAD_PAD), const),
                  pl.BlockSpec((HEADS * V_HEAD, KV_LORA), const),
                  pl.BlockSpec((LANE, HEADS * HEAD_PAD), const),
                  pl.BlockSpec((RB, HEAD_PAD), row),
                  pl.BlockSpec((RB, HEAD_PAD), row),
                  pl.BlockSpec((RB, LANE), row)],
        out_specs=[pl.BlockSpec((RB, HEADS * HEAD_PAD), row),
                   pl.BlockSpec((RB, HEADS * HEAD_PAD), row),
                   pl.BlockSpec((1, HEADS * VT_ROWS, RB), lambda i: (i, 0, 0))],
        out_shape=[jax.ShapeDtypeStruct((T, HEADS * HEAD_PAD), BF16),
                   jax.ShapeDtypeStruct((T, HEADS * HEAD_PAD), BF16),
                   jax.ShapeDtypeStruct((NRB, HEADS * VT_ROWS, RB), BF16)],
        scratch_shapes=[pltpu.VMEM((D, Q_LORA), BF16), pltpu.VMEM((D, KV_LORA), BF16)],
        compiler_params=_cparams(("arbitrary",)),
        name="mla_proj",
    )(h, w_in, b_in, w_in, b_in, wkr, bkr, qg, kvg, wq, wqr, wk, wvt, e, tq_c, tq_s, tk)


KV_TILE = RB
KV_STAGE = 1024
N_STAGE = SEQ // KV_STAGE
TILES_PER_STAGE = KV_STAGE // KV_TILE


HEAD_GROUP = 2


def _attn_kernel(q_ref, k_ref, vt_ref, o_ref, sa_ref, sb_ref, m_ref, acc_ref):
    i = pl.program_id(1)
    heads = range(HEAD_GROUP)
    qs = [q_ref[:, g * HEAD_PAD:(g + 1) * HEAD_PAD] for g in heads]

    def scores(g, start, n):
        kt = k_ref[pl.ds(start, n), g * HEAD_PAD:(g + 1) * HEAD_PAD]
        return lax.dot_general(kt, qs[g], (((1,), (1,)), ((), ())),
                               preferred_element_type=F32)

    def values(g, tile):
        return vt_ref[tile, g * VT_ROWS:(g + 1) * VT_ROWS, :]

    def softmax_pv(g, s_ref, tile0, m, acc):
        m_new = jnp.maximum(m, jnp.max(s_ref[g], axis=0, keepdims=True))
        p = jnp.exp2(s_ref[g] - m_new).astype(BF16)
        acc = jnp.exp2(m - m_new) * acc
        for u in range(TILES_PER_STAGE):
            acc = acc + _dot(values(g, tile0 + u), p[u * KV_TILE:(u + 1) * KV_TILE, :])
        return m_new, acc

    for g in heads:
        s0 = scores(g, 0, N_CTX)
        m0 = jnp.max(s0, axis=0, keepdims=True)
        m_ref[g] = m0
        acc_ref[g] = _dot(values(g, 0), jnp.exp2(s0 - m0).astype(BF16))

    @pl.when(i > 0)
    def _():
        first_tile = N_CTX // KV_TILE
        for g in heads:
            sa_ref[g] = scores(g, N_CTX, KV_STAGE)

        def body(j, carry):
            ms, accs = list(carry[0]), list(carry[1])
            base = pl.multiple_of(N_CTX + 2 * j * KV_STAGE, KV_TILE)
            tile = first_tile + 2 * j * TILES_PER_STAGE
            for g in heads:
                sb_ref[g] = scores(g, base + KV_STAGE, KV_STAGE)
            for g in heads:
                ms[g], accs[g] = softmax_pv(g, sa_ref, tile, ms[g], accs[g])
            for g in heads:
                sa_ref[g] = scores(g, base + 2 * KV_STAGE, KV_STAGE)
            for g in heads:
                ms[g], accs[g] = softmax_pv(g, sb_ref, tile + TILES_PER_STAGE, ms[g], accs[g])
            return tuple(ms), tuple(accs)

        init = (tuple(m_ref[g] for g in heads), tuple(acc_ref[g] for g in heads))
        ms, accs = lax.fori_loop(0, N_STAGE // 2 - 1, body, init)
        ms, accs = list(ms), list(accs)
        for g in heads:
            sb_ref[g] = scores(g, N_CTX + (N_STAGE - 1) * KV_STAGE, KV_STAGE)
        for g in heads:
            ms[g], accs[g] = softmax_pv(g, sa_ref, first_tile + (N_STAGE - 2) * TILES_PER_STAGE, ms[g], accs[g])
        for g in heads:
            ms[g], accs[g] = softmax_pv(g, sb_ref, first_tile + (N_STAGE - 1) * TILES_PER_STAGE, ms[g], accs[g])
            acc_ref[g] = accs[g]

    for g in heads:
        acc = acc_ref[g]
        o_ref[g * V_HEAD:(g + 1) * V_HEAD, :] = acc[:V_HEAD, :] / acc[V_HEAD:V_HEAD + 1, :]


def attention(q, k, vt):
    return pl.pallas_call(
        _attn_kernel,
        grid=(HEADS // HEAD_GROUP, NRB),
        in_specs=[pl.BlockSpec((RB, HEAD_GROUP * HEAD_PAD), lambda h, i: (i, h)),
                  pl.BlockSpec((T, HEAD_GROUP * HEAD_PAD), lambda h, i: (0, h)),
                  pl.BlockSpec((NRB, HEAD_GROUP * VT_ROWS, KV_TILE), lambda h, i: (0, h, 0))],
        out_specs=pl.BlockSpec((HEAD_GROUP * V_HEAD, RB), lambda h, i: (h, i)),
        out_shape=jax.ShapeDtypeStruct((HEADS * V_HEAD, T), F32),
        scratch_shapes=[pltpu.VMEM((HEAD_GROUP, KV_STAGE, RB), F32),
                        pltpu.VMEM((HEAD_GROUP, KV_STAGE, RB), F32),
                        pltpu.VMEM((HEAD_GROUP, 1, RB), F32),
                        pltpu.VMEM((HEAD_GROUP, VT_ROWS, RB), F32)],
        compiler_params=_cparams(("arbitrary", "arbitrary")),
        name="attention",
    )(q, k, vt)


MERGE_TN = 512


N_BRANCH = 4


MERGE_RB = 768


def _merge_kernel(h_ref, a_ref, b_ref, c_ref, dt_ref, *refs):
    wg_refs, bg_refs = refs[:N_BRANCH], refs[N_BRANCH:2 * N_BRANCH]
    wb_ref, y_ref, wg_bf = refs[2 * N_BRANCH:]

    @pl.when(pl.program_id(1) == 0)
    def _():
        for n in range(N_BRANCH):
            _cast_transposed(wg_refs[n], wg_bf.at[n])

    h = h_ref[...]
    branches = (a_ref[...], b_ref[...], c_ref[...].astype(BF16), dt_ref[...].T.astype(BF16))
    y = jnp.zeros((MERGE_RB, MERGE_TN), F32)
    for n, br in enumerate(branches):
        gate = jax.nn.sigmoid(_dot(h, wg_bf[n]) + bg_refs[n][...])
        y = y + gate * _dot(br, wb_ref[0, n])
    y_ref[...] = y.astype(BF16)


def merge(l, h, br_a, br_b, br_c, br_dt, w_in_t, bg, wb):
    row = lambda j, i: (i, 0)
    n_col = D // MERGE_TN
    gate_specs = [pl.BlockSpec((pl.Squeezed(), pl.Element(MERGE_TN), pl.Element(D)),
                               functools.partial(
                                   lambda j, i, n: (l, pl.multiple_of(COL_G + n * D + j * MERGE_TN, QK_ROPE), 0), n=n),
                               pipeline_mode=pl.Buffered(1))
                  for n in range(N_BRANCH)]
    bias_specs = [pl.BlockSpec((1, MERGE_TN), functools.partial(lambda j, i, n: (0, n * n_col + j), n=n))
                  for n in range(N_BRANCH)]
    return pl.pallas_call(
        _merge_kernel,
        grid=(n_col, T // MERGE_RB),
        in_specs=[pl.BlockSpec((MERGE_RB, D), row),
                  pl.BlockSpec((MERGE_RB, BW), row),
                  pl.BlockSpec((MERGE_RB, BW), row),
                  pl.BlockSpec((MERGE_RB, BW), row),
                  pl.BlockSpec((BW, MERGE_RB), lambda j, i: (0, i))]
                 + gate_specs + bias_specs
                 + [pl.BlockSpec((1, N_BRANCH, BW, MERGE_TN), lambda j, i: (l, 0, 0, j))],
        out_specs=pl.BlockSpec((MERGE_RB, MERGE_TN), lambda j, i: (i, j)),
        out_shape=jax.ShapeDtypeStruct((T, D), BF16),
        scratch_shapes=[pltpu.VMEM((N_BRANCH, D, MERGE_TN), BF16)],
        compiler_params=_cparams(("arbitrary", "arbitrary")),
        name="merge",
    )(h, br_a, br_b, br_c, br_dt, *([w_in_t] * N_BRANCH), *([bg] * N_BRANCH), wb)


ROUTER_W = 128


def _post_mix_kernel(y_ref, xc_ref, xl_ref, mod_ref, wo_ref, g_ref, b_ref, wr_ref, br_ref,
                     x1_ref, h2_ref, ids_ref, gates_ref):
    mix = _dot(y_ref[...], wo_ref[0])
    x1 = _ln0(ALPHA * _stream_block(xc_ref, xl_ref) + _mod_row(mod_ref, 2) * mix) * g_ref[...] + b_ref[...]
    x1_ref[...] = x1
    h2 = _ln0(x1) * (1.0 + _mod_row(mod_ref, 4)) + _mod_row(mod_ref, 3)
    _rows_to_slabs(h2, h2_ref)
    logits = _dot3(h2, wr_ref) + br_ref[...]
    lane = lax.broadcasted_iota(jnp.int32, (RB, ROUTER_W), 1).astype(F32)
    neg = jnp.float32(-jnp.inf)
    big = jnp.float32(ROUTER_W)
    gl = jnp.where(lane < N_GROUPS, logits, neg)
    gmax = jnp.max(gl, axis=-1, keepdims=True)
    g_idx = jnp.min(jnp.where(gl == gmax, lane, big), axis=-1, keepdims=True)
    g_p = 1.0 / jnp.sum(jnp.exp(gl - gmax), axis=-1, keepdims=True)
    e_lane = lane - N_GROUPS
    in_grp = jnp.logical_and(e_lane >= g_idx * EXP_PER_GROUP, e_lane < (g_idx + 1) * EXP_PER_GROUP)
    el = jnp.where(in_grp, logits, neg)
    v1 = jnp.max(el, axis=-1, keepdims=True)
    i1 = jnp.min(jnp.where(el == v1, lane, big), axis=-1, keepdims=True)
    el2 = jnp.where(lane == i1, neg, el)
    v2 = jnp.max(el2, axis=-1, keepdims=True)
    i2 = jnp.min(jnp.where(el2 == v2, lane, big), axis=-1, keepdims=True)
    e21 = jnp.exp(v2 - v1)
    p1 = 1.0 / (1.0 + e21)
    p2 = e21 * p1
    ids_ref[...] = jnp.where(lane == 0, i1 - N_GROUPS, jnp.where(lane == 1, i2 - N_GROUPS, 0.0)).astype(jnp.int32)
    gates_ref[...] = jnp.where(lane == 0, p1 * g_p, jnp.where(lane == 1, p2 * g_p, 0.0))


def post_mix(l, y, x_ctx, x_lat, mod, wo, ln_g, ln_b, wr, br):
    row = lambda i: (i, 0)
    const = lambda i: (0, 0)
    return pl.pallas_call(
        _post_mix_kernel,
        grid=(NRB,),
        in_specs=[pl.BlockSpec((RB, D), row)] + _stream_specs()
                 + [pl.BlockSpec((8, 6 * D), const),
                  pl.BlockSpec((1, D, D), lambda i: (l, 0, 0)),
                  pl.BlockSpec((1, D), const),
                  pl.BlockSpec((1, D), const),
                  pl.BlockSpec((2, D, ROUTER_W), lambda i: (0, 0, 0)),
                  pl.BlockSpec((1, ROUTER_W), const)],
        out_specs=[pl.BlockSpec((RB, D), row),
                   pl.BlockSpec((RB * ROW_SUB, LANE), row),
                   pl.BlockSpec((RB, ROUTER_W), row),
                   pl.BlockSpec((RB, ROUTER_W), row)],
        out_shape=[jax.ShapeDtypeStruct((T, D), F32),
                   jax.ShapeDtypeStruct((T * ROW_SUB, LANE), F32),
                   jax.ShapeDtypeStruct((T, ROUTER_W), jnp.int32),
                   jax.ShapeDtypeStruct((T, ROUTER_W), F32)],
        compiler_params=_cparams(("arbitrary",)),
        name="post_mix",
    )(y, x_ctx, x_lat, mod, wo, ln_g, ln_b, wr, br)


N_ROWS = -(-(T * TOP_K + N_EXPERTS * (MOE_BLOCK - 1)) // MOE_BLOCK) * MOE_BLOCK
N_BLK = N_ROWS // MOE_BLOCK


N_DMA_PRIORITIES = 2


GATHER_UNROLL = 8


def _row_gather(src_hbm, idx_ref, base, n, dst_ref, sem, first=0):
    def issue(t, _):
        for u in range(GATHER_UNROLL):
            r = first + t * GATHER_UNROLL + u
            src_row = pl.multiple_of(idx_ref[base + r] * ROW_SUB, ROW_SUB)
            dst_row = pl.multiple_of(r * ROW_SUB, ROW_SUB)
            pltpu.make_async_copy(src_hbm.at[pl.ds(src_row, ROW_SUB), :],
                                  dst_ref.at[pl.ds(dst_row, ROW_SUB), :], sem).start(priority=u % N_DMA_PRIORITIES)
        return 0
    lax.fori_loop(0, n // GATHER_UNROLL, issue, 0)


def _row_gather_wait(src_hbm, n, dst_ref, sem):
    pltpu.make_async_copy(src_hbm.at[pl.ds(0, n * ROW_SUB), :], dst_ref, sem).wait()


def _rows_from_slabs(buf_ref, start, n, stride):
    return jnp.concatenate([buf_ref[pl.ds(start + c, n, stride=stride), :] for c in range(ROW_SUB)], axis=-1)


def _rows_to_slabs(val, out_ref):
    n = val.shape[0]
    for c in range(ROW_SUB):
        out_ref[pl.ds(c, n, stride=ROW_SUB), :] = val[:, c * LANE:(c + 1) * LANE]


ITEM_BLKS = 3
ITEM_ROWS = ITEM_BLKS * MOE_BLOCK
N_ITEMS = N_BLK // ITEM_BLKS + N_EXPERTS
EXP_CHUNK = 512
N_CHUNK = D_EXPERT // EXP_CHUNK
BLK_SLABS = MOE_BLOCK * ROW_SUB


WAIT_ROWS = 8


def _expert_kernel(item_exp_ref, item_blk0_ref, item_nblk_ref, item_cnt_ref, n_used_ref, row_tok_ref, h_hbm, w1_ref,
                   w3_ref, w2_ref, y_hbm, xslab, yacc, stage, w1c, w3c, w2c, sem_x, sem_y):
    i, j = pl.program_id(0), pl.program_id(1)
    nblk = item_nblk_ref[i]
    blk0 = item_blk0_ref[i]
    xs = xslab.at[i % 2]

    def gather_item(item):
        _row_gather(h_hbm, row_tok_ref, item_blk0_ref[item] * MOE_BLOCK, item_cnt_ref[item], xslab.at[item % 2],
                    sem_x)

    @pl.when(jnp.logical_and(i == 0, j == 0))
    def _():
        xslab[...] = jnp.zeros(xslab.shape, F32)
        gather_item(0)

    w1c[...] = w1_ref[0, 0].astype(BF16)
    w3c[...] = w3_ref[0, 0].astype(BF16)
    w2c[...] = w2_ref[0, 0].astype(BF16)
    last = j == N_CHUNK - 1

    def rows_of(b):
        return pl.ds(pl.multiple_of(b * MOE_BLOCK, MOE_BLOCK), MOE_BLOCK)

    def partial_out(xb):
        a = _dot(xb, w1c[...])
        g = _dot(xb, w3c[...])
        return _dot((a * jax.nn.sigmoid(a) * g).astype(BF16), w2c[...])

    def y_copy(b, slot):
        dst = pl.multiple_of((blk0 + b) * BLK_SLABS, BLK_SLABS)
        return pltpu.make_async_copy(stage.at[slot], y_hbm.at[pl.ds(dst, BLK_SLABS), :], sem_y.at[slot])

    def x_block(b):
        return _rows_from_slabs(xs, b * BLK_SLABS, MOE_BLOCK, ROW_SUB).astype(BF16)

    @pl.when(j == 0)
    def _():
        def wait_rows(t, _):
            pltpu.make_async_copy(h_hbm.at[pl.ds(0, WAIT_ROWS * ROW_SUB), :],
                                  xs.at[pl.ds(0, WAIT_ROWS * ROW_SUB), :], sem_x).wait()
            return 0
        lax.fori_loop(0, item_cnt_ref[i] // WAIT_ROWS, wait_rows, 0)

        @pl.when(i + 1 < N_ITEMS)
        def _():
            gather_item(i + 1)

        def first_block(b, _):
            yacc[rows_of(b), :] = partial_out(x_block(b))
            return 0
        lax.fori_loop(0, nblk, first_block, 0)

    @pl.when(jnp.logical_and(j > 0, jnp.logical_not(last)))
    def _():
        def mid_block(b, _):
            yacc[rows_of(b), :] += partial_out(x_block(b))
            return 0
        lax.fori_loop(0, nblk, mid_block, 0)

    @pl.when(last)
    def _():
        def last_block(b, _):
            slot = b % 2

            @pl.when(b >= 2)
            def _():
                y_copy(b - 2, slot).wait()
            _rows_to_slabs(yacc[rows_of(b), :] + partial_out(x_block(b)), stage.at[slot])
            y_copy(b, slot).start()
            return 0
        lax.fori_loop(0, nblk, last_block, 0)

        for back in (1, 2):
            @pl.when(nblk >= back)
            def _():
                b = nblk - back
                y_copy(b, b % 2).wait()

    @pl.when(jnp.logical_and(i == N_ITEMS - 1, last))
    def _():
        stage[0] = jnp.zeros((BLK_SLABS, LANE), F32)

        def fill(b, _):
            dst = pl.multiple_of(b * BLK_SLABS, BLK_SLABS)
            cp = pltpu.make_async_copy(stage.at[0], y_hbm.at[pl.ds(dst, BLK_SLABS), :], sem_y.at[0])
            cp.start()
            cp.wait()
            return 0
        lax.fori_loop(n_used_ref[0], N_BLK, fill, 0)


def experts(l, item_exp, item_blk0, item_nblk, item_cnt, n_used, row_tok, h2_slabs, w1, w3, w2):
    wspec = lambda blk, imap: pl.BlockSpec(blk, imap)
    grid_spec = pltpu.PrefetchScalarGridSpec(
        num_scalar_prefetch=6,
        grid=(N_ITEMS, N_CHUNK),
        in_specs=[pl.BlockSpec(memory_space=pl.ANY),
                  wspec((1, 1, D, EXP_CHUNK), lambda i, j, ie, *_: (l, ie[i], 0, j)),
                  wspec((1, 1, D, EXP_CHUNK), lambda i, j, ie, *_: (l, ie[i], 0, j)),
                  wspec((1, 1, EXP_CHUNK, D), lambda i, j, ie, *_: (l, ie[i], j, 0))],
        out_specs=pl.BlockSpec(memory_space=pl.ANY),
        scratch_shapes=[pltpu.VMEM((2, ITEM_BLKS * BLK_SLABS, LANE), F32),
                        pltpu.VMEM((ITEM_ROWS, D), F32),
                        pltpu.VMEM((2, BLK_SLABS, LANE), F32),
                        pltpu.VMEM((D, EXP_CHUNK), BF16),
                        pltpu.VMEM((D, EXP_CHUNK), BF16),
                        pltpu.VMEM((EXP_CHUNK, D), BF16),
                        pltpu.SemaphoreType.DMA(()),
                        pltpu.SemaphoreType.DMA((2,))],
    )
    return pl.pallas_call(
        _expert_kernel,
        grid_spec=grid_spec,
        out_shape=jax.ShapeDtypeStruct((N_ROWS * ROW_SUB, LANE), F32),
        compiler_params=_cparams(("arbitrary", "arbitrary")),
        name="experts",
    )(item_exp, item_blk0, item_nblk, item_cnt, n_used, row_tok, h2_slabs, w1, w3, w2)


COMBINE_SUB = 64


def _combine_kernel(pos_ref, y_hbm, gates_ref, x1_ref, mod_ref, g_ref, b_ref, nmod_ref,
                    x2c_ref, x2l_ref, hn_ref, ybuf, sems):
    i = pl.program_id(0)
    slot = i % 2
    n = RB * TOP_K
    n_sub = RB // COMBINE_SUB

    @pl.when(i == 0)
    def _():
        _row_gather(y_hbm, pos_ref, 0, n, ybuf.at[0], sems.at[0])

    _row_gather_wait(y_hbm, n, ybuf.at[slot], sems.at[slot])
    shift5, shift0, scale1 = _mod_row(mod_ref, 5), _mod_row(nmod_ref, 0), 1.0 + _mod_row(nmod_ref, 1)
    for s in range(n_sub):
        @pl.when(i + 1 < NRB)
        def _():
            _row_gather(y_hbm, pos_ref, (i + 1) * n, n // n_sub, ybuf.at[1 - slot], sems.at[1 - slot],
                        first=s * (n // n_sub))

        rows = slice(s * COMBINE_SUB, (s + 1) * COMBINE_SUB)
        base = s * COMBINE_SUB * TOP_K * ROW_SUB
        y0 = _rows_from_slabs(ybuf.at[slot], base, COMBINE_SUB, TOP_K * ROW_SUB)
        y1 = _rows_from_slabs(ybuf.at[slot], base + ROW_SUB, COMBINE_SUB, TOP_K * ROW_SUB)
        gates = gates_ref[rows, :]
        f = gates[:, 0:1] * y0 + gates[:, 1:2] * y1
        x2 = _ln0(ALPHA * x1_ref[rows, :] + shift5 * f) * g_ref[...] + b_ref[...]
        x2l_ref[rows, :] = x2

        @pl.when(i == 0)
        def _():
            x2c_ref[rows, :] = x2

        hn_ref[rows, :] = (_ln0(x2) * scale1 + shift0).astype(BF16)


def combine(pos, y_slabs, gates, x1, mod, ln_g, ln_b, next_mod):
    row = lambda i, p: (i, 0)
    const = lambda i, p: (0, 0)
    grid_spec = pltpu.PrefetchScalarGridSpec(
        num_scalar_prefetch=1,
        grid=(NRB,),
        in_specs=[pl.BlockSpec(memory_space=pl.ANY),
                  pl.BlockSpec((RB, ROUTER_W), row),
                  pl.BlockSpec((RB, D), row),
                  pl.BlockSpec((8, 6 * D), const),
                  pl.BlockSpec((1, D), const),
                  pl.BlockSpec((1, D), const),
                  pl.BlockSpec((8, 6 * D), const)],
        out_specs=[pl.BlockSpec((RB, D), const),
                   pl.BlockSpec((RB, D), lambda i, p: (jnp.maximum(i - 1, 0), 0)),
                   pl.BlockSpec((RB, D), row)],
        scratch_shapes=[pltpu.VMEM((2, RB * TOP_K * ROW_SUB, LANE), F32),
                        pltpu.SemaphoreType.DMA((2,))],
    )
    return pl.pallas_call(
        _combine_kernel,
        grid_spec=grid_spec,
        out_shape=[jax.ShapeDtypeStruct((N_CTX, D), F32), jax.ShapeDtypeStruct((SEQ, D), F32),
                   jax.ShapeDtypeStruct((T, D), BF16)],
        compiler_params=_cparams(("arbitrary",)),
        name="combine",
    )(pos, y_slabs, gates, x1, mod, ln_g, ln_b, next_mod)


def _routing_tables(ids):
    expert = ids[:, :TOP_K].reshape(-1)
    n = expert.shape[0]
    onehot = (expert[:, None] == jnp.arange(N_EXPERTS, dtype=jnp.int32)[None, :]).astype(jnp.int32)
    csum = jnp.cumsum(onehot, axis=0)
    counts = csum[-1]
    rank = jnp.sum(onehot * csum, axis=1) - 1
    nb = (counts + MOE_BLOCK - 1) // MOE_BLOCK
    bstart = jnp.cumsum(nb) - nb
    pos = (jnp.sum(onehot * (bstart * MOE_BLOCK)[None, :], axis=1) + rank).astype(jnp.int32)
    tok = jnp.arange(n, dtype=jnp.int32) // TOP_K
    row_tok = jnp.zeros((N_ROWS,), jnp.int32).at[pos].set(tok)
    n_it = (nb + ITEM_BLKS - 1) // ITEM_BLKS
    it_end = jnp.cumsum(n_it)
    it_start = it_end - n_it
    idx = jnp.arange(N_ITEMS, dtype=jnp.int32)
    e_of = jnp.minimum(jnp.sum(it_end[None, :] <= idx[:, None], axis=1), N_EXPERTS - 1).astype(jnp.int32)
    k = idx - it_start[e_of]
    valid = idx < it_end[-1]
    item_exp = e_of
    item_blk0 = jnp.where(valid, bstart[e_of] + ITEM_BLKS * k, 0).astype(jnp.int32)
    item_nblk = jnp.where(valid, jnp.clip(nb[e_of] - ITEM_BLKS * k, 0, ITEM_BLKS), 0).astype(jnp.int32)
    item_cnt = jnp.where(valid, jnp.clip(counts[e_of] - ITEM_ROWS * k, 0, ITEM_ROWS), 0)
    item_cnt = ((item_cnt + GATHER_UNROLL - 1) // GATHER_UNROLL * GATHER_UNROLL).astype(jnp.int32)
    n_used = jnp.sum(nb).astype(jnp.int32)[None]
    return item_exp, item_blk0, item_nblk, item_cnt, n_used, row_tok, pos


def _rope_tables():
    rows = SEQ // GRID_W
    row = jnp.repeat(jnp.arange(rows, dtype=F32), GRID_W)
    col = jnp.tile(jnp.arange(GRID_W, dtype=F32), rows)
    half = QK_ROPE // 2
    inv = ROPE_THETA ** (-jnp.arange(0, half, 2, dtype=F32) / half)
    ar = row[:, None] * inv
    ac = col[:, None] * inv
    ang = jnp.concatenate([ar, ar, ac, ac], -1)
    cos = jnp.concatenate([jnp.ones((N_CTX, QK_ROPE), F32), jnp.cos(ang)], 0)
    sin = jnp.concatenate([jnp.zeros((N_CTX, QK_ROPE), F32), jnp.sin(ang)], 0)
    one = jnp.ones((T, QK_NOPE), F32)
    zero_n = jnp.zeros((T, QK_NOPE), F32)
    zero_p = jnp.zeros((T, HEAD_PAD - QK_NOPE - QK_ROPE), F32)
    tq_c = jnp.concatenate([one, cos, zero_p], -1)
    tq_s = jnp.concatenate([zero_n, sin, zero_p], -1)
    tk = jnp.concatenate([cos, sin, jnp.zeros((T, LANE - 2 * QK_ROPE), F32)], -1)
    return tq_c, tq_s, tk


_ROT_SRC = np.concatenate([np.arange(8, 16), np.arange(0, 8), np.arange(24, 32), np.arange(16, 24)])
_ROT_SIGN = np.concatenate([-np.ones(8), np.ones(8), -np.ones(8), np.ones(8)]).astype(np.float32)


def _rot_cols(w):
    return w[..., _ROT_SRC] * _ROT_SIGN


def _layer_weights(l, w_in_t, b_in, d_w_uq, d_w_uk, d_w_uv, r_w_grp, r_b_grp, r_w_exp, r_b_exp, a_b_s, b_w_dw):
    bi = b_in[l]
    p = {}
    w_kr, b_kr = w_in_t[l, COL_KR:COL_G, :].T, bi[COL_KR:COL_G]
    padw = LANE - 2 * QK_ROPE
    p["wkr"] = jnp.concatenate([w_kr, _rot_cols(w_kr), jnp.zeros((D, padw), F32)], -1).astype(BF16)
    p["bkr"] = jnp.concatenate([b_kr, _rot_cols(b_kr), jnp.zeros((padw,), F32)])[None]
    wq = d_w_uq[l].reshape(Q_LORA, HEADS, QK_NOPE + QK_ROPE)
    zpad = jnp.zeros((Q_LORA, HEADS, HEAD_PAD - QK_NOPE - QK_ROPE), F32)
    p["wq"] = jnp.concatenate([wq, zpad], -1).reshape(Q_LORA, HEADS * HEAD_PAD).astype(BF16)
    p["wqr"] = jnp.concatenate([jnp.zeros((Q_LORA, HEADS, QK_NOPE), F32), _rot_cols(wq[..., QK_NOPE:]), zpad],
                               -1).reshape(Q_LORA, HEADS * HEAD_PAD).astype(BF16)
    wk = d_w_uk[l].reshape(KV_LORA, HEADS, QK_NOPE)
    p["wk"] = jnp.concatenate([wk, jnp.zeros((KV_LORA, HEADS, HEAD_PAD - QK_NOPE), F32)],
                              -1).reshape(KV_LORA, HEADS * HEAD_PAD).astype(BF16)
    p["wvt"] = d_w_uv[l].T.astype(BF16)
    e = np.zeros((LANE, HEADS, HEAD_PAD), np.float32)
    for j in range(QK_ROPE):
        e[j, :, QK_NOPE + j] = 1.0
        e[QK_ROPE + j, :, QK_NOPE + j] = 1.0
    p["e"] = jnp.asarray(e.reshape(LANE, HEADS * HEAD_PAD), dtype=BF16)
    p["bg"] = bi[None, COL_G:]
    padr = ROUTER_W - N_GROUPS - N_EXPERTS
    p["wr"] = _hilo(jnp.concatenate([r_w_grp[l], r_w_exp[l], jnp.zeros((D, padr), F32)], -1))
    p["br"] = jnp.concatenate([r_b_grp[l], r_b_exp[l], jnp.zeros((padr,), F32)])[None]
    p["bs"] = jnp.broadcast_to(a_b_s[l][:, :, None], (A_GROUPS, CHUNK, CHUNK))
    p["wdw"] = jnp.concatenate([b_w_dw[l], jnp.zeros((1, BW), F32)], 0)
    return p


def kernel(x, c, ctx, c_ctx, w_ada, b_ada, w_in, b_in, a_ln_g, a_ln_b, a_w_s, a_b_s, b_w_dw, b_b_dw, b_ln_g,
           b_ln_b, d_q_g, d_w_uq, d_kv_g, d_w_uk, d_w_uv, w_branch, w_o, ln1_g, ln1_b, r_w_grp, r_b_grp,
           r_w_exp, r_b_exp, e_w1, e_w3, e_w2, ln2_g, ln2_b):
    assert x.shape == (1, SEQ, D) and ctx.shape == (1, N_CTX, D)
    x_ctx, x_lat = ctx[0], x[0]
    w_in_t = jnp.transpose(w_in, (0, 2, 1))
    c_rep = jnp.broadcast_to(jnp.concatenate([c, c_ctx[None]], axis=0)[:, :, None], (N_COND, D, LANE))
    mods = ada_mod(c_rep, w_ada, b_ada)
    tq_c, tq_s, tk = _rope_tables()
    fc = _fourier_consts()
    b_in3 = b_in.reshape(DEPTH, 1, -1)
    w_br, w_ob = w_branch.astype(BF16), w_o.astype(BF16)

    h = modln(x_ctx, x_lat, mods[0])
    for l in range(DEPTH):
        p = _layer_weights(l, w_in_t, b_in, d_w_uq, d_w_uk, d_w_uv, r_w_grp, r_b_grp, r_w_exp, r_b_exp, a_b_s,
                           b_w_dw)
        mod = mods[l]
        br_a = mix_a(l, h, w_in_t, b_in3, a_ln_g[l][None], a_ln_b[l][None], a_w_s[l].astype(BF16), p["bs"])
        br_b = conv_ln(glu(l, h, w_in_t, b_in3), p["wdw"], b_b_dw[l][None], b_ln_g[l][None], b_ln_b[l][None])
        br_c = fourier_mix(l, h, w_in_t, b_in3, fc)
        q, k, vt = mla_proj(l, h, w_in_t, b_in3, p["wkr"], p["bkr"], d_q_g[l][None], d_kv_g[l][None], p["wq"],
                            p["wqr"], p["wk"], p["wvt"], p["e"], tq_c, tq_s, tk)
        br_dt = attention(q, k, vt)
        y = merge(l, h, br_a, br_b, br_c, br_dt, w_in_t, p["bg"], w_br)
        x1, h2, ids, gates = post_mix(l, y, x_ctx, x_lat, mod, w_ob, ln1_g[l][None], ln1_b[l][None], p["wr"],
                                      p["br"])
        item_exp, item_blk0, item_nblk, item_cnt, n_used, row_tok, pos = _routing_tables(ids)
        ys = experts(l, item_exp, item_blk0, item_nblk, item_cnt, n_used, row_tok, h2, e_w1, e_w3, e_w2)
        x_ctx, x_lat, h = combine(pos, ys, gates, x1, mod, ln2_g[l][None], ln2_b[l][None],
                                  mods[min(l + 1, DEPTH - 1)])
    return x_lat[None]
```

```python
import functools
import math

import numpy as np
import jax
import jax.numpy as jnp
from jax import lax
from jax.experimental import pallas as pl
from jax.experimental.pallas import tpu as pltpu

F32 = jnp.float32
BF16 = jnp.bfloat16

D = 2048
SEQ = 8192
N_CTX = 256
T = N_CTX + SEQ
DEPTH = 2
GRID_W = 64
BW = 512
CHUNK = 128
A_GROUPS = 4
CONV_W = 31
F_GROUPS = 4
QK_NOPE = 64
QK_ROPE = 32
V_HEAD = 64
HEADS = 8
Q_LORA = 512
KV_LORA = 256
ROPE_THETA = 10000.0
MLA_SCALE = (QK_NOPE + QK_ROPE) ** -0.5
N_GROUPS = 4
EXP_PER_GROUP = 8
N_EXPERTS = 32
TOP_K = 2
D_EXPERT = 1024
MOE_BLOCK = 256
COL_A = 0
COL_B = 1024
COL_C = 2048
COL_D = 2560
COL_KV = 3072
COL_KR = 3328
COL_G = 3360
ALPHA = (2 * DEPTH) ** 0.25
LN_EPS = 1e-6

LANE = 128
HEAD_PAD = 128
RB = 256
NRB = T // RB
FFT_N1 = 64
FFT_N2 = 128
ROW_SUB = D // LANE
VMEM_LIMIT = 56 * 1024 * 1024


def _cparams(sem):
    return pltpu.CompilerParams(dimension_semantics=sem, vmem_limit_bytes=VMEM_LIMIT)


def _dot(a, b):
    return jnp.dot(a, b, preferred_element_type=F32)


def _split(x):
    hi = lax.bitcast_convert_type(lax.bitcast_convert_type(x, jnp.int32) & jnp.int32(-65536), F32)
    return hi.astype(BF16), (x - hi).astype(BF16)


def _hilo(x):
    return jnp.stack(_split(jnp.asarray(x, F32)))


def _dot3(a, t_ref):
    a_hi, a_lo = _split(a)
    return _dot(a_hi, t_ref[0]) + (_dot(a_hi, t_ref[1]) + _dot(a_lo, t_ref[0]))


def _dot3_t(t_ref, b):
    b_hi, b_lo = _split(b)
    return _dot(t_ref[0], b_hi) + (_dot(t_ref[1], b_hi) + _dot(t_ref[0], b_lo))


def _ln0(x):
    mu = jnp.mean(x, axis=-1, keepdims=True)
    xc = x - mu
    var = jnp.mean(xc * xc, axis=-1, keepdims=True)
    return xc * lax.rsqrt(var + LN_EPS)


def _mod_row(mod_ref, q):
    lat = mod_ref[0:1, q * D:(q + 1) * D]
    ctx = mod_ref[1:2, q * D:(q + 1) * D]
    return jnp.where(pl.program_id(0) == 0, ctx, lat)


ADA_TN = 1536
ADA_TK = 256
N_COND = 2


def _ada_kernel(c_ref, w_ref, b_ref, o_ref):
    accs = [b_ref[0] for _ in range(N_COND)]
    for k0 in range(0, D, ADA_TK):
        w = w_ref[0, k0:k0 + ADA_TK, :]
        for r in range(N_COND):
            s = c_ref[r, k0:k0 + ADA_TK, :]
            s = s * jax.nn.sigmoid(s)
            s = jnp.concatenate([s] * (ADA_TN // LANE), axis=-1)
            accs[r] = accs[r] + jnp.sum(w * s, axis=0, keepdims=True)
    o_ref[0] = jnp.concatenate(accs + [jnp.zeros((8 - N_COND, ADA_TN), F32)], axis=0)


def ada_mod(c_rep, w_ada, b_ada):
    return pl.pallas_call(
        _ada_kernel,
        grid=(DEPTH, 6 * D // ADA_TN),
        in_specs=[pl.BlockSpec((N_COND, D, LANE), lambda l, j: (0, 0, 0)),
                  pl.BlockSpec((1, D, ADA_TN), lambda l, j: (l, 0, j)),
                  pl.BlockSpec((1, 1, ADA_TN), lambda l, j: (l, 0, j))],
        out_specs=pl.BlockSpec((1, 8, ADA_TN), lambda l, j: (l, 0, j)),
        out_shape=jax.ShapeDtypeStruct((DEPTH, 8, 6 * D), F32),
        compiler_params=_cparams(("arbitrary", "arbitrary")),
        name="ada_mod",
    )(c_rep, w_ada, b_ada.reshape(DEPTH, 1, 6 * D))


def _stream_specs():
    return [pl.BlockSpec((RB, D), lambda i, *_: (0, 0)),
            pl.BlockSpec((RB, D), lambda i, *_: (jnp.maximum(i - 1, 0), 0))]


def _stream_block(ctx_ref, lat_ref):
    return jnp.where(pl.program_id(0) == 0, ctx_ref[...], lat_ref[...])


def _modln_kernel(xc_ref, xl_ref, mod_ref, h_ref):
    h = _ln0(_stream_block(xc_ref, xl_ref)) * (1.0 + _mod_row(mod_ref, 1)) + _mod_row(mod_ref, 0)
    h_ref[...] = h.astype(BF16)


def modln(x_ctx, x_lat, mod):
    return pl.pallas_call(
        _modln_kernel,
        grid=(NRB,),
        in_specs=_stream_specs() + [pl.BlockSpec((8, 6 * D), lambda i: (0, 0))],
        out_specs=pl.BlockSpec((RB, D), lambda i: (i, 0)),
        out_shape=jax.ShapeDtypeStruct((T, D), BF16),
        compiler_params=_cparams(("arbitrary",)),
        name="modln",
    )(x_ctx, x_lat, mod)


def _w_in_spec(l, width, col):
    return pl.BlockSpec((1, width, D), lambda i: (l, col, 0))


def _b_in_spec(l, width, col):
    return pl.BlockSpec((1, 1, width), lambda i: (l, 0, col))


CAST_CHUNK = 256


def _cast_transposed(w_ref, wbf_ref):
    width = w_ref.shape[0]
    for c0 in range(0, width, CAST_CHUNK):
        wbf_ref[:, c0:c0 + CAST_CHUNK] = w_ref[c0:c0 + CAST_CHUNK, :].T.astype(BF16)


def _cast_once(w_ref, wbf_ref):
    @pl.when(pl.program_id(0) == 0)
    def _():
        _cast_transposed(w_ref.at[0], wbf_ref)


def _mix_a_kernel(h_ref, w_ref, b_ref, g_ref, bb_ref, ws_ref, bs_ref, o_ref, wbf_ref):
    _cast_once(w_ref, wbf_ref)
    uv = jax.nn.gelu(_dot(h_ref[...], wbf_ref[...]) + b_ref[0])
    u = uv[:, :BW]
    v = _ln0(uv[:, BW:]) * g_ref[...] + bb_ref[...]
    v = v.astype(BF16)
    gw = BW // A_GROUPS
    for ch in range(RB // CHUNK):
        rows = slice(ch * CHUNK, (ch + 1) * CHUNK)
        parts = [_dot(ws_ref[g], v[rows, g * gw:(g + 1) * gw]) + bs_ref[g] for g in range(A_GROUPS)]
        mixed = jnp.concatenate(parts, axis=-1)
        o_ref[rows, :] = (u[rows, :] * mixed).astype(BF16)


def mix_a(l, h, w_in, b_in, ln_g, ln_b, ws, bs):
    return pl.pallas_call(
        _mix_a_kernel,
        grid=(NRB,),
        in_specs=[pl.BlockSpec((RB, D), lambda i: (i, 0)),
                  _w_in_spec(l, 2 * BW, COL_A // (2 * BW)),
                  _b_in_spec(l, 2 * BW, COL_A // (2 * BW)),
                  pl.BlockSpec((1, BW), lambda i: (0, 0)),
                  pl.BlockSpec((1, BW), lambda i: (0, 0)),
                  pl.BlockSpec((A_GROUPS, CHUNK, CHUNK), lambda i: (0, 0, 0)),
                  pl.BlockSpec((A_GROUPS, CHUNK, CHUNK), lambda i: (0, 0, 0))],
        out_specs=pl.BlockSpec((RB, BW), lambda i: (i, 0)),
        out_shape=jax.ShapeDtypeStruct((T, BW), BF16),
        scratch_shapes=[pltpu.VMEM((D, 2 * BW), BF16)],
        compiler_params=_cparams(("arbitrary",)),
        name="mix_a",
    )(h, w_in, b_in, ln_g, ln_b, ws, bs)


def _glu_kernel(h_ref, w_ref, b_ref, o_ref, wbf_ref):
    _cast_once(w_ref, wbf_ref)
    ab = _dot(h_ref[...], wbf_ref[...]) + b_ref[0]
    o_ref[...] = ab[:, :BW] * jax.nn.sigmoid(ab[:, BW:])


def glu(l, h, w_in, b_in):
    return pl.pallas_call(
        _glu_kernel,
        grid=(NRB,),
        in_specs=[pl.BlockSpec((RB, D), lambda i: (i, 0)),
                  _w_in_spec(l, 2 * BW, COL_B // (2 * BW)),
                  _b_in_spec(l, 2 * BW, COL_B // (2 * BW))],
        out_specs=pl.BlockSpec((RB, BW), lambda i: (i, 0)),
        out_shape=jax.ShapeDtypeStruct((T, BW), F32),
        scratch_shapes=[pltpu.VMEM((D, 2 * BW), BF16)],
        compiler_params=_cparams(("arbitrary",)),
        name="glu",
    )(h, w_in, b_in)


CONV_HALO = 16


def _conv_kernel(prev_ref, cur_ref, next_ref, w_ref, b_ref, g_ref, bb_ref, o_ref, ext_ref):
    i = pl.program_id(0)
    has_prev = i >= 2
    has_next = jnp.logical_and(i >= 1, i < NRB - 1)
    ext_ref[0:CONV_HALO, :] = jnp.where(has_prev, prev_ref[RB - CONV_HALO:RB, :], 0.0)
    ext_ref[CONV_HALO:CONV_HALO + RB, :] = cur_ref[...]
    ext_ref[CONV_HALO + RB:2 * CONV_HALO + RB, :] = jnp.where(has_next, next_ref[0:CONV_HALO, :], 0.0)
    off = CONV_HALO - CONV_W // 2
    acc = jnp.zeros((RB, BW), F32)
    for k in range(CONV_W):
        acc = acc + ext_ref[off + k:off + k + RB, :] * w_ref[k:k + 1, :]
    y = _ln0(acc + b_ref[...]) * g_ref[...] + bb_ref[...]
    o_ref[...] = (y * jax.nn.sigmoid(y)).astype(BF16)


def conv_ln(y, w_dw, b_dw, ln_g, ln_b):
    return pl.pallas_call(
        _conv_kernel,
        grid=(NRB,),
        in_specs=[pl.BlockSpec((RB, BW), lambda i: (jnp.maximum(i - 1, 0), 0)),
                  pl.BlockSpec((RB, BW), lambda i: (i, 0)),
                  pl.BlockSpec((RB, BW), lambda i: (jnp.minimum(i + 1, NRB - 1), 0)),
                  pl.BlockSpec((CONV_W + 1, BW), lambda i: (0, 0)),
                  pl.BlockSpec((1, BW), lambda i: (0, 0)),
                  pl.BlockSpec((1, BW), lambda i: (0, 0)),
                  pl.BlockSpec((1, BW), lambda i: (0, 0))],
        out_specs=pl.BlockSpec((RB, BW), lambda i: (i, 0)),
        out_shape=jax.ShapeDtypeStruct((T, BW), BF16),
        scratch_shapes=[pltpu.VMEM((RB + 2 * CONV_HALO, BW), F32)],
        compiler_params=_cparams(("arbitrary",)),
        name="conv_ln",
    )(y, y, y, w_dw, b_dw, ln_g, ln_b)


def _dft_tables(n):
    k = np.arange(n, dtype=np.int64)
    ang = 2.0 * np.pi * ((k[:, None] * k[None, :]) % n).astype(np.float64) / n
    return np.cos(ang), np.sin(ang)


def _fproj_kernel(h_ref, w_ref, b_ref, cs_ref, ar_ref, ai_ref, wbf_ref):
    _cast_once(w_ref, wbf_ref)
    z = _dot(h_ref[...], wbf_ref[...]) + b_ref[0]
    a = _dot3(z, cs_ref)
    ar_ref[...] = a[:, :BW]
    ai_ref[...] = a[:, BW:]


def fourier_proj(l, h, w_in, b_in, cs):
    return pl.pallas_call(
        _fproj_kernel,
        grid=(NRB,),
        in_specs=[pl.BlockSpec((RB, D), lambda i: (i, 0)),
                  _w_in_spec(l, BW, COL_C // BW),
                  _b_in_spec(l, BW, COL_C // BW),
                  pl.BlockSpec((2, BW, 2 * BW), lambda i: (0, 0, 0))],
        out_specs=[pl.BlockSpec((RB, BW), lambda i: (i, 0)),
                   pl.BlockSpec((RB, BW), lambda i: (i, 0))],
        out_shape=[jax.ShapeDtypeStruct((T, BW), F32), jax.ShapeDtypeStruct((T, BW), F32)],
        scratch_shapes=[pltpu.VMEM((D, BW), BF16)],
        compiler_params=_cparams(("arbitrary",)),
        name="fourier_proj",
    )(h, w_in, b_in, cs)


FFT_TILE_N2 = 4


def _fft1_kernel(ar_ref, ai_ref, m_ref, ct_ref, st_ref, br_ref, bi_ref):
    x = jnp.concatenate([ar_ref[...], ai_ref[...]], axis=0)
    b = _dot3_t(m_ref, x)
    b_re, b_im = b[:FFT_N1], b[FFT_N1:]
    reps = BW // LANE
    for j in range(FFT_TILE_N2):
        ct = jnp.concatenate([ct_ref[j]] * reps, axis=-1)
        st = jnp.concatenate([st_ref[j]] * reps, axis=-1)
        lanes = slice(j * BW, (j + 1) * BW)
        br_ref[:, lanes] = b_re[:, lanes] * ct + b_im[:, lanes] * st
        bi_ref[:, lanes] = b_im[:, lanes] * ct - b_re[:, lanes] * st


def fft_stage1(ar2d, ai2d, m1, ct, st):
    tn = FFT_TILE_N2 * BW
    return pl.pallas_call(
        _fft1_kernel,
        grid=(FFT_N2 // FFT_TILE_N2,),
        in_specs=[pl.BlockSpec((FFT_N1, tn), lambda j: (0, j)),
                  pl.BlockSpec((FFT_N1, tn), lambda j: (0, j)),
                  pl.BlockSpec((2, 2 * FFT_N1, 2 * FFT_N1), lambda j: (0, 0, 0)),
                  pl.BlockSpec((FFT_TILE_N2, FFT_N1, LANE), lambda j: (j, 0, 0)),
                  pl.BlockSpec((FFT_TILE_N2, FFT_N1, LANE), lambda j: (j, 0, 0))],
        out_specs=[pl.BlockSpec((FFT_N1, tn), lambda j: (0, j)),
                   pl.BlockSpec((FFT_N1, tn), lambda j: (0, j))],
        out_shape=[jax.ShapeDtypeStruct((FFT_N1, FFT_N2 * BW), F32)] * 2,
        compiler_params=_cparams(("arbitrary",)),
        name="fft_stage1",
    )(ar2d, ai2d, m1, ct, st)


FFT_TILE_K1 = 8


def _fft2_kernel(br_ref, bi_ref, c_ref, s_ref, p_ref, o_ref, *, scale):
    ys = [(_dot3_t(c_ref, br_ref[a]) + _dot3_t(s_ref, bi_ref[a])) * scale for a in range(FFT_TILE_K1)]
    y = jnp.concatenate(ys, axis=0).astype(BF16)
    y = _dot(p_ref[...], y)
    o_ref[...] = y.reshape(FFT_N2, FFT_TILE_K1, BW)


def fft_stage2(br3d, bi3d, c128, s128, perm, scale):
    return pl.pallas_call(
        functools.partial(_fft2_kernel, scale=scale),
        grid=(FFT_N1 // FFT_TILE_K1,),
        in_specs=[pl.BlockSpec((FFT_TILE_K1, FFT_N2, BW), lambda a: (a, 0, 0)),
                  pl.BlockSpec((FFT_TILE_K1, FFT_N2, BW), lambda a: (a, 0, 0)),
                  pl.BlockSpec((2, FFT_N2, FFT_N2), lambda a: (0, 0, 0)),
                  pl.BlockSpec((2, FFT_N2, FFT_N2), lambda a: (0, 0, 0)),
                  pl.BlockSpec((FFT_TILE_K1 * FFT_N2, FFT_TILE_K1 * FFT_N2), lambda a: (0, 0))],
        out_specs=pl.BlockSpec((FFT_N2, FFT_TILE_K1, BW), lambda a: (0, a, 0)),
        out_shape=jax.ShapeDtypeStruct((FFT_N2, FFT_N1, BW), F32),
        compiler_params=_cparams(("arbitrary",)),
        name="fft_stage2",
    )(br3d, bi3d, c128, s128, perm)


def _dft_ctx_kernel(ar_ref, ai_ref, c_ref, s_ref, o_ref, *, scale):
    o_ref[...] = (_dot3_t(c_ref, ar_ref[...]) + _dot3_t(s_ref, ai_ref[...])) * scale


def dft_ctx(ar, ai, c, s, scale):
    return pl.pallas_call(
        functools.partial(_dft_ctx_kernel, scale=scale),
        grid=(1,),
        in_specs=[pl.BlockSpec((N_CTX, BW), lambda i: (0, 0)),
                  pl.BlockSpec((N_CTX, BW), lambda i: (0, 0)),
                  pl.BlockSpec((2, N_CTX, N_CTX), lambda i: (0, 0, 0)),
                  pl.BlockSpec((2, N_CTX, N_CTX), lambda i: (0, 0, 0))],
        out_specs=pl.BlockSpec((N_CTX, BW), lambda i: (0, 0)),
        out_shape=jax.ShapeDtypeStruct((N_CTX, BW), F32),
        compiler_params=_cparams(("arbitrary",)),
        name="dft_ctx",
    )(ar, ai, c, s)


def _fourier_consts():
    c64, s64 = _dft_tables(FFT_N1)
    m1 = np.block([[c64, s64], [-s64, c64]]).astype(np.float32)
    k1 = np.arange(FFT_N1, dtype=np.float64)
    n2 = np.arange(FFT_N2, dtype=np.float64)
    phi = 2.0 * np.pi * n2[:, None] * k1[None, :] / SEQ
    ct = np.repeat(np.cos(phi)[:, :, None], LANE, axis=2).astype(np.float32)
    st = np.repeat(np.sin(phi)[:, :, None], LANE, axis=2).astype(np.float32)
    c128, s128 = _dft_tables(FFT_N2)
    gw = BW // F_GROUPS
    cg, sg = _dft_tables(gw)
    cs = np.zeros((BW, 2 * BW), np.float32)
    for g in range(F_GROUPS):
        sl = slice(g * gw, (g + 1) * gw)
        cs[sl, sl] = cg
        cs[sl, BW + g * gw:BW + (g + 1) * gw] = -sg
    rows = np.arange(FFT_TILE_K1 * FFT_N2)
    perm = np.zeros((rows.size, rows.size), np.float32)
    perm[rows, (rows % FFT_TILE_K1) * FFT_N2 + rows // FFT_TILE_K1] = 1.0
    cc, sc = _dft_tables(N_CTX)
    return dict(m1=_hilo(m1), ct=jnp.asarray(ct), st=jnp.asarray(st), c128=_hilo(c128), s128=_hilo(s128),
                cs=_hilo(cs), perm=jnp.asarray(perm, dtype=BF16), cc=_hilo(cc), sc=_hilo(sc))


def fourier_mix(l, h, w_in, b_in, fc):
    gw = BW // F_GROUPS
    ar, ai = fourier_proj(l, h, w_in, b_in, fc["cs"])
    y_ctx = dft_ctx(ar, ai, fc["cc"], fc["sc"], 1.0 / math.sqrt(N_CTX * gw))
    ar2 = ar[N_CTX:].reshape(FFT_N1, FFT_N2 * BW)
    ai2 = ai[N_CTX:].reshape(FFT_N1, FFT_N2 * BW)
    br, bi = fft_stage1(ar2, ai2, fc["m1"], fc["ct"], fc["st"])
    y = fft_stage2(br.reshape(FFT_N1, FFT_N2, BW), bi.reshape(FFT_N1, FFT_N2, BW),
                   fc["c128"], fc["s128"], fc["perm"], 1.0 / math.sqrt(SEQ * gw))
    return jnp.concatenate([y_ctx, y.reshape(SEQ, BW)], axis=0)


VT_ROWS = 80
LOG2E = math.log2(math.e)


def _mla_proj_kernel(h_ref, wcq_ref, bcq_ref, wkv_ref, bkv_ref, wkr_ref, bkr_ref, qg_ref, kvg_ref, wq_ref, wqr_ref,
                     wk_ref, wvt_ref, e_ref, tq_c_ref, tq_s_ref, tk_ref, q_ref, k_ref, vt_ref, wcq_bf, wkv_bf):
    _cast_once(wcq_ref, wcq_bf)
    _cast_once(wkv_ref, wkv_bf)
    h = h_ref[...]
    cq = _dot(h, wcq_bf[...]) + bcq_ref[0]
    cq = cq * lax.rsqrt(jnp.mean(cq * cq, axis=-1, keepdims=True) + LN_EPS) * qg_ref[...]
    cq = cq.astype(BF16)
    cos_t = jnp.concatenate([tq_c_ref[...]] * HEADS, axis=-1)
    sin_t = jnp.concatenate([tq_s_ref[...]] * HEADS, axis=-1)
    q = (_dot(cq, wq_ref[...]) * cos_t + _dot(cq, wqr_ref[...]) * sin_t) * (MLA_SCALE * LOG2E)
    q_ref[...] = q.astype(BF16)
    ckv = _dot(h, wkv_bf[...]) + bkv_ref[0]
    ckv = ckv * lax.rsqrt(jnp.mean(ckv * ckv, axis=-1, keepdims=True) + LN_EPS) * kvg_ref[...]
    ckv = ckv.astype(BF16)
    kr = ((_dot(h, wkr_ref[...]) + bkr_ref[...]) * tk_ref[...]).astype(BF16)
    k_ref[...] = (_dot(ckv, wk_ref[...]) + _dot(kr, e_ref[...])).astype(BF16)
    vt = lax.dot_general(wvt_ref[...], ckv, (((1,), (1,)), ((), ())), preferred_element_type=F32)
    ones_rows = (lax.broadcasted_iota(jnp.int32, (VT_ROWS - V_HEAD, RB), 0) == 0).astype(BF16)
    for hd in range(HEADS):
        vt_ref[0, hd * VT_ROWS:hd * VT_ROWS + V_HEAD, :] = vt[hd * V_HEAD:(hd + 1) * V_HEAD, :].astype(BF16)
        vt_ref[0, hd * VT_ROWS + V_HEAD:(hd + 1) * VT_ROWS, :] = ones_rows


def mla_proj(l, h, w_in, b_in, wkr, bkr, qg, kvg, wq, wqr, wk, wvt, e, tq_c, tq_s, tk):
    const = lambda i: (0, 0)
    row = lambda i: (i, 0)
    return pl.pallas_call(
        _mla_proj_kernel,
        grid=(NRB,),
        in_specs=[pl.BlockSpec((RB, D), row),
                  _w_in_spec(l, Q_LORA, COL_D // Q_LORA),
                  _b_in_spec(l, Q_LORA, COL_D // Q_LORA),
                  _w_in_spec(l, KV_LORA, COL_KV // KV_LORA),
                  _b_in_spec(l, KV_LORA, COL_KV // KV_LORA),
                  pl.BlockSpec((D, LANE), const),
                  pl.BlockSpec((1, LANE), const),
                  pl.BlockSpec((1, Q_LORA), const),
                  pl.BlockSpec((1, KV_LORA), const),
                  pl.BlockSpec((Q_LORA, HEADS * HEAD_PAD), const),
                  pl.BlockSpec((Q_LORA, HEADS * HEAD_PAD), const),
                  pl.BlockSpec((KV_LORA, HEADS * HEAD_PAD), const),
                  pl.BlockSpec((HEADS * V_HEAD, KV_LORA), const),
                  pl.BlockSpec((LANE, HEADS * HEAD_PAD), const),
                  pl.BlockSpec((RB, HEAD_PAD), row),
                  pl.BlockSpec((RB, HEAD_PAD), row),
                  pl.BlockSpec((RB, LANE), row)],
        out_specs=[pl.BlockSpec((RB, HEADS * HEAD_PAD), row),
                   pl.BlockSpec((RB, HEADS * HEAD_PAD), row),
                   pl.BlockSpec((1, HEADS * VT_ROWS, RB), lambda i: (i, 0, 0))],
        out_shape=[jax.ShapeDtypeStruct((T, HEADS * HEAD_PAD), BF16),
                   jax.ShapeDtypeStruct((T, HEADS * HEAD_PAD), BF16),
                   jax.ShapeDtypeStruct((NRB, HEADS * VT_ROWS, RB), BF16)],
        scratch_shapes=[pltpu.VMEM((D, Q_LORA), BF16), pltpu.VMEM((D, KV_LORA), BF16)],
        compiler_params=_cparams(("arbitrary",)),
        name="mla_proj",
    )(h, w_in, b_in, w_in, b_in, wkr, bkr, qg, kvg, wq, wqr, wk, wvt, e, tq_c, tq_s, tk)


KV_TILE = RB
KV_STAGE = 1024
N_STAGE = SEQ // KV_STAGE
TILES_PER_STAGE = KV_STAGE // KV_TILE
CTX_TILES = N_CTX // KV_TILE


HEAD_GROUP = 2


def _attn_scores(k_ref, q, g, start, n):
    kt = k_ref[pl.ds(start, n), g * HEAD_PAD:(g + 1) * HEAD_PAD]
    return lax.dot_general(kt, q, (((1,), (1,)), ((), ())), preferred_element_type=F32)


def _attn_out(acc):
    return acc[:V_HEAD, :] / acc[V_HEAD:V_HEAD + 1, :]


def _attn_kernel(q_ref, k_ref, vt_ref, o_ref, sa_ref, sb_ref):
    heads = range(HEAD_GROUP)
    qs = [q_ref[:, g * HEAD_PAD:(g + 1) * HEAD_PAD] for g in heads]

    def scores(g, stage):
        return _attn_scores(k_ref, qs[g], g, pl.multiple_of(N_CTX + stage * KV_STAGE, KV_TILE), KV_STAGE)

    def values(g, tile):
        return vt_ref[tile, g * VT_ROWS:(g + 1) * VT_ROWS, :]

    def softmax_pv(g, s_ref, stage, m, acc):
        m_new = jnp.maximum(m, jnp.max(s_ref[g], axis=0, keepdims=True))
        p = jnp.exp2(s_ref[g] - m_new).astype(BF16)
        acc = jnp.exp2(m - m_new) * acc
        for u in range(TILES_PER_STAGE):
            acc = acc + _dot(values(g, CTX_TILES + stage * TILES_PER_STAGE + u), p[u * KV_TILE:(u + 1) * KV_TILE, :])
        return m_new, acc

    ms, accs = [], []
    for g in heads:
        s0 = _attn_scores(k_ref, qs[g], g, 0, N_CTX)
        ms.append(jnp.max(s0, axis=0, keepdims=True))
        accs.append(_dot(values(g, 0), jnp.exp2(s0 - ms[g]).astype(BF16)))
        sa_ref[g] = scores(g, 0)

    def body(j, carry):
        ms, accs = list(carry[0]), list(carry[1])
        for g in heads:
            sb_ref[g] = scores(g, 2 * j + 1)
        for g in heads:
            ms[g], accs[g] = softmax_pv(g, sa_ref, 2 * j, ms[g], accs[g])
        for g in heads:
            sa_ref[g] = scores(g, 2 * j + 2)
        for g in heads:
            ms[g], accs[g] = softmax_pv(g, sb_ref, 2 * j + 1, ms[g], accs[g])
        return tuple(ms), tuple(accs)

    ms, accs = lax.fori_loop(0, N_STAGE // 2 - 1, body, (tuple(ms), tuple(accs)))
    ms, accs = list(ms), list(accs)
    for g in heads:
        sb_ref[g] = scores(g, N_STAGE - 1)
    for g in heads:
        ms[g], accs[g] = softmax_pv(g, sa_ref, N_STAGE - 2, ms[g], accs[g])
    for g in heads:
        _, acc = softmax_pv(g, sb_ref, N_STAGE - 1, ms[g], accs[g])
        o_ref[g * V_HEAD:(g + 1) * V_HEAD, :] = _attn_out(acc)


def _attn_ctx_kernel(q_ref, k_ref, vt_ref, o_ref):
    for g in range(HEAD_GROUP):
        s = _attn_scores(k_ref, q_ref[:, g * HEAD_PAD:(g + 1) * HEAD_PAD], g, 0, N_CTX)
        p = jnp.exp2(s - jnp.max(s, axis=0, keepdims=True)).astype(BF16)
        o_ref[g * V_HEAD:(g + 1) * V_HEAD, :] = _attn_out(_dot(vt_ref[0, g * VT_ROWS:(g + 1) * VT_ROWS, :], p))


def attention(q, k, vt):
    assert N_CTX == KV_TILE and N_STAGE % 2 == 0
    o_ctx = pl.pallas_call(
        _attn_ctx_kernel,
        grid=(HEADS // HEAD_GROUP,),
        in_specs=[pl.BlockSpec((N_CTX, HEAD_GROUP * HEAD_PAD), lambda h: (0, h)),
                  pl.BlockSpec((N_CTX, HEAD_GROUP * HEAD_PAD), lambda h: (0, h)),
                  pl.BlockSpec((1, HEAD_GROUP * VT_ROWS, KV_TILE), lambda h: (0, h, 0))],
        out_specs=pl.BlockSpec((HEAD_GROUP * V_HEAD, N_CTX), lambda h: (h, 0)),
        out_shape=jax.ShapeDtypeStruct((HEADS * V_HEAD, N_CTX), F32),
        compiler_params=_cparams(("arbitrary",)),
        name="attention_ctx",
    )(q, k, vt)
    o_lat = pl.pallas_call(
        _attn_kernel,
        grid=(HEADS // HEAD_GROUP, SEQ // RB),
        in_specs=[pl.BlockSpec((RB, HEAD_GROUP * HEAD_PAD), lambda h, i: (i + N_CTX // RB, h)),
                  pl.BlockSpec((T, HEAD_GROUP * HEAD_PAD), lambda h, i: (0, h)),
                  pl.BlockSpec((NRB, HEAD_GROUP * VT_ROWS, KV_TILE), lambda h, i: (0, h, 0))],
        out_specs=pl.BlockSpec((HEAD_GROUP * V_HEAD, RB), lambda h, i: (h, i)),
        out_shape=jax.ShapeDtypeStruct((HEADS * V_HEAD, SEQ), F32),
        scratch_shapes=[pltpu.VMEM((HEAD_GROUP, KV_STAGE, RB), F32),
                        pltpu.VMEM((HEAD_GROUP, KV_STAGE, RB), F32)],
        compiler_params=_cparams(("arbitrary", "arbitrary")),
        name="attention",
    )(q, k, vt)
    return jnp.concatenate([o_ctx, o_lat], axis=1)


MERGE_TN = 512


N_BRANCH = 4


MERGE_RB = 768


def _merge_kernel(h_ref, a_ref, b_ref, c_ref, dt_ref, *refs):
    wg_refs, bg_refs = refs[:N_BRANCH], refs[N_BRANCH:2 * N_BRANCH]
    wb_ref, y_ref, wg_bf = refs[2 * N_BRANCH:]

    @pl.when(pl.program_id(1) == 0)
    def _():
        for n in range(N_BRANCH):
            _cast_transposed(wg_refs[n], wg_bf.at[n])

    h = h_ref[...]
    branches = (a_ref[...], b_ref[...], c_ref[...].astype(BF16), dt_ref[...].T.astype(BF16))
    y = jnp.zeros((MERGE_RB, MERGE_TN), F32)
    for n, br in enumerate(branches):
        gate = jax.nn.sigmoid(_dot(h, wg_bf[n]) + bg_refs[n][...])
        y = y + gate * _dot(br, wb_ref[0, n])
    y_ref[...] = y.astype(BF16)


def merge(l, h, br_a, br_b, br_c, br_dt, w_in_t, bg, wb):
    row = lambda j, i: (i, 0)
    n_col = D // MERGE_TN
    gate_specs = [pl.BlockSpec((pl.Squeezed(), pl.Element(MERGE_TN), pl.Element(D)),
                               functools.partial(
                                   lambda j, i, n: (l, pl.multiple_of(COL_G + n * D + j * MERGE_TN, QK_ROPE), 0), n=n),
                               pipeline_mode=pl.Buffered(1))
                  for n in range(N_BRANCH)]
    bias_specs = [pl.BlockSpec((1, MERGE_TN), functools.partial(lambda j, i, n: (0, n * n_col + j), n=n))
                  for n in range(N_BRANCH)]
    return pl.pallas_call(
        _merge_kernel,
        grid=(n_col, T // MERGE_RB),
        in_specs=[pl.BlockSpec((MERGE_RB, D), row),
                  pl.BlockSpec((MERGE_RB, BW), row),
                  pl.BlockSpec((MERGE_RB, BW), row),
                  pl.BlockSpec((MERGE_RB, BW), row),
                  pl.BlockSpec((BW, MERGE_RB), lambda j, i: (0, i))]
                 + gate_specs + bias_specs
                 + [pl.BlockSpec((1, N_BRANCH, BW, MERGE_TN), lambda j, i: (l, 0, 0, j))],
        out_specs=pl.BlockSpec((MERGE_RB, MERGE_TN), lambda j, i: (i, j)),
        out_shape=jax.ShapeDtypeStruct((T, D), BF16),
        scratch_shapes=[pltpu.VMEM((N_BRANCH, D, MERGE_TN), BF16)],
        compiler_params=_cparams(("arbitrary", "arbitrary")),
        name="merge",
    )(h, br_a, br_b, br_c, br_dt, *([w_in_t] * N_BRANCH), *([bg] * N_BRANCH), wb)


ROUTER_W = 128


def _post_mix_kernel(y_ref, xc_ref, xl_ref, mod_ref, wo_ref, g_ref, b_ref, wr_ref, br_ref,
                     x1_ref, h2_ref, ids_ref, gates_ref):
    mix = _dot(y_ref[...], wo_ref[0])
    x1 = _ln0(ALPHA * _stream_block(xc_ref, xl_ref) + _mod_row(mod_ref, 2) * mix) * g_ref[...] + b_ref[...]
    x1_ref[...] = x1
    h2 = _ln0(x1) * (1.0 + _mod_row(mod_ref, 4)) + _mod_row(mod_ref, 3)
    _rows_to_slabs(h2, h2_ref)
    logits = _dot3(h2, wr_ref) + br_ref[...]
    lane = lax.broadcasted_iota(jnp.int32, (RB, ROUTER_W), 1).astype(F32)
    neg = jnp.float32(-jnp.inf)
    big = jnp.float32(ROUTER_W)
    gl = jnp.where(lane < N_GROUPS, logits, neg)
    gmax = jnp.max(gl, axis=-1, keepdims=True)
    g_idx = jnp.min(jnp.where(gl == gmax, lane, big), axis=-1, keepdims=True)
    g_p = 1.0 / jnp.sum(jnp.exp(gl - gmax), axis=-1, keepdims=True)
    e_lane = lane - N_GROUPS
    in_grp = jnp.logical_and(e_lane >= g_idx * EXP_PER_GROUP, e_lane < (g_idx + 1) * EXP_PER_GROUP)
    el = jnp.where(in_grp, logits, neg)
    v1 = jnp.max(el, axis=-1, keepdims=True)
    i1 = jnp.min(jnp.where(el == v1, lane, big), axis=-1, keepdims=True)
    el2 = jnp.where(lane == i1, neg, el)
    v2 = jnp.max(el2, axis=-1, keepdims=True)
    i2 = jnp.min(jnp.where(el2 == v2, lane, big), axis=-1, keepdims=True)
    e21 = jnp.exp(v2 - v1)
    p1 = 1.0 / (1.0 + e21)
    p2 = e21 * p1
    ids_ref[...] = jnp.where(lane == 0, i1 - N_GROUPS, jnp.where(lane == 1, i2 - N_GROUPS, 0.0)).astype(jnp.int32)
    gates_ref[...] = jnp.where(lane == 0, p1 * g_p, jnp.where(lane == 1, p2 * g_p, 0.0))


def post_mix(l, y, x_ctx, x_lat, mod, wo, ln_g, ln_b, wr, br):
    row = lambda i: (i, 0)
    const = lambda i: (0, 0)
    return pl.pallas_call(
        _post_mix_kernel,
        grid=(NRB,),
        in_specs=[pl.BlockSpec((RB, D), row)] + _stream_specs()
                 + [pl.BlockSpec((8, 6 * D), const),
                  pl.BlockSpec((1, D, D), lambda i: (l, 0, 0)),
                  pl.BlockSpec((1, D), const),
                  pl.BlockSpec((1, D), const),
                  pl.BlockSpec((2, D, ROUTER_W), lambda i: (0, 0, 0)),
                  pl.BlockSpec((1, ROUTER_W), const)],
        out_specs=[pl.BlockSpec((RB, D), row),
                   pl.BlockSpec((RB * ROW_SUB, LANE), row),
                   pl.BlockSpec((RB, ROUTER_W), row),
                   pl.BlockSpec((RB, ROUTER_W), row)],
        out_shape=[jax.ShapeDtypeStruct((T, D), F32),
                   jax.ShapeDtypeStruct((T * ROW_SUB, LANE), F32),
                   jax.ShapeDtypeStruct((T, ROUTER_W), jnp.int32),
                   jax.ShapeDtypeStruct((T, ROUTER_W), F32)],
        compiler_params=_cparams(("arbitrary",)),
        name="post_mix",
    )(y, x_ctx, x_lat, mod, wo, ln_g, ln_b, wr, br)


N_ROWS = -(-(T * TOP_K + N_EXPERTS * (MOE_BLOCK - 1)) // MOE_BLOCK) * MOE_BLOCK
N_BLK = N_ROWS // MOE_BLOCK


N_DMA_PRIORITIES = 2


GATHER_UNROLL = 8


def _row_gather(src_hbm, idx_ref, base, n, dst_ref, sem, first=0):
    def issue(t, _):
        for u in range(GATHER_UNROLL):
            r = first + t * GATHER_UNROLL + u
            src_row = pl.multiple_of(idx_ref[base + r] * ROW_SUB, ROW_SUB)
            dst_row = pl.multiple_of(r * ROW_SUB, ROW_SUB)
            pltpu.make_async_copy(src_hbm.at[pl.ds(src_row, ROW_SUB), :],
                                  dst_ref.at[pl.ds(dst_row, ROW_SUB), :], sem).start(priority=u % N_DMA_PRIORITIES)
        return 0
    lax.fori_loop(0, n // GATHER_UNROLL, issue, 0)


def _row_gather_wait(src_hbm, n, dst_ref, sem):
    pltpu.make_async_copy(src_hbm.at[pl.ds(0, n * ROW_SUB), :], dst_ref, sem).wait()


def _rows_from_slabs(buf_ref, start, n, stride):
    return jnp.concatenate([buf_ref[pl.ds(start + c, n, stride=stride), :] for c in range(ROW_SUB)], axis=-1)


def _rows_to_slabs(val, out_ref):
    n = val.shape[0]
    for c in range(ROW_SUB):
        out_ref[pl.ds(c, n, stride=ROW_SUB), :] = val[:, c * LANE:(c + 1) * LANE]


ITEM_BLKS = 3
ITEM_ROWS = ITEM_BLKS * MOE_BLOCK
N_ITEMS = N_BLK // ITEM_BLKS + N_EXPERTS
EXP_CHUNK = 512
N_CHUNK = D_EXPERT // EXP_CHUNK
BLK_SLABS = MOE_BLOCK * ROW_SUB


WAIT_ROWS = 8


def _expert_kernel(item_exp_ref, item_blk0_ref, item_nblk_ref, item_cnt_ref, n_used_ref, row_tok_ref, h_hbm, w1_ref,
                   w3_ref, w2_ref, y_hbm, xslab, yacc, stage, w1c, w3c, w2c, sem_x, sem_y):
    i, j = pl.program_id(0), pl.program_id(1)
    nblk = item_nblk_ref[i]
    blk0 = item_blk0_ref[i]
    xs = xslab.at[i % 2]

    def gather_item(item):
        _row_gather(h_hbm, row_tok_ref, item_blk0_ref[item] * MOE_BLOCK, item_cnt_ref[item], xslab.at[item % 2],
                    sem_x)

    @pl.when(jnp.logical_and(i == 0, j == 0))
    def _():
        xslab[...] = jnp.zeros(xslab.shape, F32)
        gather_item(0)

    w1c[...] = w1_ref[0, 0].astype(BF16)
    w3c[...] = w3_ref[0, 0].astype(BF16)
    w2c[...] = w2_ref[0, 0].astype(BF16)
    last = j == N_CHUNK - 1

    def rows_of(b):
        return pl.ds(pl.multiple_of(b * MOE_BLOCK, MOE_BLOCK), MOE_BLOCK)

    def partial_out(xb):
        a = _dot(xb, w1c[...])
        g = _dot(xb, w3c[...])
        return _dot((a * jax.nn.sigmoid(a) * g).astype(BF16), w2c[...])

    def y_copy(b, slot):
        dst = pl.multiple_of((blk0 + b) * BLK_SLABS, BLK_SLABS)
        return pltpu.make_async_copy(stage.at[slot], y_hbm.at[pl.ds(dst, BLK_SLABS), :], sem_y.at[slot])

    def x_block(b):
        return _rows_from_slabs(xs, b * BLK_SLABS, MOE_BLOCK, ROW_SUB).astype(BF16)

    @pl.when(j == 0)
    def _():
        def wait_rows(t, _):
            pltpu.make_async_copy(h_hbm.at[pl.ds(0, WAIT_ROWS * ROW_SUB), :],
                                  xs.at[pl.ds(0, WAIT_ROWS * ROW_SUB), :], sem_x).wait()
            return 0
        lax.fori_loop(0, item_cnt_ref[i] // WAIT_ROWS, wait_rows, 0)

        @pl.when(i + 1 < N_ITEMS)
        def _():
            gather_item(i + 1)

        def first_block(b, _):
            yacc[rows_of(b), :] = partial_out(x_block(b))
            return 0
        lax.fori_loop(0, nblk, first_block, 0)

    @pl.when(jnp.logical_and(j > 0, jnp.logical_not(last)))
    def _():
        def mid_block(b, _):
            yacc[rows_of(b), :] += partial_out(x_block(b))
            return 0
        lax.fori_loop(0, nblk, mid_block, 0)

    @pl.when(last)
    def _():
        def last_block(b, _):
            slot = b % 2

            @pl.when(b >= 2)
            def _():
                y_copy(b - 2, slot).wait()
            _rows_to_slabs(yacc[rows_of(b), :] + partial_out(x_block(b)), stage.at[slot])
            y_copy(b, slot).start()
            return 0
        lax.fori_loop(0, nblk, last_block, 0)

        for back in (1, 2):
            @pl.when(nblk >= back)
            def _():
                b = nblk - back
                y_copy(b, b % 2).wait()

    @pl.when(jnp.logical_and(i == N_ITEMS - 1, last))
    def _():
        stage[0] = jnp.zeros((BLK_SLABS, LANE), F32)

        def fill(b, _):
            dst = pl.multiple_of(b * BLK_SLABS, BLK_SLABS)
            cp = pltpu.make_async_copy(stage.at[0], y_hbm.at[pl.ds(dst, BLK_SLABS), :], sem_y.at[0])
            cp.start()
            cp.wait()
            return 0
        lax.fori_loop(n_used_ref[0], N_BLK, fill, 0)


def experts(l, item_exp, item_blk0, item_nblk, item_cnt, n_used, row_tok, h2_slabs, w1, w3, w2):
    wspec = lambda blk, imap: pl.BlockSpec(blk, imap)
    grid_spec = pltpu.PrefetchScalarGridSpec(
        num_scalar_prefetch=6,
        grid=(N_ITEMS, N_CHUNK),
        in_specs=[pl.BlockSpec(memory_space=pl.ANY),
                  wspec((1, 1, D, EXP_CHUNK), lambda i, j, ie, *_: (l, ie[i], 0, j)),
                  wspec((1, 1, D, EXP_CHUNK), lambda i, j, ie, *_: (l, ie[i], 0, j)),
                  wspec((1, 1, EXP_CHUNK, D), lambda i, j, ie, *_: (l, ie[i], j, 0))],
        out_specs=pl.BlockSpec(memory_space=pl.ANY),
        scratch_shapes=[pltpu.VMEM((2, ITEM_BLKS * BLK_SLABS, LANE), F32),
                        pltpu.VMEM((ITEM_ROWS, D), F32),
                        pltpu.VMEM((2, BLK_SLABS, LANE), F32),
                        pltpu.VMEM((D, EXP_CHUNK), BF16),
                        pltpu.VMEM((D, EXP_CHUNK), BF16),
                        pltpu.VMEM((EXP_CHUNK, D), BF16),
                        pltpu.SemaphoreType.DMA(()),
                        pltpu.SemaphoreType.DMA((2,))],
    )
    return pl.pallas_call(
        _expert_kernel,
        grid_spec=grid_spec,
        out_shape=jax.ShapeDtypeStruct((N_ROWS * ROW_SUB, LANE), F32),
        compiler_params=_cparams(("arbitrary", "arbitrary")),
        name="experts",
    )(item_exp, item_blk0, item_nblk, item_cnt, n_used, row_tok, h2_slabs, w1, w3, w2)


COMBINE_SUB = 64


def _combine_kernel(pos_ref, y_hbm, gates_ref, x1_ref, mod_ref, g_ref, b_ref, nmod_ref,
                    x2c_ref, x2l_ref, hn_ref, ybuf, sems):
    i = pl.program_id(0)
    slot = i % 2
    n = RB * TOP_K
    n_sub = RB // COMBINE_SUB

    @pl.when(i == 0)
    def _():
        _row_gather(y_hbm, pos_ref, 0, n, ybuf.at[0], sems.at[0])

    _row_gather_wait(y_hbm, n, ybuf.at[slot], sems.at[slot])
    shift5, shift0, scale1 = _mod_row(mod_ref, 5), _mod_row(nmod_ref, 0), 1.0 + _mod_row(nmod_ref, 1)
    for s in range(n_sub):
        @pl.when(i + 1 < NRB)
        def _():
            _row_gather(y_hbm, pos_ref, (i + 1) * n, n // n_sub, ybuf.at[1 - slot], sems.at[1 - slot],
                        first=s * (n // n_sub))

        rows = slice(s * COMBINE_SUB, (s + 1) * COMBINE_SUB)
        base = s * COMBINE_SUB * TOP_K * ROW_SUB
        y0 = _rows_from_slabs(ybuf.at[slot], base, COMBINE_SUB, TOP_K * ROW_SUB)
        y1 = _rows_from_slabs(ybuf.at[slot], base + ROW_SUB, COMBINE_SUB, TOP_K * ROW_SUB)
        gates = gates_ref[rows, :]
        f = gates[:, 0:1] * y0 + gates[:, 1:2] * y1
        x2 = _ln0(ALPHA * x1_ref[rows, :] + shift5 * f) * g_ref[...] + b_ref[...]
        x2l_ref[rows, :] = x2

        @pl.when(i == 0)
        def _():
            x2c_ref[rows, :] = x2

        hn_ref[rows, :] = (_ln0(x2) * scale1 + shift0).astype(BF16)


def combine(pos, y_slabs, gates, x1, mod, ln_g, ln_b, next_mod):
    row = lambda i, p: (i, 0)
    const = lambda i, p: (0, 0)
    grid_spec = pltpu.PrefetchScalarGridSpec(
        num_scalar_prefetch=1,
        grid=(NRB,),
        in_specs=[pl.BlockSpec(memory_space=pl.ANY),
                  pl.BlockSpec((RB, ROUTER_W), row),
                  pl.BlockSpec((RB, D), row),
                  pl.BlockSpec((8, 6 * D), const),
                  pl.BlockSpec((1, D), const),
                  pl.BlockSpec((1, D), const),
                  pl.BlockSpec((8, 6 * D), const)],
        out_specs=[pl.BlockSpec((RB, D), const),
                   pl.BlockSpec((RB, D), lambda i, p: (jnp.maximum(i - 1, 0), 0)),
                   pl.BlockSpec((RB, D), row)],
        scratch_shapes=[pltpu.VMEM((2, RB * TOP_K * ROW_SUB, LANE), F32),
                        pltpu.SemaphoreType.DMA((2,))],
    )
    return pl.pallas_call(
        _combine_kernel,
        grid_spec=grid_spec,
        out_shape=[jax.ShapeDtypeStruct((N_CTX, D), F32), jax.ShapeDtypeStruct((SEQ, D), F32),
                   jax.ShapeDtypeStruct((T, D), BF16)],
        compiler_params=_cparams(("arbitrary",)),
        name="combine",
    )(pos, y_slabs, gates, x1, mod, ln_g, ln_b, next_mod)


def _routing_tables(ids):
    expert = ids[:, :TOP_K].reshape(-1)
    n = expert.shape[0]
    onehot = (expert[:, None] == jnp.arange(N_EXPERTS, dtype=jnp.int32)[None, :]).astype(jnp.int32)
    csum = jnp.cumsum(onehot, axis=0)
    counts = csum[-1]
    rank = jnp.sum(onehot * csum, axis=1) - 1
    nb = (counts + MOE_BLOCK - 1) // MOE_BLOCK
    bstart = jnp.cumsum(nb) - nb
    pos = (jnp.sum(onehot * (bstart * MOE_BLOCK)[None, :], axis=1) + rank).astype(jnp.int32)
    tok = jnp.arange(n, dtype=jnp.int32) // TOP_K
    row_tok = jnp.zeros((N_ROWS,), jnp.int32).at[pos].set(tok)
    n_it = (nb + ITEM_BLKS - 1) // ITEM_BLKS
    it_end = jnp.cumsum(n_it)
    it_start = it_end - n_it
    idx = jnp.arange(N_ITEMS, dtype=jnp.int32)
    e_of = jnp.minimum(jnp.sum(it_end[None, :] <= idx[:, None], axis=1), N_EXPERTS - 1).astype(jnp.int32)
    k = idx - it_start[e_of]
    valid = idx < it_end[-1]
    item_exp = e_of
    item_blk0 = jnp.where(valid, bstart[e_of] + ITEM_BLKS * k, 0).astype(jnp.int32)
    item_nblk = jnp.where(valid, jnp.clip(nb[e_of] - ITEM_BLKS * k, 0, ITEM_BLKS), 0).astype(jnp.int32)
    item_cnt = jnp.where(valid, jnp.clip(counts[e_of] - ITEM_ROWS * k, 0, ITEM_ROWS), 0)
    item_cnt = ((item_cnt + GATHER_UNROLL - 1) // GATHER_UNROLL * GATHER_UNROLL).astype(jnp.int32)
    n_used = jnp.sum(nb).astype(jnp.int32)[None]
    return item_exp, item_blk0, item_nblk, item_cnt, n_used, row_tok, pos


def _rope_tables():
    rows = SEQ // GRID_W
    row = jnp.repeat(jnp.arange(rows, dtype=F32), GRID_W)
    col = jnp.tile(jnp.arange(GRID_W, dtype=F32), rows)
    half = QK_ROPE // 2
    inv = ROPE_THETA ** (-jnp.arange(0, half, 2, dtype=F32) / half)
    ar = row[:, None] * inv
    ac = col[:, None] * inv
    ang = jnp.concatenate([ar, ar, ac, ac], -1)
    cos = jnp.concatenate([jnp.ones((N_CTX, QK_ROPE), F32), jnp.cos(ang)], 0)
    sin = jnp.concatenate([jnp.zeros((N_CTX, QK_ROPE), F32), jnp.sin(ang)], 0)
    one = jnp.ones((T, QK_NOPE), F32)
    zero_n = jnp.zeros((T, QK_NOPE), F32)
    zero_p = jnp.zeros((T, HEAD_PAD - QK_NOPE - QK_ROPE), F32)
    tq_c = jnp.concatenate([one, cos, zero_p], -1)
    tq_s = jnp.concatenate([zero_n, sin, zero_p], -1)
    tk = jnp.concatenate([cos, sin, jnp.zeros((T, LANE - 2 * QK_ROPE), F32)], -1)
    return tq_c, tq_s, tk


_ROT_SRC = np.concatenate([np.arange(8, 16), np.arange(0, 8), np.arange(24, 32), np.arange(16, 24)])
_ROT_SIGN = np.concatenate([-np.ones(8), np.ones(8), -np.ones(8), np.ones(8)]).astype(np.float32)


def _rot_cols(w):
    return w[..., _ROT_SRC] * _ROT_SIGN


def _layer_weights(l, w_in_t, b_in, d_w_uq, d_w_uk, d_w_uv, r_w_grp, r_b_grp, r_w_exp, r_b_exp, a_b_s, b_w_dw):
    bi = b_in[l]
    p = {}
    w_kr, b_kr = w_in_t[l, COL_KR:COL_G, :].T, bi[COL_KR:COL_G]
    padw = LANE - 2 * QK_ROPE
    p["wkr"] = jnp.concatenate([w_kr, _rot_cols(w_kr), jnp.zeros((D, padw), F32)], -1).astype(BF16)
    p["bkr"] = jnp.concatenate([b_kr, _rot_cols(b_kr), jnp.zeros((padw,), F32)])[None]
    wq = d_w_uq[l].reshape(Q_LORA, HEADS, QK_NOPE + QK_ROPE)
    zpad = jnp.zeros((Q_LORA, HEADS, HEAD_PAD - QK_NOPE - QK_ROPE), F32)
    p["wq"] = jnp.concatenate([wq, zpad], -1).reshape(Q_LORA, HEADS * HEAD_PAD).astype(BF16)
    p["wqr"] = jnp.concatenate([jnp.zeros((Q_LORA, HEADS, QK_NOPE), F32), _rot_cols(wq[..., QK_NOPE:]), zpad],
                               -1).reshape(Q_LORA, HEADS * HEAD_PAD).astype(BF16)
    wk = d_w_uk[l].reshape(KV_LORA, HEADS, QK_NOPE)
    p["wk"] = jnp.concatenate([wk, jnp.zeros((KV_LORA, HEADS, HEAD_PAD - QK_NOPE), F32)],
                              -1).reshape(KV_LORA, HEADS * HEAD_PAD).astype(BF16)
    p["wvt"] = d_w_uv[l].T.astype(BF16)
    e = np.zeros((LANE, HEADS, HEAD_PAD), np.float32)
    for j in range(QK_ROPE):
        e[j, :, QK_NOPE + j] = 1.0
        e[QK_ROPE + j, :, QK_NOPE + j] = 1.0
    p["e"] = jnp.asarray(e.reshape(LANE, HEADS * HEAD_PAD), dtype=BF16)
    p["bg"] = bi[None, COL_G:]
    padr = ROUTER_W - N_GROUPS - N_EXPERTS
    p["wr"] = _hilo(jnp.concatenate([r_w_grp[l], r_w_exp[l], jnp.zeros((D, padr), F32)], -1))
    p["br"] = jnp.concatenate([r_b_grp[l], r_b_exp[l], jnp.zeros((padr,), F32)])[None]
    p["bs"] = jnp.broadcast_to(a_b_s[l][:, :, None], (A_GROUPS, CHUNK, CHUNK))
    p["wdw"] = jnp.concatenate([b_w_dw[l], jnp.zeros((1, BW), F32)], 0)
    return p


def kernel(x, c, ctx, c_ctx, w_ada, b_ada, w_in, b_in, a_ln_g, a_ln_b, a_w_s, a_b_s, b_w_dw, b_b_dw, b_ln_g,
           b_ln_b, d_q_g, d_w_uq, d_kv_g, d_w_uk, d_w_uv, w_branch, w_o, ln1_g, ln1_b, r_w_grp, r_b_grp,
           r_w_exp, r_b_exp, e_w1, e_w3, e_w2, ln2_g, ln2_b):
    assert x.shape == (1, SEQ, D) and ctx.shape == (1, N_CTX, D)
    x_ctx, x_lat = ctx[0], x[0]
    w_in_t = jnp.transpose(w_in, (0, 2, 1))
    c_rep = jnp.broadcast_to(jnp.concatenate([c, c_ctx[None]], axis=0)[:, :, None], (N_COND, D, LANE))
    mods = ada_mod(c_rep, w_ada, b_ada)
    tq_c, tq_s, tk = _rope_tables()
    fc = _fourier_consts()
    b_in3 = b_in.reshape(DEPTH, 1, -1)
    w_br, w_ob = w_branch.astype(BF16), w_o.astype(BF16)

    h = modln(x_ctx, x_lat, mods[0])
    for l in range(DEPTH):
        p = _layer_weights(l, w_in_t, b_in, d_w_uq, d_w_uk, d_w_uv, r_w_grp, r_b_grp, r_w_exp, r_b_exp, a_b_s,
                           b_w_dw)
        mod = mods[l]
        br_a = mix_a(l, h, w_in_t, b_in3, a_ln_g[l][None], a_ln_b[l][None], a_w_s[l].astype(BF16), p["bs"])
        br_b = conv_ln(glu(l, h, w_in_t, b_in3), p["wdw"], b_b_dw[l][None], b_ln_g[l][None], b_ln_b[l][None])
        br_c = fourier_mix(l, h, w_in_t, b_in3, fc)
        q, k, vt = mla_proj(l, h, w_in_t, b_in3, p["wkr"], p["bkr"], d_q_g[l][None], d_kv_g[l][None], p["wq"],
                            p["wqr"], p["wk"], p["wvt"], p["e"], tq_c, tq_s, tk)
        br_dt = attention(q, k, vt)
        y = merge(l, h, br_a, br_b, br_c, br_dt, w_in_t, p["bg"], w_br)
        x1, h2, ids, gates = post_mix(l, y, x_ctx, x_lat, mod, w_ob, ln1_g[l][None], ln1_b[l][None], p["wr"],
                                      p["br"])
        item_exp, item_blk0, item_nblk, item_cnt, n_used, row_tok, pos = _routing_tables(ids)
        ys = experts(l, item_exp, item_blk0, item_nblk, item_cnt, n_used, row_tok, h2, e_w1, e_w3, e_w2)
        x_ctx, x_lat, h = combine(pos, ys, gates, x1, mod, ln2_g[l][None], ln2_b[l][None],
                                  mods[min(l + 1, DEPTH - 1)])
    return x_lat[None]
```

```python
import functools
import math

import numpy as np
import jax
import jax.numpy as jnp
from jax import lax
from jax.experimental import pallas as pl
from jax.experimental.pallas import tpu as pltpu

F32 = jnp.float32
BF16 = jnp.bfloat16

D = 2048
SEQ = 8192
N_CTX = 256
T = N_CTX + SEQ
DEPTH = 2
GRID_W = 64
BW = 512
CHUNK = 128
A_GROUPS = 4
CONV_W = 31
F_GROUPS = 4
QK_NOPE = 64
QK_ROPE = 32
V_HEAD = 64
HEADS = 8
Q_LORA = 512
KV_LORA = 256
ROPE_THETA = 10000.0
MLA_SCALE = (QK_NOPE + QK_ROPE) ** -0.5
N_GROUPS = 4
EXP_PER_GROUP = 8
N_EXPERTS = 32
TOP_K = 2
D_EXPERT = 1024
MOE_BLOCK = 256
COL_A = 0
COL_B = 1024
COL_C = 2048
COL_D = 2560
COL_KV = 3072
COL_KR = 3328
COL_G = 3360
ALPHA = (2 * DEPTH) ** 0.25
LN_EPS = 1e-6

LANE = 128
HEAD_PAD = 128
RB = 256
NRB = T // RB
FFT_N1 = 64
FFT_N2 = 128
ROW_SUB = D // LANE
VMEM_LIMIT = 56 * 1024 * 1024


def _cparams(sem):
    return pltpu.CompilerParams(dimension_semantics=sem, vmem_limit_bytes=VMEM_LIMIT)


def _dot(a, b):
    return jnp.dot(a, b, preferred_element_type=F32)


def _split(x):
    hi = lax.bitcast_convert_type(lax.bitcast_convert_type(x, jnp.int32) & jnp.int32(-65536), F32)
    return hi.astype(BF16), (x - hi).astype(BF16)


def _hilo(x):
    return jnp.stack(_split(jnp.asarray(x, F32)))


def _dot3(a, t_ref):
    a_hi, a_lo = _split(a)
    return _dot(a_hi, t_ref[0]) + (_dot(a_hi, t_ref[1]) + _dot(a_lo, t_ref[0]))


def _dot3_t(t_ref, b):
    b_hi, b_lo = _split(b)
    return _dot(t_ref[0], b_hi) + (_dot(t_ref[1], b_hi) + _dot(t_ref[0], b_lo))


def _ln0(x):
    mu = jnp.mean(x, axis=-1, keepdims=True)
    xc = x - mu
    var = jnp.mean(xc * xc, axis=-1, keepdims=True)
    return xc * lax.rsqrt(var + LN_EPS)


def _mod_row(mod_ref, q):
    lat = mod_ref[0:1, q * D:(q + 1) * D]
    ctx = mod_ref[1:2, q * D:(q + 1) * D]
    return jnp.where(pl.program_id(0) == 0, ctx, lat)


ADA_TN = 1536
ADA_TK = 256
N_COND = 2


def _ada_kernel(c_ref, w_ref, b_ref, o_ref):
    accs = [b_ref[0] for _ in range(N_COND)]
    for k0 in range(0, D, ADA_TK):
        w = w_ref[0, k0:k0 + ADA_TK, :]
        for r in range(N_COND):
            s = c_ref[r, k0:k0 + ADA_TK, :]
            s = s * jax.nn.sigmoid(s)
            s = jnp.concatenate([s] * (ADA_TN // LANE), axis=-1)
            accs[r] = accs[r] + jnp.sum(w * s, axis=0, keepdims=True)
    o_ref[0] = jnp.concatenate(accs + [jnp.zeros((8 - N_COND, ADA_TN), F32)], axis=0)


def ada_mod(c_rep, w_ada, b_ada):
    return pl.pallas_call(
        _ada_kernel,
        grid=(DEPTH, 6 * D // ADA_TN),
        in_specs=[pl.BlockSpec((N_COND, D, LANE), lambda l, j: (0, 0, 0)),
                  pl.BlockSpec((1, D, ADA_TN), lambda l, j: (l, 0, j)),
                  pl.BlockSpec((1, 1, ADA_TN), lambda l, j: (l, 0, j))],
        out_specs=pl.BlockSpec((1, 8, ADA_TN), lambda l, j: (l, 0, j)),
        out_shape=jax.ShapeDtypeStruct((DEPTH, 8, 6 * D), F32),
        compiler_params=_cparams(("arbitrary", "arbitrary")),
        name="ada_mod",
    )(c_rep, w_ada, b_ada.reshape(DEPTH, 1, 6 * D))


def _stream_specs():
    return [pl.BlockSpec((RB, D), lambda i, *_: (0, 0)),
            pl.BlockSpec((RB, D), lambda i, *_: (jnp.maximum(i - 1, 0), 0))]


def _stream_block(ctx_ref, lat_ref):
    return jnp.where(pl.program_id(0) == 0, ctx_ref[...], lat_ref[...])


def _modln_kernel(xc_ref, xl_ref, mod_ref, h_ref):
    h = _ln0(_stream_block(xc_ref, xl_ref)) * (1.0 + _mod_row(mod_ref, 1)) + _mod_row(mod_ref, 0)
    h_ref[...] = h.astype(BF16)


def modln(x_ctx, x_lat, mod):
    return pl.pallas_call(
        _modln_kernel,
        grid=(NRB,),
        in_specs=_stream_specs() + [pl.BlockSpec((8, 6 * D), lambda i: (0, 0))],
        out_specs=pl.BlockSpec((RB, D), lambda i: (i, 0)),
        out_shape=jax.ShapeDtypeStruct((T, D), BF16),
        compiler_params=_cparams(("arbitrary",)),
        name="modln",
    )(x_ctx, x_lat, mod)


def _w_in_spec(l, width, col):
    return pl.BlockSpec((1, width, D), lambda i: (l, col, 0))


def _b_in_spec(l, width, col):
    return pl.BlockSpec((1, 1, width), lambda i: (l, 0, col))


CAST_CHUNK = 256


def _cast_transposed(w_ref, wbf_ref):
    width = w_ref.shape[0]
    for c0 in range(0, width, CAST_CHUNK):
        wbf_ref[:, c0:c0 + CAST_CHUNK] = w_ref[c0:c0 + CAST_CHUNK, :].T.astype(BF16)


def _cast_once(w_ref, wbf_ref):
    @pl.when(pl.program_id(0) == 0)
    def _():
        _cast_transposed(w_ref.at[0], wbf_ref)


def _mix_a_kernel(h_ref, w_ref, b_ref, g_ref, bb_ref, ws_ref, bs_ref, o_ref, wbf_ref):
    _cast_once(w_ref, wbf_ref)
    uv = jax.nn.gelu(_dot(h_ref[...], wbf_ref[...]) + b_ref[0])
    u = uv[:, :BW]
    v = _ln0(uv[:, BW:]) * g_ref[...] + bb_ref[...]
    v = v.astype(BF16)
    gw = BW // A_GROUPS
    for ch in range(RB // CHUNK):
        rows = slice(ch * CHUNK, (ch + 1) * CHUNK)
        parts = [_dot(ws_ref[g], v[rows, g * gw:(g + 1) * gw]) + bs_ref[g] for g in range(A_GROUPS)]
        mixed = jnp.concatenate(parts, axis=-1)
        o_ref[rows, :] = (u[rows, :] * mixed).astype(BF16)


def mix_a(l, h, w_in, b_in, ln_g, ln_b, ws, bs):
    return pl.pallas_call(
        _mix_a_kernel,
        grid=(NRB,),
        in_specs=[pl.BlockSpec((RB, D), lambda i: (i, 0)),
                  _w_in_spec(l, 2 * BW, COL_A // (2 * BW)),
                  _b_in_spec(l, 2 * BW, COL_A // (2 * BW)),
                  pl.BlockSpec((1, BW), lambda i: (0, 0)),
                  pl.BlockSpec((1, BW), lambda i: (0, 0)),
                  pl.BlockSpec((A_GROUPS, CHUNK, CHUNK), lambda i: (0, 0, 0)),
                  pl.BlockSpec((A_GROUPS, CHUNK, CHUNK), lambda i: (0, 0, 0))],
        out_specs=pl.BlockSpec((RB, BW), lambda i: (i, 0)),
        out_shape=jax.ShapeDtypeStruct((T, BW), BF16),
        scratch_shapes=[pltpu.VMEM((D, 2 * BW), BF16)],
        compiler_params=_cparams(("arbitrary",)),
        name="mix_a",
    )(h, w_in, b_in, ln_g, ln_b, ws, bs)


def _glu_kernel(h_ref, w_ref, b_ref, o_ref, wbf_ref):
    _cast_once(w_ref, wbf_ref)
    ab = _dot(h_ref[...], wbf_ref[...]) + b_ref[0]
    o_ref[...] = ab[:, :BW] * jax.nn.sigmoid(ab[:, BW:])


def glu(l, h, w_in, b_in):
    return pl.pallas_call(
        _glu_kernel,
        grid=(NRB,),
        in_specs=[pl.BlockSpec((RB, D), lambda i: (i, 0)),
                  _w_in_spec(l, 2 * BW, COL_B // (2 * BW)),
                  _b_in_spec(l, 2 * BW, COL_B // (2 * BW))],
        out_specs=pl.BlockSpec((RB, BW), lambda i: (i, 0)),
        out_shape=jax.ShapeDtypeStruct((T, BW), F32),
        scratch_shapes=[pltpu.VMEM((D, 2 * BW), BF16)],
        compiler_params=_cparams(("arbitrary",)),
        name="glu",
    )(h, w_in, b_in)


CONV_HALO = 16


def _conv_kernel(prev_ref, cur_ref, next_ref, w_ref, b_ref, g_ref, bb_ref, o_ref, ext_ref):
    i = pl.program_id(0)
    has_prev = i >= 2
    has_next = jnp.logical_and(i >= 1, i < NRB - 1)
    ext_ref[0:CONV_HALO, :] = jnp.where(has_prev, prev_ref[RB - CONV_HALO:RB, :], 0.0)
    ext_ref[CONV_HALO:CONV_HALO + RB, :] = cur_ref[...]
    ext_ref[CONV_HALO + RB:2 * CONV_HALO + RB, :] = jnp.where(has_next, next_ref[0:CONV_HALO, :], 0.0)
    off = CONV_HALO - CONV_W // 2
    acc = jnp.zeros((RB, BW), F32)
    for k in range(CONV_W):
        acc = acc + ext_ref[off + k:off + k + RB, :] * w_ref[k:k + 1, :]
    y = _ln0(acc + b_ref[...]) * g_ref[...] + bb_ref[...]
    o_ref[...] = (y * jax.nn.sigmoid(y)).astype(BF16)


def conv_ln(y, w_dw, b_dw, ln_g, ln_b):
    return pl.pallas_call(
        _conv_kernel,
        grid=(NRB,),
        in_specs=[pl.BlockSpec((RB, BW), lambda i: (jnp.maximum(i - 1, 0), 0)),
                  pl.BlockSpec((RB, BW), lambda i: (i, 0)),
                  pl.BlockSpec((RB, BW), lambda i: (jnp.minimum(i + 1, NRB - 1), 0)),
                  pl.BlockSpec((CONV_W + 1, BW), lambda i: (0, 0)),
                  pl.BlockSpec((1, BW), lambda i: (0, 0)),
                  pl.BlockSpec((1, BW), lambda i: (0, 0)),
                  pl.BlockSpec((1, BW), lambda i: (0, 0))],
        out_specs=pl.BlockSpec((RB, BW), lambda i: (i, 0)),
        out_shape=jax.ShapeDtypeStruct((T, BW), BF16),
        scratch_shapes=[pltpu.VMEM((RB + 2 * CONV_HALO, BW), F32)],
        compiler_params=_cparams(("arbitrary",)),
        name="conv_ln",
    )(y, y, y, w_dw, b_dw, ln_g, ln_b)


def _dft_tables(n):
    k = np.arange(n, dtype=np.int64)
    ang = 2.0 * np.pi * ((k[:, None] * k[None, :]) % n).astype(np.float64) / n
    return np.cos(ang), np.sin(ang)


def _fproj_kernel(h_ref, w_ref, b_ref, cs_ref, ar_ref, ai_ref, wbf_ref):
    _cast_once(w_ref, wbf_ref)
    z = _dot(h_ref[...], wbf_ref[...]) + b_ref[0]
    a = _dot3(z, cs_ref)
    ar_ref[...] = a[:, :BW]
    ai_ref[...] = a[:, BW:]


def fourier_proj(l, h, w_in, b_in, cs):
    return pl.pallas_call(
        _fproj_kernel,
        grid=(NRB,),
        in_specs=[pl.BlockSpec((RB, D), lambda i: (i, 0)),
                  _w_in_spec(l, BW, COL_C // BW),
                  _b_in_spec(l, BW, COL_C // BW),
                  pl.BlockSpec((2, BW, 2 * BW), lambda i: (0, 0, 0))],
        out_specs=[pl.BlockSpec((RB, BW), lambda i: (i, 0)),
                   pl.BlockSpec((RB, BW), lambda i: (i, 0))],
        out_shape=[jax.ShapeDtypeStruct((T, BW), F32), jax.ShapeDtypeStruct((T, BW), F32)],
        scratch_shapes=[pltpu.VMEM((D, BW), BF16)],
        compiler_params=_cparams(("arbitrary",)),
        name="fourier_proj",
    )(h, w_in, b_in, cs)


FFT_TILE_N2 = 4


def _fft1_kernel(ar_ref, ai_ref, m_ref, ct_ref, st_ref, br_ref, bi_ref):
    x = jnp.concatenate([ar_ref[...], ai_ref[...]], axis=0)
    b = _dot3_t(m_ref, x)
    b_re, b_im = b[:FFT_N1], b[FFT_N1:]
    reps = BW // LANE
    for j in range(FFT_TILE_N2):
        ct = jnp.concatenate([ct_ref[j]] * reps, axis=-1)
        st = jnp.concatenate([st_ref[j]] * reps, axis=-1)
        lanes = slice(j * BW, (j + 1) * BW)
        br_ref[:, lanes] = b_re[:, lanes] * ct + b_im[:, lanes] * st
        bi_ref[:, lanes] = b_im[:, lanes] * ct - b_re[:, lanes] * st


def fft_stage1(ar2d, ai2d, m1, ct, st):
    tn = FFT_TILE_N2 * BW
    return pl.pallas_call(
        _fft1_kernel,
        grid=(FFT_N2 // FFT_TILE_N2,),
        in_specs=[pl.BlockSpec((FFT_N1, tn), lambda j: (0, j)),
                  pl.BlockSpec((FFT_N1, tn), lambda j: (0, j)),
                  pl.BlockSpec((2, 2 * FFT_N1, 2 * FFT_N1), lambda j: (0, 0, 0)),
                  pl.BlockSpec((FFT_TILE_N2, FFT_N1, LANE), lambda j: (j, 0, 0)),
                  pl.BlockSpec((FFT_TILE_N2, FFT_N1, LANE), lambda j: (j, 0, 0))],
        out_specs=[pl.BlockSpec((FFT_N1, tn), lambda j: (0, j)),
                   pl.BlockSpec((FFT_N1, tn), lambda j: (0, j))],
        out_shape=[jax.ShapeDtypeStruct((FFT_N1, FFT_N2 * BW), F32)] * 2,
        compiler_params=_cparams(("arbitrary",)),
        name="fft_stage1",
    )(ar2d, ai2d, m1, ct, st)


FFT_TILE_K1 = 8


def _fft2_kernel(br_ref, bi_ref, c_ref, s_ref, p_ref, o_ref, *, scale):
    ys = [(_dot3_t(c_ref, br_ref[a]) + _dot3_t(s_ref, bi_ref[a])) * scale for a in range(FFT_TILE_K1)]
    y = jnp.concatenate(ys, axis=0).astype(BF16)
    y = _dot(p_ref[...], y)
    o_ref[...] = y.reshape(FFT_N2, FFT_TILE_K1, BW)


def fft_stage2(br3d, bi3d, c128, s128, perm, scale):
    return pl.pallas_call(
        functools.partial(_fft2_kernel, scale=scale),
        grid=(FFT_N1 // FFT_TILE_K1,),
        in_specs=[pl.BlockSpec((FFT_TILE_K1, FFT_N2, BW), lambda a: (a, 0, 0)),
                  pl.BlockSpec((FFT_TILE_K1, FFT_N2, BW), lambda a: (a, 0, 0)),
                  pl.BlockSpec((2, FFT_N2, FFT_N2), lambda a: (0, 0, 0)),
                  pl.BlockSpec((2, FFT_N2, FFT_N2), lambda a: (0, 0, 0)),
                  pl.BlockSpec((FFT_TILE_K1 * FFT_N2, FFT_TILE_K1 * FFT_N2), lambda a: (0, 0))],
        out_specs=pl.BlockSpec((FFT_N2, FFT_TILE_K1, BW), lambda a: (0, a, 0)),
        out_shape=jax.ShapeDtypeStruct((FFT_N2, FFT_N1, BW), F32),
        compiler_params=_cparams(("arbitrary",)),
        name="fft_stage2",
    )(br3d, bi3d, c128, s128, perm)


def _dft_ctx_kernel(ar_ref, ai_ref, c_ref, s_ref, o_ref, *, scale):
    o_ref[...] = (_dot3_t(c_ref, ar_ref[...]) + _dot3_t(s_ref, ai_ref[...])) * scale


def dft_ctx(ar, ai, c, s, scale):
    return pl.pallas_call(
        functools.partial(_dft_ctx_kernel, scale=scale),
        grid=(1,),
        in_specs=[pl.BlockSpec((N_CTX, BW), lambda i: (0, 0)),
                  pl.BlockSpec((N_CTX, BW), lambda i: (0, 0)),
                  pl.BlockSpec((2, N_CTX, N_CTX), lambda i: (0, 0, 0)),
                  pl.BlockSpec((2, N_CTX, N_CTX), lambda i: (0, 0, 0))],
        out_specs=pl.BlockSpec((N_CTX, BW), lambda i: (0, 0)),
        out_shape=jax.ShapeDtypeStruct((N_CTX, BW), F32),
        compiler_params=_cparams(("arbitrary",)),
        name="dft_ctx",
    )(ar, ai, c, s)


def _fourier_consts():
    c64, s64 = _dft_tables(FFT_N1)
    m1 = np.block([[c64, s64], [-s64, c64]]).astype(np.float32)
    k1 = np.arange(FFT_N1, dtype=np.float64)
    n2 = np.arange(FFT_N2, dtype=np.float64)
    phi = 2.0 * np.pi * n2[:, None] * k1[None, :] / SEQ
    ct = np.repeat(np.cos(phi)[:, :, None], LANE, axis=2).astype(np.float32)
    st = np.repeat(np.sin(phi)[:, :, None], LANE, axis=2).astype(np.float32)
    c128, s128 = _dft_tables(FFT_N2)
    gw = BW // F_GROUPS
    cg, sg = _dft_tables(gw)
    cs = np.zeros((BW, 2 * BW), np.float32)
    for g in range(F_GROUPS):
        sl = slice(g * gw, (g + 1) * gw)
        cs[sl, sl] = cg
        cs[sl, BW + g * gw:BW + (g + 1) * gw] = -sg
    rows = np.arange(FFT_TILE_K1 * FFT_N2)
    perm = np.zeros((rows.size, rows.size), np.float32)
    perm[rows, (rows % FFT_TILE_K1) * FFT_N2 + rows // FFT_TILE_K1] = 1.0
    cc, sc = _dft_tables(N_CTX)
    return dict(m1=_hilo(m1), ct=jnp.asarray(ct), st=jnp.asarray(st), c128=_hilo(c128), s128=_hilo(s128),
                cs=_hilo(cs), perm=jnp.asarray(perm, dtype=BF16), cc=_hilo(cc), sc=_hilo(sc))


def fourier_mix(l, h, w_in, b_in, fc):
    gw = BW // F_GROUPS
    ar, ai = fourier_proj(l, h, w_in, b_in, fc["cs"])
    y_ctx = dft_ctx(ar, ai, fc["cc"], fc["sc"], 1.0 / math.sqrt(N_CTX * gw))
    ar2 = ar[N_CTX:].reshape(FFT_N1, FFT_N2 * BW)
    ai2 = ai[N_CTX:].reshape(FFT_N1, FFT_N2 * BW)
    br, bi = fft_stage1(ar2, ai2, fc["m1"], fc["ct"], fc["st"])
    y = fft_stage2(br.reshape(FFT_N1, FFT_N2, BW), bi.reshape(FFT_N1, FFT_N2, BW),
                   fc["c128"], fc["s128"], fc["perm"], 1.0 / math.sqrt(SEQ * gw))
    return jnp.concatenate([y_ctx, y.reshape(SEQ, BW)], axis=0)


VT_ROWS = 80
LOG2E = math.log2(math.e)


def _mla_proj_kernel(h_ref, wcq_ref, bcq_ref, wkv_ref, bkv_ref, wkr_ref, bkr_ref, qg_ref, kvg_ref, wq_ref, wqr_ref,
                     wk_ref, wvt_ref, e_ref, tq_c_ref, tq_s_ref, tk_ref, q_ref, k_ref, vt_ref, wcq_bf, wkv_bf):
    _cast_once(wcq_ref, wcq_bf)
    _cast_once(wkv_ref, wkv_bf)
    h = h_ref[...]
    cq = _dot(h, wcq_bf[...]) + bcq_ref[0]
    cq = cq * lax.rsqrt(jnp.mean(cq * cq, axis=-1, keepdims=True) + LN_EPS) * qg_ref[...]
    cq = cq.astype(BF16)
    cos_t = jnp.concatenate([tq_c_ref[...]] * HEADS, axis=-1)
    sin_t = jnp.concatenate([tq_s_ref[...]] * HEADS, axis=-1)
    q = (_dot(cq, wq_ref[...]) * cos_t + _dot(cq, wqr_ref[...]) * sin_t) * (MLA_SCALE * LOG2E)
    q_ref[...] = q.astype(BF16)
    ckv = _dot(h, wkv_bf[...]) + bkv_ref[0]
    ckv = ckv * lax.rsqrt(jnp.mean(ckv * ckv, axis=-1, keepdims=True) + LN_EPS) * kvg_ref[...]
    ckv = ckv.astype(BF16)
    kr = ((_dot(h, wkr_ref[...]) + bkr_ref[...]) * tk_ref[...]).astype(BF16)
    k_ref[...] = (_dot(ckv, wk_ref[...]) + _dot(kr, e_ref[...])).astype(BF16)
    vt = lax.dot_general(wvt_ref[...], ckv, (((1,), (1,)), ((), ())), preferred_element_type=F32)
    ones_rows = (lax.broadcasted_iota(jnp.int32, (VT_ROWS - V_HEAD, RB), 0) == 0).astype(BF16)
    for hd in range(HEADS):
        vt_ref[0, hd * VT_ROWS:hd * VT_ROWS + V_HEAD, :] = vt[hd * V_HEAD:(hd + 1) * V_HEAD, :].astype(BF16)
        vt_ref[0, hd * VT_ROWS + V_HEAD:(hd + 1) * VT_ROWS, :] = ones_rows


def mla_proj(l, h, w_in, b_in, wkr, bkr, qg, kvg, wq, wqr, wk, wvt, e, tq_c, tq_s, tk):
    const = lambda i: (0, 0)
    row = lambda i: (i, 0)
    return pl.pallas_call(
        _mla_proj_kernel,
        grid=(NRB,),
        in_specs=[pl.BlockSpec((RB, D), row),
                  _w_in_spec(l, Q_LORA, COL_D // Q_LORA),
                  _b_in_spec(l, Q_LORA, COL_D // Q_LORA),
                  _w_in_spec(l, KV_LORA, COL_KV // KV_LORA),
                  _b_in_spec(l, KV_LORA, COL_KV // KV_LORA),
                  pl.BlockSpec((D, LANE), const),
                  pl.BlockSpec((1, LANE), const),
                  pl.BlockSpec((1, Q_LORA), const),
                  pl.BlockSpec((1, KV_LORA), const),
                  pl.BlockSpec((Q_LORA, HEADS * HEAD_PAD), const),
                  pl.BlockSpec((Q_LORA, HEADS * HEAD_PAD), const),
                  pl.BlockSpec((KV_LORA, HEADS * HEAD_PAD), const),
                  pl.BlockSpec((HEADS * V_HEAD, KV_LORA), const),
                  pl.BlockSpec((LANE, HEADS * HEAD_PAD), const),
                  pl.BlockSpec((RB, HEAD_PAD), row),
                  pl.BlockSpec((RB, HEAD_PAD), row),
                  pl.BlockSpec((RB, LANE), row)],
        out_specs=[pl.BlockSpec((RB, HEADS * HEAD_PAD), row),
                   pl.BlockSpec((RB, HEADS * HEAD_PAD), row),
                   pl.BlockSpec((1, HEADS * VT_ROWS, RB), lambda i: (i, 0, 0))],
        out_shape=[jax.ShapeDtypeStruct((T, HEADS * HEAD_PAD), BF16),
                   jax.ShapeDtypeStruct((T, HEADS * HEAD_PAD), BF16),
                   jax.ShapeDtypeStruct((NRB, HEADS * VT_ROWS, RB), BF16)],
        scratch_shapes=[pltpu.VMEM((D, Q_LORA), BF16), pltpu.VMEM((D, KV_LORA), BF16)],
        compiler_params=_cparams(("arbitrary",)),
        name="mla_proj",
    )(h, w_in, b_in, w_in, b_in, wkr, bkr, qg, kvg, wq, wqr, wk, wvt, e, tq_c, tq_s, tk)


KV_TILE = RB
KV_STAGE = 1024
N_STAGE = SEQ // KV_STAGE
TILES_PER_STAGE = KV_STAGE // KV_TILE
CTX_TILES = N_CTX // KV_TILE


HEAD_GROUP = 2


def _attn_scores(k_ref, q, g, start, n):
    kt = k_ref[pl.ds(start, n), g * HEAD_PAD:(g + 1) * HEAD_PAD]
    return lax.dot_general(kt, q, (((1,), (1,)), ((), ())), preferred_element_type=F32)


def _attn_out(acc):
    return acc[:V_HEAD, :] / acc[V_HEAD:V_HEAD + 1, :]


def _attn_kernel(q_ref, k_ref, vt_ref, o_ref, sa_ref, sb_ref):
    heads = range(HEAD_GROUP)
    qs = [q_ref[:, g * HEAD_PAD:(g + 1) * HEAD_PAD] for g in heads]

    def scores(g, stage):
        return _attn_scores(k_ref, qs[g], g, pl.multiple_of(N_CTX + stage * KV_STAGE, KV_TILE), KV_STAGE)

    def values(g, tile):
        return vt_ref[tile, g * VT_ROWS:(g + 1) * VT_ROWS, :]

    def softmax_pv(g, s_ref, stage, m, acc):
        m_new = jnp.maximum(m, jnp.max(s_ref[g], axis=0, keepdims=True))
        p = jnp.exp2(s_ref[g] - m_new).astype(BF16)
        acc = jnp.exp2(m - m_new) * acc
        for u in range(TILES_PER_STAGE):
            acc = acc + _dot(values(g, CTX_TILES + stage * TILES_PER_STAGE + u), p[u * KV_TILE:(u + 1) * KV_TILE, :])
        return m_new, acc

    ms, accs = [], []
    for g in heads:
        s0 = _attn_scores(k_ref, qs[g], g, 0, N_CTX)
        ms.append(jnp.max(s0, axis=0, keepdims=True))
        accs.append(_dot(values(g, 0), jnp.exp2(s0 - ms[g]).astype(BF16)))
        sa_ref[g] = scores(g, 0)

    def body(j, carry):
        ms, accs = list(carry[0]), list(carry[1])
        for g in heads:
            sb_ref[g] = scores(g, 2 * j + 1)
        for g in heads:
            ms[g], accs[g] = softmax_pv(g, sa_ref, 2 * j, ms[g], accs[g])
        for g in heads:
            sa_ref[g] = scores(g, 2 * j + 2)
        for g in heads:
            ms[g], accs[g] = softmax_pv(g, sb_ref, 2 * j + 1, ms[g], accs[g])
        return tuple(ms), tuple(accs)

    ms, accs = lax.fori_loop(0, N_STAGE // 2 - 1, body, (tuple(ms), tuple(accs)))
    ms, accs = list(ms), list(accs)
    for g in heads:
        sb_ref[g] = scores(g, N_STAGE - 1)
    for g in heads:
        ms[g], accs[g] = softmax_pv(g, sa_ref, N_STAGE - 2, ms[g], accs[g])
    for g in heads:
        _, acc = softmax_pv(g, sb_ref, N_STAGE - 1, ms[g], accs[g])
        o_ref[g * V_HEAD:(g + 1) * V_HEAD, :] = _attn_out(acc)


def _attn_ctx_kernel(q_ref, k_ref, vt_ref, o_ref):
    for g in range(HEAD_GROUP):
        s = _attn_scores(k_ref, q_ref[:, g * HEAD_PAD:(g + 1) * HEAD_PAD], g, 0, N_CTX)
        p = jnp.exp2(s - jnp.max(s, axis=0, keepdims=True)).astype(BF16)
        o_ref[g * V_HEAD:(g + 1) * V_HEAD, :] = _attn_out(_dot(vt_ref[0, g * VT_ROWS:(g + 1) * VT_ROWS, :], p))


def attention(q, k, vt):
    assert N_CTX == KV_TILE and N_STAGE % 2 == 0
    o_ctx = pl.pallas_call(
        _attn_ctx_kernel,
        grid=(HEADS // HEAD_GROUP,),
        in_specs=[pl.BlockSpec((N_CTX, HEAD_GROUP * HEAD_PAD), lambda h: (0, h)),
                  pl.BlockSpec((N_CTX, HEAD_GROUP * HEAD_PAD), lambda h: (0, h)),
                  pl.BlockSpec((1, HEAD_GROUP * VT_ROWS, KV_TILE), lambda h: (0, h, 0))],
        out_specs=pl.BlockSpec((HEAD_GROUP * V_HEAD, N_CTX), lambda h: (h, 0)),
        out_shape=jax.ShapeDtypeStruct((HEADS * V_HEAD, N_CTX), F32),
        compiler_params=_cparams(("arbitrary",)),
        name="attention_ctx",
    )(q, k, vt)
    o_lat = pl.pallas_call(
        _attn_kernel,
        grid=(HEADS // HEAD_GROUP, SEQ // RB),
        in_specs=[pl.BlockSpec((RB, HEAD_GROUP * HEAD_PAD), lambda h, i: (i + N_CTX // RB, h)),
                  pl.BlockSpec((T, HEAD_GROUP * HEAD_PAD), lambda h, i: (0, h)),
                  pl.BlockSpec((NRB, HEAD_GROUP * VT_ROWS, KV_TILE), lambda h, i: (0, h, 0))],
        out_specs=pl.BlockSpec((HEAD_GROUP * V_HEAD, RB), lambda h, i: (h, i)),
        out_shape=jax.ShapeDtypeStruct((HEADS * V_HEAD, SEQ), F32),
        scratch_shapes=[pltpu.VMEM((HEAD_GROUP, KV_STAGE, RB), F32),
                        pltpu.VMEM((HEAD_GROUP, KV_STAGE, RB), F32)],
        compiler_params=_cparams(("arbitrary", "arbitrary")),
        name="attention",
    )(q, k, vt)
    return jnp.concatenate([o_ctx, o_lat], axis=1)


MERGE_TN = 512


N_BRANCH = 4


MERGE_RB = 768


def _merge_kernel(h_ref, a_ref, b_ref, c_ref, dt_ref, *refs):
    wg_refs, bg_refs = refs[:N_BRANCH], refs[N_BRANCH:2 * N_BRANCH]
    wb_ref, y_ref, wg_bf = refs[2 * N_BRANCH:]

    @pl.when(pl.program_id(1) == 0)
    def _():
        for n in range(N_BRANCH):
            _cast_transposed(wg_refs[n], wg_bf.at[n])

    h = h_ref[...]
    branches = (a_ref[...], b_ref[...], c_ref[...].astype(BF16), dt_ref[...].T.astype(BF16))
    y = jnp.zeros((MERGE_RB, MERGE_TN), F32)
    for n, br in enumerate(branches):
        gate = jax.nn.sigmoid(_dot(h, wg_bf[n]) + bg_refs[n][...])
        y = y + gate * _dot(br, wb_ref[0, n])
    y_ref[...] = y.astype(BF16)


def merge(l, h, br_a, br_b, br_c, br_dt, w_in_t, bg, wb):
    row = lambda j, i: (i, 0)
    n_col = D // MERGE_TN
    gate_specs = [pl.BlockSpec((pl.Squeezed(), pl.Element(MERGE_TN), pl.Element(D)),
                               functools.partial(
                                   lambda j, i, n: (l, pl.multiple_of(COL_G + n * D + j * MERGE_TN, QK_ROPE), 0), n=n),
                               pipeline_mode=pl.Buffered(1))
                  for n in range(N_BRANCH)]
    bias_specs = [pl.BlockSpec((1, MERGE_TN), functools.partial(lambda j, i, n: (0, n * n_col + j), n=n))
                  for n in range(N_BRANCH)]
    return pl.pallas_call(
        _merge_kernel,
        grid=(n_col, T // MERGE_RB),
        in_specs=[pl.BlockSpec((MERGE_RB, D), row),
                  pl.BlockSpec((MERGE_RB, BW), row),
                  pl.BlockSpec((MERGE_RB, BW), row),
                  pl.BlockSpec((MERGE_RB, BW), row),
                  pl.BlockSpec((BW, MERGE_RB), lambda j, i: (0, i))]
                 + gate_specs + bias_specs
                 + [pl.BlockSpec((1, N_BRANCH, BW, MERGE_TN), lambda j, i: (l, 0, 0, j))],
        out_specs=pl.BlockSpec((MERGE_RB, MERGE_TN), lambda j, i: (i, j)),
        out_shape=jax.ShapeDtypeStruct((T, D), BF16),
        scratch_shapes=[pltpu.VMEM((N_BRANCH, D, MERGE_TN), BF16)],
        compiler_params=_cparams(("arbitrary", "arbitrary")),
        name="merge",
    )(h, br_a, br_b, br_c, br_dt, *([w_in_t] * N_BRANCH), *([bg] * N_BRANCH), wb)


ROUTER_W = 128


def _post_mix_kernel(y_ref, xc_ref, xl_ref, mod_ref, wo_ref, g_ref, b_ref, wr_ref, br_ref,
                     x1_ref, h2_ref, ids_ref, gates_ref):
    mix = _dot(y_ref[...], wo_ref[0])
    x1 = _ln0(ALPHA * _stream_block(xc_ref, xl_ref) + _mod_row(mod_ref, 2) * mix) * g_ref[...] + b_ref[...]
    x1_ref[...] = x1
    h2 = _ln0(x1) * (1.0 + _mod_row(mod_ref, 4)) + _mod_row(mod_ref, 3)
    _rows_to_slabs(h2, h2_ref)
    logits = _dot3(h2, wr_ref) + br_ref[...]
    lane = lax.broadcasted_iota(jnp.int32, (RB, ROUTER_W), 1).astype(F32)
    neg = jnp.float32(-jnp.inf)
    big = jnp.float32(ROUTER_W)
    gl = jnp.where(lane < N_GROUPS, logits, neg)
    gmax = jnp.max(gl, axis=-1, keepdims=True)
    g_idx = jnp.min(jnp.where(gl == gmax, lane, big), axis=-1, keepdims=True)
    g_p = 1.0 / jnp.sum(jnp.exp(gl - gmax), axis=-1, keepdims=True)
    e_lane = lane - N_GROUPS
    in_grp = jnp.logical_and(e_lane >= g_idx * EXP_PER_GROUP, e_lane < (g_idx + 1) * EXP_PER_GROUP)
    el = jnp.where(in_grp, logits, neg)
    v1 = jnp.max(el, axis=-1, keepdims=True)
    i1 = jnp.min(jnp.where(el == v1, lane, big), axis=-1, keepdims=True)
    el2 = jnp.where(lane == i1, neg, el)
    v2 = jnp.max(el2, axis=-1, keepdims=True)
    i2 = jnp.min(jnp.where(el2 == v2, lane, big), axis=-1, keepdims=True)
    e21 = jnp.exp(v2 - v1)
    p1 = 1.0 / (1.0 + e21)
    p2 = e21 * p1
    ids_ref[...] = jnp.where(lane == 0, i1 - N_GROUPS, jnp.where(lane == 1, i2 - N_GROUPS, 0.0)).astype(jnp.int32)
    gates_ref[...] = jnp.where(lane == 0, p1 * g_p, jnp.where(lane == 1, p2 * g_p, 0.0))


def post_mix(l, y, x_ctx, x_lat, mod, wo, ln_g, ln_b, wr, br):
    row = lambda i: (i, 0)
    const = lambda i: (0, 0)
    return pl.pallas_call(
        _post_mix_kernel,
        grid=(NRB,),
        in_specs=[pl.BlockSpec((RB, D), row)] + _stream_specs()
                 + [pl.BlockSpec((8, 6 * D), const),
                  pl.BlockSpec((1, D, D), lambda i: (l, 0, 0)),
                  pl.BlockSpec((1, D), const),
                  pl.BlockSpec((1, D), const),
                  pl.BlockSpec((2, D, ROUTER_W), lambda i: (0, 0, 0)),
                  pl.BlockSpec((1, ROUTER_W), const)],
        out_specs=[pl.BlockSpec((RB, D), row),
                   pl.BlockSpec((RB * ROW_SUB, LANE), row),
                   pl.BlockSpec((RB, ROUTER_W), row),
                   pl.BlockSpec((RB, ROUTER_W), row)],
        out_shape=[jax.ShapeDtypeStruct((T, D), F32),
                   jax.ShapeDtypeStruct((T * ROW_SUB, LANE), F32),
                   jax.ShapeDtypeStruct((T, ROUTER_W), jnp.int32),
                   jax.ShapeDtypeStruct((T, ROUTER_W), F32)],
        compiler_params=_cparams(("arbitrary",)),
        name="post_mix",
    )(y, x_ctx, x_lat, mod, wo, ln_g, ln_b, wr, br)


N_ROWS = -(-(T * TOP_K + N_EXPERTS * (MOE_BLOCK - 1)) // MOE_BLOCK) * MOE_BLOCK
N_BLK = N_ROWS // MOE_BLOCK


N_DMA_PRIORITIES = 2


GATHER_UNROLL = 8


def _row_gather(src_hbm, idx_ref, base, n, dst_ref, sem, first=0):
    def issue(t, _):
        for u in range(GATHER_UNROLL):
            r = first + t * GATHER_UNROLL + u
            src_row = pl.multiple_of(idx_ref[base + r] * ROW_SUB, ROW_SUB)
            dst_row = pl.multiple_of(r * ROW_SUB, ROW_SUB)
            pltpu.make_async_copy(src_hbm.at[pl.ds(src_row, ROW_SUB), :],
                                  dst_ref.at[pl.ds(dst_row, ROW_SUB), :], sem).start(priority=u % N_DMA_PRIORITIES)
        return 0
    lax.fori_loop(0, n // GATHER_UNROLL, issue, 0)


def _row_gather_wait(src_hbm, n, dst_ref, sem):
    pltpu.make_async_copy(src_hbm.at[pl.ds(0, n * ROW_SUB), :], dst_ref, sem).wait()


def _rows_from_slabs(buf_ref, start, n, stride):
    return jnp.concatenate([buf_ref[pl.ds(start + c, n, stride=stride), :] for c in range(ROW_SUB)], axis=-1)


def _rows_to_slabs(val, out_ref):
    n = val.shape[0]
    for c in range(ROW_SUB):
        out_ref[pl.ds(c, n, stride=ROW_SUB), :] = val[:, c * LANE:(c + 1) * LANE]


ITEM_BLKS = 3
ITEM_ROWS = ITEM_BLKS * MOE_BLOCK
N_ITEMS = (N_BLK + (ITEM_BLKS - 1) * N_EXPERTS) // ITEM_BLKS
EXP_CHUNK = 512
N_CHUNK = D_EXPERT // EXP_CHUNK
BLK_SLABS = MOE_BLOCK * ROW_SUB


WAIT_ROWS = 8


def _expert_kernel(item_exp_ref, item_blk0_ref, item_nblk_ref, item_cnt_ref, n_used_ref, row_tok_ref, h_hbm, w1_ref,
                   w3_ref, w2_ref, y_hbm, xslab, yacc, stage, w1c, w3c, w2c, sem_x, sem_y):
    i, j = pl.program_id(0), pl.program_id(1)
    nblk = item_nblk_ref[i]
    blk0 = item_blk0_ref[i]
    xs = xslab.at[i % 2]

    def gather_item(item):
        _row_gather(h_hbm, row_tok_ref, item_blk0_ref[item] * MOE_BLOCK, item_cnt_ref[item], xslab.at[item % 2],
                    sem_x)

    @pl.when(jnp.logical_and(i == 0, j == 0))
    def _():
        xslab[...] = jnp.zeros(xslab.shape, F32)
        gather_item(0)

    @pl.when(nblk > 0)
    def _():
        w1c[...] = w1_ref[0, 0].astype(BF16)
        w3c[...] = w3_ref[0, 0].astype(BF16)
        w2c[...] = w2_ref[0, 0].astype(BF16)

    last = j == N_CHUNK - 1

    def rows_of(b):
        return pl.ds(pl.multiple_of(b * MOE_BLOCK, MOE_BLOCK), MOE_BLOCK)

    def partial_out(xb):
        a = _dot(xb, w1c[...])
        g = _dot(xb, w3c[...])
        return _dot((a * jax.nn.sigmoid(a) * g).astype(BF16), w2c[...])

    def y_copy(b, slot):
        dst = pl.multiple_of((blk0 + b) * BLK_SLABS, BLK_SLABS)
        return pltpu.make_async_copy(stage.at[slot], y_hbm.at[pl.ds(dst, BLK_SLABS), :], sem_y.at[slot])

    def x_block(b):
        return _rows_from_slabs(xs, b * BLK_SLABS, MOE_BLOCK, ROW_SUB).astype(BF16)

    @pl.when(j == 0)
    def _():
        def wait_rows(t, _):
            pltpu.make_async_copy(h_hbm.at[pl.ds(0, WAIT_ROWS * ROW_SUB), :],
                                  xs.at[pl.ds(0, WAIT_ROWS * ROW_SUB), :], sem_x).wait()
            return 0
        lax.fori_loop(0, item_cnt_ref[i] // WAIT_ROWS, wait_rows, 0)

        @pl.when(i + 1 < N_ITEMS)
        def _():
            gather_item(i + 1)

        def first_block(b, _):
            yacc[rows_of(b), :] = partial_out(x_block(b))
            return 0
        lax.fori_loop(0, nblk, first_block, 0)

    @pl.when(jnp.logical_and(j > 0, jnp.logical_not(last)))
    def _():
        def mid_block(b, _):
            yacc[rows_of(b), :] += partial_out(x_block(b))
            return 0
        lax.fori_loop(0, nblk, mid_block, 0)

    @pl.when(last)
    def _():
        def last_block(b, _):
            slot = b % 2

            @pl.when(b >= 2)
            def _():
                y_copy(b - 2, slot).wait()
            _rows_to_slabs(yacc[rows_of(b), :] + partial_out(x_block(b)), stage.at[slot])
            y_copy(b, slot).start()
            return 0
        lax.fori_loop(0, nblk, last_block, 0)

        for back in (1, 2):
            @pl.when(nblk >= back)
            def _():
                b = nblk - back
                y_copy(b, b % 2).wait()

    @pl.when(jnp.logical_and(i == N_ITEMS - 1, last))
    def _():
        stage[0] = jnp.zeros((BLK_SLABS, LANE), F32)

        def fill(b, _):
            dst = pl.multiple_of(b * BLK_SLABS, BLK_SLABS)
            cp = pltpu.make_async_copy(stage.at[0], y_hbm.at[pl.ds(dst, BLK_SLABS), :], sem_y.at[0])
            cp.start()
            cp.wait()
            return 0
        lax.fori_loop(n_used_ref[0], N_BLK, fill, 0)


def experts(l, item_exp, item_blk0, item_nblk, item_cnt, n_used, row_tok, h2_slabs, w1, w3, w2):
    wspec = lambda blk, imap: pl.BlockSpec(blk, imap)
    grid_spec = pltpu.PrefetchScalarGridSpec(
        num_scalar_prefetch=6,
        grid=(N_ITEMS, N_CHUNK),
        in_specs=[pl.BlockSpec(memory_space=pl.ANY),
                  wspec((1, 1, D, EXP_CHUNK), lambda i, j, ie, *_: (l, ie[i], 0, j)),
                  wspec((1, 1, D, EXP_CHUNK), lambda i, j, ie, *_: (l, ie[i], 0, j)),
                  wspec((1, 1, EXP_CHUNK, D), lambda i, j, ie, *_: (l, ie[i], j, 0))],
        out_specs=pl.BlockSpec(memory_space=pl.ANY),
        scratch_shapes=[pltpu.VMEM((2, ITEM_BLKS * BLK_SLABS, LANE), F32),
                        pltpu.VMEM((ITEM_ROWS, D), F32),
                        pltpu.VMEM((2, BLK_SLABS, LANE), F32),
                        pltpu.VMEM((D, EXP_CHUNK), BF16),
                        pltpu.VMEM((D, EXP_CHUNK), BF16),
                        pltpu.VMEM((EXP_CHUNK, D), BF16),
                        pltpu.SemaphoreType.DMA(()),
                        pltpu.SemaphoreType.DMA((2,))],
    )
    return pl.pallas_call(
        _expert_kernel,
        grid_spec=grid_spec,
        out_shape=jax.ShapeDtypeStruct((N_ROWS * ROW_SUB, LANE), F32),
        compiler_params=_cparams(("arbitrary", "arbitrary")),
        name="experts",
    )(item_exp, item_blk0, item_nblk, item_cnt, n_used, row_tok, h2_slabs, w1, w3, w2)


COMBINE_SUB = 64


def _combine_kernel(pos_ref, y_hbm, gates_ref, x1_ref, mod_ref, g_ref, b_ref, nmod_ref,
                    x2c_ref, x2l_ref, hn_ref, ybuf, sems):
    i = pl.program_id(0)
    slot = i % 2
    n = RB * TOP_K
    n_sub = RB // COMBINE_SUB

    @pl.when(i == 0)
    def _():
        _row_gather(y_hbm, pos_ref, 0, n, ybuf.at[0], sems.at[0])

    _row_gather_wait(y_hbm, n, ybuf.at[slot], sems.at[slot])
    shift5, shift0, scale1 = _mod_row(mod_ref, 5), _mod_row(nmod_ref, 0), 1.0 + _mod_row(nmod_ref, 1)
    for s in range(n_sub):
        @pl.when(i + 1 < NRB)
        def _():
            _row_gather(y_hbm, pos_ref, (i + 1) * n, n // n_sub, ybuf.at[1 - slot], sems.at[1 - slot],
                        first=s * (n // n_sub))

        rows = slice(s * COMBINE_SUB, (s + 1) * COMBINE_SUB)
        base = s * COMBINE_SUB * TOP_K * ROW_SUB
        y0 = _rows_from_slabs(ybuf.at[slot], base, COMBINE_SUB, TOP_K * ROW_SUB)
        y1 = _rows_from_slabs(ybuf.at[slot], base + ROW_SUB, COMBINE_SUB, TOP_K * ROW_SUB)
        gates = gates_ref[rows, :]
        f = gates[:, 0:1] * y0 + gates[:, 1:2] * y1
        x2 = _ln0(ALPHA * x1_ref[rows, :] + shift5 * f) * g_ref[...] + b_ref[...]
        x2l_ref[rows, :] = x2

        @pl.when(i == 0)
        def _():
            x2c_ref[rows, :] = x2

        hn_ref[rows, :] = (_ln0(x2) * scale1 + shift0).astype(BF16)


def combine(pos, y_slabs, gates, x1, mod, ln_g, ln_b, next_mod):
    row = lambda i, p: (i, 0)
    const = lambda i, p: (0, 0)
    grid_spec = pltpu.PrefetchScalarGridSpec(
        num_scalar_prefetch=1,
        grid=(NRB,),
        in_specs=[pl.BlockSpec(memory_space=pl.ANY),
                  pl.BlockSpec((RB, ROUTER_W), row),
                  pl.BlockSpec((RB, D), row),
                  pl.BlockSpec((8, 6 * D), const),
                  pl.BlockSpec((1, D), const),
                  pl.BlockSpec((1, D), const),
                  pl.BlockSpec((8, 6 * D), const)],
        out_specs=[pl.BlockSpec((RB, D), const),
                   pl.BlockSpec((RB, D), lambda i, p: (jnp.maximum(i - 1, 0), 0)),
                   pl.BlockSpec((RB, D), row)],
        scratch_shapes=[pltpu.VMEM((2, RB * TOP_K * ROW_SUB, LANE), F32),
                        pltpu.SemaphoreType.DMA((2,))],
    )
    return pl.pallas_call(
        _combine_kernel,
        grid_spec=grid_spec,
        out_shape=[jax.ShapeDtypeStruct((N_CTX, D), F32), jax.ShapeDtypeStruct((SEQ, D), F32),
                   jax.ShapeDtypeStruct((T, D), BF16)],
        compiler_params=_cparams(("arbitrary",)),
        name="combine",
    )(pos, y_slabs, gates, x1, mod, ln_g, ln_b, next_mod)


def _routing_tables(ids):
    expert = ids[:, :TOP_K].reshape(-1)
    n = expert.shape[0]
    onehot = (expert[:, None] == jnp.arange(N_EXPERTS, dtype=jnp.int32)[None, :]).astype(jnp.int32)
    csum = jnp.cumsum(onehot, axis=0)
    counts = csum[-1]
    rank = jnp.sum(onehot * csum, axis=1) - 1
    nb = (counts + MOE_BLOCK - 1) // MOE_BLOCK
    bstart = jnp.cumsum(nb) - nb
    pos = (jnp.sum(onehot * (bstart * MOE_BLOCK)[None, :], axis=1) + rank).astype(jnp.int32)
    tok = jnp.arange(n, dtype=jnp.int32) // TOP_K
    row_tok = jnp.zeros((N_ROWS,), jnp.int32).at[pos].set(tok)
    n_it = (nb + ITEM_BLKS - 1) // ITEM_BLKS
    it_end = jnp.cumsum(n_it)
    it_start = it_end - n_it
    idx = jnp.arange(N_ITEMS, dtype=jnp.int32)
    e_of = jnp.minimum(jnp.sum(it_end[None, :] <= idx[:, None], axis=1), N_EXPERTS - 1).astype(jnp.int32)
    k = idx - it_start[e_of]
    valid = idx < it_end[-1]
    item_exp = e_of
    item_blk0 = jnp.where(valid, bstart[e_of] + ITEM_BLKS * k, 0).astype(jnp.int32)
    item_nblk = jnp.where(valid, jnp.clip(nb[e_of] - ITEM_BLKS * k, 0, ITEM_BLKS), 0).astype(jnp.int32)
    item_cnt = jnp.where(valid, jnp.clip(counts[e_of] - ITEM_ROWS * k, 0, ITEM_ROWS), 0)
    item_cnt = ((item_cnt + GATHER_UNROLL - 1) // GATHER_UNROLL * GATHER_UNROLL).astype(jnp.int32)
    n_used = jnp.sum(nb).astype(jnp.int32)[None]
    return item_exp, item_blk0, item_nblk, item_cnt, n_used, row_tok, pos


def _rope_tables():
    rows = SEQ // GRID_W
    row = jnp.repeat(jnp.arange(rows, dtype=F32), GRID_W)
    col = jnp.tile(jnp.arange(GRID_W, dtype=F32), rows)
    half = QK_ROPE // 2
    inv = ROPE_THETA ** (-jnp.arange(0, half, 2, dtype=F32) / half)
    ar = row[:, None] * inv
    ac = col[:, None] * inv
    ang = jnp.concatenate([ar, ar, ac, ac], -1)
    cos = jnp.concatenate([jnp.ones((N_CTX, QK_ROPE), F32), jnp.cos(ang)], 0)
    sin = jnp.concatenate([jnp.zeros((N_CTX, QK_ROPE), F32), jnp.sin(ang)], 0)
    one = jnp.ones((T, QK_NOPE), F32)
    zero_n = jnp.zeros((T, QK_NOPE), F32)
    zero_p = jnp.zeros((T, HEAD_PAD - QK_NOPE - QK_ROPE), F32)
    tq_c = jnp.concatenate([one, cos, zero_p], -1)
    tq_s = jnp.concatenate([zero_n, sin, zero_p], -1)
    tk = jnp.concatenate([cos, sin, jnp.zeros((T, LANE - 2 * QK_ROPE), F32)], -1)
    return tq_c, tq_s, tk


_ROT_SRC = np.concatenate([np.arange(8, 16), np.arange(0, 8), np.arange(24, 32), np.arange(16, 24)])
_ROT_SIGN = np.concatenate([-np.ones(8), np.ones(8), -np.ones(8), np.ones(8)]).astype(np.float32)


def _rot_cols(w):
    return w[..., _ROT_SRC] * _ROT_SIGN


def _layer_weights(l, w_in_t, b_in, d_w_uq, d_w_uk, d_w_uv, r_w_grp, r_b_grp, r_w_exp, r_b_exp, a_b_s, b_w_dw):
    bi = b_in[l]
    p = {}
    w_kr, b_kr = w_in_t[l, COL_KR:COL_G, :].T, bi[COL_KR:COL_G]
    padw = LANE - 2 * QK_ROPE
    p["wkr"] = jnp.concatenate([w_kr, _rot_cols(w_kr), jnp.zeros((D, padw), F32)], -1).astype(BF16)
    p["bkr"] = jnp.concatenate([b_kr, _rot_cols(b_kr), jnp.zeros((padw,), F32)])[None]
    wq = d_w_uq[l].reshape(Q_LORA, HEADS, QK_NOPE + QK_ROPE)
    zpad = jnp.zeros((Q_LORA, HEADS, HEAD_PAD - QK_NOPE - QK_ROPE), F32)
    p["wq"] = jnp.concatenate([wq, zpad], -1).reshape(Q_LORA, HEADS * HEAD_PAD).astype(BF16)
    p["wqr"] = jnp.concatenate([jnp.zeros((Q_LORA, HEADS, QK_NOPE), F32), _rot_cols(wq[..., QK_NOPE:]), zpad],
                               -1).reshape(Q_LORA, HEADS * HEAD_PAD).astype(BF16)
    wk = d_w_uk[l].reshape(KV_LORA, HEADS, QK_NOPE)
    p["wk"] = jnp.concatenate([wk, jnp.zeros((KV_LORA, HEADS, HEAD_PAD - QK_NOPE), F32)],
                              -1).reshape(KV_LORA, HEADS * HEAD_PAD).astype(BF16)
    p["wvt"] = d_w_uv[l].T.astype(BF16)
    e = np.zeros((LANE, HEADS, HEAD_PAD), np.float32)
    for j in range(QK_ROPE):
        e[j, :, QK_NOPE + j] = 1.0
        e[QK_ROPE + j, :, QK_NOPE + j] = 1.0
    p["e"] = jnp.asarray(e.reshape(LANE, HEADS * HEAD_PAD), dtype=BF16)
    p["bg"] = bi[None, COL_G:]
    padr = ROUTER_W - N_GROUPS - N_EXPERTS
    p["wr"] = _hilo(jnp.concatenate([r_w_grp[l], r_w_exp[l], jnp.zeros((D, padr), F32)], -1))
    p["br"] = jnp.concatenate([r_b_grp[l], r_b_exp[l], jnp.zeros((padr,), F32)])[None]
    p["bs"] = jnp.broadcast_to(a_b_s[l][:, :, None], (A_GROUPS, CHUNK, CHUNK))
    p["wdw"] = jnp.concatenate([b_w_dw[l], jnp.zeros((1, BW), F32)], 0)
    return p


def kernel(x, c, ctx, c_ctx, w_ada, b_ada, w_in, b_in, a_ln_g, a_ln_b, a_w_s, a_b_s, b_w_dw, b_b_dw, b_ln_g,
           b_ln_b, d_q_g, d_w_uq, d_kv_g, d_w_uk, d_w_uv, w_branch, w_o, ln1_g, ln1_b, r_w_grp, r_b_grp,
           r_w_exp, r_b_exp, e_w1, e_w3, e_w2, ln2_g, ln2_b):
    assert x.shape == (1, SEQ, D) and ctx.shape == (1, N_CTX, D)
    x_ctx, x_lat = ctx[0], x[0]
    w_in_t = jnp.transpose(w_in, (0, 2, 1))
    c_rep = jnp.broadcast_to(jnp.concatenate([c, c_ctx[None]], axis=0)[:, :, None], (N_COND, D, LANE))
    mods = ada_mod(c_rep, w_ada, b_ada)
    tq_c, tq_s, tk = _rope_tables()
    fc = _fourier_consts()
    b_in3 = b_in.reshape(DEPTH, 1, -1)
    w_br, w_ob = w_branch.astype(BF16), w_o.astype(BF16)

    h = modln(x_ctx, x_lat, mods[0])
    for l in range(DEPTH):
        p = _layer_weights(l, w_in_t, b_in, d_w_uq, d_w_uk, d_w_uv, r_w_grp, r_b_grp, r_w_exp, r_b_exp, a_b_s,
                           b_w_dw)
        mod = mods[l]
        br_a = mix_a(l, h, w_in_t, b_in3, a_ln_g[l][None], a_ln_b[l][None], a_w_s[l].astype(BF16), p["bs"])
        br_b = conv_ln(glu(l, h, w_in_t, b_in3), p["wdw"], b_b_dw[l][None], b_ln_g[l][None], b_ln_b[l][None])
        br_c = fourier_mix(l, h, w_in_t, b_in3, fc)
        q, k, vt = mla_proj(l, h, w_in_t, b_in3, p["wkr"], p["bkr"], d_q_g[l][None], d_kv_g[l][None], p["wq"],
                            p["wqr"], p["wk"], p["wvt"], p["e"], tq_c, tq_s, tk)
        br_dt = attention(q, k, vt)
        y = merge(l, h, br_a, br_b, br_c, br_dt, w_in_t, p["bg"], w_br)
        x1, h2, ids, gates = post_mix(l, y, x_ctx, x_lat, mod, w_ob, ln1_g[l][None], ln1_b[l][None], p["wr"],
                                      p["br"])
        item_exp, item_blk0, item_nblk, item_cnt, n_used, row_tok, pos = _routing_tables(ids)
        ys = experts(l, item_exp, item_blk0, item_nblk, item_cnt, n_used, row_tok, h2, e_w1, e_w3, e_w2)
        x_ctx, x_lat, h = combine(pos, ys, gates, x1, mod, ln2_g[l][None], ln2_b[l][None],
                                  mods[min(l + 1, DEPTH - 1)])
    return x_lat[None]
```

```python
import functools
import math

import numpy as np
import jax
import jax.numpy as jnp
from jax import lax
from jax.experimental import pallas as pl
from jax.experimental.pallas import tpu as pltpu

F32 = jnp.float32
BF16 = jnp.bfloat16

D = 2048
SEQ = 8192
N_CTX = 256
T = N_CTX + SEQ
DEPTH = 2
GRID_W = 64
BW = 512
CHUNK = 128
A_GROUPS = 4
CONV_W = 31
F_GROUPS = 4
QK_NOPE = 64
QK_ROPE = 32
V_HEAD = 64
HEADS = 8
Q_LORA = 512
KV_LORA = 256
ROPE_THETA = 10000.0
MLA_SCALE = (QK_NOPE + QK_ROPE) ** -0.5
N_GROUPS = 4
EXP_PER_GROUP = 8
N_EXPERTS = 32
TOP_K = 2
D_EXPERT = 1024
MOE_BLOCK = 256
COL_A = 0
COL_B = 1024
COL_C = 2048
COL_D = 2560
COL_KV = 3072
COL_KR = 3328
COL_G = 3360
ALPHA = (2 * DEPTH) ** 0.25
LN_EPS = 1e-6

LANE = 128
HEAD_PAD = 128
RB = 256
NRB = T // RB
FFT_N1 = 64
FFT_N2 = 128
ROW_SUB = D // LANE
V7X_VMEM_BYTES = 64 * 1024 * 1024
VMEM_LIMIT = V7X_VMEM_BYTES * 7 // 8
BF16_BITS_MASK = -(1 << 16)


def _cparams(sem):
    return pltpu.CompilerParams(dimension_semantics=sem, vmem_limit_bytes=VMEM_LIMIT)


def _dot(a, b):
    return jnp.dot(a, b, preferred_element_type=F32)


def _split(x):
    hi = lax.bitcast_convert_type(lax.bitcast_convert_type(x, jnp.int32) & jnp.int32(BF16_BITS_MASK), F32)
    return hi.astype(BF16), (x - hi).astype(BF16)


def _hilo(x):
    return jnp.stack(_split(jnp.asarray(x, F32)))


def _dot3(a, t_ref):
    a_hi, a_lo = _split(a)
    return _dot(a_hi, t_ref[0]) + (_dot(a_hi, t_ref[1]) + _dot(a_lo, t_ref[0]))


def _dot3_t(t_ref, b):
    b_hi, b_lo = _split(b)
    return _dot(t_ref[0], b_hi) + (_dot(t_ref[1], b_hi) + _dot(t_ref[0], b_lo))


def _ln0(x):
    mu = jnp.mean(x, axis=-1, keepdims=True)
    xc = x - mu
    var = jnp.mean(xc * xc, axis=-1, keepdims=True)
    return xc * lax.rsqrt(var + LN_EPS)


def _mod_row(mod_ref, q):
    lat = mod_ref[0:1, q * D:(q + 1) * D]
    ctx = mod_ref[1:2, q * D:(q + 1) * D]
    return jnp.where(pl.program_id(0) == 0, ctx, lat)


ADA_TN = 1536
ADA_TK = 256
N_COND = 2


def _ada_kernel(c_ref, w_ref, b_ref, o_ref):
    accs = [b_ref[0] for _ in range(N_COND)]
    for k0 in range(0, D, ADA_TK):
        w = w_ref[0, k0:k0 + ADA_TK, :]
        for r in range(N_COND):
            s = c_ref[r, k0:k0 + ADA_TK, :]
            s = s * jax.nn.sigmoid(s)
            s = jnp.concatenate([s] * (ADA_TN // LANE), axis=-1)
            accs[r] = accs[r] + jnp.sum(w * s, axis=0, keepdims=True)
    o_ref[0] = jnp.concatenate(accs + [jnp.zeros((8 - N_COND, ADA_TN), F32)], axis=0)


def ada_mod(c_rep, w_ada, b_ada):
    return pl.pallas_call(
        _ada_kernel,
        grid=(DEPTH, 6 * D // ADA_TN),
        in_specs=[pl.BlockSpec((N_COND, D, LANE), lambda l, j: (0, 0, 0)),
                  pl.BlockSpec((1, D, ADA_TN), lambda l, j: (l, 0, j)),
                  pl.BlockSpec((1, 1, ADA_TN), lambda l, j: (l, 0, j))],
        out_specs=pl.BlockSpec((1, 8, ADA_TN), lambda l, j: (l, 0, j)),
        out_shape=jax.ShapeDtypeStruct((DEPTH, 8, 6 * D), F32),
        compiler_params=_cparams(("arbitrary", "arbitrary")),
        name="ada_mod",
    )(c_rep, w_ada, b_ada.reshape(DEPTH, 1, 6 * D))


def _stream_specs():
    return [pl.BlockSpec((RB, D), lambda i, *_: (0, 0)),
            pl.BlockSpec((RB, D), lambda i, *_: (jnp.maximum(i - 1, 0), 0))]


def _stream_block(ctx_ref, lat_ref):
    return jnp.where(pl.program_id(0) == 0, ctx_ref[...], lat_ref[...])


def _modln_kernel(xc_ref, xl_ref, mod_ref, h_ref):
    h = _ln0(_stream_block(xc_ref, xl_ref)) * (1.0 + _mod_row(mod_ref, 1)) + _mod_row(mod_ref, 0)
    h_ref[...] = h.astype(BF16)


def modln(x_ctx, x_lat, mod):
    return pl.pallas_call(
        _modln_kernel,
        grid=(NRB,),
        in_specs=_stream_specs() + [pl.BlockSpec((8, 6 * D), lambda i: (0, 0))],
        out_specs=pl.BlockSpec((RB, D), lambda i: (i, 0)),
        out_shape=jax.ShapeDtypeStruct((T, D), BF16),
        compiler_params=_cparams(("arbitrary",)),
        name="modln",
    )(x_ctx, x_lat, mod)


def _w_in_spec(l, width, col):
    return pl.BlockSpec((1, width, D), lambda i: (l, col, 0))


def _b_in_spec(l, width, col):
    return pl.BlockSpec((1, 1, width), lambda i: (l, 0, col))


CAST_CHUNK = 256


def _cast_transposed(w_ref, wbf_ref):
    width = w_ref.shape[0]
    for c0 in range(0, width, CAST_CHUNK):
        wbf_ref[:, c0:c0 + CAST_CHUNK] = w_ref[c0:c0 + CAST_CHUNK, :].T.astype(BF16)


def _cast_once(w_ref, wbf_ref):
    @pl.when(pl.program_id(0) == 0)
    def _():
        _cast_transposed(w_ref.at[0], wbf_ref)


def _mix_a_kernel(h_ref, w_ref, b_ref, g_ref, bb_ref, ws_ref, bs_ref, o_ref, wbf_ref):
    _cast_once(w_ref, wbf_ref)
    uv = jax.nn.gelu(_dot(h_ref[...], wbf_ref[...]) + b_ref[0])
    u = uv[:, :BW]
    v = _ln0(uv[:, BW:]) * g_ref[...] + bb_ref[...]
    v = v.astype(BF16)
    gw = BW // A_GROUPS
    for ch in range(RB // CHUNK):
        rows = slice(ch * CHUNK, (ch + 1) * CHUNK)
        parts = [_dot(ws_ref[g], v[rows, g * gw:(g + 1) * gw]) + bs_ref[g] for g in range(A_GROUPS)]
        mixed = jnp.concatenate(parts, axis=-1)
        o_ref[rows, :] = (u[rows, :] * mixed).astype(BF16)


def mix_a(l, h, w_in, b_in, ln_g, ln_b, ws, bs):
    return pl.pallas_call(
        _mix_a_kernel,
        grid=(NRB,),
        in_specs=[pl.BlockSpec((RB, D), lambda i: (i, 0)),
                  _w_in_spec(l, 2 * BW, COL_A // (2 * BW)),
                  _b_in_spec(l, 2 * BW, COL_A // (2 * BW)),
                  pl.BlockSpec((1, BW), lambda i: (0, 0)),
                  pl.BlockSpec((1, BW), lambda i: (0, 0)),
                  pl.BlockSpec((A_GROUPS, CHUNK, CHUNK), lambda i: (0, 0, 0)),
                  pl.BlockSpec((A_GROUPS, CHUNK, CHUNK), lambda i: (0, 0, 0))],
        out_specs=pl.BlockSpec((RB, BW), lambda i: (i, 0)),
        out_shape=jax.ShapeDtypeStruct((T, BW), BF16),
        scratch_shapes=[pltpu.VMEM((D, 2 * BW), BF16)],
        compiler_params=_cparams(("arbitrary",)),
        name="mix_a",
    )(h, w_in, b_in, ln_g, ln_b, ws, bs)


def _glu_kernel(h_ref, w_ref, b_ref, o_ref, wbf_ref):
    _cast_once(w_ref, wbf_ref)
    ab = _dot(h_ref[...], wbf_ref[...]) + b_ref[0]
    o_ref[...] = ab[:, :BW] * jax.nn.sigmoid(ab[:, BW:])


def glu(l, h, w_in, b_in):
    return pl.pallas_call(
        _glu_kernel,
        grid=(NRB,),
        in_specs=[pl.BlockSpec((RB, D), lambda i: (i, 0)),
                  _w_in_spec(l, 2 * BW, COL_B // (2 * BW)),
                  _b_in_spec(l, 2 * BW, COL_B // (2 * BW))],
        out_specs=pl.BlockSpec((RB, BW), lambda i: (i, 0)),
        out_shape=jax.ShapeDtypeStruct((T, BW), F32),
        scratch_shapes=[pltpu.VMEM((D, 2 * BW), BF16)],
        compiler_params=_cparams(("arbitrary",)),
        name="glu",
    )(h, w_in, b_in)


CONV_HALO = 16


def _conv_kernel(prev_ref, cur_ref, next_ref, w_ref, b_ref, g_ref, bb_ref, o_ref, ext_ref):
    i = pl.program_id(0)
    has_prev = i >= 2
    has_next = jnp.logical_and(i >= 1, i < NRB - 1)
    ext_ref[0:CONV_HALO, :] = jnp.where(has_prev, prev_ref[RB - CONV_HALO:RB, :], 0.0)
    ext_ref[CONV_HALO:CONV_HALO + RB, :] = cur_ref[...]
    ext_ref[CONV_HALO + RB:2 * CONV_HALO + RB, :] = jnp.where(has_next, next_ref[0:CONV_HALO, :], 0.0)
    off = CONV_HALO - CONV_W // 2
    acc = jnp.zeros((RB, BW), F32)
    for k in range(CONV_W):
        acc = acc + ext_ref[off + k:off + k + RB, :] * w_ref[k:k + 1, :]
    y = _ln0(acc + b_ref[...]) * g_ref[...] + bb_ref[...]
    o_ref[...] = (y * jax.nn.sigmoid(y)).astype(BF16)


def conv_ln(y, w_dw, b_dw, ln_g, ln_b):
    return pl.pallas_call(
        _conv_kernel,
        grid=(NRB,),
        in_specs=[pl.BlockSpec((RB, BW), lambda i: (jnp.maximum(i - 1, 0), 0)),
                  pl.BlockSpec((RB, BW), lambda i: (i, 0)),
                  pl.BlockSpec((RB, BW), lambda i: (jnp.minimum(i + 1, NRB - 1), 0)),
                  pl.BlockSpec((CONV_W + 1, BW), lambda i: (0, 0)),
                  pl.BlockSpec((1, BW), lambda i: (0, 0)),
                  pl.BlockSpec((1, BW), lambda i: (0, 0)),
                  pl.BlockSpec((1, BW), lambda i: (0, 0))],
        out_specs=pl.BlockSpec((RB, BW), lambda i: (i, 0)),
        out_shape=jax.ShapeDtypeStruct((T, BW), BF16),
        scratch_shapes=[pltpu.VMEM((RB + 2 * CONV_HALO, BW), F32)],
        compiler_params=_cparams(("arbitrary",)),
        name="conv_ln",
    )(y, y, y, w_dw, b_dw, ln_g, ln_b)


def _dft_tables(n):
    k = np.arange(n, dtype=np.int64)
    ang = 2.0 * np.pi * ((k[:, None] * k[None, :]) % n).astype(np.float64) / n
    return np.cos(ang), np.sin(ang)


def _fproj_kernel(h_ref, w_ref, b_ref, cs_ref, ar_ref, ai_ref, wbf_ref):
    _cast_once(w_ref, wbf_ref)
    z = _dot(h_ref[...], wbf_ref[...]) + b_ref[0]
    a = _dot3(z, cs_ref)
    ar_ref[...] = a[:, :BW]
    ai_ref[...] = a[:, BW:]


def fourier_proj(l, h, w_in, b_in, cs):
    return pl.pallas_call(
        _fproj_kernel,
        grid=(NRB,),
        in_specs=[pl.BlockSpec((RB, D), lambda i: (i, 0)),
                  _w_in_spec(l, BW, COL_C // BW),
                  _b_in_spec(l, BW, COL_C // BW),
                  pl.BlockSpec((2, BW, 2 * BW), lambda i: (0, 0, 0))],
        out_specs=[pl.BlockSpec((RB, BW), lambda i: (i, 0)),
                   pl.BlockSpec((RB, BW), lambda i: (i, 0))],
        out_shape=[jax.ShapeDtypeStruct((T, BW), F32), jax.ShapeDtypeStruct((T, BW), F32)],
        scratch_shapes=[pltpu.VMEM((D, BW), BF16)],
        compiler_params=_cparams(("arbitrary",)),
        name="fourier_proj",
    )(h, w_in, b_in, cs)


FFT_TILE_N2 = 4


def _fft1_kernel(ar_ref, ai_ref, m_ref, ct_ref, st_ref, br_ref, bi_ref):
    x = jnp.concatenate([ar_ref[...], ai_ref[...]], axis=0)
    b = _dot3_t(m_ref, x)
    b_re, b_im = b[:FFT_N1], b[FFT_N1:]
    reps = BW // LANE
    for j in range(FFT_TILE_N2):
        ct = jnp.concatenate([ct_ref[j]] * reps, axis=-1)
        st = jnp.concatenate([st_ref[j]] * reps, axis=-1)
        lanes = slice(j * BW, (j + 1) * BW)
        br_ref[:, lanes] = b_re[:, lanes] * ct + b_im[:, lanes] * st
        bi_ref[:, lanes] = b_im[:, lanes] * ct - b_re[:, lanes] * st


def fft_stage1(ar2d, ai2d, m1, ct, st):
    tn = FFT_TILE_N2 * BW
    return pl.pallas_call(
        _fft1_kernel,
        grid=(FFT_N2 // FFT_TILE_N2,),
        in_specs=[pl.BlockSpec((FFT_N1, tn), lambda j: (0, j)),
                  pl.BlockSpec((FFT_N1, tn), lambda j: (0, j)),
                  pl.BlockSpec((2, 2 * FFT_N1, 2 * FFT_N1), lambda j: (0, 0, 0)),
                  pl.BlockSpec((FFT_TILE_N2, FFT_N1, LANE), lambda j: (j, 0, 0)),
                  pl.BlockSpec((FFT_TILE_N2, FFT_N1, LANE), lambda j: (j, 0, 0))],
        out_specs=[pl.BlockSpec((FFT_N1, tn), lambda j: (0, j)),
                   pl.BlockSpec((FFT_N1, tn), lambda j: (0, j))],
        out_shape=[jax.ShapeDtypeStruct((FFT_N1, FFT_N2 * BW), F32)] * 2,
        compiler_params=_cparams(("arbitrary",)),
        name="fft_stage1",
    )(ar2d, ai2d, m1, ct, st)


FFT_TILE_K1 = 8


def _fft2_kernel(br_ref, bi_ref, c_ref, s_ref, p_ref, o_ref, *, scale):
    ys = [(_dot3_t(c_ref, br_ref[a]) + _dot3_t(s_ref, bi_ref[a])) * scale for a in range(FFT_TILE_K1)]
    y = jnp.concatenate(ys, axis=0).astype(BF16)
    y = _dot(p_ref[...], y)
    o_ref[...] = y.reshape(FFT_N2, FFT_TILE_K1, BW)


def fft_stage2(br3d, bi3d, c128, s128, perm, scale):
    return pl.pallas_call(
        functools.partial(_fft2_kernel, scale=scale),
        grid=(FFT_N1 // FFT_TILE_K1,),
        in_specs=[pl.BlockSpec((FFT_TILE_K1, FFT_N2, BW), lambda a: (a, 0, 0)),
                  pl.BlockSpec((FFT_TILE_K1, FFT_N2, BW), lambda a: (a, 0, 0)),
                  pl.BlockSpec((2, FFT_N2, FFT_N2), lambda a: (0, 0, 0)),
                  pl.BlockSpec((2, FFT_N2, FFT_N2), lambda a: (0, 0, 0)),
                  pl.BlockSpec((FFT_TILE_K1 * FFT_N2, FFT_TILE_K1 * FFT_N2), lambda a: (0, 0))],
        out_specs=pl.BlockSpec((FFT_N2, FFT_TILE_K1, BW), lambda a: (0, a, 0)),
        out_shape=jax.ShapeDtypeStruct((FFT_N2, FFT_N1, BW), F32),
        compiler_params=_cparams(("arbitrary",)),
        name="fft_stage2",
    )(br3d, bi3d, c128, s128, perm)


def _dft_ctx_kernel(ar_ref, ai_ref, c_ref, s_ref, o_ref, *, scale):
    o_ref[...] = (_dot3_t(c_ref, ar_ref[...]) + _dot3_t(s_ref, ai_ref[...])) * scale


def dft_ctx(ar, ai, c, s, scale):
    return pl.pallas_call(
        functools.partial(_dft_ctx_kernel, scale=scale),
        grid=(1,),
        in_specs=[pl.BlockSpec((N_CTX, BW), lambda i: (0, 0)),
                  pl.BlockSpec((N_CTX, BW), lambda i: (0, 0)),
                  pl.BlockSpec((2, N_CTX, N_CTX), lambda i: (0, 0, 0)),
                  pl.BlockSpec((2, N_CTX, N_CTX), lambda i: (0, 0, 0))],
        out_specs=pl.BlockSpec((N_CTX, BW), lambda i: (0, 0)),
        out_shape=jax.ShapeDtypeStruct((N_CTX, BW), F32),
        compiler_params=_cparams(("arbitrary",)),
        name="dft_ctx",
    )(ar, ai, c, s)


def _fourier_consts():
    c64, s64 = _dft_tables(FFT_N1)
    m1 = np.block([[c64, s64], [-s64, c64]]).astype(np.float32)
    k1 = np.arange(FFT_N1, dtype=np.float64)
    n2 = np.arange(FFT_N2, dtype=np.float64)
    phi = 2.0 * np.pi * n2[:, None] * k1[None, :] / SEQ
    ct = np.repeat(np.cos(phi)[:, :, None], LANE, axis=2).astype(np.float32)
    st = np.repeat(np.sin(phi)[:, :, None], LANE, axis=2).astype(np.float32)
    c128, s128 = _dft_tables(FFT_N2)
    gw = BW // F_GROUPS
    cg, sg = _dft_tables(gw)
    cs = np.zeros((BW, 2 * BW), np.float32)
    for g in range(F_GROUPS):
        sl = slice(g * gw, (g + 1) * gw)
        cs[sl, sl] = cg
        cs[sl, BW + g * gw:BW + (g + 1) * gw] = -sg
    rows = np.arange(FFT_TILE_K1 * FFT_N2)
    perm = np.zeros((rows.size, rows.size), np.float32)
    perm[rows, (rows % FFT_TILE_K1) * FFT_N2 + rows // FFT_TILE_K1] = 1.0
    cc, sc = _dft_tables(N_CTX)
    return dict(m1=_hilo(m1), ct=jnp.asarray(ct), st=jnp.asarray(st), c128=_hilo(c128), s128=_hilo(s128),
                cs=_hilo(cs), perm=jnp.asarray(perm, dtype=BF16), cc=_hilo(cc), sc=_hilo(sc))


def fourier_mix(l, h, w_in, b_in, fc):
    gw = BW // F_GROUPS
    ar, ai = fourier_proj(l, h, w_in, b_in, fc["cs"])
    y_ctx = dft_ctx(ar, ai, fc["cc"], fc["sc"], 1.0 / math.sqrt(N_CTX * gw))
    ar2 = ar[N_CTX:].reshape(FFT_N1, FFT_N2 * BW)
    ai2 = ai[N_CTX:].reshape(FFT_N1, FFT_N2 * BW)
    br, bi = fft_stage1(ar2, ai2, fc["m1"], fc["ct"], fc["st"])
    y = fft_stage2(br.reshape(FFT_N1, FFT_N2, BW), bi.reshape(FFT_N1, FFT_N2, BW),
                   fc["c128"], fc["s128"], fc["perm"], 1.0 / math.sqrt(SEQ * gw))
    return jnp.concatenate([y_ctx, y.reshape(SEQ, BW)], axis=0)


VT_ROWS = 80
LOG2E = math.log2(math.e)


def _mla_proj_kernel(h_ref, wcq_ref, bcq_ref, wkv_ref, bkv_ref, wkr_ref, bkr_ref, qg_ref, kvg_ref, wq_ref, wqr_ref,
                     wk_ref, wvt_ref, e_ref, tq_c_ref, tq_s_ref, tk_ref, q_ref, k_ref, vt_ref, wcq_bf, wkv_bf):
    _cast_once(wcq_ref, wcq_bf)
    _cast_once(wkv_ref, wkv_bf)
    h = h_ref[...]
    cq = _dot(h, wcq_bf[...]) + bcq_ref[0]
    cq = cq * lax.rsqrt(jnp.mean(cq * cq, axis=-1, keepdims=True) + LN_EPS) * qg_ref[...]
    cq = cq.astype(BF16)
    cos_t = jnp.concatenate([tq_c_ref[...]] * HEADS, axis=-1)
    sin_t = jnp.concatenate([tq_s_ref[...]] * HEADS, axis=-1)
    q = (_dot(cq, wq_ref[...]) * cos_t + _dot(cq, wqr_ref[...]) * sin_t) * (MLA_SCALE * LOG2E)
    q_ref[...] = q.astype(BF16)
    ckv = _dot(h, wkv_bf[...]) + bkv_ref[0]
    ckv = ckv * lax.rsqrt(jnp.mean(ckv * ckv, axis=-1, keepdims=True) + LN_EPS) * kvg_ref[...]
    ckv = ckv.astype(BF16)
    kr = ((_dot(h, wkr_ref[...]) + bkr_ref[...]) * tk_ref[...]).astype(BF16)
    k_ref[...] = (_dot(ckv, wk_ref[...]) + _dot(kr, e_ref[...])).astype(BF16)
    vt = lax.dot_general(wvt_ref[...], ckv, (((1,), (1,)), ((), ())), preferred_element_type=F32)
    ones_rows = (lax.broadcasted_iota(jnp.int32, (VT_ROWS - V_HEAD, RB), 0) == 0).astype(BF16)
    for hd in range(HEADS):
        vt_ref[0, hd * VT_ROWS:hd * VT_ROWS + V_HEAD, :] = vt[hd * V_HEAD:(hd + 1) * V_HEAD, :].astype(BF16)
        vt_ref[0, hd * VT_ROWS + V_HEAD:(hd + 1) * VT_ROWS, :] = ones_rows


def mla_proj(l, h, w_in, b_in, wkr, bkr, qg, kvg, wq, wqr, wk, wvt, e, tq_c, tq_s, tk):
    const = lambda i: (0, 0)
    row = lambda i: (i, 0)
    return pl.pallas_call(
        _mla_proj_kernel,
        grid=(NRB,),
        in_specs=[pl.BlockSpec((RB, D), row),
                  _w_in_spec(l, Q_LORA, COL_D // Q_LORA),
                  _b_in_spec(l, Q_LORA, COL_D // Q_LORA),
                  _w_in_spec(l, KV_LORA, COL_KV // KV_LORA),
                  _b_in_spec(l, KV_LORA, COL_KV // KV_LORA),
                  pl.BlockSpec((D, LANE), const),
                  pl.BlockSpec((1, LANE), const),
                  pl.BlockSpec((1, Q_LORA), const),
                  pl.BlockSpec((1, KV_LORA), const),
                  pl.BlockSpec((Q_LORA, HEADS * HEAD_PAD), const),
                  pl.BlockSpec((Q_LORA, HEADS * HEAD_PAD), const),
                  pl.BlockSpec((KV_LORA, HEADS * HEAD_PAD), const),
                  pl.BlockSpec((HEADS * V_HEAD, KV_LORA), const),
                  pl.BlockSpec((LANE, HEADS * HEAD_PAD), const),
                  pl.BlockSpec((RB, HEAD_PAD), row),
                  pl.BlockSpec((RB, HEAD_PAD), row),
                  pl.BlockSpec((RB, LANE), row)],
        out_specs=[pl.BlockSpec((RB, HEADS * HEAD_PAD), row),
                   pl.BlockSpec((RB, HEADS * HEAD_PAD), row),
                   pl.BlockSpec((1, HEADS * VT_ROWS, RB), lambda i: (i, 0, 0))],
        out_shape=[jax.ShapeDtypeStruct((T, HEADS * HEAD_PAD), BF16),
                   jax.ShapeDtypeStruct((T, HEADS * HEAD_PAD), BF16),
                   jax.ShapeDtypeStruct((NRB, HEADS * VT_ROWS, RB), BF16)],
        scratch_shapes=[pltpu.VMEM((D, Q_LORA), BF16), pltpu.VMEM((D, KV_LORA), BF16)],
        compiler_params=_cparams(("arbitrary",)),
        name="mla_proj",
    )(h, w_in, b_in, w_in, b_in, wkr, bkr, qg, kvg, wq, wqr, wk, wvt, e, tq_c, tq_s, tk)


KV_TILE = RB
KV_STAGE = 1024
N_STAGE = SEQ // KV_STAGE
TILES_PER_STAGE = KV_STAGE // KV_TILE
CTX_TILES = N_CTX // KV_TILE


HEAD_GROUP = 2


def _attn_scores(k_ref, q, g, start, n):
    kt = k_ref[pl.ds(start, n), g * HEAD_PAD:(g + 1) * HEAD_PAD]
    return lax.dot_general(kt, q, (((1,), (1,)), ((), ())), preferred_element_type=F32)


def _attn_out(acc):
    return acc[:V_HEAD, :] / acc[V_HEAD:V_HEAD + 1, :]


def _attn_kernel(q_ref, k_ref, vt_ref, o_ref, sa_ref, sb_ref):
    heads = range(HEAD_GROUP)
    qs = [q_ref[:, g * HEAD_PAD:(g + 1) * HEAD_PAD] for g in heads]

    def scores(g, stage):
        return _attn_scores(k_ref, qs[g], g, pl.multiple_of(N_CTX + stage * KV_STAGE, KV_TILE), KV_STAGE)

    def values(g, tile):
        return vt_ref[tile, g * VT_ROWS:(g + 1) * VT_ROWS, :]

    def softmax_pv(g, s_ref, stage, m, acc):
        m_new = jnp.maximum(m, jnp.max(s_ref[g], axis=0, keepdims=True))
        p = jnp.exp2(s_ref[g] - m_new).astype(BF16)
        acc = jnp.exp2(m - m_new) * acc
        for u in range(TILES_PER_STAGE):
            acc = acc + _dot(values(g, CTX_TILES + stage * TILES_PER_STAGE + u), p[u * KV_TILE:(u + 1) * KV_TILE, :])
        return m_new, acc

    ms, accs = [], []
    for g in heads:
        s0 = _attn_scores(k_ref, qs[g], g, 0, N_CTX)
        ms.append(jnp.max(s0, axis=0, keepdims=True))
        accs.append(_dot(values(g, 0), jnp.exp2(s0 - ms[g]).astype(BF16)))
        sa_ref[g] = scores(g, 0)

    def body(j, carry):
        ms, accs = list(carry[0]), list(carry[1])
        for g in heads:
            sb_ref[g] = scores(g, 2 * j + 1)
        for g in heads:
            ms[g], accs[g] = softmax_pv(g, sa_ref, 2 * j, ms[g], accs[g])
        for g in heads:
            sa_ref[g] = scores(g, 2 * j + 2)
        for g in heads:
            ms[g], accs[g] = softmax_pv(g, sb_ref, 2 * j + 1, ms[g], accs[g])
        return tuple(ms), tuple(accs)

    ms, accs = lax.fori_loop(0, N_STAGE // 2 - 1, body, (tuple(ms), tuple(accs)))
    ms, accs = list(ms), list(accs)
    for g in heads:
        sb_ref[g] = scores(g, N_STAGE - 1)
    for g in heads:
        ms[g], accs[g] = softmax_pv(g, sa_ref, N_STAGE - 2, ms[g], accs[g])
    for g in heads:
        _, acc = softmax_pv(g, sb_ref, N_STAGE - 1, ms[g], accs[g])
        o_ref[g * V_HEAD:(g + 1) * V_HEAD, :] = _attn_out(acc)


def _attn_ctx_kernel(q_ref, k_ref, vt_ref, o_ref):
    for g in range(HEAD_GROUP):
        s = _attn_scores(k_ref, q_ref[:, g * HEAD_PAD:(g + 1) * HEAD_PAD], g, 0, N_CTX)
        p = jnp.exp2(s - jnp.max(s, axis=0, keepdims=True)).astype(BF16)
        o_ref[g * V_HEAD:(g + 1) * V_HEAD, :] = _attn_out(_dot(vt_ref[0, g * VT_ROWS:(g + 1) * VT_ROWS, :], p))


def attention(q, k, vt):
    assert N_CTX == KV_TILE and N_STAGE % 2 == 0
    o_ctx = pl.pallas_call(
        _attn_ctx_kernel,
        grid=(HEADS // HEAD_GROUP,),
        in_specs=[pl.BlockSpec((N_CTX, HEAD_GROUP * HEAD_PAD), lambda h: (0, h)),
                  pl.BlockSpec((N_CTX, HEAD_GROUP * HEAD_PAD), lambda h: (0, h)),
                  pl.BlockSpec((1, HEAD_GROUP * VT_ROWS, KV_TILE), lambda h: (0, h, 0))],
        out_specs=pl.BlockSpec((HEAD_GROUP * V_HEAD, N_CTX), lambda h: (h, 0)),
        out_shape=jax.ShapeDtypeStruct((HEADS * V_HEAD, N_CTX), F32),
        compiler_params=_cparams(("arbitrary",)),
        name="attention_ctx",
    )(q, k, vt)
    o_lat = pl.pallas_call(
        _attn_kernel,
        grid=(HEADS // HEAD_GROUP, SEQ // RB),
        in_specs=[pl.BlockSpec((RB, HEAD_GROUP * HEAD_PAD), lambda h, i: (i + N_CTX // RB, h)),
                  pl.BlockSpec((T, HEAD_GROUP * HEAD_PAD), lambda h, i: (0, h)),
                  pl.BlockSpec((NRB, HEAD_GROUP * VT_ROWS, KV_TILE), lambda h, i: (0, h, 0))],
        out_specs=pl.BlockSpec((HEAD_GROUP * V_HEAD, RB), lambda h, i: (h, i)),
        out_shape=jax.ShapeDtypeStruct((HEADS * V_HEAD, SEQ), F32),
        scratch_shapes=[pltpu.VMEM((HEAD_GROUP, KV_STAGE, RB), F32),
                        pltpu.VMEM((HEAD_GROUP, KV_STAGE, RB), F32)],
        compiler_params=_cparams(("arbitrary", "arbitrary")),
        name="attention",
    )(q, k, vt)
    return jnp.concatenate([o_ctx, o_lat], axis=1)


MERGE_TN = 512


N_BRANCH = 4


MERGE_RB = 768


def _merge_kernel(h_ref, a_ref, b_ref, c_ref, dt_ref, *refs):
    wg_refs, bg_refs = refs[:N_BRANCH], refs[N_BRANCH:2 * N_BRANCH]
    wb_ref, y_ref, wg_bf = refs[2 * N_BRANCH:]

    @pl.when(pl.program_id(1) == 0)
    def _():
        for n in range(N_BRANCH):
            _cast_transposed(wg_refs[n], wg_bf.at[n])

    h = h_ref[...]
    branches = (a_ref[...], b_ref[...], c_ref[...].astype(BF16), dt_ref[...].T.astype(BF16))
    y = jnp.zeros((MERGE_RB, MERGE_TN), F32)
    for n, br in enumerate(branches):
        gate = jax.nn.sigmoid(_dot(h, wg_bf[n]) + bg_refs[n][...])
        y = y + gate * _dot(br, wb_ref[0, n])
    y_ref[...] = y.astype(BF16)


def merge(l, h, br_a, br_b, br_c, br_dt, w_in_t, bg, wb):
    row = lambda j, i: (i, 0)
    n_col = D // MERGE_TN
    gate_specs = [pl.BlockSpec((pl.Squeezed(), pl.Element(MERGE_TN), pl.Element(D)),
                               functools.partial(
                                   lambda j, i, n: (l, pl.multiple_of(COL_G + n * D + j * MERGE_TN, QK_ROPE), 0), n=n),
                               pipeline_mode=pl.Buffered(1))
                  for n in range(N_BRANCH)]
    bias_specs = [pl.BlockSpec((1, MERGE_TN), functools.partial(lambda j, i, n: (0, n * n_col + j), n=n))
                  for n in range(N_BRANCH)]
    return pl.pallas_call(
        _merge_kernel,
        grid=(n_col, T // MERGE_RB),
        in_specs=[pl.BlockSpec((MERGE_RB, D), row),
                  pl.BlockSpec((MERGE_RB, BW), row),
                  pl.BlockSpec((MERGE_RB, BW), row),
                  pl.BlockSpec((MERGE_RB, BW), row),
                  pl.BlockSpec((BW, MERGE_RB), lambda j, i: (0, i))]
                 + gate_specs + bias_specs
                 + [pl.BlockSpec((1, N_BRANCH, BW, MERGE_TN), lambda j, i: (l, 0, 0, j))],
        out_specs=pl.BlockSpec((MERGE_RB, MERGE_TN), lambda j, i: (i, j)),
        out_shape=jax.ShapeDtypeStruct((T, D), BF16),
        scratch_shapes=[pltpu.VMEM((N_BRANCH, D, MERGE_TN), BF16)],
        compiler_params=_cparams(("arbitrary", "arbitrary")),
        name="merge",
    )(h, br_a, br_b, br_c, br_dt, *([w_in_t] * N_BRANCH), *([bg] * N_BRANCH), wb)


ROUTER_W = 128


def _post_mix_kernel(y_ref, xc_ref, xl_ref, mod_ref, wo_ref, g_ref, b_ref, wr_ref, br_ref,
                     x1_ref, h2_ref, ids_ref, gates_ref):
    mix = _dot(y_ref[...], wo_ref[0])
    x1 = _ln0(ALPHA * _stream_block(xc_ref, xl_ref) + _mod_row(mod_ref, 2) * mix) * g_ref[...] + b_ref[...]
    x1_ref[...] = x1
    h2 = _ln0(x1) * (1.0 + _mod_row(mod_ref, 4)) + _mod_row(mod_ref, 3)
    _rows_to_slabs(h2, h2_ref)
    logits = _dot3(h2, wr_ref) + br_ref[...]
    lane = lax.broadcasted_iota(jnp.int32, (RB, ROUTER_W), 1).astype(F32)
    neg = jnp.float32(-jnp.inf)
    big = jnp.float32(ROUTER_W)
    gl = jnp.where(lane < N_GROUPS, logits, neg)
    gmax = jnp.max(gl, axis=-1, keepdims=True)
    g_idx = jnp.min(jnp.where(gl == gmax, lane, big), axis=-1, keepdims=True)
    g_p = 1.0 / jnp.sum(jnp.exp(gl - gmax), axis=-1, keepdims=True)
    e_lane = lane - N_GROUPS
    in_grp = jnp.logical_and(e_lane >= g_idx * EXP_PER_GROUP, e_lane < (g_idx + 1) * EXP_PER_GROUP)
    el = jnp.where(in_grp, logits, neg)
    v1 = jnp.max(el, axis=-1, keepdims=True)
    i1 = jnp.min(jnp.where(el == v1, lane, big), axis=-1, keepdims=True)
    el2 = jnp.where(lane == i1, neg, el)
    v2 = jnp.max(el2, axis=-1, keepdims=True)
    i2 = jnp.min(jnp.where(el2 == v2, lane, big), axis=-1, keepdims=True)
    e21 = jnp.exp(v2 - v1)
    p1 = 1.0 / (1.0 + e21)
    p2 = e21 * p1
    ids_ref[...] = jnp.where(lane == 0, i1 - N_GROUPS, jnp.where(lane == 1, i2 - N_GROUPS, 0.0)).astype(jnp.int32)
    gates_ref[...] = jnp.where(lane == 0, p1 * g_p, jnp.where(lane == 1, p2 * g_p, 0.0))


def post_mix(l, y, x_ctx, x_lat, mod, wo, ln_g, ln_b, wr, br):
    row = lambda i: (i, 0)
    const = lambda i: (0, 0)
    return pl.pallas_call(
        _post_mix_kernel,
        grid=(NRB,),
        in_specs=[pl.BlockSpec((RB, D), row)] + _stream_specs()
                 + [pl.BlockSpec((8, 6 * D), const),
                  pl.BlockSpec((1, D, D), lambda i: (l, 0, 0)),
                  pl.BlockSpec((1, D), const),
                  pl.BlockSpec((1, D), const),
                  pl.BlockSpec((2, D, ROUTER_W), lambda i: (0, 0, 0)),
                  pl.BlockSpec((1, ROUTER_W), const)],
        out_specs=[pl.BlockSpec((RB, D), row),
                   pl.BlockSpec((RB * ROW_SUB, LANE), row),
                   pl.BlockSpec((RB, ROUTER_W), row),
                   pl.BlockSpec((RB, ROUTER_W), row)],
        out_shape=[jax.ShapeDtypeStruct((T, D), F32),
                   jax.ShapeDtypeStruct((T * ROW_SUB, LANE), F32),
                   jax.ShapeDtypeStruct((T, ROUTER_W), jnp.int32),
                   jax.ShapeDtypeStruct((T, ROUTER_W), F32)],
        compiler_params=_cparams(("arbitrary",)),
        name="post_mix",
    )(y, x_ctx, x_lat, mod, wo, ln_g, ln_b, wr, br)


N_ROWS = -(-(T * TOP_K + N_EXPERTS * (MOE_BLOCK - 1)) // MOE_BLOCK) * MOE_BLOCK
N_BLK = N_ROWS // MOE_BLOCK


N_DMA_PRIORITIES = 2


GATHER_UNROLL = 8


def _dense_dst_row(r0, u):
    return (r0 + u) * ROW_SUB


def _row_gather(src_hbm, idx_ref, base, n, dst_ref, sem, first=0, dst_row_of=_dense_dst_row):
    def issue(t, _):
        r0 = first + t * GATHER_UNROLL
        for u in range(GATHER_UNROLL):
            r = r0 + u
            src_row = pl.multiple_of(idx_ref[base + r] * ROW_SUB, ROW_SUB)
            dst_row = pl.multiple_of(dst_row_of(r0, u), 8)
            pltpu.make_async_copy(src_hbm.at[pl.ds(src_row, ROW_SUB), :],
                                  dst_ref.at[pl.ds(dst_row, ROW_SUB), :], sem).start(priority=u % N_DMA_PRIORITIES)
        return 0
    lax.fori_loop(0, n // GATHER_UNROLL, issue, 0)


def _rows_from_slabs(buf_ref, start, n, stride):
    return jnp.concatenate([buf_ref[pl.ds(start + c, n, stride=stride), :] for c in range(ROW_SUB)], axis=-1)


def _rows_to_slabs(val, out_ref):
    n = val.shape[0]
    for c in range(ROW_SUB):
        out_ref[pl.ds(c, n, stride=ROW_SUB), :] = val[:, c * LANE:(c + 1) * LANE]


ITEM_BLKS = 3
ITEM_ROWS = ITEM_BLKS * MOE_BLOCK
N_ITEMS = (N_BLK + (ITEM_BLKS - 1) * N_EXPERTS) // ITEM_BLKS
EXP_CHUNK = 512
N_CHUNK = D_EXPERT // EXP_CHUNK
BLK_SLABS = MOE_BLOCK * ROW_SUB


WAIT_ROWS = 8


def _expert_kernel(item_exp_ref, item_blk0_ref, item_nblk_ref, item_cnt_ref, n_used_ref, row_tok_ref, h_hbm, w1_ref,
                   w3_ref, w2_ref, y_hbm, xslab, yacc, stage, w1c, w3c, w2c, sem_x, sem_y):
    i, j = pl.program_id(0), pl.program_id(1)
    nblk = item_nblk_ref[i]
    blk0 = item_blk0_ref[i]
    xs = xslab.at[i % 2]

    def gather_item(item):
        _row_gather(h_hbm, row_tok_ref, item_blk0_ref[item] * MOE_BLOCK, item_cnt_ref[item], xslab.at[item % 2],
                    sem_x)

    @pl.when(jnp.logical_and(i == 0, j == 0))
    def _():
        xslab[...] = jnp.zeros(xslab.shape, F32)
        gather_item(0)

    @pl.when(nblk > 0)
    def _():
        w1c[...] = w1_ref[0, 0].astype(BF16)
        w3c[...] = w3_ref[0, 0].astype(BF16)
        w2c[...] = w2_ref[0, 0].astype(BF16)

    last = j == N_CHUNK - 1

    def rows_of(b):
        return pl.ds(pl.multiple_of(b * MOE_BLOCK, MOE_BLOCK), MOE_BLOCK)

    def partial_out(xb):
        a = _dot(xb, w1c[...])
        g = _dot(xb, w3c[...])
        return _dot((a * jax.nn.sigmoid(a) * g).astype(BF16), w2c[...])

    def y_copy(b, slot):
        dst = pl.multiple_of((blk0 + b) * BLK_SLABS, BLK_SLABS)
        return pltpu.make_async_copy(stage.at[slot], y_hbm.at[pl.ds(dst, BLK_SLABS), :], sem_y.at[slot])

    def x_block(b):
        return _rows_from_slabs(xs, b * BLK_SLABS, MOE_BLOCK, ROW_SUB).astype(BF16)

    @pl.when(j == 0)
    def _():
        def wait_rows(t, _):
            pltpu.make_async_copy(h_hbm.at[pl.ds(0, WAIT_ROWS * ROW_SUB), :],
                                  xs.at[pl.ds(0, WAIT_ROWS * ROW_SUB), :], sem_x).wait()
            return 0
        lax.fori_loop(0, item_cnt_ref[i] // WAIT_ROWS, wait_rows, 0)

        @pl.when(i + 1 < N_ITEMS)
        def _():
            gather_item(i + 1)

        def first_block(b, _):
            yacc[rows_of(b), :] = partial_out(x_block(b))
            return 0
        lax.fori_loop(0, nblk, first_block, 0)

    @pl.when(jnp.logical_and(j > 0, jnp.logical_not(last)))
    def _():
        def mid_block(b, _):
            yacc[rows_of(b), :] += partial_out(x_block(b))
            return 0
        lax.fori_loop(0, nblk, mid_block, 0)

    @pl.when(last)
    def _():
        def last_block(b, _):
            slot = b % 2

            @pl.when(b >= 2)
            def _():
                y_copy(b - 2, slot).wait()
            _rows_to_slabs(yacc[rows_of(b), :] + partial_out(x_block(b)), stage.at[slot])
            y_copy(b, slot).start()
            return 0
        lax.fori_loop(0, nblk, last_block, 0)

        for back in (1, 2):
            @pl.when(nblk >= back)
            def _():
                b = nblk - back
                y_copy(b, b % 2).wait()

    @pl.when(jnp.logical_and(i == N_ITEMS - 1, last))
    def _():
        stage[0] = jnp.zeros((BLK_SLABS, LANE), F32)

        def fill(b, _):
            dst = pl.multiple_of(b * BLK_SLABS, BLK_SLABS)
            cp = pltpu.make_async_copy(stage.at[0], y_hbm.at[pl.ds(dst, BLK_SLABS), :], sem_y.at[0])
            cp.start()
            cp.wait()
            return 0
        lax.fori_loop(n_used_ref[0], N_BLK, fill, 0)


def experts(l, item_exp, item_blk0, item_nblk, item_cnt, n_used, row_tok, h2_slabs, w1, w3, w2):
    wspec = lambda blk, imap: pl.BlockSpec(blk, imap)
    grid_spec = pltpu.PrefetchScalarGridSpec(
        num_scalar_prefetch=6,
        grid=(N_ITEMS, N_CHUNK),
        in_specs=[pl.BlockSpec(memory_space=pl.ANY),
                  wspec((1, 1, D, EXP_CHUNK), lambda i, j, ie, *_: (l, ie[i], 0, j)),
                  wspec((1, 1, D, EXP_CHUNK), lambda i, j, ie, *_: (l, ie[i], 0, j)),
                  wspec((1, 1, EXP_CHUNK, D), lambda i, j, ie, *_: (l, ie[i], j, 0))],
        out_specs=pl.BlockSpec(memory_space=pl.ANY),
        scratch_shapes=[pltpu.VMEM((2, ITEM_BLKS * BLK_SLABS, LANE), F32),
                        pltpu.VMEM((ITEM_ROWS, D), F32),
                        pltpu.VMEM((2, BLK_SLABS, LANE), F32),
                        pltpu.VMEM((D, EXP_CHUNK), BF16),
                        pltpu.VMEM((D, EXP_CHUNK), BF16),
                        pltpu.VMEM((EXP_CHUNK, D), BF16),
                        pltpu.SemaphoreType.DMA(()),
                        pltpu.SemaphoreType.DMA((2,))],
    )
    return pl.pallas_call(
        _expert_kernel,
        grid_spec=grid_spec,
        out_shape=jax.ShapeDtypeStruct((N_ROWS * ROW_SUB, LANE), F32),
        compiler_params=_cparams(("arbitrary", "arbitrary")),
        name="experts",
    )(item_exp, item_blk0, item_nblk, item_cnt, n_used, row_tok, h2_slabs, w1, w3, w2)


COMBINE_SUB = 64
COMBINE_PITCH = 40


def _combine_dst_row(r0, u):
    return r0 * (COMBINE_PITCH // TOP_K) + (u // TOP_K) * COMBINE_PITCH + (u % TOP_K) * ROW_SUB


def _combine_kernel(pos_ref, y_hbm, gates_ref, x1_ref, mod_ref, g_ref, b_ref, nmod_ref,
                    x2c_ref, x2l_ref, hn_ref, ybuf, sems):
    i = pl.program_id(0)
    slot = i % 2
    n = RB * TOP_K
    n_sub = RB // COMBINE_SUB

    @pl.when(i == 0)
    def _():
        _row_gather(y_hbm, pos_ref, 0, n, ybuf.at[0], sems.at[0], dst_row_of=_combine_dst_row)

    pltpu.make_async_copy(y_hbm.at[pl.ds(0, n * ROW_SUB), :], ybuf.at[slot, pl.ds(0, n * ROW_SUB), :],
                          sems.at[slot]).wait()
    shift5, shift0, scale1 = _mod_row(mod_ref, 5), _mod_row(nmod_ref, 0), 1.0 + _mod_row(nmod_ref, 1)
    for s in range(n_sub):
        @pl.when(i + 1 < NRB)
        def _():
            _row_gather(y_hbm, pos_ref, (i + 1) * n, n // n_sub, ybuf.at[1 - slot], sems.at[1 - slot],
                        first=s * (n // n_sub), dst_row_of=_combine_dst_row)

        rows = slice(s * COMBINE_SUB, (s + 1) * COMBINE_SUB)
        base = s * COMBINE_SUB * COMBINE_PITCH
        y0 = _rows_from_slabs(ybuf.at[slot], base, COMBINE_SUB, COMBINE_PITCH)
        y1 = _rows_from_slabs(ybuf.at[slot], base + ROW_SUB, COMBINE_SUB, COMBINE_PITCH)
        gates = gates_ref[rows, :]
        f = gates[:, 0:1] * y0 + gates[:, 1:2] * y1
        x2 = _ln0(ALPHA * x1_ref[rows, :] + shift5 * f) * g_ref[...] + b_ref[...]
        x2l_ref[rows, :] = x2

        @pl.when(i == 0)
        def _():
            x2c_ref[rows, :] = x2

        hn_ref[rows, :] = (_ln0(x2) * scale1 + shift0).astype(BF16)


def combine(pos, y_slabs, gates, x1, mod, ln_g, ln_b, next_mod):
    row = lambda i, p: (i, 0)
    const = lambda i, p: (0, 0)
    grid_spec = pltpu.PrefetchScalarGridSpec(
        num_scalar_prefetch=1,
        grid=(NRB,),
        in_specs=[pl.BlockSpec(memory_space=pl.ANY),
                  pl.BlockSpec((RB, ROUTER_W), row),
                  pl.BlockSpec((RB, D), row),
                  pl.BlockSpec((8, 6 * D), const),
                  pl.BlockSpec((1, D), const),
                  pl.BlockSpec((1, D), const),
                  pl.BlockSpec((8, 6 * D), const)],
        out_specs=[pl.BlockSpec((RB, D), const),
                   pl.BlockSpec((RB, D), lambda i, p: (jnp.maximum(i - 1, 0), 0)),
                   pl.BlockSpec((RB, D), row)],
        scratch_shapes=[pltpu.VMEM((2, RB * COMBINE_PITCH, LANE), F32),
                        pltpu.SemaphoreType.DMA((2,))],
    )
    return pl.pallas_call(
        _combine_kernel,
        grid_spec=grid_spec,
        out_shape=[jax.ShapeDtypeStruct((N_CTX, D), F32), jax.ShapeDtypeStruct((SEQ, D), F32),
                   jax.ShapeDtypeStruct((T, D), BF16)],
        compiler_params=_cparams(("arbitrary",)),
        name="combine",
    )(pos, y_slabs, gates, x1, mod, ln_g, ln_b, next_mod)


def _routing_tables(ids):
    expert = ids[:, :TOP_K].reshape(-1)
    n = expert.shape[0]
    onehot = (expert[:, None] == jnp.arange(N_EXPERTS, dtype=jnp.int32)[None, :]).astype(jnp.int32)
    csum = jnp.cumsum(onehot, axis=0)
    counts = csum[-1]
    rank = jnp.sum(onehot * csum, axis=1) - 1
    nb = (counts + MOE_BLOCK - 1) // MOE_BLOCK
    bstart = jnp.cumsum(nb) - nb
    pos = (jnp.sum(onehot * (bstart * MOE_BLOCK)[None, :], axis=1) + rank).astype(jnp.int32)
    tok = jnp.arange(n, dtype=jnp.int32) // TOP_K
    row_tok = jnp.zeros((N_ROWS,), jnp.int32).at[pos].set(tok)
    n_it = (nb + ITEM_BLKS - 1) // ITEM_BLKS
    it_end = jnp.cumsum(n_it)
    it_start = it_end - n_it
    idx = jnp.arange(N_ITEMS, dtype=jnp.int32)
    e_of = jnp.minimum(jnp.sum(it_end[None, :] <= idx[:, None], axis=1), N_EXPERTS - 1).astype(jnp.int32)
    k = idx - it_start[e_of]
    valid = idx < it_end[-1]
    item_exp = e_of
    item_blk0 = jnp.where(valid, bstart[e_of] + ITEM_BLKS * k, 0).astype(jnp.int32)
    item_nblk = jnp.where(valid, jnp.clip(nb[e_of] - ITEM_BLKS * k, 0, ITEM_BLKS), 0).astype(jnp.int32)
    item_cnt = jnp.where(valid, jnp.clip(counts[e_of] - ITEM_ROWS * k, 0, ITEM_ROWS), 0)
    item_cnt = ((item_cnt + GATHER_UNROLL - 1) // GATHER_UNROLL * GATHER_UNROLL).astype(jnp.int32)
    n_used = jnp.sum(nb).astype(jnp.int32)[None]
    return item_exp, item_blk0, item_nblk, item_cnt, n_used, row_tok, pos


def _rope_tables():
    rows = SEQ // GRID_W
    row = jnp.repeat(jnp.arange(rows, dtype=F32), GRID_W)
    col = jnp.tile(jnp.arange(GRID_W, dtype=F32), rows)
    half = QK_ROPE // 2
    inv = ROPE_THETA ** (-jnp.arange(0, half, 2, dtype=F32) / half)
    ar = row[:, None] * inv
    ac = col[:, None] * inv
    ang = jnp.concatenate([ar, ar, ac, ac], -1)
    cos = jnp.concatenate([jnp.ones((N_CTX, QK_ROPE), F32), jnp.cos(ang)], 0)
    sin = jnp.concatenate([jnp.zeros((N_CTX, QK_ROPE), F32), jnp.sin(ang)], 0)
    one = jnp.ones((T, QK_NOPE), F32)
    zero_n = jnp.zeros((T, QK_NOPE), F32)
    zero_p = jnp.zeros((T, HEAD_PAD - QK_NOPE - QK_ROPE), F32)
    tq_c = jnp.concatenate([one, cos, zero_p], -1)
    tq_s = jnp.concatenate([zero_n, sin, zero_p], -1)
    tk = jnp.concatenate([cos, sin, jnp.zeros((T, LANE - 2 * QK_ROPE), F32)], -1)
    return tq_c, tq_s, tk


_ROT_SRC = np.concatenate([np.arange(8, 16), np.arange(0, 8), np.arange(24, 32), np.arange(16, 24)])
_ROT_SIGN = np.concatenate([-np.ones(8), np.ones(8), -np.ones(8), np.ones(8)]).astype(np.float32)


def _rot_cols(w):
    return w[..., _ROT_SRC] * _ROT_SIGN


def _layer_weights(l, w_in_t, b_in, d_w_uq, d_w_uk, d_w_uv, r_w_grp, r_b_grp, r_w_exp, r_b_exp, a_b_s, b_w_dw):
    bi = b_in[l]
    p = {}
    w_kr, b_kr = w_in_t[l, COL_KR:COL_G, :].T, bi[COL_KR:COL_G]
    padw = LANE - 2 * QK_ROPE
    p["wkr"] = jnp.concatenate([w_kr, _rot_cols(w_kr), jnp.zeros((D, padw), F32)], -1).astype(BF16)
    p["bkr"] = jnp.concatenate([b_kr, _rot_cols(b_kr), jnp.zeros((padw,), F32)])[None]
    wq = d_w_uq[l].reshape(Q_LORA, HEADS, QK_NOPE + QK_ROPE)
    zpad = jnp.zeros((Q_LORA, HEADS, HEAD_PAD - QK_NOPE - QK_ROPE), F32)
    p["wq"] = jnp.concatenate([wq, zpad], -1).reshape(Q_LORA, HEADS * HEAD_PAD).astype(BF16)
    p["wqr"] = jnp.concatenate([jnp.zeros((Q_LORA, HEADS, QK_NOPE), F32), _rot_cols(wq[..., QK_NOPE:]), zpad],
                               -1).reshape(Q_LORA, HEADS * HEAD_PAD).astype(BF16)
    wk = d_w_uk[l].reshape(KV_LORA, HEADS, QK_NOPE)
    p["wk"] = jnp.concatenate([wk, jnp.zeros((KV_LORA, HEADS, HEAD_PAD - QK_NOPE), F32)],
                              -1).reshape(KV_LORA, HEADS * HEAD_PAD).astype(BF16)
    p["wvt"] = d_w_uv[l].T.astype(BF16)
    e = np.zeros((LANE, HEADS, HEAD_PAD), np.float32)
    for j in range(QK_ROPE):
        e[j, :, QK_NOPE + j] = 1.0
        e[QK_ROPE + j, :, QK_NOPE + j] = 1.0
    p["e"] = jnp.asarray(e.reshape(LANE, HEADS * HEAD_PAD), dtype=BF16)
    p["bg"] = bi[None, COL_G:]
    padr = ROUTER_W - N_GROUPS - N_EXPERTS
    p["wr"] = _hilo(jnp.concatenate([r_w_grp[l], r_w_exp[l], jnp.zeros((D, padr), F32)], -1))
    p["br"] = jnp.concatenate([r_b_grp[l], r_b_exp[l], jnp.zeros((padr,), F32)])[None]
    p["bs"] = jnp.broadcast_to(a_b_s[l][:, :, None], (A_GROUPS, CHUNK, CHUNK))
    p["wdw"] = jnp.concatenate([b_w_dw[l], jnp.zeros((1, BW), F32)], 0)
    return p


def kernel(x, c, ctx, c_ctx, w_ada, b_ada, w_in, b_in, a_ln_g, a_ln_b, a_w_s, a_b_s, b_w_dw, b_b_dw, b_ln_g,
           b_ln_b, d_q_g, d_w_uq, d_kv_g, d_w_uk, d_w_uv, w_branch, w_o, ln1_g, ln1_b, r_w_grp, r_b_grp,
           r_w_exp, r_b_exp, e_w1, e_w3, e_w2, ln2_g, ln2_b):
    assert x.shape == (1, SEQ, D) and ctx.shape == (1, N_CTX, D)
    x_ctx, x_lat = ctx[0], x[0]
    w_in_t = jnp.transpose(w_in, (0, 2, 1))
    c_rep = jnp.broadcast_to(jnp.concatenate([c, c_ctx[None]], axis=0)[:, :, None], (N_COND, D, LANE))
    mods = ada_mod(c_rep, w_ada, b_ada)
    tq_c, tq_s, tk = _rope_tables()
    fc = _fourier_consts()
    b_in3 = b_in.reshape(DEPTH, 1, -1)
    w_br, w_ob = w_branch.astype(BF16), w_o.astype(BF16)

    h = modln(x_ctx, x_lat, mods[0])
    for l in range(DEPTH):
        p = _layer_weights(l, w_in_t, b_in, d_w_uq, d_w_uk, d_w_uv, r_w_grp, r_b_grp, r_w_exp, r_b_exp, a_b_s,
                           b_w_dw)
        mod = mods[l]
        br_a = mix_a(l, h, w_in_t, b_in3, a_ln_g[l][None], a_ln_b[l][None], a_w_s[l].astype(BF16), p["bs"])
        br_b = conv_ln(glu(l, h, w_in_t, b_in3), p["wdw"], b_b_dw[l][None], b_ln_g[l][None], b_ln_b[l][None])
        br_c = fourier_mix(l, h, w_in_t, b_in3, fc)
        q, k, vt = mla_proj(l, h, w_in_t, b_in3, p["wkr"], p["bkr"], d_q_g[l][None], d_kv_g[l][None], p["wq"],
                            p["wqr"], p["wk"], p["wvt"], p["e"], tq_c, tq_s, tk)
        br_dt = attention(q, k, vt)
        y = merge(l, h, br_a, br_b, br_c, br_dt, w_in_t, p["bg"], w_br)
        x1, h2, ids, gates = post_mix(l, y, x_ctx, x_lat, mod, w_ob, ln1_g[l][None], ln1_b[l][None], p["wr"],
                                      p["br"])
        item_exp, item_blk0, item_nblk, item_cnt, n_used, row_tok, pos = _routing_tables(ids)
        ys = experts(l, item_exp, item_blk0, item_nblk, item_cnt, n_used, row_tok, h2, e_w1, e_w3, e_w2)
        x_ctx, x_lat, h = combine(pos, ys, gates, x1, mod, ln2_g[l][None], ln2_b[l][None],
                                  mods[min(l + 1, DEPTH - 1)])
    return x_lat[None]
```

```python
import functools
import math

import numpy as np
import jax
import jax.numpy as jnp
from jax import lax
from jax.experimental import pallas as pl
from jax.experimental.pallas import tpu as pltpu

F32 = jnp.float32
BF16 = jnp.bfloat16

D = 2048
SEQ = 8192
N_CTX = 256
T = N_CTX + SEQ
DEPTH = 2
GRID_W = 64
BW = 512
CHUNK = 128
A_GROUPS = 4
CONV_W = 31
F_GROUPS = 4
QK_NOPE = 64
QK_ROPE = 32
V_HEAD = 64
HEADS = 8
Q_LORA = 512
KV_LORA = 256
ROPE_THETA = 10000.0
MLA_SCALE = (QK_NOPE + QK_ROPE) ** -0.5
N_GROUPS = 4
EXP_PER_GROUP = 8
N_EXPERTS = 32
TOP_K = 2
D_EXPERT = 1024
MOE_BLOCK = 256
COL_A = 0
COL_B = 1024
COL_C = 2048
COL_D = 2560
COL_KV = 3072
COL_KR = 3328
COL_G = 3360
ALPHA = (2 * DEPTH) ** 0.25
LN_EPS = 1e-6

LANE = 128
HEAD_PAD = 128
RB = 256
NRB = T // RB
FFT_N1 = 64
FFT_N2 = 128
ROW_SUB = D // LANE
V7X_VMEM_BYTES = 64 * 1024 * 1024
VMEM_LIMIT = V7X_VMEM_BYTES * 7 // 8
BF16_BITS_MASK = -(1 << 16)


def _cparams(sem):
    return pltpu.CompilerParams(dimension_semantics=sem, vmem_limit_bytes=VMEM_LIMIT)


def _dot(a, b):
    return jnp.dot(a, b, preferred_element_type=F32)


def _split(x):
    hi = lax.bitcast_convert_type(lax.bitcast_convert_type(x, jnp.int32) & jnp.int32(BF16_BITS_MASK), F32)
    return hi.astype(BF16), (x - hi).astype(BF16)


def _hilo(x):
    return jnp.stack(_split(jnp.asarray(x, F32)))


def _dot3(a, t_ref):
    a_hi, a_lo = _split(a)
    return _dot(a_hi, t_ref[0]) + (_dot(a_hi, t_ref[1]) + _dot(a_lo, t_ref[0]))


def _dot3_t(t_ref, b):
    b_hi, b_lo = _split(b)
    return _dot(t_ref[0], b_hi) + (_dot(t_ref[1], b_hi) + _dot(t_ref[0], b_lo))


def _ln0(x):
    mu = jnp.mean(x, axis=-1, keepdims=True)
    xc = x - mu
    var = jnp.mean(xc * xc, axis=-1, keepdims=True)
    return xc * lax.rsqrt(var + LN_EPS)


def _mod_row(mod_ref, q):
    lat = mod_ref[0:1, q * D:(q + 1) * D]
    ctx = mod_ref[1:2, q * D:(q + 1) * D]
    return jnp.where(pl.program_id(0) == 0, ctx, lat)


ADA_TN = 1536
ADA_TK = 256
N_COND = 2


def _ada_kernel(c_ref, w_ref, b_ref, o_ref):
    accs = [b_ref[0] for _ in range(N_COND)]
    for k0 in range(0, D, ADA_TK):
        w = w_ref[0, k0:k0 + ADA_TK, :]
        for r in range(N_COND):
            s = c_ref[r, k0:k0 + ADA_TK, :]
            s = s * jax.nn.sigmoid(s)
            s = jnp.concatenate([s] * (ADA_TN // LANE), axis=-1)
            accs[r] = accs[r] + jnp.sum(w * s, axis=0, keepdims=True)
    o_ref[0] = jnp.concatenate(accs + [jnp.zeros((8 - N_COND, ADA_TN), F32)], axis=0)


def ada_mod(c_rep, w_ada, b_ada):
    return pl.pallas_call(
        _ada_kernel,
        grid=(DEPTH, 6 * D // ADA_TN),
        in_specs=[pl.BlockSpec((N_COND, D, LANE), lambda l, j: (0, 0, 0)),
                  pl.BlockSpec((1, D, ADA_TN), lambda l, j: (l, 0, j)),
                  pl.BlockSpec((1, 1, ADA_TN), lambda l, j: (l, 0, j))],
        out_specs=pl.BlockSpec((1, 8, ADA_TN), lambda l, j: (l, 0, j)),
        out_shape=jax.ShapeDtypeStruct((DEPTH, 8, 6 * D), F32),
        compiler_params=_cparams(("arbitrary", "arbitrary")),
        name="ada_mod",
    )(c_rep, w_ada, b_ada.reshape(DEPTH, 1, 6 * D))


def _stream_specs():
    return [pl.BlockSpec((RB, D), lambda i, *_: (0, 0)),
            pl.BlockSpec((RB, D), lambda i, *_: (jnp.maximum(i - 1, 0), 0))]


def _stream_block(ctx_ref, lat_ref):
    return jnp.where(pl.program_id(0) == 0, ctx_ref[...], lat_ref[...])


def _modln_kernel(xc_ref, xl_ref, mod_ref, h_ref):
    h = _ln0(_stream_block(xc_ref, xl_ref)) * (1.0 + _mod_row(mod_ref, 1)) + _mod_row(mod_ref, 0)
    h_ref[...] = h.astype(BF16)


def modln(x_ctx, x_lat, mod):
    return pl.pallas_call(
        _modln_kernel,
        grid=(NRB,),
        in_specs=_stream_specs() + [pl.BlockSpec((8, 6 * D), lambda i: (0, 0))],
        out_specs=pl.BlockSpec((RB, D), lambda i: (i, 0)),
        out_shape=jax.ShapeDtypeStruct((T, D), BF16),
        compiler_params=_cparams(("arbitrary",)),
        name="modln",
    )(x_ctx, x_lat, mod)


def _w_in_spec(l, width, col):
    return pl.BlockSpec((1, width, D), lambda i: (l, col, 0))


def _b_in_spec(l, width, col):
    return pl.BlockSpec((1, 1, width), lambda i: (l, 0, col))


CAST_CHUNK = 256


def _cast_transposed(w_ref, wbf_ref):
    width = w_ref.shape[0]
    for c0 in range(0, width, CAST_CHUNK):
        wbf_ref[:, c0:c0 + CAST_CHUNK] = w_ref[c0:c0 + CAST_CHUNK, :].T.astype(BF16)


def _cast_once(w_ref, wbf_ref):
    @pl.when(pl.program_id(0) == 0)
    def _():
        _cast_transposed(w_ref.at[0], wbf_ref)


def _mix_a_kernel(h_ref, w_ref, b_ref, g_ref, bb_ref, ws_ref, bs_ref, o_ref, wbf_ref):
    _cast_once(w_ref, wbf_ref)
    uv = jax.nn.gelu(_dot(h_ref[...], wbf_ref[...]) + b_ref[0])
    u = uv[:, :BW]
    v = _ln0(uv[:, BW:]) * g_ref[...] + bb_ref[...]
    v = v.astype(BF16)
    gw = BW // A_GROUPS
    for ch in range(RB // CHUNK):
        rows = slice(ch * CHUNK, (ch + 1) * CHUNK)
        parts = [_dot(ws_ref[g], v[rows, g * gw:(g + 1) * gw]) + bs_ref[g] for g in range(A_GROUPS)]
        mixed = jnp.concatenate(parts, axis=-1)
        o_ref[rows, :] = (u[rows, :] * mixed).astype(BF16)


def mix_a(l, h, w_in, b_in, ln_g, ln_b, ws, bs):
    return pl.pallas_call(
        _mix_a_kernel,
        grid=(NRB,),
        in_specs=[pl.BlockSpec((RB, D), lambda i: (i, 0)),
                  _w_in_spec(l, 2 * BW, COL_A // (2 * BW)),
                  _b_in_spec(l, 2 * BW, COL_A // (2 * BW)),
                  pl.BlockSpec((1, BW), lambda i: (0, 0)),
                  pl.BlockSpec((1, BW), lambda i: (0, 0)),
                  pl.BlockSpec((A_GROUPS, CHUNK, CHUNK), lambda i: (0, 0, 0)),
                  pl.BlockSpec((A_GROUPS, CHUNK, CHUNK), lambda i: (0, 0, 0))],
        out_specs=pl.BlockSpec((RB, BW), lambda i: (i, 0)),
        out_shape=jax.ShapeDtypeStruct((T, BW), BF16),
        scratch_shapes=[pltpu.VMEM((D, 2 * BW), BF16)],
        compiler_params=_cparams(("arbitrary",)),
        name="mix_a",
    )(h, w_in, b_in, ln_g, ln_b, ws, bs)


def _glu_kernel(h_ref, w_ref, b_ref, o_ref, wbf_ref):
    _cast_once(w_ref, wbf_ref)
    ab = _dot(h_ref[...], wbf_ref[...]) + b_ref[0]
    o_ref[...] = ab[:, :BW] * jax.nn.sigmoid(ab[:, BW:])


def glu(l, h, w_in, b_in):
    return pl.pallas_call(
        _glu_kernel,
        grid=(NRB,),
        in_specs=[pl.BlockSpec((RB, D), lambda i: (i, 0)),
                  _w_in_spec(l, 2 * BW, COL_B // (2 * BW)),
                  _b_in_spec(l, 2 * BW, COL_B // (2 * BW))],
        out_specs=pl.BlockSpec((RB, BW), lambda i: (i, 0)),
        out_shape=jax.ShapeDtypeStruct((T, BW), F32),
        scratch_shapes=[pltpu.VMEM((D, 2 * BW), BF16)],
        compiler_params=_cparams(("arbitrary",)),
        name="glu",
    )(h, w_in, b_in)


CONV_HALO = 16


def _conv_kernel(prev_ref, cur_ref, next_ref, w_ref, b_ref, g_ref, bb_ref, o_ref, ext_ref):
    i = pl.program_id(0)
    has_prev = i >= 2
    has_next = jnp.logical_and(i >= 1, i < NRB - 1)
    ext_ref[0:CONV_HALO, :] = jnp.where(has_prev, prev_ref[RB - CONV_HALO:RB, :], 0.0)
    ext_ref[CONV_HALO:CONV_HALO + RB, :] = cur_ref[...]
    ext_ref[CONV_HALO + RB:2 * CONV_HALO + RB, :] = jnp.where(has_next, next_ref[0:CONV_HALO, :], 0.0)
    off = CONV_HALO - CONV_W // 2
    acc = jnp.zeros((RB, BW), F32)
    for k in range(CONV_W):
        acc = acc + ext_ref[off + k:off + k + RB, :] * w_ref[k:k + 1, :]
    y = _ln0(acc + b_ref[...]) * g_ref[...] + bb_ref[...]
    o_ref[...] = (y * jax.nn.sigmoid(y)).astype(BF16)


def conv_ln(y, w_dw, b_dw, ln_g, ln_b):
    return pl.pallas_call(
        _conv_kernel,
        grid=(NRB,),
        in_specs=[pl.BlockSpec((RB, BW), lambda i: (jnp.maximum(i - 1, 0), 0)),
                  pl.BlockSpec((RB, BW), lambda i: (i, 0)),
                  pl.BlockSpec((RB, BW), lambda i: (jnp.minimum(i + 1, NRB - 1), 0)),
                  pl.BlockSpec((CONV_W + 1, BW), lambda i: (0, 0)),
                  pl.BlockSpec((1, BW), lambda i: (0, 0)),
                  pl.BlockSpec((1, BW), lambda i: (0, 0)),
                  pl.BlockSpec((1, BW), lambda i: (0, 0))],
        out_specs=pl.BlockSpec((RB, BW), lambda i: (i, 0)),
        out_shape=jax.ShapeDtypeStruct((T, BW), BF16),
        scratch_shapes=[pltpu.VMEM((RB + 2 * CONV_HALO, BW), F32)],
        compiler_params=_cparams(("arbitrary",)),
        name="conv_ln",
    )(y, y, y, w_dw, b_dw, ln_g, ln_b)


def _dft_tables(n):
    k = np.arange(n, dtype=np.int64)
    ang = 2.0 * np.pi * ((k[:, None] * k[None, :]) % n).astype(np.float64) / n
    return np.cos(ang), np.sin(ang)


def _fproj_kernel(h_ref, w_ref, b_ref, cs_ref, ar_ref, ai_ref, wbf_ref):
    _cast_once(w_ref, wbf_ref)
    z = _dot(h_ref[...], wbf_ref[...]) + b_ref[0]
    a = _dot3(z, cs_ref)
    ar_ref[...] = a[:, :BW]
    ai_ref[...] = a[:, BW:]


def fourier_proj(l, h, w_in, b_in, cs):
    return pl.pallas_call(
        _fproj_kernel,
        grid=(NRB,),
        in_specs=[pl.BlockSpec((RB, D), lambda i: (i, 0)),
                  _w_in_spec(l, BW, COL_C // BW),
                  _b_in_spec(l, BW, COL_C // BW),
                  pl.BlockSpec((2, BW, 2 * BW), lambda i: (0, 0, 0))],
        out_specs=[pl.BlockSpec((RB, BW), lambda i: (i, 0)),
                   pl.BlockSpec((RB, BW), lambda i: (i, 0))],
        out_shape=[jax.ShapeDtypeStruct((T, BW), F32), jax.ShapeDtypeStruct((T, BW), F32)],
        scratch_shapes=[pltpu.VMEM((D, BW), BF16)],
        compiler_params=_cparams(("arbitrary",)),
        name="fourier_proj",
    )(h, w_in, b_in, cs)


FFT_TILE_N2 = 4


def _fft1_kernel(ar_ref, ai_ref, m_ref, ct_ref, st_ref, br_ref, bi_ref):
    x = jnp.concatenate([ar_ref[...], ai_ref[...]], axis=0)
    b = _dot3_t(m_ref, x)
    b_re, b_im = b[:FFT_N1], b[FFT_N1:]
    reps = BW // LANE
    for j in range(FFT_TILE_N2):
        ct = jnp.concatenate([ct_ref[j]] * reps, axis=-1)
        st = jnp.concatenate([st_ref[j]] * reps, axis=-1)
        lanes = slice(j * BW, (j + 1) * BW)
        br_ref[:, lanes] = b_re[:, lanes] * ct + b_im[:, lanes] * st
        bi_ref[:, lanes] = b_im[:, lanes] * ct - b_re[:, lanes] * st


def fft_stage1(ar2d, ai2d, m1, ct, st):
    tn = FFT_TILE_N2 * BW
    return pl.pallas_call(
        _fft1_kernel,
        grid=(FFT_N2 // FFT_TILE_N2,),
        in_specs=[pl.BlockSpec((FFT_N1, tn), lambda j: (0, j)),
                  pl.BlockSpec((FFT_N1, tn), lambda j: (0, j)),
                  pl.BlockSpec((2, 2 * FFT_N1, 2 * FFT_N1), lambda j: (0, 0, 0)),
                  pl.BlockSpec((FFT_TILE_N2, FFT_N1, LANE), lambda j: (j, 0, 0)),
                  pl.BlockSpec((FFT_TILE_N2, FFT_N1, LANE), lambda j: (j, 0, 0))],
        out_specs=[pl.BlockSpec((FFT_N1, tn), lambda j: (0, j)),
                   pl.BlockSpec((FFT_N1, tn), lambda j: (0, j))],
        out_shape=[jax.ShapeDtypeStruct((FFT_N1, FFT_N2 * BW), F32)] * 2,
        compiler_params=_cparams(("arbitrary",)),
        name="fft_stage1",
    )(ar2d, ai2d, m1, ct, st)


FFT_TILE_K1 = 8


def _fft2_kernel(br_ref, bi_ref, c_ref, s_ref, p_ref, o_ref, *, scale):
    ys = [(_dot3_t(c_ref, br_ref[a]) + _dot3_t(s_ref, bi_ref[a])) * scale for a in range(FFT_TILE_K1)]
    y = jnp.concatenate(ys, axis=0).astype(BF16)
    y = _dot(p_ref[...], y)
    o_ref[...] = y.reshape(FFT_N2, FFT_TILE_K1, BW)


def fft_stage2(br3d, bi3d, c128, s128, perm, scale):
    return pl.pallas_call(
        functools.partial(_fft2_kernel, scale=scale),
        grid=(FFT_N1 // FFT_TILE_K1,),
        in_specs=[pl.BlockSpec((FFT_TILE_K1, FFT_N2, BW), lambda a: (a, 0, 0)),
                  pl.BlockSpec((FFT_TILE_K1, FFT_N2, BW), lambda a: (a, 0, 0)),
                  pl.BlockSpec((2, FFT_N2, FFT_N2), lambda a: (0, 0, 0)),
                  pl.BlockSpec((2, FFT_N2, FFT_N2), lambda a: (0, 0, 0)),
                  pl.BlockSpec((FFT_TILE_K1 * FFT_N2, FFT_TILE_K1 * FFT_N2), lambda a: (0, 0))],
        out_specs=pl.BlockSpec((FFT_N2, FFT_TILE_K1, BW), lambda a: (0, a, 0)),
        out_shape=jax.ShapeDtypeStruct((FFT_N2, FFT_N1, BW), F32),
        compiler_params=_cparams(("arbitrary",)),
        name="fft_stage2",
    )(br3d, bi3d, c128, s128, perm)


def _dft_ctx_kernel(ar_ref, ai_ref, c_ref, s_ref, o_ref, *, scale):
    o_ref[...] = (_dot3_t(c_ref, ar_ref[...]) + _dot3_t(s_ref, ai_ref[...])) * scale


def dft_ctx(ar, ai, c, s, scale):
    return pl.pallas_call(
        functools.partial(_dft_ctx_kernel, scale=scale),
        grid=(1,),
        in_specs=[pl.BlockSpec((N_CTX, BW), lambda i: (0, 0)),
                  pl.BlockSpec((N_CTX, BW), lambda i: (0, 0)),
                  pl.BlockSpec((2, N_CTX, N_CTX), lambda i: (0, 0, 0)),
                  pl.BlockSpec((2, N_CTX, N_CTX), lambda i: (0, 0, 0))],
        out_specs=pl.BlockSpec((N_CTX, BW), lambda i: (0, 0)),
        out_shape=jax.ShapeDtypeStruct((N_CTX, BW), F32),
        compiler_params=_cparams(("arbitrary",)),
        name="dft_ctx",
    )(ar, ai, c, s)


def _fourier_consts():
    c64, s64 = _dft_tables(FFT_N1)
    m1 = np.block([[c64, s64], [-s64, c64]]).astype(np.float32)
    k1 = np.arange(FFT_N1, dtype=np.float64)
    n2 = np.arange(FFT_N2, dtype=np.float64)
    phi = 2.0 * np.pi * n2[:, None] * k1[None, :] / SEQ
    ct = np.repeat(np.cos(phi)[:, :, None], LANE, axis=2).astype(np.float32)
    st = np.repeat(np.sin(phi)[:, :, None], LANE, axis=2).astype(np.float32)
    c128, s128 = _dft_tables(FFT_N2)
    gw = BW // F_GROUPS
    cg, sg = _dft_tables(gw)
    cs = np.zeros((BW, 2 * BW), np.float32)
    for g in range(F_GROUPS):
        sl = slice(g * gw, (g + 1) * gw)
        cs[sl, sl] = cg
        cs[sl, BW + g * gw:BW + (g + 1) * gw] = -sg
    rows = np.arange(FFT_TILE_K1 * FFT_N2)
    perm = np.zeros((rows.size, rows.size), np.float32)
    perm[rows, (rows % FFT_TILE_K1) * FFT_N2 + rows // FFT_TILE_K1] = 1.0
    cc, sc = _dft_tables(N_CTX)
    return dict(m1=_hilo(m1), ct=jnp.asarray(ct), st=jnp.asarray(st), c128=_hilo(c128), s128=_hilo(s128),
                cs=_hilo(cs), perm=jnp.asarray(perm, dtype=BF16), cc=_hilo(cc), sc=_hilo(sc))


def fourier_mix(l, h, w_in, b_in, fc):
    gw = BW // F_GROUPS
    ar, ai = fourier_proj(l, h, w_in, b_in, fc["cs"])
    y_ctx = dft_ctx(ar, ai, fc["cc"], fc["sc"], 1.0 / math.sqrt(N_CTX * gw))
    ar2 = ar[N_CTX:].reshape(FFT_N1, FFT_N2 * BW)
    ai2 = ai[N_CTX:].reshape(FFT_N1, FFT_N2 * BW)
    br, bi = fft_stage1(ar2, ai2, fc["m1"], fc["ct"], fc["st"])
    y = fft_stage2(br.reshape(FFT_N1, FFT_N2, BW), bi.reshape(FFT_N1, FFT_N2, BW),
                   fc["c128"], fc["s128"], fc["perm"], 1.0 / math.sqrt(SEQ * gw))
    return jnp.concatenate([y_ctx, y.reshape(SEQ, BW)], axis=0)


VT_ROWS = 80
LOG2E = math.log2(math.e)


def _mla_proj_kernel(h_ref, wcq_ref, bcq_ref, wkv_ref, bkv_ref, wkr_ref, bkr_ref, qg_ref, kvg_ref, wq_ref, wqr_ref,
                     wk_ref, wvt_ref, e_ref, tq_c_ref, tq_s_ref, tk_ref, q_ref, k_ref, vt_ref, wcq_bf, wkv_bf):
    _cast_once(wcq_ref, wcq_bf)
    _cast_once(wkv_ref, wkv_bf)
    h = h_ref[...]
    cq = _dot(h, wcq_bf[...]) + bcq_ref[0]
    cq = cq * lax.rsqrt(jnp.mean(cq * cq, axis=-1, keepdims=True) + LN_EPS) * qg_ref[...]
    cq = cq.astype(BF16)
    cos_t = jnp.concatenate([tq_c_ref[...]] * HEADS, axis=-1)
    sin_t = jnp.concatenate([tq_s_ref[...]] * HEADS, axis=-1)
    q = (_dot(cq, wq_ref[...]) * cos_t + _dot(cq, wqr_ref[...]) * sin_t) * (MLA_SCALE * LOG2E)
    q_ref[...] = q.astype(BF16)
    ckv = _dot(h, wkv_bf[...]) + bkv_ref[0]
    ckv = ckv * lax.rsqrt(jnp.mean(ckv * ckv, axis=-1, keepdims=True) + LN_EPS) * kvg_ref[...]
    ckv = ckv.astype(BF16)
    kr = ((_dot(h, wkr_ref[...]) + bkr_ref[...]) * tk_ref[...]).astype(BF16)
    k_ref[...] = (_dot(ckv, wk_ref[...]) + _dot(kr, e_ref[...])).astype(BF16)
    vt = lax.dot_general(wvt_ref[...], ckv, (((1,), (1,)), ((), ())), preferred_element_type=F32)
    ones_rows = (lax.broadcasted_iota(jnp.int32, (VT_ROWS - V_HEAD, RB), 0) == 0).astype(BF16)
    for hd in range(HEADS):
        vt_ref[0, hd * VT_ROWS:hd * VT_ROWS + V_HEAD, :] = vt[hd * V_HEAD:(hd + 1) * V_HEAD, :].astype(BF16)
        vt_ref[0, hd * VT_ROWS + V_HEAD:(hd + 1) * VT_ROWS, :] = ones_rows


def mla_proj(l, h, w_in, b_in, wkr, bkr, qg, kvg, wq, wqr, wk, wvt, e, tq_c, tq_s, tk):
    const = lambda i: (0, 0)
    row = lambda i: (i, 0)
    return pl.pallas_call(
        _mla_proj_kernel,
        grid=(NRB,),
        in_specs=[pl.BlockSpec((RB, D), row),
                  _w_in_spec(l, Q_LORA, COL_D // Q_LORA),
                  _b_in_spec(l, Q_LORA, COL_D // Q_LORA),
                  _w_in_spec(l, KV_LORA, COL_KV // KV_LORA),
                  _b_in_spec(l, KV_LORA, COL_KV // KV_LORA),
                  pl.BlockSpec((D, LANE), const),
                  pl.BlockSpec((1, LANE), const),
                  pl.BlockSpec((1, Q_LORA), const),
                  pl.BlockSpec((1, KV_LORA), const),
                  pl.BlockSpec((Q_LORA, HEADS * HEAD_PAD), const),
                  pl.BlockSpec((Q_LORA, HEADS * HEAD_PAD), const),
                  pl.BlockSpec((KV_LORA, HEADS * HEAD_PAD), const),
                  pl.BlockSpec((HEADS * V_HEAD, KV_LORA), const),
                  pl.BlockSpec((LANE, HEADS * HEAD_PAD), const),
                  pl.BlockSpec((RB, HEAD_PAD), row),
                  pl.BlockSpec((RB, HEAD_PAD), row),
                  pl.BlockSpec((RB, LANE), row)],
        out_specs=[pl.BlockSpec((RB, HEADS * HEAD_PAD), row),
                   pl.BlockSpec((RB, HEADS * HEAD_PAD), row),
                   pl.BlockSpec((1, HEADS * VT_ROWS, RB), lambda i: (i, 0, 0))],
        out_shape=[jax.ShapeDtypeStruct((T, HEADS * HEAD_PAD), BF16),
                   jax.ShapeDtypeStruct((T, HEADS * HEAD_PAD), BF16),
                   jax.ShapeDtypeStruct((NRB, HEADS * VT_ROWS, RB), BF16)],
        scratch_shapes=[pltpu.VMEM((D, Q_LORA), BF16), pltpu.VMEM((D, KV_LORA), BF16)],
        compiler_params=_cparams(("arbitrary",)),
        name="mla_proj",
    )(h, w_in, b_in, w_in, b_in, wkr, bkr, qg, kvg, wq, wqr, wk, wvt, e, tq_c, tq_s, tk)


KV_TILE = RB
KV_STAGE = 1024
N_STAGE = SEQ // KV_STAGE
TILES_PER_STAGE = KV_STAGE // KV_TILE
CTX_TILES = N_CTX // KV_TILE


HEAD_GROUP = 2


def _attn_scores(k_ref, q, g, start, n):
    kt = k_ref[pl.ds(start, n), g * HEAD_PAD:(g + 1) * HEAD_PAD]
    return lax.dot_general(kt, q, (((1,), (1,)), ((), ())), preferred_element_type=F32)


def _attn_out(acc):
    return acc[:V_HEAD, :] / acc[V_HEAD:V_HEAD + 1, :]


def _attn_kernel(q_ref, k_ref, vt_ref, o_ref, sa_ref, sb_ref):
    heads = range(HEAD_GROUP)
    qs = [q_ref[:, g * HEAD_PAD:(g + 1) * HEAD_PAD] for g in heads]

    def scores(g, stage):
        return _attn_scores(k_ref, qs[g], g, pl.multiple_of(N_CTX + stage * KV_STAGE, KV_TILE), KV_STAGE)

    def values(g, tile):
        return vt_ref[tile, g * VT_ROWS:(g + 1) * VT_ROWS, :]

    def softmax_pv(g, s_ref, stage, m, acc):
        m_new = jnp.maximum(m, jnp.max(s_ref[g], axis=0, keepdims=True))
        p = jnp.exp2(s_ref[g] - m_new).astype(BF16)
        acc = jnp.exp2(m - m_new) * acc
        for u in range(TILES_PER_STAGE):
            acc = acc + _dot(values(g, CTX_TILES + stage * TILES_PER_STAGE + u), p[u * KV_TILE:(u + 1) * KV_TILE, :])
        return m_new, acc

    ms, accs = [], []
    for g in heads:
        s0 = _attn_scores(k_ref, qs[g], g, 0, N_CTX)
        ms.append(jnp.max(s0, axis=0, keepdims=True))
        accs.append(_dot(values(g, 0), jnp.exp2(s0 - ms[g]).astype(BF16)))
        sa_ref[g] = scores(g, 0)

    def body(j, carry):
        ms, accs = list(carry[0]), list(carry[1])
        for g in heads:
            sb_ref[g] = scores(g, 2 * j + 1)
        for g in heads:
            ms[g], accs[g] = softmax_pv(g, sa_ref, 2 * j, ms[g], accs[g])
        for g in heads:
            sa_ref[g] = scores(g, 2 * j + 2)
        for g in heads:
            ms[g], accs[g] = softmax_pv(g, sb_ref, 2 * j + 1, ms[g], accs[g])
        return tuple(ms), tuple(accs)

    ms, accs = lax.fori_loop(0, N_STAGE // 2 - 1, body, (tuple(ms), tuple(accs)))
    ms, accs = list(ms), list(accs)
    for g in heads:
        sb_ref[g] = scores(g, N_STAGE - 1)
    for g in heads:
        ms[g], accs[g] = softmax_pv(g, sa_ref, N_STAGE - 2, ms[g], accs[g])
    for g in heads:
        _, acc = softmax_pv(g, sb_ref, N_STAGE - 1, ms[g], accs[g])
        o_ref[g * V_HEAD:(g + 1) * V_HEAD, :] = _attn_out(acc)


def _attn_ctx_kernel(q_ref, k_ref, vt_ref, o_ref):
    for g in range(HEAD_GROUP):
        s = _attn_scores(k_ref, q_ref[:, g * HEAD_PAD:(g + 1) * HEAD_PAD], g, 0, N_CTX)
        p = jnp.exp2(s - jnp.max(s, axis=0, keepdims=True)).astype(BF16)
        o_ref[g * V_HEAD:(g + 1) * V_HEAD, :] = _attn_out(_dot(vt_ref[0, g * VT_ROWS:(g + 1) * VT_ROWS, :], p))


def attention(q, k, vt):
    assert N_CTX == KV_TILE and N_STAGE % 2 == 0
    o_ctx = pl.pallas_call(
        _attn_ctx_kernel,
        grid=(HEADS // HEAD_GROUP,),
        in_specs=[pl.BlockSpec((N_CTX, HEAD_GROUP * HEAD_PAD), lambda h: (0, h)),
                  pl.BlockSpec((N_CTX, HEAD_GROUP * HEAD_PAD), lambda h: (0, h)),
                  pl.BlockSpec((1, HEAD_GROUP * VT_ROWS, KV_TILE), lambda h: (0, h, 0))],
        out_specs=pl.BlockSpec((HEAD_GROUP * V_HEAD, N_CTX), lambda h: (h, 0)),
        out_shape=jax.ShapeDtypeStruct((HEADS * V_HEAD, N_CTX), F32),
        compiler_params=_cparams(("arbitrary",)),
        name="attention_ctx",
    )(q, k, vt)
    o_lat = pl.pallas_call(
        _attn_kernel,
        grid=(HEADS // HEAD_GROUP, SEQ // RB),
        in_specs=[pl.BlockSpec((RB, HEAD_GROUP * HEAD_PAD), lambda h, i: (i + N_CTX // RB, h)),
                  pl.BlockSpec((T, HEAD_GROUP * HEAD_PAD), lambda h, i: (0, h)),
                  pl.BlockSpec((NRB, HEAD_GROUP * VT_ROWS, KV_TILE), lambda h, i: (0, h, 0))],
        out_specs=pl.BlockSpec((HEAD_GROUP * V_HEAD, RB), lambda h, i: (h, i)),
        out_shape=jax.ShapeDtypeStruct((HEADS * V_HEAD, SEQ), F32),
        scratch_shapes=[pltpu.VMEM((HEAD_GROUP, KV_STAGE, RB), F32),
                        pltpu.VMEM((HEAD_GROUP, KV_STAGE, RB), F32)],
        compiler_params=_cparams(("arbitrary", "arbitrary")),
        name="attention",
    )(q, k, vt)
    return jnp.concatenate([o_ctx, o_lat], axis=1)


MERGE_TN = 512


N_BRANCH = 4


MERGE_RB = 768


def _merge_kernel(h_ref, a_ref, b_ref, c_ref, dt_ref, *refs):
    wg_refs, bg_refs = refs[:N_BRANCH], refs[N_BRANCH:2 * N_BRANCH]
    wb_ref, y_ref, wg_bf = refs[2 * N_BRANCH:]

    @pl.when(pl.program_id(1) == 0)
    def _():
        for n in range(N_BRANCH):
            _cast_transposed(wg_refs[n], wg_bf.at[n])

    h = h_ref[...]
    branches = (a_ref[...], b_ref[...], c_ref[...].astype(BF16), dt_ref[...].T.astype(BF16))
    y = jnp.zeros((MERGE_RB, MERGE_TN), F32)
    for n, br in enumerate(branches):
        gate = jax.nn.sigmoid(_dot(h, wg_bf[n]) + bg_refs[n][...])
        y = y + gate * _dot(br, wb_ref[0, n])
    y_ref[...] = y.astype(BF16)


def merge(l, h, br_a, br_b, br_c, br_dt, w_in_t, bg, wb):
    row = lambda j, i: (i, 0)
    n_col = D // MERGE_TN
    gate_specs = [pl.BlockSpec((pl.Squeezed(), pl.Element(MERGE_TN), pl.Element(D)),
                               functools.partial(
                                   lambda j, i, n: (l, pl.multiple_of(COL_G + n * D + j * MERGE_TN, QK_ROPE), 0), n=n),
                               pipeline_mode=pl.Buffered(1))
                  for n in range(N_BRANCH)]
    bias_specs = [pl.BlockSpec((1, MERGE_TN), functools.partial(lambda j, i, n: (0, n * n_col + j), n=n))
                  for n in range(N_BRANCH)]
    return pl.pallas_call(
        _merge_kernel,
        grid=(n_col, T // MERGE_RB),
        in_specs=[pl.BlockSpec((MERGE_RB, D), row),
                  pl.BlockSpec((MERGE_RB, BW), row),
                  pl.BlockSpec((MERGE_RB, BW), row),
                  pl.BlockSpec((MERGE_RB, BW), row),
                  pl.BlockSpec((BW, MERGE_RB), lambda j, i: (0, i))]
                 + gate_specs + bias_specs
                 + [pl.BlockSpec((1, N_BRANCH, BW, MERGE_TN), lambda j, i: (l, 0, 0, j))],
        out_specs=pl.BlockSpec((MERGE_RB, MERGE_TN), lambda j, i: (i, j)),
        out_shape=jax.ShapeDtypeStruct((T, D), BF16),
        scratch_shapes=[pltpu.VMEM((N_BRANCH, D, MERGE_TN), BF16)],
        compiler_params=_cparams(("arbitrary", "arbitrary")),
        name="merge",
    )(h, br_a, br_b, br_c, br_dt, *([w_in_t] * N_BRANCH), *([bg] * N_BRANCH), wb)


ROUTER_W = 128


def _post_mix_kernel(y_ref, xc_ref, xl_ref, mod_ref, wo_ref, g_ref, b_ref, wr_ref, br_ref,
                     x1_ref, h2_ref, ids_ref, gates_ref):
    mix = _dot(y_ref[...], wo_ref[0])
    x1 = _ln0(ALPHA * _stream_block(xc_ref, xl_ref) + _mod_row(mod_ref, 2) * mix) * g_ref[...] + b_ref[...]
    x1_ref[...] = x1
    h2 = _ln0(x1) * (1.0 + _mod_row(mod_ref, 4)) + _mod_row(mod_ref, 3)
    _rows_to_slabs(h2, h2_ref)
    logits = _dot3(h2, wr_ref) + br_ref[...]
    lane = lax.broadcasted_iota(jnp.int32, (RB, ROUTER_W), 1).astype(F32)
    neg = jnp.float32(-jnp.inf)
    big = jnp.float32(ROUTER_W)
    gl = jnp.where(lane < N_GROUPS, logits, neg)
    gmax = jnp.max(gl, axis=-1, keepdims=True)
    g_idx = jnp.min(jnp.where(gl == gmax, lane, big), axis=-1, keepdims=True)
    g_p = 1.0 / jnp.sum(jnp.exp(gl - gmax), axis=-1, keepdims=True)
    e_lane = lane - N_GROUPS
    in_grp = jnp.logical_and(e_lane >= g_idx * EXP_PER_GROUP, e_lane < (g_idx + 1) * EXP_PER_GROUP)
    el = jnp.where(in_grp, logits, neg)
    v1 = jnp.max(el, axis=-1, keepdims=True)
    i1 = jnp.min(jnp.where(el == v1, lane, big), axis=-1, keepdims=True)
    el2 = jnp.where(lane == i1, neg, el)
    v2 = jnp.max(el2, axis=-1, keepdims=True)
    i2 = jnp.min(jnp.where(el2 == v2, lane, big), axis=-1, keepdims=True)
    e21 = jnp.exp(v2 - v1)
    p1 = 1.0 / (1.0 + e21)
    p2 = e21 * p1
    ids_ref[...] = jnp.where(lane == 0, i1 - N_GROUPS, jnp.where(lane == 1, i2 - N_GROUPS, 0.0)).astype(jnp.int32)
    gates_ref[...] = jnp.where(lane == 0, p1 * g_p, jnp.where(lane == 1, p2 * g_p, 0.0))


def post_mix(l, y, x_ctx, x_lat, mod, wo, ln_g, ln_b, wr, br):
    row = lambda i: (i, 0)
    const = lambda i: (0, 0)
    return pl.pallas_call(
        _post_mix_kernel,
        grid=(NRB,),
        in_specs=[pl.BlockSpec((RB, D), row)] + _stream_specs()
                 + [pl.BlockSpec((8, 6 * D), const),
                  pl.BlockSpec((1, D, D), lambda i: (l, 0, 0)),
                  pl.BlockSpec((1, D), const),
                  pl.BlockSpec((1, D), const),
                  pl.BlockSpec((2, D, ROUTER_W), lambda i: (0, 0, 0)),
                  pl.BlockSpec((1, ROUTER_W), const)],
        out_specs=[pl.BlockSpec((RB, D), row),
                   pl.BlockSpec((RB * ROW_SUB, LANE), row),
                   pl.BlockSpec((RB, ROUTER_W), row),
                   pl.BlockSpec((RB, ROUTER_W), row)],
        out_shape=[jax.ShapeDtypeStruct((T, D), F32),
                   jax.ShapeDtypeStruct((T * ROW_SUB, LANE), F32),
                   jax.ShapeDtypeStruct((T, ROUTER_W), jnp.int32),
                   jax.ShapeDtypeStruct((T, ROUTER_W), F32)],
        compiler_params=_cparams(("arbitrary",)),
        name="post_mix",
    )(y, x_ctx, x_lat, mod, wo, ln_g, ln_b, wr, br)


N_ROWS = -(-(T * TOP_K + N_EXPERTS * (MOE_BLOCK - 1)) // MOE_BLOCK) * MOE_BLOCK
N_BLK = N_ROWS // MOE_BLOCK


N_DMA_PRIORITIES = 2


GATHER_UNROLL = 8


def _dense_dst_row(r0, u):
    return (r0 + u) * ROW_SUB


def _row_gather(src_hbm, idx_ref, base, n, dst_ref, sem, first=0, dst_row_of=_dense_dst_row):
    def issue(t, _):
        r0 = first + t * GATHER_UNROLL
        for u in range(GATHER_UNROLL):
            r = r0 + u
            src_row = pl.multiple_of(idx_ref[base + r] * ROW_SUB, ROW_SUB)
            dst_row = pl.multiple_of(dst_row_of(r0, u), 8)
            pltpu.make_async_copy(src_hbm.at[pl.ds(src_row, ROW_SUB), :],
                                  dst_ref.at[pl.ds(dst_row, ROW_SUB), :], sem).start(priority=u % N_DMA_PRIORITIES)
        return 0
    lax.fori_loop(0, n // GATHER_UNROLL, issue, 0)


def _rows_from_slabs(buf_ref, start, n, stride):
    return jnp.concatenate([buf_ref[pl.ds(start + c, n, stride=stride), :] for c in range(ROW_SUB)], axis=-1)


def _rows_to_slabs(val, out_ref):
    n = val.shape[0]
    for c in range(ROW_SUB):
        out_ref[pl.ds(c, n, stride=ROW_SUB), :] = val[:, c * LANE:(c + 1) * LANE]


ITEM_BLKS = 3
ITEM_ROWS = ITEM_BLKS * MOE_BLOCK
N_ITEMS = (N_BLK + (ITEM_BLKS - 1) * N_EXPERTS) // ITEM_BLKS
EXP_CHUNK = 512
N_CHUNK = D_EXPERT // EXP_CHUNK
BLK_SLABS = MOE_BLOCK * ROW_SUB


WAIT_ROWS = 8


def _expert_kernel(item_exp_ref, item_blk0_ref, item_nblk_ref, item_cnt_ref, n_used_ref, row_tok_ref, h_hbm, w1_ref,
                   w3_ref, w2_ref, y_hbm, xslab, yacc, stage, w1c, w3c, w2c, sem_x, sem_y):
    i, j = pl.program_id(0), pl.program_id(1)
    nblk = item_nblk_ref[i]
    blk0 = item_blk0_ref[i]
    xs = xslab.at[i % 2]

    def gather_item(item):
        _row_gather(h_hbm, row_tok_ref, item_blk0_ref[item] * MOE_BLOCK, item_cnt_ref[item], xslab.at[item % 2],
                    sem_x)

    @pl.when(jnp.logical_and(i == 0, j == 0))
    def _():
        xslab[...] = jnp.zeros(xslab.shape, F32)
        gather_item(0)

    @pl.when(nblk > 0)
    def _():
        w1c[...] = w1_ref[0, 0].astype(BF16)
        w3c[...] = w3_ref[0, 0].astype(BF16)
        w2c[...] = w2_ref[0, 0].astype(BF16)

    last = j == N_CHUNK - 1

    def rows_of(b):
        return pl.ds(pl.multiple_of(b * MOE_BLOCK, MOE_BLOCK), MOE_BLOCK)

    def partial_out(xb):
        a = _dot(xb, w1c[...])
        g = _dot(xb, w3c[...])
        return _dot((a * jax.nn.sigmoid(a) * g).astype(BF16), w2c[...])

    def y_copy(b, slot):
        dst = pl.multiple_of((blk0 + b) * BLK_SLABS, BLK_SLABS)
        return pltpu.make_async_copy(stage.at[slot], y_hbm.at[pl.ds(dst, BLK_SLABS), :], sem_y.at[slot])

    def x_block(b):
        return _rows_from_slabs(xs, b * BLK_SLABS, MOE_BLOCK, ROW_SUB).astype(BF16)

    @pl.when(j == 0)
    def _():
        def wait_rows(t, _):
            pltpu.make_async_copy(h_hbm.at[pl.ds(0, WAIT_ROWS * ROW_SUB), :],
                                  xs.at[pl.ds(0, WAIT_ROWS * ROW_SUB), :], sem_x).wait()
            return 0
        lax.fori_loop(0, item_cnt_ref[i] // WAIT_ROWS, wait_rows, 0)

        @pl.when(i + 1 < N_ITEMS)
        def _():
            gather_item(i + 1)

        def first_block(b, _):
            yacc[rows_of(b), :] = partial_out(x_block(b))
            return 0
        lax.fori_loop(0, nblk, first_block, 0)

    @pl.when(jnp.logical_and(j > 0, jnp.logical_not(last)))
    def _():
        def mid_block(b, _):
            yacc[rows_of(b), :] += partial_out(x_block(b))
            return 0
        lax.fori_loop(0, nblk, mid_block, 0)

    @pl.when(last)
    def _():
        def last_block(b, _):
            slot = b % 2

            @pl.when(b >= 2)
            def _():
                y_copy(b - 2, slot).wait()
            _rows_to_slabs(yacc[rows_of(b), :] + partial_out(x_block(b)), stage.at[slot])
            y_copy(b, slot).start()
            return 0
        lax.fori_loop(0, nblk, last_block, 0)

        for back in (1, 2):
            @pl.when(nblk >= back)
            def _():
                b = nblk - back
                y_copy(b, b % 2).wait()

    @pl.when(jnp.logical_and(i == N_ITEMS - 1, last))
    def _():
        stage[0] = jnp.zeros((BLK_SLABS, LANE), F32)

        def fill(b, _):
            dst = pl.multiple_of(b * BLK_SLABS, BLK_SLABS)
            cp = pltpu.make_async_copy(stage.at[0], y_hbm.at[pl.ds(dst, BLK_SLABS), :], sem_y.at[0])
            cp.start()
            cp.wait()
            return 0
        lax.fori_loop(n_used_ref[0], N_BLK, fill, 0)


def experts(l, item_exp, item_blk0, item_nblk, item_cnt, n_used, row_tok, h2_slabs, w1, w3, w2):
    wspec = lambda blk, imap: pl.BlockSpec(blk, imap)
    grid_spec = pltpu.PrefetchScalarGridSpec(
        num_scalar_prefetch=6,
        grid=(N_ITEMS, N_CHUNK),
        in_specs=[pl.BlockSpec(memory_space=pl.ANY),
                  wspec((1, 1, D, EXP_CHUNK), lambda i, j, ie, *_: (l, ie[i], 0, j)),
                  wspec((1, 1, D, EXP_CHUNK), lambda i, j, ie, *_: (l, ie[i], 0, j)),
                  wspec((1, 1, EXP_CHUNK, D), lambda i, j, ie, *_: (l, ie[i], j, 0))],
        out_specs=pl.BlockSpec(memory_space=pl.ANY),
        scratch_shapes=[pltpu.VMEM((2, ITEM_BLKS * BLK_SLABS, LANE), F32),
                        pltpu.VMEM((ITEM_ROWS, D), F32),
                        pltpu.VMEM((2, BLK_SLABS, LANE), F32),
                        pltpu.VMEM((D, EXP_CHUNK), BF16),
                        pltpu.VMEM((D, EXP_CHUNK), BF16),
                        pltpu.VMEM((EXP_CHUNK, D), BF16),
                        pltpu.SemaphoreType.DMA(()),
                        pltpu.SemaphoreType.DMA((2,))],
    )
    return pl.pallas_call(
        _expert_kernel,
        grid_spec=grid_spec,
        out_shape=jax.ShapeDtypeStruct((N_ROWS * ROW_SUB, LANE), F32),
        compiler_params=_cparams(("arbitrary", "arbitrary")),
        name="experts",
    )(item_exp, item_blk0, item_nblk, item_cnt, n_used, row_tok, h2_slabs, w1, w3, w2)


COMBINE_SUB = 64
COMBINE_PITCH = 40


def _combine_dst_row(r0, u):
    return r0 * (COMBINE_PITCH // TOP_K) + (u // TOP_K) * COMBINE_PITCH + (u % TOP_K) * ROW_SUB


def _combine_kernel(pos_ref, y_hbm, gates_ref, x1_ref, mod_ref, g_ref, b_ref, nmod_ref,
                    x2c_ref, x2l_ref, hn_ref, ybuf, sems):
    i = pl.program_id(0)
    slot = i % 2
    n = RB * TOP_K
    n_sub = RB // COMBINE_SUB

    @pl.when(i == 0)
    def _():
        _row_gather(y_hbm, pos_ref, 0, n, ybuf.at[0], sems.at[0], dst_row_of=_combine_dst_row)

    pltpu.make_async_copy(y_hbm.at[pl.ds(0, n * ROW_SUB), :], ybuf.at[slot, pl.ds(0, n * ROW_SUB), :],
                          sems.at[slot]).wait()
    shift5, shift0, scale1 = _mod_row(mod_ref, 5), _mod_row(nmod_ref, 0), 1.0 + _mod_row(nmod_ref, 1)
    for s in range(n_sub):
        @pl.when(i + 1 < NRB)
        def _():
            _row_gather(y_hbm, pos_ref, (i + 1) * n, n // n_sub, ybuf.at[1 - slot], sems.at[1 - slot],
                        first=s * (n // n_sub), dst_row_of=_combine_dst_row)

        rows = slice(s * COMBINE_SUB, (s + 1) * COMBINE_SUB)
        base = s * COMBINE_SUB * COMBINE_PITCH
        y0 = _rows_from_slabs(ybuf.at[slot], base, COMBINE_SUB, COMBINE_PITCH)
        y1 = _rows_from_slabs(ybuf.at[slot], base + ROW_SUB, COMBINE_SUB, COMBINE_PITCH)
        gates = gates_ref[rows, :]
        f = gates[:, 0:1] * y0 + gates[:, 1:2] * y1
        x2 = _ln0(ALPHA * x1_ref[rows, :] + shift5 * f) * g_ref[...] + b_ref[...]
        x2l_ref[rows, :] = x2

        @pl.when(i == 0)
        def _():
            x2c_ref[rows, :] = x2

        hn_ref[rows, :] = (_ln0(x2) * scale1 + shift0).astype(BF16)


def combine(pos, y_slabs, gates, x1, mod, ln_g, ln_b, next_mod):
    row = lambda i, p: (i, 0)
    const = lambda i, p: (0, 0)
    grid_spec = pltpu.PrefetchScalarGridSpec(
        num_scalar_prefetch=1,
        grid=(NRB,),
        in_specs=[pl.BlockSpec(memory_space=pl.ANY),
                  pl.BlockSpec((RB, ROUTER_W), row),
                  pl.BlockSpec((RB, D), row),
                  pl.BlockSpec((8, 6 * D), const),
                  pl.BlockSpec((1, D), const),
                  pl.BlockSpec((1, D), const),
                  pl.BlockSpec((8, 6 * D), const)],
        out_specs=[pl.BlockSpec((RB, D), const),
                   pl.BlockSpec((RB, D), lambda i, p: (jnp.maximum(i - 1, 0), 0)),
                   pl.BlockSpec((RB, D), row)],
        scratch_shapes=[pltpu.VMEM((2, RB * COMBINE_PITCH, LANE), F32),
                        pltpu.SemaphoreType.DMA((2,))],
    )
    return pl.pallas_call(
        _combine_kernel,
        grid_spec=grid_spec,
        out_shape=[jax.ShapeDtypeStruct((N_CTX, D), F32), jax.ShapeDtypeStruct((SEQ, D), F32),
                   jax.ShapeDtypeStruct((T, D), BF16)],
        compiler_params=_cparams(("arbitrary",)),
        name="combine",
    )(pos, y_slabs, gates, x1, mod, ln_g, ln_b, next_mod)


def _routing_tables(ids):
    expert = ids[:, :TOP_K].reshape(-1)
    n = expert.shape[0]
    onehot = (expert[:, None] == jnp.arange(N_EXPERTS, dtype=jnp.int32)[None, :]).astype(jnp.int32)
    csum = jnp.cumsum(onehot, axis=0)
    counts = csum[-1]
    rank = jnp.sum(onehot * csum, axis=1) - 1
    nb = (counts + MOE_BLOCK - 1) // MOE_BLOCK
    bstart = jnp.cumsum(nb) - nb
    pos = (jnp.sum(onehot * (bstart * MOE_BLOCK)[None, :], axis=1) + rank).astype(jnp.int32)
    tok = jnp.arange(n, dtype=jnp.int32) // TOP_K
    row_tok = jnp.zeros((N_ROWS,), jnp.int32).at[pos].set(tok, unique_indices=True, mode="promise_in_bounds")
    n_it = (nb + ITEM_BLKS - 1) // ITEM_BLKS
    it_end = jnp.cumsum(n_it)
    it_start = it_end - n_it
    idx = jnp.arange(N_ITEMS, dtype=jnp.int32)
    e_of = jnp.minimum(jnp.sum(it_end[None, :] <= idx[:, None], axis=1), N_EXPERTS - 1).astype(jnp.int32)
    k = idx - it_start[e_of]
    valid = idx < it_end[-1]
    item_exp = e_of
    item_blk0 = jnp.where(valid, bstart[e_of] + ITEM_BLKS * k, 0).astype(jnp.int32)
    item_nblk = jnp.where(valid, jnp.clip(nb[e_of] - ITEM_BLKS * k, 0, ITEM_BLKS), 0).astype(jnp.int32)
    item_cnt = jnp.where(valid, jnp.clip(counts[e_of] - ITEM_ROWS * k, 0, ITEM_ROWS), 0)
    item_cnt = ((item_cnt + GATHER_UNROLL - 1) // GATHER_UNROLL * GATHER_UNROLL).astype(jnp.int32)
    n_used = jnp.sum(nb).astype(jnp.int32)[None]
    return item_exp, item_blk0, item_nblk, item_cnt, n_used, row_tok, pos


def _rope_tables():
    rows = SEQ // GRID_W
    row = jnp.repeat(jnp.arange(rows, dtype=F32), GRID_W)
    col = jnp.tile(jnp.arange(GRID_W, dtype=F32), rows)
    half = QK_ROPE // 2
    inv = ROPE_THETA ** (-jnp.arange(0, half, 2, dtype=F32) / half)
    ar = row[:, None] * inv
    ac = col[:, None] * inv
    ang = jnp.concatenate([ar, ar, ac, ac], -1)
    cos = jnp.concatenate([jnp.ones((N_CTX, QK_ROPE), F32), jnp.cos(ang)], 0)
    sin = jnp.concatenate([jnp.zeros((N_CTX, QK_ROPE), F32), jnp.sin(ang)], 0)
    one = jnp.ones((T, QK_NOPE), F32)
    zero_n = jnp.zeros((T, QK_NOPE), F32)
    zero_p = jnp.zeros((T, HEAD_PAD - QK_NOPE - QK_ROPE), F32)
    tq_c = jnp.concatenate([one, cos, zero_p], -1)
    tq_s = jnp.concatenate([zero_n, sin, zero_p], -1)
    tk = jnp.concatenate([cos, sin, jnp.zeros((T, LANE - 2 * QK_ROPE), F32)], -1)
    return tq_c, tq_s, tk


_ROT_SRC = np.concatenate([np.arange(8, 16), np.arange(0, 8), np.arange(24, 32), np.arange(16, 24)])
_ROT_SIGN = np.concatenate([-np.ones(8), np.ones(8), -np.ones(8), np.ones(8)]).astype(np.float32)


def _rot_cols(w):
    return w[..., _ROT_SRC] * _ROT_SIGN


def _layer_weights(l, w_in_t, b_in, d_w_uq, d_w_uk, d_w_uv, r_w_grp, r_b_grp, r_w_exp, r_b_exp, a_b_s, b_w_dw):
    bi = b_in[l]
    p = {}
    w_kr, b_kr = w_in_t[l, COL_KR:COL_G, :].T, bi[COL_KR:COL_G]
    padw = LANE - 2 * QK_ROPE
    p["wkr"] = jnp.concatenate([w_kr, _rot_cols(w_kr), jnp.zeros((D, padw), F32)], -1).astype(BF16)
    p["bkr"] = jnp.concatenate([b_kr, _rot_cols(b_kr), jnp.zeros((padw,), F32)])[None]
    wq = d_w_uq[l].reshape(Q_LORA, HEADS, QK_NOPE + QK_ROPE)
    zpad = jnp.zeros((Q_LORA, HEADS, HEAD_PAD - QK_NOPE - QK_ROPE), F32)
    p["wq"] = jnp.concatenate([wq, zpad], -1).reshape(Q_LORA, HEADS * HEAD_PAD).astype(BF16)
    p["wqr"] = jnp.concatenate([jnp.zeros((Q_LORA, HEADS, QK_NOPE), F32), _rot_cols(wq[..., QK_NOPE:]), zpad],
                               -1).reshape(Q_LORA, HEADS * HEAD_PAD).astype(BF16)
    wk = d_w_uk[l].reshape(KV_LORA, HEADS, QK_NOPE)
    p["wk"] = jnp.concatenate([wk, jnp.zeros((KV_LORA, HEADS, HEAD_PAD - QK_NOPE), F32)],
                              -1).reshape(KV_LORA, HEADS * HEAD_PAD).astype(BF16)
    p["wvt"] = d_w_uv[l].T.astype(BF16)
    e = np.zeros((LANE, HEADS, HEAD_PAD), np.float32)
    for j in range(QK_ROPE):
        e[j, :, QK_NOPE + j] = 1.0
        e[QK_ROPE + j, :, QK_NOPE + j] = 1.0
    p["e"] = jnp.asarray(e.reshape(LANE, HEADS * HEAD_PAD), dtype=BF16)
    p["bg"] = bi[None, COL_G:]
    padr = ROUTER_W - N_GROUPS - N_EXPERTS
    p["wr"] = _hilo(jnp.concatenate([r_w_grp[l], r_w_exp[l], jnp.zeros((D, padr), F32)], -1))
    p["br"] = jnp.concatenate([r_b_grp[l], r_b_exp[l], jnp.zeros((padr,), F32)])[None]
    p["bs"] = jnp.broadcast_to(a_b_s[l][:, :, None], (A_GROUPS, CHUNK, CHUNK))
    p["wdw"] = jnp.concatenate([b_w_dw[l], jnp.zeros((1, BW), F32)], 0)
    return p


def kernel(x, c, ctx, c_ctx, w_ada, b_ada, w_in, b_in, a_ln_g, a_ln_b, a_w_s, a_b_s, b_w_dw, b_b_dw, b_ln_g,
           b_ln_b, d_q_g, d_w_uq, d_kv_g, d_w_uk, d_w_uv, w_branch, w_o, ln1_g, ln1_b, r_w_grp, r_b_grp,
           r_w_exp, r_b_exp, e_w1, e_w3, e_w2, ln2_g, ln2_b):
    assert x.shape == (1, SEQ, D) and ctx.shape == (1, N_CTX, D)
    x_ctx, x_lat = ctx[0], x[0]
    w_in_t = jnp.transpose(w_in, (0, 2, 1))
    c_rep = jnp.broadcast_to(jnp.concatenate([c, c_ctx[None]], axis=0)[:, :, None], (N_COND, D, LANE))
    mods = ada_mod(c_rep, w_ada, b_ada)
    tq_c, tq_s, tk = _rope_tables()
    fc = _fourier_consts()
    b_in3 = b_in.reshape(DEPTH, 1, -1)
    w_br, w_ob = w_branch.astype(BF16), w_o.astype(BF16)

    h = modln(x_ctx, x_lat, mods[0])
    for l in range(DEPTH):
        p = _layer_weights(l, w_in_t, b_in, d_w_uq, d_w_uk, d_w_uv, r_w_grp, r_b_grp, r_w_exp, r_b_exp, a_b_s,
                           b_w_dw)
        mod = mods[l]
        br_a = mix_a(l, h, w_in_t, b_in3, a_ln_g[l][None], a_ln_b[l][None], a_w_s[l].astype(BF16), p["bs"])
        br_b = conv_ln(glu(l, h, w_in_t, b_in3), p["wdw"], b_b_dw[l][None], b_ln_g[l][None], b_ln_b[l][None])
        br_c = fourier_mix(l, h, w_in_t, b_in3, fc)
        q, k, vt = mla_proj(l, h, w_in_t, b_in3, p["wkr"], p["bkr"], d_q_g[l][None], d_kv_g[l][None], p["wq"],
                            p["wqr"], p["wk"], p["wvt"], p["e"], tq_c, tq_s, tk)
        br_dt = attention(q, k, vt)
        y = merge(l, h, br_a, br_b, br_c, br_dt, w_in_t, p["bg"], w_br)
        x1, h2, ids, gates = post_mix(l, y, x_ctx, x_lat, mod, w_ob, ln1_g[l][None], ln1_b[l][None], p["wr"],
                                      p["br"])
        item_exp, item_blk0, item_nblk, item_cnt, n_used, row_tok, pos = _routing_tables(ids)
        ys = experts(l, item_exp, item_blk0, item_nblk, item_cnt, n_used, row_tok, h2, e_w1, e_w3, e_w2)
        x_ctx, x_lat, h = combine(pos, ys, gates, x1, mod, ln2_g[l][None], ln2_b[l][None],
                                  mods[min(l + 1, DEPTH - 1)])
    return x_lat[None]
```
